```python
import math
import jax, jax.numpy as jnp
from jax import lax
import numpy as np

D_MODEL = 1024
BATCH = 4
SEQ = 8192
DEPTH = 2

GRID_W = 64
ROPE_THETA = 10000.0
NORM_EPS = 1e-6
Q_BLOCK = 128

GDN_HEADS = 4
GDN_DK = 128
GDN_DV = 128
GDN_CONV = 5
GDN_CHUNK = 64
DIFF_HEADS = 4
DIFF_DQK = 64
DIFF_DV = 2 * DIFF_DQK
GQA_HEADS = 8
GQA_KV = 2
GQA_DH = 64

N_BRANCH = 3
GDN_W = GDN_HEADS * GDN_DV
DIFF_W = DIFF_HEADS * DIFF_DV
GQA_W = GQA_HEADS * GQA_DH
GDN_QKV = GDN_HEADS * (2 * GDN_DK + GDN_DV)

IN_SIZES = (
    GDN_QKV,
    GDN_W,
    2 * GDN_HEADS,
    2 * GDN_HEADS,
    DIFF_HEADS * 2 * DIFF_DQK,
    DIFF_HEADS * 2 * DIFF_DQK,
    DIFF_W,
    GQA_HEADS * GQA_DH,
    GQA_KV * GQA_DH,
    GQA_KV * GQA_DH,
    N_BRANCH * D_MODEL,
)
IN_COLS = sum(IN_SIZES)

N_GROUPS = 4
EXPERTS_PER_GROUP = 8
N_EXPERTS = N_GROUPS * EXPERTS_PER_GROUP
TOPK = 2
EXPERT_FF = 512
MOE_BLOCK = 256

kernel_name = 'hybrid_gdn_diff_axialgqa_hiermoe_encoder'


def _rmsnorm(x, w):
    xf = x.astype(jnp.float32)
    y = xf * lax.rsqrt(jnp.mean(xf * xf, axis=-1, keepdims=True) + NORM_EPS)
    return (y * w.astype(jnp.float32)).astype(x.dtype)


def _l2norm(x):
    return x * lax.rsqrt(jnp.sum(x * x, axis=-1, keepdims=True) + NORM_EPS)


def _split_cols(p):
    idx = []
    acc = 0
    for s in IN_SIZES[:-1]:
        acc += s
        idx.append(acc)
    return jnp.split(p, idx, axis=-1)


def _rope_tables(pos, dim):
    inv = 1.0 / (ROPE_THETA ** (jnp.arange(0, dim, 2, dtype=jnp.float32) / dim))
    ang = pos.astype(jnp.float32)[:, None] * inv[None, :]
    ang = jnp.concatenate([ang, ang], axis=-1)
    return (jnp.cos(ang), jnp.sin(ang))


def _apply_rope(x, cos, sin):
    c = cos[:, None, :].astype(x.dtype)
    s = sin[:, None, :].astype(x.dtype)
    half = x.shape[-1] // 2
    rot = jnp.concatenate([-x[..., half:], x[..., :half]], axis=-1)
    return x * c + rot * s


def _axial_rope(x, cos_r, sin_r, cos_c, sin_c):
    half = x.shape[-1] // 2
    return jnp.concatenate([_apply_rope(x[..., :half], cos_r, sin_r),
                            _apply_rope(x[..., half:], cos_c, sin_c)], axis=-1)


def _to_qblocks(x):
    b, s = x.shape[:2]
    x = x.reshape((b, s // Q_BLOCK, Q_BLOCK) + x.shape[2:])
    return jnp.moveaxis(x, 1, 0)


def _from_qblocks(y):
    nb, b, qb = y.shape[:3]
    return jnp.moveaxis(y, 0, 1).reshape((b, nb * qb) + y.shape[3:])


def _short_conv(x, w):
    c = x.shape[-1]
    pad = GDN_CONV // 2
    return lax.conv_general_dilated(x, w[:, None, :].astype(x.dtype), window_strides=(1,),
                                    padding=[(pad, pad)], dimension_numbers=('NWC', 'WIO', 'NWC'),
                                    feature_group_count=c)


def _gated_delta_chunked(q, k, v, beta, g):
    bsz, nh, s, dk = q.shape
    dv = v.shape[-1]
    c = GDN_CHUNK
    n = s // c
    q = q.reshape(bsz, nh, n, c, dk) * (dk ** -0.5)
    k = k.reshape(bsz, nh, n, c, dk)
    v = v.reshape(bsz, nh, n, c, dv)
    beta = beta.reshape(bsz, nh, n, c)
    gc = jnp.cumsum(g.reshape(bsz, nh, n, c), axis=-1)
    idx = jnp.arange(c)
    incl = idx[:, None] >= idx[None, :]
    strict = idx[:, None] > idx[None, :]
    decay = jnp.exp(jnp.where(incl, gc[..., :, None] - gc[..., None, :], -jnp.inf))
    kb = k * beta[..., None]
    lmat = jnp.where(strict, jnp.einsum('bhnid,bhnjd->bhnij', kb, k) * decay, 0.0)
    tmat = lmat + jnp.eye(c, dtype=jnp.float32)
    rhs = jnp.concatenate([v * beta[..., None], kb * jnp.exp(gc)[..., None]], axis=-1)
    sol = lax.linalg.triangular_solve(tmat, rhs, left_side=True, lower=True, unit_diagonal=True)
    u, w = sol[..., :dv], sol[..., dv:]
    intra = jnp.einsum('bhnid,bhnjd->bhnij', q, k) * decay
    q_dec = q * jnp.exp(gc)[..., None]
    k_dec = k * jnp.exp(gc[..., -1:] - gc)[..., None]
    g_last = jnp.exp(gc[..., -1])
    xs = (jnp.moveaxis(u, 2, 0), jnp.moveaxis(w, 2, 0), jnp.moveaxis(intra, 2, 0),
          jnp.moveaxis(q_dec, 2, 0), jnp.moveaxis(k_dec, 2, 0), jnp.moveaxis(g_last, 2, 0))

    def step(state, inp):
        u_n, w_n, intra_n, qd_n, kd_n, gl_n = inp
        v_new = u_n - jnp.einsum('bhck,bhkv->bhcv', w_n, state)
        o_n = jnp.einsum('bhck,bhkv->bhcv', qd_n, state) + jnp.einsum('bhij,bhjv->bhiv', intra_n, v_new)
        state = state * gl_n[..., None, None] + jnp.einsum('bhck,bhcv->bhkv', kd_n, v_new)
        return state, o_n

    s0 = jnp.zeros((bsz, nh, dk, dv), jnp.float32)
    _, o = lax.scan(step, s0, xs)
    return jnp.moveaxis(o, 0, 2).reshape(bsz, nh, s, dv)


def _gdn_mixer(qkv, z, b_raw, a_raw, conv_w, a_log, dt_bias, norm_w):
    bsz, s, _ = qkv.shape
    qkv = jax.nn.silu(_short_conv(qkv, conv_w))
    q, k, v = jnp.split(qkv, [GDN_HEADS * GDN_DK, 2 * GDN_HEADS * GDN_DK], axis=-1)

    def heads(t, d):
        return jnp.transpose(t.reshape(bsz, s, GDN_HEADS, d).astype(jnp.float32), (0, 2, 1, 3))

    q = _l2norm(heads(q, GDN_DK))
    k = _l2norm(heads(k, GDN_DK))
    v = heads(v, GDN_DV)
    beta = jax.nn.sigmoid(b_raw.astype(jnp.float32)).reshape(bsz, s, 2, GDN_HEADS)
    g = -jnp.exp(a_log.astype(jnp.float32)) * jax.nn.softplus(
        a_raw.astype(jnp.float32).reshape(bsz, s, 2, GDN_HEADS) + dt_bias.astype(jnp.float32))
    beta = jnp.transpose(beta, (2, 0, 3, 1))
    g = jnp.transpose(g, (2, 0, 3, 1))
    o_fwd = _gated_delta_chunked(q, k, v, beta[0], g[0])
    fl = lambda t: jnp.flip(t, axis=2)
    o_bwd = fl(_gated_delta_chunked(fl(q), fl(k), fl(v), fl(beta[1]), fl(g[1])))
    o = jnp.transpose(o_fwd + o_bwd, (0, 2, 1, 3))
    o = _rmsnorm(o, norm_w) * jax.nn.silu(z.astype(jnp.float32).reshape(bsz, s, GDN_HEADS, GDN_DV))
    return o.reshape(bsz, s, GDN_W).astype(qkv.dtype)


def _diff_mixer(q, k, v, lam_vecs, norm_w, lambda_init, cos, sin):
    bsz, s, _ = q.shape
    q = _apply_rope(q.reshape(bsz, s, 2 * DIFF_HEADS, DIFF_DQK), cos, sin)
    q = q.reshape(bsz, s, DIFF_HEADS, 2, DIFF_DQK) * (DIFF_DQK ** -0.5)
    k = _apply_rope(k.reshape(bsz, s, 2 * DIFF_HEADS, DIFF_DQK), cos, sin)
    k = k.reshape(bsz, s, DIFF_HEADS, 2, DIFF_DQK)
    v = v.reshape(bsz, s, DIFF_HEADS, DIFF_DV)
    lv = lam_vecs.astype(jnp.float32)
    lam = jnp.exp(jnp.sum(lv[0] * lv[1])) - jnp.exp(jnp.sum(lv[2] * lv[3])) + lambda_init

    def block(qb):
        sc = jnp.einsum('bqhmd,bshmd->bhmqs', qb, k).astype(jnp.float32)
        p = jax.nn.softmax(sc, axis=-1)
        pd = (p[:, :, 0] - lam * p[:, :, 1]).astype(v.dtype)
        return jnp.einsum('bhqs,bshe->bqhe', pd, v)

    o = _from_qblocks(lax.map(block, _to_qblocks(q)))
    o = _rmsnorm(o, norm_w) * (1.0 - lambda_init)
    return o.reshape(bsz, s, DIFF_W)


def _gqa_mixer(q, k, v, qn_w, kn_w, cos_r, sin_r, cos_c, sin_c):
    bsz, s, _ = q.shape
    q = _rmsnorm(q.reshape(bsz, s, GQA_HEADS, GQA_DH), qn_w)
    k = _rmsnorm(k.reshape(bsz, s, GQA_KV, GQA_DH), kn_w)
    v = v.reshape(bsz, s, GQA_KV, GQA_DH)
    q = _axial_rope(q, cos_r, sin_r, cos_c, sin_c) * (GQA_DH ** -0.5)
    k = _axial_rope(k, cos_r, sin_r, cos_c, sin_c)
    q = q.reshape(bsz, s, GQA_KV, GQA_HEADS // GQA_KV, GQA_DH)

    def block(qb):
        sc = jnp.einsum('bqkgd,bskd->bkgqs', qb, k).astype(jnp.float32)
        p = jax.nn.softmax(sc, axis=-1).astype(v.dtype)
        return jnp.einsum('bkgqs,bskd->bqkgd', p, v)

    o = _from_qblocks(lax.map(block, _to_qblocks(q)))
    return o.reshape(bsz, s, GQA_W)


def _hier_moe(h, wg, bg, we, be, w1, w3, w2):
    bsz, s, d = h.shape
    t = bsz * s
    ht = h.reshape(t, d)
    gprob = jax.nn.softmax((ht @ wg).astype(jnp.float32) + bg.astype(jnp.float32), axis=-1)
    gp, gidx = lax.top_k(gprob, 1)
    elog = ((ht @ we).astype(jnp.float32) + be.astype(jnp.float32)).reshape(t, N_GROUPS, EXPERTS_PER_GROUP)
    elog = jnp.take_along_axis(elog, gidx[:, :, None], axis=1)[:, 0]
    ep, eidx = lax.top_k(jax.nn.softmax(elog, axis=-1), TOPK)
    ep = ep / jnp.sum(ep, axis=-1, keepdims=True)
    weights = (gp * ep).astype(h.dtype)
    expert_id = (gidx * EXPERTS_PER_GROUP + eidx).astype(jnp.int32)
    a = t * TOPK
    p_len = ((a + N_EXPERTS * (MOE_BLOCK - 1) + MOE_BLOCK - 1) // MOE_BLOCK) * MOE_BLOCK
    n_blocks = p_len // MOE_BLOCK
    flat_e = expert_id.reshape(-1)
    flat_w = weights.reshape(-1)
    flat_t = jnp.repeat(jnp.arange(t, dtype=jnp.int32), TOPK)
    order = jnp.argsort(flat_e)
    se = flat_e[order]
    counts = jnp.bincount(flat_e, length=N_EXPERTS).astype(jnp.int32)
    start = jnp.cumsum(counts) - counts
    pcounts = ((counts + MOE_BLOCK - 1) // MOE_BLOCK) * MOE_BLOCK
    pend = jnp.cumsum(pcounts)
    pstart = pend - pcounts
    dest = pstart[se] + (jnp.arange(a, dtype=jnp.int32) - start[se])
    tok_buf = jnp.full((p_len,), t, jnp.int32).at[dest].set(flat_t[order])
    w_buf = jnp.zeros((p_len,), h.dtype).at[dest].set(flat_w[order])
    blk_e = jnp.minimum(jnp.searchsorted(pend, jnp.arange(n_blocks, dtype=jnp.int32) * MOE_BLOCK, side='right'),
                        N_EXPERTS - 1)
    x_pad = jnp.concatenate([ht, jnp.zeros((1, d), h.dtype)], axis=0)
    xb = x_pad[tok_buf].reshape(n_blocks, MOE_BLOCK, d)

    def expert_block(args):
        xblk, e = args
        return (jax.nn.silu(xblk @ w1[e]) * (xblk @ w3[e])) @ w2[e]

    yb = lax.map(expert_block, (xb, blk_e)).reshape(p_len, d) * w_buf[:, None]
    out = jnp.zeros((t + 1, d), h.dtype).at[tok_buf].add(yb)[:t]
    return out.reshape(bsz, s, d)


def setup_inputs(seed: int = 0) -> dict:
    key = jax.random.key(seed)
    ks = jax.random.split(key, 32)
    f32 = jnp.float32
    nrm = lambda k, shape, scale: jax.random.normal(k, shape, f32) * scale
    gain = lambda k, shape: 1.0 + 0.02 * jax.random.normal(k, shape, f32)
    lo, hi = math.log(0.001), math.log(0.1)
    dt = jnp.exp(jax.random.uniform(ks[5], (DEPTH, 2, GDN_HEADS), f32) * (hi - lo) + lo)
    return {
        'x': jax.random.normal(ks[0], (BATCH, SEQ, D_MODEL), f32),
        'attn_norm_w': gain(ks[1], (DEPTH, D_MODEL)),
        'w_in': nrm(ks[2], (DEPTH, D_MODEL, IN_COLS), D_MODEL ** -0.5),
        'gdn_conv_w': nrm(ks[3], (DEPTH, GDN_CONV, GDN_QKV), GDN_CONV ** -0.5),
        'gdn_a_log': jnp.log(jax.random.uniform(ks[4], (DEPTH, 2, GDN_HEADS), f32, 1.0, 16.0)),
        'gdn_dt_bias': dt + jnp.log(-jnp.expm1(-dt)),
        'gdn_norm_w': gain(ks[6], (DEPTH, GDN_DV)),
        'diff_lambda': nrm(ks[7], (DEPTH, 4, DIFF_DQK), 0.1),
        'diff_norm_w': gain(ks[8], (DEPTH, DIFF_DV)),
        'gqa_q_norm_w': gain(ks[9], (DEPTH, GQA_DH)),
        'gqa_k_norm_w': gain(ks[10], (DEPTH, GQA_DH)),
        'w_branch_a': nrm(ks[11], (DEPTH, GDN_W, D_MODEL), GDN_W ** -0.5),
        'w_branch_b': nrm(ks[12], (DEPTH, DIFF_W, D_MODEL), DIFF_W ** -0.5),
        'w_branch_c': nrm(ks[13], (DEPTH, GQA_W, D_MODEL), GQA_W ** -0.5),
        'w_out': nrm(ks[14], (DEPTH, D_MODEL, D_MODEL), D_MODEL ** -0.5),
        'ffn_norm_w': gain(ks[15], (DEPTH, D_MODEL)),
        'router_group_w': nrm(ks[16], (DEPTH, D_MODEL, N_GROUPS), D_MODEL ** -0.5),
        'router_group_b': nrm(ks[17], (DEPTH, N_GROUPS), 0.01),
        'router_expert_w': nrm(ks[18], (DEPTH, D_MODEL, N_EXPERTS), D_MODEL ** -0.5),
        'router_expert_b': nrm(ks[19], (DEPTH, N_EXPERTS), 0.01),
        'expert_w_gate': nrm(ks[20], (DEPTH, N_EXPERTS, D_MODEL, EXPERT_FF), D_MODEL ** -0.5),
        'expert_w_up': nrm(ks[21], (DEPTH, N_EXPERTS, D_MODEL, EXPERT_FF), D_MODEL ** -0.5),
        'expert_w_down': nrm(ks[22], (DEPTH, N_EXPERTS, EXPERT_FF, D_MODEL), EXPERT_FF ** -0.5),
        'final_norm_w': gain(ks[23], (D_MODEL,)),
    }


def reference(x, attn_norm_w, w_in, gdn_conv_w, gdn_a_log, gdn_dt_bias, gdn_norm_w, diff_lambda,
              diff_norm_w, gqa_q_norm_w, gqa_k_norm_w, w_branch_a, w_branch_b, w_branch_c, w_out,
              ffn_norm_w, router_group_w, router_group_b, router_expert_w, router_expert_b,
              expert_w_gate, expert_w_up, expert_w_down, final_norm_w):
    bsz, s, _ = x.shape
    rows = s // GRID_W
    row = jnp.broadcast_to(jnp.arange(rows)[:, None], (rows, GRID_W)).reshape(s)
    col = jnp.broadcast_to(jnp.arange(GRID_W)[None, :], (rows, GRID_W)).reshape(s)
    cos1, sin1 = _rope_tables(jnp.arange(s), DIFF_DQK)
    cos_r, sin_r = _rope_tables(row, GQA_DH // 2)
    cos_c, sin_c = _rope_tables(col, GQA_DH // 2)
    for l in range(DEPTH):
        lambda_init = 0.8 - 0.6 * math.exp(-0.3 * l)
        h = _rmsnorm(x, attn_norm_w[l])
        (g_qkv, g_z, g_b, g_a, d_q, d_k, d_v, c_q, c_k, c_v, gate_raw) = _split_cols(h @ w_in[l])
        y_a = _gdn_mixer(g_qkv, g_z, g_b, g_a, gdn_conv_w[l], gdn_a_log[l], gdn_dt_bias[l], gdn_norm_w[l])
        y_b = _diff_mixer(d_q, d_k, d_v, diff_lambda[l], diff_norm_w[l], lambda_init, cos1, sin1)
        y_c = _gqa_mixer(c_q, c_k, c_v, gqa_q_norm_w[l], gqa_k_norm_w[l], cos_r, sin_r, cos_c, sin_c)
        gates = jax.nn.sigmoid(gate_raw.astype(jnp.float32)).astype(x.dtype).reshape(bsz, s, N_BRANCH, D_MODEL)
        merged = (gates[:, :, 0] * (y_a @ w_branch_a[l]) + gates[:, :, 1] * (y_b @ w_branch_b[l])
                  + gates[:, :, 2] * (y_c @ w_branch_c[l]))
        x = x + merged @ w_out[l]
        x = x + _hier_moe(_rmsnorm(x, ffn_norm_w[l]), router_group_w[l], router_group_b[l],
                          router_expert_w[l], router_expert_b[l], expert_w_gate[l], expert_w_up[l],
                          expert_w_down[l])
    return _rmsnorm(x, final_norm_w)
```

```python
import functools
import math

import jax
import jax.numpy as jnp
from jax import lax
from jax.experimental import pallas as pl
from jax.experimental.pallas import tpu as pltpu

GRID_W = 64
ROPE_THETA = 10000.0
NORM_EPS = 1e-6
GDN_HEADS = 4
GDN_DK = 128
GDN_DV = 128
GDN_CONV = 5
GDN_CHUNK = 64
DIFF_HEADS = 4
DIFF_DQK = 64
GQA_HEADS = 8
GQA_KV = 2
GQA_DH = 64
N_GROUPS = 4
EXPERTS_PER_GROUP = 8
N_EXPERTS = N_GROUPS * EXPERTS_PER_GROUP
TOPK = 2
MOE_BLOCK = 256

LANES = 128
VMEM_LIMIT = 56 * 1024 * 1024

COL_GATE = 0
COL_QKV = 3072
COL_Z = 4608
COL_DQ = 5120
COL_DK = 5632
COL_DV = 6144
COL_CQ = 6656
COL_CK = 7168
COL_CV = 7424
N_MAIN = 7680

HI = lax.Precision.HIGHEST
F32 = jnp.float32
BF16 = jnp.bfloat16


def _cparams(sem):
    return pltpu.CompilerParams(dimension_semantics=sem, vmem_limit_bytes=VMEM_LIMIT)


def _sigmoid(x):
    return 1.0 / (1.0 + jnp.exp(-x))


def _norm_proj_kernel(x_ref, nw_ref, w_ref, o_ref, *, exact):
    x = x_ref[...]
    h = x * lax.rsqrt(jnp.mean(x * x, axis=-1, keepdims=True) + NORM_EPS) * nw_ref[...]
    if exact:
        o_ref[...] = jnp.dot(h, w_ref[...], precision=HI, preferred_element_type=F32).astype(o_ref.dtype)
    else:
        o_ref[...] = jnp.dot(h.astype(BF16), w_ref[...], preferred_element_type=F32).astype(o_ref.dtype)


def _norm_proj(x2d, norm_w, w, out_dtype, *, exact, tm, tn):
    t, d = x2d.shape
    n = w.shape[1]
    return pl.pallas_call(
        functools.partial(_norm_proj_kernel, exact=exact),
        out_shape=jax.ShapeDtypeStruct((t, n), out_dtype),
        grid=(n // tn, t // tm),
        in_specs=[pl.BlockSpec((tm, d), lambda j, i: (i, 0)),
                  pl.BlockSpec((1, d), lambda j, i: (0, 0)),
                  pl.BlockSpec((d, tn), lambda j, i: (0, j))],
        out_specs=pl.BlockSpec((tm, tn), lambda j, i: (i, j)),
        compiler_params=_cparams(("arbitrary", "arbitrary")),
        name="norm_proj_exact" if exact else "norm_proj",
    )(x2d, norm_w.reshape(1, d), w)


HALO = 16


def _rot_half(x, half):
    lane = lax.broadcasted_iota(jnp.int32, x.shape, 1)
    first = (lane % (2 * half)) < half
    return jnp.where(first, pltpu.roll(x, LANES - half, 1), pltpu.roll(x, half, 1))


def _group_sumsq(x, width):
    x2 = x * x
    if width == LANES:
        return jnp.sum(x2, axis=-1, keepdims=True)
    lane = lax.broadcasted_iota(jnp.int32, x.shape, 1)
    lo = lane < width
    s_lo = jnp.sum(jnp.where(lo, x2, 0.0), axis=-1, keepdims=True)
    s_hi = jnp.sum(jnp.where(lo, 0.0, x2), axis=-1, keepdims=True)
    return jnp.where(lo, s_lo, s_hi)


def _prep_kernel(qkv_ref, prev_ref, next_ref, dq_ref, dk_ref, cq_ref, ck_ref,
                 convw_ref, cos1_ref, sin1_ref, cos2_ref, sin2_ref, qnw_ref, knw_ref,
                 gq_ref, gk_ref, gv_ref, dqo_ref, dko_ref, cqo_ref, cko_ref, *, ts):
    i = pl.program_id(1)
    n = pl.num_programs(1)
    cur = qkv_ref[0].astype(F32)
    prev = jnp.where(i > 0, prev_ref[0].astype(F32), 0.0)
    nxt = jnp.where(i < n - 1, next_ref[0].astype(F32), 0.0)
    ext = jnp.concatenate([prev, cur, nxt], axis=0)
    pad = GDN_CONV // 2
    acc = jnp.zeros_like(cur)
    for j in range(GDN_CONV):
        off = HALO - pad + j
        acc = acc + ext[off:off + ts, :] * convw_ref[j:j + 1, :]
    act = acc * _sigmoid(acc)
    nqk = GDN_HEADS * GDN_DK
    for h in range(GDN_HEADS):
        sl = slice(h * GDN_DK, (h + 1) * GDN_DK)
        qh = act[:, sl]
        gq_ref[0, :, sl] = (qh * lax.rsqrt(_group_sumsq(qh, LANES) + NORM_EPS) * (GDN_DK ** -0.5)).astype(BF16)
        kh = act[:, nqk + h * GDN_DK: nqk + (h + 1) * GDN_DK]
        gk_ref[0, :, sl] = (kh * lax.rsqrt(_group_sumsq(kh, LANES) + NORM_EPS)).astype(BF16)
    gv_ref[0] = act[:, 2 * nqk:].astype(BF16)
    cos1, sin1 = cos1_ref[...], sin1_ref[...]
    for p in range(DIFF_HEADS):
        sl = slice(p * LANES, (p + 1) * LANES)
        xq = dq_ref[0, :, sl].astype(F32)
        dqo_ref[0, :, sl] = ((xq * cos1 + _rot_half(xq, DIFF_DQK // 2) * sin1) * (DIFF_DQK ** -0.5)).astype(BF16)
        xk = dk_ref[0, :, sl].astype(F32)
        dko_ref[0, :, sl] = (xk * cos1 + _rot_half(xk, DIFF_DQK // 2) * sin1).astype(BF16)
    cos2, sin2 = cos2_ref[...], sin2_ref[...]
    for p in range(GQA_HEADS * GQA_DH // LANES):
        sl = slice(p * LANES, (p + 1) * LANES)
        xq = cq_ref[0, :, sl].astype(F32)
        xq = xq * lax.rsqrt(_group_sumsq(xq, GQA_DH) * (1.0 / GQA_DH) + NORM_EPS) * qnw_ref[...]
        cqo_ref[0, :, sl] = ((xq * cos2 + _rot_half(xq, GQA_DH // 4) * sin2) * (GQA_DH ** -0.5)).astype(BF16)
    for p in range(2):
        sl = slice(p * LANES, (p + 1) * LANES)
        xk = ck_ref[0, :, sl].astype(F32)
        xk = xk * lax.rsqrt(_group_sumsq(xk, GQA_DH) * (1.0 / GQA_DH) + NORM_EPS) * knw_ref[...]
        cko_ref[0, :, sl] = (xk * cos2 + _rot_half(xk, GQA_DH // 4) * sin2).astype(BF16)


def _prep(main3, conv_w, cos1, sin1, cos2, sin2, qnw, knw, *, ts):
    b, s, _ = main3.shape
    nt = s // ts
    hb = ts // HALO
    last = s // HALO - 1
    row = lambda w: pl.BlockSpec((1, w), lambda bi, i: (0, 0))
    tab = pl.BlockSpec((ts, LANES), lambda bi, i: (i, 0))
    o512 = pl.BlockSpec((1, ts, 512), lambda bi, i: (bi, i, 0))
    sds = lambda w: jax.ShapeDtypeStruct((b, s, w), BF16)
    return pl.pallas_call(
        functools.partial(_prep_kernel, ts=ts),
        out_shape=(sds(512), sds(512), sds(512), sds(512), sds(512), sds(512), sds(256)),
        grid=(b, nt),
        in_specs=[
            pl.BlockSpec((1, ts, 1536), lambda bi, i: (bi, i, COL_QKV // 1536)),
            pl.BlockSpec((1, HALO, 1536), lambda bi, i: (bi, jnp.maximum(i * hb - 1, 0), COL_QKV // 1536)),
            pl.BlockSpec((1, HALO, 1536), lambda bi, i: (bi, jnp.minimum((i + 1) * hb, last), COL_QKV // 1536)),
            pl.BlockSpec((1, ts, 512), lambda bi, i: (bi, i, COL_DQ // 512)),
            pl.BlockSpec((1, ts, 512), lambda bi, i: (bi, i, COL_DK // 512)),
            pl.BlockSpec((1, ts, 512), lambda bi, i: (bi, i, COL_CQ // 512)),
            pl.BlockSpec((1, ts, 256), lambda bi, i: (bi, i, COL_CK // 256)),
            pl.BlockSpec((8, 1536), lambda bi, i: (0, 0)),
            tab, tab, tab, tab, row(LANES), row(LANES),
        ],
        out_specs=(o512, o512, o512, o512, o512, o512, pl.BlockSpec((1, ts, 256), lambda bi, i: (bi, i, 0))),
        compiler_params=_cparams(("arbitrary", "arbitrary")),
        name="mixer_prep",
    )(main3, main3, main3, main3, main3, main3, main3, conv_w, cos1, sin1, cos2, sin2, qnw, knw)


GDN_G = 8
GDN_ROWS = GDN_HEADS * GDN_CHUNK


def _stack_heads(x):
    return jnp.concatenate([x[:, h * LANES:(h + 1) * LANES] for h in range(GDN_HEADS)], axis=0)


def _row_to_col(row, eye):
    return jnp.sum(jnp.where(eye, row, 0.0), axis=1, keepdims=True)


def _gdn_kernel(a_ref, b_ref, alog_ref, dtb_ref, q_ref, k_ref, v_ref, o_ref,
                state_ref, gc_ref, gt_ref, beta_ref):
    d = pl.program_id(0)
    blk = pl.program_id(2)
    sgn = 1 - 2 * d
    n = GDN_ROWS
    c = GDN_CHUNK

    @pl.when(blk == 0)
    def _():
        state_ref[...] = jnp.zeros_like(state_ref)

    ri = lax.broadcasted_iota(jnp.int32, (n, n), 0)
    ci = lax.broadcasted_iota(jnp.int32, (n, n), 1)
    same = (ri // c) == (ci // c)
    eye = ri == ci
    after = same & ((ri - ci) * sgn > 0)
    incl = same & ((ri - ci) * sgn >= 0)

    x = a_ref[0, 0] + dtb_ref[0]
    softplus = jnp.maximum(x, 0.0) + jnp.log(1.0 + jnp.exp(-jnp.abs(x)))
    g = -jnp.exp(alog_ref[0]) * softplus
    beta_ref[...] = _sigmoid(b_ref[0, 0])
    cum_m = jnp.where(same & ((ci - ri) * sgn >= 0), 1.0, 0.0)
    gc_ref[...] = jnp.dot(g, cum_m, precision=HI, preferred_element_type=F32)
    ti = lax.broadcasted_iota(jnp.int32, (n, GDN_HEADS * LANES), 0)
    tj = lax.broadcasted_iota(jnp.int32, (n, GDN_HEADS * LANES), 1)
    tot_m = jnp.where((ti // c) == (tj // LANES), 1.0, 0.0)
    gt_ref[...] = jnp.dot(g, tot_m, precision=HI, preferred_element_type=F32)

    def chunk(j, carry):
        cc = j + d * (GDN_G - 1 - 2 * j)
        r0 = pl.multiple_of(cc * c, c)
        gc_row = gc_ref[pl.ds(cc, 1), :]
        beta_row = beta_ref[pl.ds(cc, 1), :]
        gt_row = gt_ref[pl.ds(cc, 1), :]
        gc_col = _row_to_col(gc_row, eye)
        beta_col = _row_to_col(beta_row, eye)
        k_st = _stack_heads(k_ref[0, pl.ds(r0, c), :]).astype(F32)
        q_st = _stack_heads(q_ref[0, pl.ds(r0, c), :]).astype(F32)
        v_st = _stack_heads(v_ref[0, pl.ds(r0, c), :]).astype(F32)
        egc = jnp.exp(gc_col)
        decay = jnp.exp(jnp.minimum(gc_col - gc_row, 0.0))
        kb = k_st * beta_col
        k_bf = k_st.astype(BF16)
        kk = lax.dot_general(kb.astype(BF16), k_bf, (((1,), (1,)), ((), ())), preferred_element_type=F32)
        neg_a = jnp.where(after, -(kk * decay), 0.0)
        t_m = jnp.where(eye, 1.0, 0.0) + neg_a
        p_m = neg_a
        for _ in range(int(math.log2(c)) - 1):
            p_bf = p_m.astype(BF16)
            p_m = jnp.dot(p_bf, p_bf, preferred_element_type=F32)
            t_m = t_m + jnp.dot(t_m.astype(BF16), p_m.astype(BF16), preferred_element_type=F32)
        rhs = jnp.concatenate([v_st * beta_col, kb * egc], axis=1).astype(BF16)
        sol = jnp.dot(t_m.astype(BF16), rhs, preferred_element_type=F32)
        u_st, w_st = sol[:, :LANES], sol[:, LANES:]
        qk = lax.dot_general(q_st.astype(BF16), k_bf, (((1,), (1,)), ((), ())), preferred_element_type=F32)
        intra = jnp.where(incl, qk * decay, 0.0).astype(BF16)
        q_dec = (q_st * egc).astype(BF16)
        vn, oq = [], []
        for h in range(GDN_HEADS):
            rs = slice(h * c, (h + 1) * c)
            s_h = state_ref[h].astype(BF16)
            vn.append(u_st[rs] - jnp.dot(w_st[rs].astype(BF16), s_h, preferred_element_type=F32))
            oq.append(jnp.dot(q_dec[rs], s_h, preferred_element_type=F32))
        vn_st = jnp.concatenate(vn, axis=0)
        o_st = jnp.concatenate(oq, axis=0) + jnp.dot(intra, vn_st.astype(BF16), preferred_element_type=F32)
        for h in range(GDN_HEADS):
            rs = slice(h * c, (h + 1) * c)
            gt_h = gt_row[:, h * LANES:(h + 1) * LANES]
            k_dec = (k_st[rs] * jnp.exp(gt_h[:, :1] - gc_col[rs])).astype(BF16)
            upd = lax.dot_general(k_dec, vn[h].astype(BF16), (((0,), (0,)), ((), ())), preferred_element_type=F32)
            state_ref[h] = state_ref[h] * jnp.exp(gt_h) + upd
            o_ref[0, 0, pl.ds(r0, c), h * LANES:(h + 1) * LANES] = o_st[rs]
        return carry

    lax.fori_loop(0, GDN_G, chunk, 0)


def _gdn(a_rows, b_rows, alog_row, dtb_row, gq, gk, gv):
    b, s, _ = gq.shape
    nb = s // (GDN_G * GDN_CHUNK)
    ts = GDN_G * GDN_CHUNK

    def blk_of(d, i):
        return i + d * (nb - 1 - 2 * i)

    tok = pl.BlockSpec((1, ts, 512), lambda d, bi, i: (bi, blk_of(d, i), 0))
    rows = pl.BlockSpec((1, 1, GDN_G, GDN_ROWS), lambda d, bi, i: (d, bi, blk_of(d, i), 0))
    par = pl.BlockSpec((1, 1, GDN_ROWS), lambda d, bi, i: (d, 0, 0))
    return pl.pallas_call(
        _gdn_kernel,
        out_shape=jax.ShapeDtypeStruct((2, b, s, 512), F32),
        grid=(2, b, nb),
        in_specs=[rows, rows, par, par, tok, tok, tok],
        out_specs=pl.BlockSpec((1, 1, ts, 512), lambda d, bi, i: (d, bi, blk_of(d, i), 0)),
        scratch_shapes=[pltpu.VMEM((GDN_HEADS, GDN_DK, GDN_DV), F32),
                        pltpu.VMEM((GDN_G, GDN_ROWS), F32),
                        pltpu.VMEM((GDN_G, GDN_HEADS * LANES), F32),
                        pltpu.VMEM((GDN_G, GDN_ROWS), F32)],
        compiler_params=_cparams(("arbitrary", "arbitrary", "arbitrary")),
        name="gdn_chunked",
    )(a_rows, b_rows, alog_row, dtb_row, gq, gk, gv)


def _attn_kernel(q_ref, k0_ref, k1_ref, v0_ref, v1_ref, lam_ref, nw_ref, o_ref, *, mode, tk, lambda_init):
    s_len = k0_ref.shape[1]
    tq = q_ref.shape[1]
    q = q_ref[0]
    lane = lax.broadcasted_iota(jnp.int32, q.shape, 1)
    lo = lane < (LANES // 2)
    zero = jnp.zeros_like(q)
    qs = (jnp.where(lo, q, zero), jnp.where(lo, zero, q))
    krefs = (k0_ref, k1_ref)
    vrefs = (v0_ref, v1_ref)

    def step(ci, carry):
        r0 = pl.multiple_of(ci * tk, tk)
        out = []
        for m in range(2):
            m_i, l_i, acc = carry[m]
            kc = krefs[m][0, pl.ds(r0, tk), :]
            vc = vrefs[m][0, pl.ds(r0, tk), :]
            sc = lax.dot_general(qs[m], kc, (((1,), (1,)), ((), ())), preferred_element_type=F32)
            m_new = jnp.maximum(m_i, jnp.max(sc, axis=-1, keepdims=True))
            alpha = jnp.exp(m_i - m_new)
            p = jnp.exp(sc - m_new)
            l_new = alpha * l_i + jnp.sum(p, axis=-1, keepdims=True)
            acc_new = alpha * acc + jnp.dot(p.astype(BF16), vc, preferred_element_type=F32)
            out.append((m_new, l_new, acc_new))
        return tuple(out)

    init = tuple((jnp.full((tq, 1), -jnp.inf, F32), jnp.zeros((tq, 1), F32), jnp.zeros((tq, LANES), F32))
                 for _ in range(2))
    res = lax.fori_loop(0, s_len // tk, step, init)
    o0 = res[0][2] / res[0][1]
    o1 = res[1][2] / res[1][1]
    if mode == "diff":
        lv = lam_ref[...]
        lam = (jnp.exp(jnp.sum(lv[0:1] * lv[1:2], axis=-1, keepdims=True))
               - jnp.exp(jnp.sum(lv[2:3] * lv[3:4], axis=-1, keepdims=True)) + lambda_init)
        o = o0 - lam * o1
        o = o * lax.rsqrt(jnp.mean(o * o, axis=-1, keepdims=True) + NORM_EPS) * nw_ref[...] * (1.0 - lambda_init)
    else:
        o = jnp.where(lo, o0, o1)
    o_ref[0] = o.astype(o_ref.dtype)


def _attention(q, k_arr, v_arr, lam_vecs, norm_w, *, mode, k_col, v_col, tq, tk, lambda_init=0.0):
    b, s, w = q.shape
    slabs = w // LANES
    kv = lambda col, m: pl.BlockSpec((1, s, LANES), lambda bi, p, i: (bi, 0, col(p, m)))
    return pl.pallas_call(
        functools.partial(_attn_kernel, mode=mode, tk=tk, lambda_init=lambda_init),
        out_shape=jax.ShapeDtypeStruct((b, s, w), BF16),
        grid=(b, slabs, s // tq),
        in_specs=[pl.BlockSpec((1, tq, LANES), lambda bi, p, i: (bi, i, p)),
                  kv(k_col, 0), kv(k_col, 1), kv(v_col, 0), kv(v_col, 1),
                  pl.BlockSpec((4, DIFF_DQK), lambda bi, p, i: (0, 0)),
                  pl.BlockSpec((1, LANES), lambda bi, p, i: (0, 0))],
        out_specs=pl.BlockSpec((1, tq, LANES), lambda bi, p, i: (bi, i, p)),
        compiler_params=_cparams(("arbitrary", "arbitrary", "arbitrary")),
        name="attn_" + mode,
    )(q, k_arr, k_arr, v_arr, v_arr, lam_vecs, norm_w)


def _merge_kernel(og_ref, z_ref, g0_ref, g1_ref, g2_ref, yb_ref, yc_ref, x_ref,
                  wa_ref, wb_ref, wc_ref, wo_ref, gnw_ref, fnw_ref, rw_ref, rb_ref,
                  xo_ref, h_ref, id_ref, rwgt_ref):
    o = og_ref[0] + og_ref[1]
    parts = []
    for h in range(GDN_HEADS):
        oh = o[:, h * LANES:(h + 1) * LANES]
        parts.append(oh * lax.rsqrt(jnp.mean(oh * oh, axis=-1, keepdims=True) + NORM_EPS) * gnw_ref[...])
    z = z_ref[...].astype(F32)
    ya = (jnp.concatenate(parts, axis=1) * (z * _sigmoid(z))).astype(BF16)
    merged = _sigmoid(g0_ref[...].astype(F32)) * jnp.dot(ya, wa_ref[...], preferred_element_type=F32)
    merged = merged + _sigmoid(g1_ref[...].astype(F32)) * jnp.dot(yb_ref[...], wb_ref[...], preferred_element_type=F32)
    merged = merged + _sigmoid(g2_ref[...].astype(F32)) * jnp.dot(yc_ref[...], wc_ref[...], preferred_element_type=F32)
    xn = x_ref[...] + jnp.dot(merged.astype(BF16), wo_ref[...], preferred_element_type=F32)
    xo_ref[...] = xn
    hf = xn * lax.rsqrt(jnp.mean(xn * xn, axis=-1, keepdims=True) + NORM_EPS) * fnw_ref[...]
    h_ref[...] = hf.astype(BF16)
    logits = jnp.dot(hf, rw_ref[...], precision=HI, preferred_element_type=F32) + rb_ref[...]
    lane = lax.broadcasted_iota(jnp.int32, logits.shape, 1)
    big = jnp.int32(LANES)
    ninf = -jnp.inf
    glog = jnp.where(lane < N_GROUPS, logits, ninf)
    gmax = jnp.max(glog, axis=-1, keepdims=True)
    gidx = jnp.min(jnp.where(glog == gmax, lane, big), axis=-1, keepdims=True)
    gp = 1.0 / jnp.sum(jnp.exp(glog - gmax), axis=-1, keepdims=True)
    e = lane - N_GROUPS
    sel = (e >= 0) & (e < N_EXPERTS) & ((e // EXPERTS_PER_GROUP) == gidx)
    elog = jnp.where(sel, logits, ninf)
    m1 = jnp.max(elog, axis=-1, keepdims=True)
    i1 = jnp.min(jnp.where(elog == m1, lane, big), axis=-1, keepdims=True)
    elog2 = jnp.where(lane == i1, ninf, elog)
    m2 = jnp.max(elog2, axis=-1, keepdims=True)
    i2 = jnp.min(jnp.where(elog2 == m2, lane, big), axis=-1, keepdims=True)
    e2 = jnp.exp(m2 - m1)
    w1 = 1.0 / (1.0 + e2)
    w2 = e2 * w1
    id_ref[...] = jnp.where(lane == 0, i1 - N_GROUPS, jnp.where(lane == 1, i2 - N_GROUPS, 0))
    rwgt_ref[...] = jnp.where(lane == 0, gp * w1, jnp.where(lane == 1, gp * w2, 0.0))


def _merge(og, main2, yb, yc, x2d, wa, wb, wc, wo, gnw, fnw, rw, rb, *, tm):
    t, d = x2d.shape
    full = lambda shp: pl.BlockSpec(shp, lambda i: tuple(0 for _ in shp))
    return pl.pallas_call(
        _merge_kernel,
        out_shape=(jax.ShapeDtypeStruct((t, d), F32), jax.ShapeDtypeStruct((t, d), BF16),
                   jax.ShapeDtypeStruct((t, LANES), jnp.int32), jax.ShapeDtypeStruct((t, LANES), F32)),
        grid=(t // tm,),
        in_specs=[pl.BlockSpec((2, tm, 512), lambda i: (0, i, 0)),
                  pl.BlockSpec((tm, 512), lambda i: (i, COL_Z // 512)),
                  pl.BlockSpec((tm, d), lambda i: (i, 0)),
                  pl.BlockSpec((tm, d), lambda i: (i, 1)),
                  pl.BlockSpec((tm, d), lambda i: (i, 2)),
                  pl.BlockSpec((tm, 512), lambda i: (i, 0)),
                  pl.BlockSpec((tm, 512), lambda i: (i, 0)),
                  pl.BlockSpec((tm, d), lambda i: (i, 0)),
                  full((512, d)), full((512, d)), full((512, d)), full((d, d)),
                  full((1, LANES)), full((1, d)), full((d, LANES)), full((1, LANES))],
        out_specs=(pl.BlockSpec((tm, d), lambda i: (i, 0)), pl.BlockSpec((tm, d), lambda i: (i, 0)),
                   pl.BlockSpec((tm, LANES), lambda i: (i, 0)), pl.BlockSpec((tm, LANES), lambda i: (i, 0))),
        compiler_params=_cparams(("arbitrary",)),
        name="merge_router",
    )(og, main2, main2, main2, main2, yb, yc, x2d, wa, wb, wc, wo, gnw, fnw, rw, rb)


def _expert_kernel(blk_e_ref, nused_ref, x_ref, w1_ref, w3_ref, w2_ref, o_ref):
    i = pl.program_id(0)

    @pl.when(i < nused_ref[0])
    def _():
        x = x_ref[...]
        a = jnp.dot(x, w1_ref[0], preferred_element_type=F32)
        u = jnp.dot(x, w3_ref[0], preferred_element_type=F32)
        hmid = (a * _sigmoid(a) * u).astype(BF16)
        o_ref[...] = jnp.dot(hmid, w2_ref[0], preferred_element_type=F32)

    @pl.when(i >= nused_ref[0])
    def _():
        o_ref[...] = jnp.zeros_like(o_ref)


def _experts(blk_e, nused, xb, w1, w3, w2):
    p_len, d = xb.shape
    ff = w1.shape[2]
    nblk = p_len // MOE_BLOCK
    return pl.pallas_call(
        _expert_kernel,
        out_shape=jax.ShapeDtypeStruct((p_len, d), F32),
        grid_spec=pltpu.PrefetchScalarGridSpec(
            num_scalar_prefetch=2,
            grid=(nblk,),
            in_specs=[pl.BlockSpec((MOE_BLOCK, d), lambda i, be, nu: (i, 0)),
                      pl.BlockSpec((1, d, ff), lambda i, be, nu: (be[i], 0, 0)),
                      pl.BlockSpec((1, d, ff), lambda i, be, nu: (be[i], 0, 0)),
                      pl.BlockSpec((1, ff, d), lambda i, be, nu: (be[i], 0, 0))],
            out_specs=pl.BlockSpec((MOE_BLOCK, d), lambda i, be, nu: (i, 0)),
        ),
        compiler_params=_cparams(("arbitrary",)),
        name="expert_mlp",
    )(blk_e, nused, xb, w1, w3, w2)


def _final_norm_kernel(x_ref, w_ref, o_ref):
    x = x_ref[...]
    o_ref[...] = x * lax.rsqrt(jnp.mean(x * x, axis=-1, keepdims=True) + NORM_EPS) * w_ref[...]


def _final_norm(x2d, w, *, tm):
    t, d = x2d.shape
    return pl.pallas_call(
        _final_norm_kernel,
        out_shape=jax.ShapeDtypeStruct((t, d), F32),
        grid=(t // tm,),
        in_specs=[pl.BlockSpec((tm, d), lambda i: (i, 0)), pl.BlockSpec((1, d), lambda i: (0, 0))],
        out_specs=pl.BlockSpec((tm, d), lambda i: (i, 0)),
        compiler_params=_cparams(("arbitrary",)),
        name="final_norm",
    )(x2d, w.reshape(1, d))


def _rope_tables(pos, dim):
    inv = 1.0 / (ROPE_THETA ** (jnp.arange(0, dim, 2, dtype=F32) / dim))
    ang = pos.astype(F32)[:, None] * inv[None, :]
    ang = jnp.concatenate([ang, ang], axis=-1)
    return jnp.cos(ang), jnp.sin(ang)


def _signed_sin(sin):
    half = sin.shape[-1] // 2
    return jnp.concatenate([-sin[:, :half], sin[:, half:]], axis=-1)


def _layout_w_in(w):
    o = 0
    parts = {}
    for name, size in (("qkv", 1536), ("z", 512), ("b", 8), ("a", 8), ("dq", 512), ("dk", 512), ("dv", 512),
                       ("cq", 512), ("ck", 128), ("cv", 128), ("gate", 3072)):
        parts[name] = w[:, o:o + size]
        o += size
    swap = lambda m: jnp.concatenate([m[:, 64:], m[:, :64]], axis=1)
    main = jnp.concatenate([parts["gate"], parts["qkv"], parts["z"], parts["dq"], parts["dk"], parts["dv"],
                            parts["cq"], parts["ck"], swap(parts["ck"]), parts["cv"], swap(parts["cv"])], axis=1)
    ba = jnp.concatenate([parts["b"], parts["a"], jnp.zeros((w.shape[0], LANES - 16), w.dtype)], axis=1)
    return main.astype(BF16), ba


def _rows_layout(t, bsz, s):
    nc = s // GDN_CHUNK
    t = t.reshape(bsz, nc, GDN_CHUNK, 2, GDN_HEADS)
    return jnp.transpose(t, (3, 0, 1, 4, 2)).reshape(2, bsz, nc, GDN_ROWS)


def _moe_dispatch(ids, wts, t):
    a = t * TOPK
    p_len = ((a + N_EXPERTS * (MOE_BLOCK - 1) + MOE_BLOCK - 1) // MOE_BLOCK) * MOE_BLOCK
    n_blocks = p_len // MOE_BLOCK
    flat_e = ids.reshape(-1)
    flat_w = wts.reshape(-1)
    flat_t = jnp.repeat(jnp.arange(t, dtype=jnp.int32), TOPK)
    order = jnp.argsort(flat_e)
    se = flat_e[order]
    counts = jnp.bincount(flat_e, length=N_EXPERTS).astype(jnp.int32)
    start = jnp.cumsum(counts) - counts
    pcounts = ((counts + MOE_BLOCK - 1) // MOE_BLOCK) * MOE_BLOCK
    pend = jnp.cumsum(pcounts)
    pstart = pend - pcounts
    dest = pstart[se] + (jnp.arange(a, dtype=jnp.int32) - start[se])
    tok_buf = jnp.full((p_len,), t, jnp.int32).at[dest].set(flat_t[order])
    w_buf = jnp.zeros((p_len,), F32).at[dest].set(flat_w[order])
    blk_e = jnp.minimum(jnp.searchsorted(pend, jnp.arange(n_blocks, dtype=jnp.int32) * MOE_BLOCK, side='right'),
                        N_EXPERTS - 1).astype(jnp.int32)
    nused = (pend[-1] // MOE_BLOCK).astype(jnp.int32).reshape(1)
    return tok_buf, w_buf, blk_e, nused


def kernel(x, attn_norm_w, w_in, gdn_conv_w, gdn_a_log, gdn_dt_bias, gdn_norm_w, diff_lambda, diff_norm_w,
           gqa_q_norm_w, gqa_k_norm_w, w_branch_a, w_branch_b, w_branch_c, w_out, ffn_norm_w,
           router_group_w, router_group_b, router_expert_w, router_expert_b,
           expert_w_gate, expert_w_up, expert_w_down, final_norm_w):
    bsz, s, d = x.shape
    t = bsz * s
    depth = w_in.shape[0]
    tm = min(512, t)
    ts = min(512, s)

    rows = s // GRID_W
    row = jnp.broadcast_to(jnp.arange(rows)[:, None], (rows, GRID_W)).reshape(s)
    col = jnp.broadcast_to(jnp.arange(GRID_W)[None, :], (rows, GRID_W)).reshape(s)
    c1, s1 = _rope_tables(jnp.arange(s), DIFF_DQK)
    cr, sr = _rope_tables(row, GQA_DH // 2)
    cc, sc = _rope_tables(col, GQA_DH // 2)
    cos1 = jnp.tile(c1, (1, 2))
    sin1 = jnp.tile(_signed_sin(s1), (1, 2))
    cos2 = jnp.tile(jnp.concatenate([cr, cc], axis=-1), (1, 2))
    sin2 = jnp.tile(jnp.concatenate([_signed_sin(sr), _signed_sin(sc)], axis=-1), (1, 2))

    x2 = x.reshape(t, d)
    for l in range(depth):
        lambda_init = 0.8 - 0.6 * math.exp(-0.3 * l)
        w_main, w_ba = _layout_w_in(w_in[l])
        main2 = _norm_proj(x2, attn_norm_w[l], w_main, BF16, exact=False, tm=tm, tn=1536)
        ba = _norm_proj(x2, attn_norm_w[l], w_ba, F32, exact=True, tm=tm, tn=LANES)
        main3 = main2.reshape(bsz, s, N_MAIN)

        conv_w = jnp.concatenate([gdn_conv_w[l], jnp.zeros((8 - GDN_CONV, gdn_conv_w.shape[2]), F32)], axis=0)
        qnw = jnp.tile(gqa_q_norm_w[l], 2).reshape(1, LANES)
        knw = jnp.tile(gqa_k_norm_w[l], 2).reshape(1, LANES)
        gq, gk, gv, dq, dk, cq, ck = _prep(main3, conv_w, cos1, sin1, cos2, sin2, qnw, knw, ts=ts)

        b_rows = _rows_layout(ba[:, 0:8], bsz, s)
        a_rows = _rows_layout(ba[:, 8:16], bsz, s)
        alog_row = jnp.repeat(gdn_a_log[l], GDN_CHUNK, axis=1).reshape(2, 1, GDN_ROWS)
        dtb_row = jnp.repeat(gdn_dt_bias[l], GDN_CHUNK, axis=1).reshape(2, 1, GDN_ROWS)
        og = _gdn(a_rows, b_rows, alog_row, dtb_row, gq, gk, gv)

        nw_diff = diff_norm_w[l].reshape(1, LANES)
        yb = _attention(dq, dk, main3, diff_lambda[l], nw_diff, mode="diff",
                        k_col=lambda p, m: p, v_col=lambda p, m: COL_DV // LANES + p,
                        tq=min(256, s), tk=min(512, s), lambda_init=lambda_init)
        var = lambda p, m: (p // 2 + m) % 2
        yc = _attention(cq, ck, main3, diff_lambda[l], nw_diff, mode="gqa",
                        k_col=var, v_col=lambda p, m: COL_CV // LANES + var(p, m),
                        tq=min(256, s), tk=min(512, s))

        rw = jnp.concatenate([router_group_w[l], router_expert_w[l],
                              jnp.zeros((d, LANES - N_GROUPS - N_EXPERTS), F32)], axis=1)
        rb = jnp.concatenate([router_group_b[l], router_expert_b[l],
                              jnp.zeros((LANES - N_GROUPS - N_EXPERTS,), F32)]).reshape(1, LANES)
        x2, h2, ids, wts = _merge(og.reshape(2, t, 512), main2, yb.reshape(t, 512), yc.reshape(t, 512), x2,
                                  w_branch_a[l].astype(BF16), w_branch_b[l].astype(BF16),
                                  w_branch_c[l].astype(BF16), w_out[l].astype(BF16),
                                  gdn_norm_w[l].reshape(1, LANES), ffn_norm_w[l].reshape(1, d), rw, rb, tm=min(256, t))

        tok_buf, w_buf, blk_e, nused = _moe_dispatch(ids[:, :TOPK], wts[:, :TOPK], t)
        h_pad = jnp.concatenate([h2, jnp.zeros((1, d), BF16)], axis=0)
        yblk = _experts(blk_e, nused, h_pad[tok_buf], expert_w_gate[l].astype(BF16),
                        expert_w_up[l].astype(BF16), expert_w_down[l].astype(BF16))
        x2 = x2 + jnp.zeros((t + 1, d), F32).at[tok_buf].add(yblk * w_buf[:, None])[:t]

    return _final_norm(x2, final_norm_w, tm=tm).reshape(bsz, s, d)
```

```python
import functools
import math

import jax
import jax.numpy as jnp
from jax import lax
from jax.experimental import pallas as pl
from jax.experimental.pallas import tpu as pltpu

GRID_W = 64
ROPE_THETA = 10000.0
NORM_EPS = 1e-6
GDN_HEADS = 4
GDN_DK = 128
GDN_DV = 128
GDN_CONV = 5
GDN_CHUNK = 64
DIFF_HEADS = 4
DIFF_DQK = 64
GQA_HEADS = 8
GQA_KV = 2
GQA_DH = 64
N_GROUPS = 4
EXPERTS_PER_GROUP = 8
N_EXPERTS = N_GROUPS * EXPERTS_PER_GROUP
TOPK = 2
MOE_BLOCK = 256

LANES = 128
VMEM_LIMIT = 56 * 1024 * 1024

COL_GATE = 0
COL_QKV = 3072
COL_Z = 4608
COL_DQ = 5120
COL_DK = 5632
COL_DV = 6144
COL_CQ = 6656
COL_CK = 7168
COL_CV = 7424
N_MAIN = 7680

LOG2E = math.log2(math.e)
ATTN_TQ = 512
ATTN_TK = 512

HI = lax.Precision.HIGHEST
F32 = jnp.float32
BF16 = jnp.bfloat16


def _cparams(sem):
    return pltpu.CompilerParams(dimension_semantics=sem, vmem_limit_bytes=VMEM_LIMIT)


def _sigmoid(x):
    return 1.0 / (1.0 + jnp.exp(-x))


def _norm_proj_kernel(x_ref, nw_ref, w_ref, o_ref, *, exact):
    x = x_ref[...]
    h = x * lax.rsqrt(jnp.mean(x * x, axis=-1, keepdims=True) + NORM_EPS) * nw_ref[...]
    if exact:
        o_ref[...] = jnp.dot(h, w_ref[...], precision=HI, preferred_element_type=F32).astype(o_ref.dtype)
    else:
        o_ref[...] = jnp.dot(h.astype(BF16), w_ref[...], preferred_element_type=F32).astype(o_ref.dtype)


def _norm_proj(x2d, norm_w, w, out_dtype, *, exact, tm, tn):
    t, d = x2d.shape
    n = w.shape[1]
    return pl.pallas_call(
        functools.partial(_norm_proj_kernel, exact=exact),
        out_shape=jax.ShapeDtypeStruct((t, n), out_dtype),
        grid=(n // tn, t // tm),
        in_specs=[pl.BlockSpec((tm, d), lambda j, i: (i, 0)),
                  pl.BlockSpec((1, d), lambda j, i: (0, 0)),
                  pl.BlockSpec((d, tn), lambda j, i: (0, j))],
        out_specs=pl.BlockSpec((tm, tn), lambda j, i: (i, j)),
        compiler_params=_cparams(("arbitrary", "arbitrary")),
        name="norm_proj_exact" if exact else "norm_proj",
    )(x2d, norm_w.reshape(1, d), w)


HALO = 16


def _rot_half(x, half):
    lane = lax.broadcasted_iota(jnp.int32, x.shape, 1)
    first = (lane % (2 * half)) < half
    return jnp.where(first, pltpu.roll(x, LANES - half, 1), pltpu.roll(x, half, 1))


def _group_sumsq(x, width):
    x2 = x * x
    if width == LANES:
        return jnp.sum(x2, axis=-1, keepdims=True)
    lane = lax.broadcasted_iota(jnp.int32, x.shape, 1)
    lo = lane < width
    s_lo = jnp.sum(jnp.where(lo, x2, 0.0), axis=-1, keepdims=True)
    s_hi = jnp.sum(jnp.where(lo, 0.0, x2), axis=-1, keepdims=True)
    return jnp.where(lo, s_lo, s_hi)


def _aug_slab(x, m):
    lane = lax.broadcasted_iota(jnp.int32, x.shape, 1)
    half = LANES // 2
    keep = (lane < half) if m == 0 else (lane >= half)
    one = jnp.where(lane == (1 - m) * half, 1.0, 0.0).astype(x.dtype)
    return jnp.where(keep, x, one)


def _prep_kernel(qkv_ref, prev_ref, next_ref, dq_ref, dk_ref, dv_ref, cq_ref, ck_ref, cv_ref,
                 convw_ref, cos1_ref, sin1_ref, cos2_ref, sin2_ref, qnw_ref, knw_ref,
                 gq_ref, gk_ref, gv_ref, dqo_ref, dko_ref, dvo_ref, cqo_ref, cko_ref, cvo_ref, *, ts):
    i = pl.program_id(1)
    n = pl.num_programs(1)
    cur = qkv_ref[0].astype(F32)
    prev = jnp.where(i > 0, prev_ref[0].astype(F32), 0.0)
    nxt = jnp.where(i < n - 1, next_ref[0].astype(F32), 0.0)
    ext = jnp.concatenate([prev, cur, nxt], axis=0)
    pad = GDN_CONV // 2
    acc = jnp.zeros_like(cur)
    for j in range(GDN_CONV):
        off = HALO - pad + j
        acc = acc + ext[off:off + ts, :] * convw_ref[j:j + 1, :]
    act = acc * _sigmoid(acc)
    nqk = GDN_HEADS * GDN_DK
    for h in range(GDN_HEADS):
        sl = slice(h * GDN_DK, (h + 1) * GDN_DK)
        qh = act[:, sl]
        gq_ref[0, :, sl] = (qh * lax.rsqrt(_group_sumsq(qh, LANES) + NORM_EPS) * (GDN_DK ** -0.5)).astype(BF16)
        kh = act[:, nqk + h * GDN_DK: nqk + (h + 1) * GDN_DK]
        gk_ref[0, :, sl] = (kh * lax.rsqrt(_group_sumsq(kh, LANES) + NORM_EPS)).astype(BF16)
    gv_ref[0] = act[:, 2 * nqk:].astype(BF16)
    cos1, sin1 = cos1_ref[...], sin1_ref[...]
    lane = lax.broadcasted_iota(jnp.int32, (ts, LANES), 1)
    ones_col = jnp.where(lane == 0, 1.0, 0.0).astype(BF16)
    for p in range(DIFF_HEADS):
        sl = slice(p * LANES, (p + 1) * LANES)
        xq = dq_ref[0, :, sl].astype(F32)
        dqo_ref[0, :, sl] = ((xq * cos1 + _rot_half(xq, DIFF_DQK // 2) * sin1) * (DIFF_DQK ** -0.5 * LOG2E)).astype(BF16)
        xk = dk_ref[0, :, sl].astype(F32)
        xk = (xk * cos1 + _rot_half(xk, DIFF_DQK // 2) * sin1).astype(BF16)
        for m in range(2):
            dko_ref[0, :, (2 * p + m) * LANES:(2 * p + m + 1) * LANES] = _aug_slab(xk, m)
        dvo_ref[0, :, 2 * p * LANES:(2 * p + 1) * LANES] = dv_ref[0, :, sl]
        dvo_ref[0, :, (2 * p + 1) * LANES:(2 * p + 2) * LANES] = ones_col
    cos2, sin2 = cos2_ref[...], sin2_ref[...]
    for p in range(GQA_HEADS * GQA_DH // LANES):
        sl = slice(p * LANES, (p + 1) * LANES)
        xq = cq_ref[0, :, sl].astype(F32)
        xq = xq * lax.rsqrt(_group_sumsq(xq, GQA_DH) * (1.0 / GQA_DH) + NORM_EPS) * qnw_ref[...]
        cqo_ref[0, :, sl] = ((xq * cos2 + _rot_half(xq, GQA_DH // 4) * sin2) * (GQA_DH ** -0.5 * LOG2E)).astype(BF16)
    for p in range(2):
        sl = slice(p * LANES, (p + 1) * LANES)
        xk = ck_ref[0, :, sl].astype(F32)
        xk = xk * lax.rsqrt(_group_sumsq(xk, GQA_DH) * (1.0 / GQA_DH) + NORM_EPS) * knw_ref[...]
        xk = (xk * cos2 + _rot_half(xk, GQA_DH // 4) * sin2).astype(BF16)
        xv = cv_ref[0, :, sl]
        for m in range(2):
            c = p if m == 0 else 1 - p
            osl = slice((2 * c + m) * LANES, (2 * c + m + 1) * LANES)
            cko_ref[0, :, osl] = _aug_slab(xk, m)
            cvo_ref[0, :, osl] = _aug_slab(xv, m)


def _prep(main3, conv_w, cos1, sin1, cos2, sin2, qnw, knw, *, ts):
    b, s, _ = main3.shape
    nt = s // ts
    hb = ts // HALO
    last = s // HALO - 1
    row = lambda w: pl.BlockSpec((1, w), lambda bi, i: (0, 0))
    tab = pl.BlockSpec((ts, LANES), lambda bi, i: (i, 0))
    col = lambda w, off: pl.BlockSpec((1, ts, w), lambda bi, i: (bi, i, off // w))
    out = lambda w: pl.BlockSpec((1, ts, w), lambda bi, i: (bi, i, 0))
    widths = (512, 512, 512, 512, 1024, 1024, 512, 512, 512)
    return pl.pallas_call(
        functools.partial(_prep_kernel, ts=ts),
        out_shape=tuple(jax.ShapeDtypeStruct((b, s, w), BF16) for w in widths),
        grid=(b, nt),
        in_specs=[
            col(1536, COL_QKV),
            pl.BlockSpec((1, HALO, 1536), lambda bi, i: (bi, jnp.maximum(i * hb - 1, 0), COL_QKV // 1536)),
            pl.BlockSpec((1, HALO, 1536), lambda bi, i: (bi, jnp.minimum((i + 1) * hb, last), COL_QKV // 1536)),
            col(512, COL_DQ), col(512, COL_DK), col(512, COL_DV), col(512, COL_CQ), col(256, COL_CK), col(256, COL_CV),
            pl.BlockSpec((8, 1536), lambda bi, i: (0, 0)),
            tab, tab, tab, tab, row(LANES), row(LANES),
        ],
        out_specs=tuple(out(w) for w in widths),
        compiler_params=_cparams(("arbitrary", "arbitrary")),
        name="mixer_prep",
    )(*([main3] * 9), conv_w, cos1, sin1, cos2, sin2, qnw, knw)


GDN_G = 8
GDN_ROWS = GDN_HEADS * GDN_CHUNK


def _stack_heads(x):
    return jnp.concatenate([x[:, h * LANES:(h + 1) * LANES] for h in range(GDN_HEADS)], axis=0)


def _row_to_col(row, eye):
    return jnp.sum(jnp.where(eye, row, 0.0), axis=1, keepdims=True)


def _gdn_kernel(a_ref, b_ref, alog_ref, dtb_ref, q_ref, k_ref, v_ref, o_ref,
                state_ref, gc_ref, gt_ref, beta_ref):
    d = pl.program_id(0)
    blk = pl.program_id(2)
    sgn = 1 - 2 * d
    n = GDN_ROWS
    c = GDN_CHUNK

    @pl.when(blk == 0)
    def _():
        state_ref[...] = jnp.zeros_like(state_ref)

    ri = lax.broadcasted_iota(jnp.int32, (n, n), 0)
    ci = lax.broadcasted_iota(jnp.int32, (n, n), 1)
    same = (ri // c) == (ci // c)
    eye = ri == ci
    after = same & ((ri - ci) * sgn > 0)
    incl = same & ((ri - ci) * sgn >= 0)

    x = a_ref[0, 0] + dtb_ref[0]
    softplus = jnp.maximum(x, 0.0) + jnp.log(1.0 + jnp.exp(-jnp.abs(x)))
    g = -jnp.exp(alog_ref[0]) * softplus
    beta_ref[...] = _sigmoid(b_ref[0, 0])
    cum_m = jnp.where(same & ((ci - ri) * sgn >= 0), 1.0, 0.0)
    gc_ref[...] = jnp.dot(g, cum_m, precision=HI, preferred_element_type=F32)
    ti = lax.broadcasted_iota(jnp.int32, (n, GDN_HEADS * LANES), 0)
    tj = lax.broadcasted_iota(jnp.int32, (n, GDN_HEADS * LANES), 1)
    tot_m = jnp.where((ti // c) == (tj // LANES), 1.0, 0.0)
    gt_ref[...] = jnp.dot(g, tot_m, precision=HI, preferred_element_type=F32)

    def chunk(j, carry):
        cc = j + d * (GDN_G - 1 - 2 * j)
        r0 = pl.multiple_of(cc * c, c)
        gc_row = gc_ref[pl.ds(cc, 1), :]
        beta_row = beta_ref[pl.ds(cc, 1), :]
        gt_row = gt_ref[pl.ds(cc, 1), :]
        gc_col = _row_to_col(gc_row, eye)
        beta_col = _row_to_col(beta_row, eye)
        k_st = _stack_heads(k_ref[0, pl.ds(r0, c), :]).astype(F32)
        q_st = _stack_heads(q_ref[0, pl.ds(r0, c), :]).astype(F32)
        v_st = _stack_heads(v_ref[0, pl.ds(r0, c), :]).astype(F32)
        egc = jnp.exp(gc_col)
        decay = jnp.exp(jnp.minimum(gc_col - gc_row, 0.0))
        kb = k_st * beta_col
        k_bf = k_st.astype(BF16)
        kk = lax.dot_general(kb.astype(BF16), k_bf, (((1,), (1,)), ((), ())), preferred_element_type=F32)
        neg_a = jnp.where(after, -(kk * decay), 0.0)
        t_m = jnp.where(eye, 1.0, 0.0) + neg_a
        p_m = neg_a
        for _ in range(int(math.log2(c)) - 1):
            p_bf = p_m.astype(BF16)
            p_m = jnp.dot(p_bf, p_bf, preferred_element_type=F32)
            t_m = t_m + jnp.dot(t_m.astype(BF16), p_m.astype(BF16), preferred_element_type=F32)
        rhs = jnp.concatenate([v_st * beta_col, kb * egc], axis=1).astype(BF16)
        sol = jnp.dot(t_m.astype(BF16), rhs, preferred_element_type=F32)
        u_st, w_st = sol[:, :LANES], sol[:, LANES:]
        qk = lax.dot_general(q_st.astype(BF16), k_bf, (((1,), (1,)), ((), ())), preferred_element_type=F32)
        intra = jnp.where(incl, qk * decay, 0.0).astype(BF16)
        q_dec = (q_st * egc).astype(BF16)
        vn, oq = [], []
        for h in range(GDN_HEADS):
            rs = slice(h * c, (h + 1) * c)
            s_h = state_ref[h].astype(BF16)
            vn.append(u_st[rs] - jnp.dot(w_st[rs].astype(BF16), s_h, preferred_element_type=F32))
            oq.append(jnp.dot(q_dec[rs], s_h, preferred_element_type=F32))
        vn_st = jnp.concatenate(vn, axis=0)
        o_st = jnp.concatenate(oq, axis=0) + jnp.dot(intra, vn_st.astype(BF16), preferred_element_type=F32)
        for h in range(GDN_HEADS):
            rs = slice(h * c, (h + 1) * c)
            gt_h = gt_row[:, h * LANES:(h + 1) * LANES]
            k_dec = (k_st[rs] * jnp.exp(gt_h[:, :1] - gc_col[rs])).astype(BF16)
            upd = lax.dot_general(k_dec, vn[h].astype(BF16), (((0,), (0,)), ((), ())), preferred_element_type=F32)
            state_ref[h] = state_ref[h] * jnp.exp(gt_h) + upd
            o_ref[0, 0, pl.ds(r0, c), h * LANES:(h + 1) * LANES] = o_st[rs]
        return carry

    lax.fori_loop(0, GDN_G, chunk, 0)


def _gdn(a_rows, b_rows, alog_row, dtb_row, gq, gk, gv):
    b, s, _ = gq.shape
    nb = s // (GDN_G * GDN_CHUNK)
    ts = GDN_G * GDN_CHUNK

    def blk_of(d, i):
        return i + d * (nb - 1 - 2 * i)

    tok = pl.BlockSpec((1, ts, 512), lambda d, bi, i: (bi, blk_of(d, i), 0))
    rows = pl.BlockSpec((1, 1, GDN_G, GDN_ROWS), lambda d, bi, i: (d, bi, blk_of(d, i), 0))
    par = pl.BlockSpec((1, 1, GDN_ROWS), lambda d, bi, i: (d, 0, 0))
    return pl.pallas_call(
        _gdn_kernel,
        out_shape=jax.ShapeDtypeStruct((2, b, s, 512), F32),
        grid=(2, b, nb),
        in_specs=[rows, rows, par, par, tok, tok, tok],
        out_specs=pl.BlockSpec((1, 1, ts, 512), lambda d, bi, i: (d, bi, blk_of(d, i), 0)),
        scratch_shapes=[pltpu.VMEM((GDN_HEADS, GDN_DK, GDN_DV), F32),
                        pltpu.VMEM((GDN_G, GDN_ROWS), F32),
                        pltpu.VMEM((GDN_G, GDN_HEADS * LANES), F32),
                        pltpu.VMEM((GDN_G, GDN_ROWS), F32)],
        compiler_params=_cparams(("arbitrary", "arbitrary", "arbitrary")),
        name="gdn_chunked",
    )(a_rows, b_rows, alog_row, dtb_row, gq, gk, gv)


def _attn_kernel(q_ref, k0_ref, k1_ref, v0_ref, v1_ref, lam_ref, nw_ref, o_ref, acc_ref, *, mode, tk, lambda_init):
    s_len = k0_ref.shape[1]
    tq = q_ref.shape[1]
    nv = v0_ref.shape[2]
    half = LANES // 2
    q = q_ref[0]
    lane = lax.broadcasted_iota(jnp.int32, q.shape, 1)
    lo = lane < half
    keep = (lo, lane >= half)
    stab = (lane == half, lane == 0)
    zero = jnp.zeros_like(q)
    krefs = (k0_ref, k1_ref)
    vrefs = (v0_ref, v1_ref)
    nchunks = s_len // tk
    dn = (((1,), (1,)), ((), ()))

    def kchunk(m, ci):
        return krefs[m][0, pl.ds(pl.multiple_of(ci * tk, tk), tk), :]

    def vchunk(m, ci):
        return vrefs[m][0, pl.ds(pl.multiple_of(ci * tk, tk), tk), :]

    qm, qa = [], []
    for m in range(2):
        qm.append(jnp.where(keep[m], q, zero))
        mx = jnp.max(lax.dot_general(qm[m], kchunk(m, 0), dn, preferred_element_type=F32), axis=-1, keepdims=True)
        qa.append(jnp.where(stab[m], (-mx).astype(BF16), qm[m]))

    def fast(ci, acc):
        out = []
        for m in range(2):
            sc = lax.dot_general(qa[m], kchunk(m, ci), dn, preferred_element_type=F32)
            out.append(acc[m] + jnp.dot(jnp.exp2(sc).astype(BF16), vchunk(m, ci), preferred_element_type=F32))
        return tuple(out)

    acc = lax.fori_loop(0, nchunks, fast, tuple(jnp.zeros((tq, nv), F32) for _ in range(2)))
    nonfinite = jnp.float32(0.0)
    for m in range(2):
        acc_ref[m] = acc[m]
        nonfinite = nonfinite + jnp.sum(jnp.where(jnp.isfinite(acc[m]), 0.0, 1.0))

    @pl.when(nonfinite > 0.0)
    def _():
        def slow(ci, carry):
            out = []
            for m in range(2):
                m_i, a_i = carry[m]
                sc = lax.dot_general(qm[m], kchunk(m, ci), dn, preferred_element_type=F32)
                m_new = jnp.maximum(m_i, jnp.max(sc, axis=-1, keepdims=True))
                p = jnp.exp2(sc - m_new).astype(BF16)
                out.append((m_new, jnp.exp2(m_i - m_new) * a_i + jnp.dot(p, vchunk(m, ci), preferred_element_type=F32)))
            return tuple(out)

        init = tuple((jnp.full((tq, 1), -jnp.inf, F32), jnp.zeros((tq, nv), F32)) for _ in range(2))
        res = lax.fori_loop(0, nchunks, slow, init)
        for m in range(2):
            acc_ref[m] = res[m][1]

    if mode == "diff":
        o0 = acc_ref[0, :, :LANES] / acc_ref[0, :, LANES:LANES + 1]
        o1 = acc_ref[1, :, :LANES] / acc_ref[1, :, LANES:LANES + 1]
    else:
        o0 = acc_ref[0] / acc_ref[0, :, half:half + 1]
        o1 = acc_ref[1] / acc_ref[1, :, 0:1]
    if mode == "diff":
        lv = lam_ref[...]
        lam = (jnp.exp(jnp.sum(lv[0:1] * lv[1:2], axis=-1, keepdims=True))
               - jnp.exp(jnp.sum(lv[2:3] * lv[3:4], axis=-1, keepdims=True)) + lambda_init)
        o = o0 - lam * o1
        o = o * lax.rsqrt(jnp.mean(o * o, axis=-1, keepdims=True) + NORM_EPS) * nw_ref[...] * (1.0 - lambda_init)
    else:
        o = jnp.where(lo, o0, o1)
    o_ref[0] = o.astype(o_ref.dtype)


def _attention(q, k_arr, v_arr, lam_vecs, norm_w, *, mode, tq, tk, lambda_init=0.0):
    b, s, w = q.shape
    slabs = w // LANES
    if mode == "diff":
        nv = 2 * LANES
        k_col = lambda p, m: 2 * p + m
        v_col = lambda p, m: p
    else:
        nv = LANES
        k_col = v_col = lambda p, m: 2 * (p // 2) + m
    kspec = lambda m: pl.BlockSpec((1, s, LANES), lambda bi, p, i: (bi, 0, k_col(p, m)))
    vspec = lambda m: pl.BlockSpec((1, s, nv), lambda bi, p, i: (bi, 0, v_col(p, m)))
    return pl.pallas_call(
        functools.partial(_attn_kernel, mode=mode, tk=tk, lambda_init=lambda_init),
        out_shape=jax.ShapeDtypeStruct((b, s, w), BF16),
        grid=(b, slabs, s // tq),
        in_specs=[pl.BlockSpec((1, tq, LANES), lambda bi, p, i: (bi, i, p)),
                  kspec(0), kspec(1), vspec(0), vspec(1),
                  pl.BlockSpec((4, DIFF_DQK), lambda bi, p, i: (0, 0)),
                  pl.BlockSpec((1, LANES), lambda bi, p, i: (0, 0))],
        out_specs=pl.BlockSpec((1, tq, LANES), lambda bi, p, i: (bi, i, p)),
        scratch_shapes=[pltpu.VMEM((2, tq, nv), F32)],
        compiler_params=_cparams(("arbitrary", "arbitrary", "arbitrary")),
        name="attn_" + mode,
    )(q, k_arr, k_arr, v_arr, v_arr, lam_vecs, norm_w)


def _merge_kernel(og_ref, z_ref, g0_ref, g1_ref, g2_ref, yb_ref, yc_ref, x_ref,
                  wa_ref, wb_ref, wc_ref, wo_ref, gnw_ref, fnw_ref, rw_ref, rb_ref,
                  xo_ref, h_ref, id_ref, rwgt_ref):
    o = og_ref[0] + og_ref[1]
    parts = []
    for h in range(GDN_HEADS):
        oh = o[:, h * LANES:(h + 1) * LANES]
        parts.append(oh * lax.rsqrt(jnp.mean(oh * oh, axis=-1, keepdims=True) + NORM_EPS) * gnw_ref[...])
    z = z_ref[...].astype(F32)
    ya = (jnp.concatenate(parts, axis=1) * (z * _sigmoid(z))).astype(BF16)
    merged = _sigmoid(g0_ref[...].astype(F32)) * jnp.dot(ya, wa_ref[...], preferred_element_type=F32)
    merged = merged + _sigmoid(g1_ref[...].astype(F32)) * jnp.dot(yb_ref[...], wb_ref[...], preferred_element_type=F32)
    merged = merged + _sigmoid(g2_ref[...].astype(F32)) * jnp.dot(yc_ref[...], wc_ref[...], preferred_element_type=F32)
    xn = x_ref[...] + jnp.dot(merged.astype(BF16), wo_ref[...], preferred_element_type=F32)
    xo_ref[...] = xn
    hf = xn * lax.rsqrt(jnp.mean(xn * xn, axis=-1, keepdims=True) + NORM_EPS) * fnw_ref[...]
    h_ref[...] = hf.astype(BF16)
    logits = jnp.dot(hf, rw_ref[...], precision=HI, preferred_element_type=F32) + rb_ref[...]
    lane = lax.broadcasted_iota(jnp.int32, logits.shape, 1)
    big = jnp.int32(LANES)
    ninf = -jnp.inf
    glog = jnp.where(lane < N_GROUPS, logits, ninf)
    gmax = jnp.max(glog, axis=-1, keepdims=True)
    gidx = jnp.min(jnp.where(glog == gmax, lane, big), axis=-1, keepdims=True)
    gp = 1.0 / jnp.sum(jnp.exp(glog - gmax), axis=-1, keepdims=True)
    e = lane - N_GROUPS
    sel = (e >= 0) & (e < N_EXPERTS) & ((e // EXPERTS_PER_GROUP) == gidx)
    elog = jnp.where(sel, logits, ninf)
    m1 = jnp.max(elog, axis=-1, keepdims=True)
    i1 = jnp.min(jnp.where(elog == m1, lane, big), axis=-1, keepdims=True)
    elog2 = jnp.where(lane == i1, ninf, elog)
    m2 = jnp.max(elog2, axis=-1, keepdims=True)
    i2 = jnp.min(jnp.where(elog2 == m2, lane, big), axis=-1, keepdims=True)
    e2 = jnp.exp(m2 - m1)
    w1 = 1.0 / (1.0 + e2)
    w2 = e2 * w1
    id_ref[...] = jnp.where(lane == 0, i1 - N_GROUPS, jnp.where(lane == 1, i2 - N_GROUPS, 0))
    rwgt_ref[...] = jnp.where(lane == 0, gp * w1, jnp.where(lane == 1, gp * w2, 0.0))


def _merge(og, main2, yb, yc, x2d, wa, wb, wc, wo, gnw, fnw, rw, rb, *, tm):
    t, d = x2d.shape
    full = lambda shp: pl.BlockSpec(shp, lambda i: tuple(0 for _ in shp))
    return pl.pallas_call(
        _merge_kernel,
        out_shape=(jax.ShapeDtypeStruct((t, d), F32), jax.ShapeDtypeStruct((t, d), BF16),
                   jax.ShapeDtypeStruct((t, LANES), jnp.int32), jax.ShapeDtypeStruct((t, LANES), F32)),
        grid=(t // tm,),
        in_specs=[pl.BlockSpec((2, tm, 512), lambda i: (0, i, 0)),
                  pl.BlockSpec((tm, 512), lambda i: (i, COL_Z // 512)),
                  pl.BlockSpec((tm, d), lambda i: (i, 0)),
                  pl.BlockSpec((tm, d), lambda i: (i, 1)),
                  pl.BlockSpec((tm, d), lambda i: (i, 2)),
                  pl.BlockSpec((tm, 512), lambda i: (i, 0)),
                  pl.BlockSpec((tm, 512), lambda i: (i, 0)),
                  pl.BlockSpec((tm, d), lambda i: (i, 0)),
                  full((512, d)), full((512, d)), full((512, d)), full((d, d)),
                  full((1, LANES)), full((1, d)), full((d, LANES)), full((1, LANES))],
        out_specs=(pl.BlockSpec((tm, d), lambda i: (i, 0)), pl.BlockSpec((tm, d), lambda i: (i, 0)),
                   pl.BlockSpec((tm, LANES), lambda i: (i, 0)), pl.BlockSpec((tm, LANES), lambda i: (i, 0))),
        compiler_params=_cparams(("arbitrary",)),
        name="merge_router",
    )(og, main2, main2, main2, main2, yb, yc, x2d, wa, wb, wc, wo, gnw, fnw, rw, rb)


def _expert_kernel(blk_e_ref, nused_ref, x_ref, w1_ref, w3_ref, w2_ref, o_ref):
    i = pl.program_id(0)

    @pl.when(i < nused_ref[0])
    def _():
        x = x_ref[...]
        a = jnp.dot(x, w1_ref[0], preferred_element_type=F32)
        u = jnp.dot(x, w3_ref[0], preferred_element_type=F32)
        hmid = (a * _sigmoid(a) * u).astype(BF16)
        o_ref[...] = jnp.dot(hmid, w2_ref[0], preferred_element_type=F32)

    @pl.when(i >= nused_ref[0])
    def _():
        o_ref[...] = jnp.zeros_like(o_ref)


def _experts(blk_e, nused, xb, w1, w3, w2):
    p_len, d = xb.shape
    ff = w1.shape[2]
    nblk = p_len // MOE_BLOCK
    return pl.pallas_call(
        _expert_kernel,
        out_shape=jax.ShapeDtypeStruct((p_len, d), F32),
        grid_spec=pltpu.PrefetchScalarGridSpec(
            num_scalar_prefetch=2,
            grid=(nblk,),
            in_specs=[pl.BlockSpec((MOE_BLOCK, d), lambda i, be, nu: (i, 0)),
                      pl.BlockSpec((1, d, ff), lambda i, be, nu: (be[i], 0, 0)),
                      pl.BlockSpec((1, d, ff), lambda i, be, nu: (be[i], 0, 0)),
                      pl.BlockSpec((1, ff, d), lambda i, be, nu: (be[i], 0, 0))],
            out_specs=pl.BlockSpec((MOE_BLOCK, d), lambda i, be, nu: (i, 0)),
        ),
        compiler_params=_cparams(("arbitrary",)),
        name="expert_mlp",
    )(blk_e, nused, xb, w1, w3, w2)


def _final_norm_kernel(x_ref, w_ref, o_ref):
    x = x_ref[...]
    o_ref[...] = x * lax.rsqrt(jnp.mean(x * x, axis=-1, keepdims=True) + NORM_EPS) * w_ref[...]


def _final_norm(x2d, w, *, tm):
    t, d = x2d.shape
    return pl.pallas_call(
        _final_norm_kernel,
        out_shape=jax.ShapeDtypeStruct((t, d), F32),
        grid=(t // tm,),
        in_specs=[pl.BlockSpec((tm, d), lambda i: (i, 0)), pl.BlockSpec((1, d), lambda i: (0, 0))],
        out_specs=pl.BlockSpec((tm, d), lambda i: (i, 0)),
        compiler_params=_cparams(("arbitrary",)),
        name="final_norm",
    )(x2d, w.reshape(1, d))


def _rope_tables(pos, dim):
    inv = 1.0 / (ROPE_THETA ** (jnp.arange(0, dim, 2, dtype=F32) / dim))
    ang = pos.astype(F32)[:, None] * inv[None, :]
    ang = jnp.concatenate([ang, ang], axis=-1)
    return jnp.cos(ang), jnp.sin(ang)


def _signed_sin(sin):
    half = sin.shape[-1] // 2
    return jnp.concatenate([-sin[:, :half], sin[:, half:]], axis=-1)


def _layout_w_in(w):
    o = 0
    parts = {}
    for name, size in (("qkv", 1536), ("z", 512), ("b", 8), ("a", 8), ("dq", 512), ("dk", 512), ("dv", 512),
                       ("cq", 512), ("ck", 128), ("cv", 128), ("gate", 3072)):
        parts[name] = w[:, o:o + size]
        o += size
    swap = lambda m: jnp.concatenate([m[:, 64:], m[:, :64]], axis=1)
    main = jnp.concatenate([parts["gate"], parts["qkv"], parts["z"], parts["dq"], parts["dk"], parts["dv"],
                            parts["cq"], parts["ck"], swap(parts["ck"]), parts["cv"], swap(parts["cv"])], axis=1)
    ba = jnp.concatenate([parts["b"], parts["a"], jnp.zeros((w.shape[0], LANES - 16), w.dtype)], axis=1)
    return main.astype(BF16), ba


def _rows_layout(t, bsz, s):
    nc = s // GDN_CHUNK
    t = t.reshape(bsz, nc, GDN_CHUNK, 2, GDN_HEADS)
    return jnp.transpose(t, (3, 0, 1, 4, 2)).reshape(2, bsz, nc, GDN_ROWS)


def _moe_dispatch(ids, wts, t):
    a = t * TOPK
    p_len = ((a + N_EXPERTS * (MOE_BLOCK - 1) + MOE_BLOCK - 1) // MOE_BLOCK) * MOE_BLOCK
    n_blocks = p_len // MOE_BLOCK
    flat_e = ids.reshape(-1)
    flat_w = wts.reshape(-1)
    flat_t = jnp.repeat(jnp.arange(t, dtype=jnp.int32), TOPK)
    order = jnp.argsort(flat_e)
    se = flat_e[order]
    counts = jnp.bincount(flat_e, length=N_EXPERTS).astype(jnp.int32)
    start = jnp.cumsum(counts) - counts
    pcounts = ((counts + MOE_BLOCK - 1) // MOE_BLOCK) * MOE_BLOCK
    pend = jnp.cumsum(pcounts)
    pstart = pend - pcounts
    dest = pstart[se] + (jnp.arange(a, dtype=jnp.int32) - start[se])
    tok_buf = jnp.full((p_len,), t, jnp.int32).at[dest].set(flat_t[order])
    w_buf = jnp.zeros((p_len,), F32).at[dest].set(flat_w[order])
    blk_e = jnp.minimum(jnp.searchsorted(pend, jnp.arange(n_blocks, dtype=jnp.int32) * MOE_BLOCK, side='right'),
                        N_EXPERTS - 1).astype(jnp.int32)
    nused = (pend[-1] // MOE_BLOCK).astype(jnp.int32).reshape(1)
    return tok_buf, w_buf, blk_e, nused


def kernel(x, attn_norm_w, w_in, gdn_conv_w, gdn_a_log, gdn_dt_bias, gdn_norm_w, diff_lambda, diff_norm_w,
           gqa_q_norm_w, gqa_k_norm_w, w_branch_a, w_branch_b, w_branch_c, w_out, ffn_norm_w,
           router_group_w, router_group_b, router_expert_w, router_expert_b,
           expert_w_gate, expert_w_up, expert_w_down, final_norm_w):
    bsz, s, d = x.shape
    t = bsz * s
    depth = w_in.shape[0]
    tm = min(512, t)
    ts = min(512, s)

    rows = s // GRID_W
    row = jnp.broadcast_to(jnp.arange(rows)[:, None], (rows, GRID_W)).reshape(s)
    col = jnp.broadcast_to(jnp.arange(GRID_W)[None, :], (rows, GRID_W)).reshape(s)
    c1, s1 = _rope_tables(jnp.arange(s), DIFF_DQK)
    cr, sr = _rope_tables(row, GQA_DH // 2)
    cc, sc = _rope_tables(col, GQA_DH // 2)
    cos1 = jnp.tile(c1, (1, 2))
    sin1 = jnp.tile(_signed_sin(s1), (1, 2))
    cos2 = jnp.tile(jnp.concatenate([cr, cc], axis=-1), (1, 2))
    sin2 = jnp.tile(jnp.concatenate([_signed_sin(sr), _signed_sin(sc)], axis=-1), (1, 2))

    x2 = x.reshape(t, d)
    for l in range(depth):
        lambda_init = 0.8 - 0.6 * math.exp(-0.3 * l)
        w_main, w_ba = _layout_w_in(w_in[l])
        main2 = _norm_proj(x2, attn_norm_w[l], w_main, BF16, exact=False, tm=tm, tn=1536)
        ba = _norm_proj(x2, attn_norm_w[l], w_ba, F32, exact=True, tm=tm, tn=LANES)
        main3 = main2.reshape(bsz, s, N_MAIN)

        conv_w = jnp.concatenate([gdn_conv_w[l], jnp.zeros((8 - GDN_CONV, gdn_conv_w.shape[2]), F32)], axis=0)
        qnw = jnp.tile(gqa_q_norm_w[l], 2).reshape(1, LANES)
        knw = jnp.tile(gqa_k_norm_w[l], 2).reshape(1, LANES)
        gq, gk, gv, dq, dk, dv, cq, ck, cv = _prep(main3, conv_w, cos1, sin1, cos2, sin2, qnw, knw, ts=ts)

        b_rows = _rows_layout(ba[:, 0:8], bsz, s)
        a_rows = _rows_layout(ba[:, 8:16], bsz, s)
        alog_row = jnp.repeat(gdn_a_log[l], GDN_CHUNK, axis=1).reshape(2, 1, GDN_ROWS)
        dtb_row = jnp.repeat(gdn_dt_bias[l], GDN_CHUNK, axis=1).reshape(2, 1, GDN_ROWS)
        og = _gdn(a_rows, b_rows, alog_row, dtb_row, gq, gk, gv)

        nw_diff = diff_norm_w[l].reshape(1, LANES)
        yb = _attention(dq, dk, dv, diff_lambda[l], nw_diff, mode="diff",
                        tq=min(ATTN_TQ, s), tk=min(ATTN_TK, s), lambda_init=lambda_init)
        yc = _attention(cq, ck, cv, diff_lambda[l], nw_diff, mode="gqa",
                        tq=min(ATTN_TQ, s), tk=min(ATTN_TK, s))

        rw = jnp.concatenate([router_group_w[l], router_expert_w[l],
                              jnp.zeros((d, LANES - N_GROUPS - N_EXPERTS), F32)], axis=1)
        rb = jnp.concatenate([router_group_b[l], router_expert_b[l],
                              jnp.zeros((LANES - N_GROUPS - N_EXPERTS,), F32)]).reshape(1, LANES)
        x2, h2, ids, wts = _merge(og.reshape(2, t, 512), main2, yb.reshape(t, 512), yc.reshape(t, 512), x2,
                                  w_branch_a[l].astype(BF16), w_branch_b[l].astype(BF16),
                                  w_branch_c[l].astype(BF16), w_out[l].astype(BF16),
                                  gdn_norm_w[l].reshape(1, LANES), ffn_norm_w[l].reshape(1, d), rw, rb, tm=min(256, t))

        tok_buf, w_buf, blk_e, nused = _moe_dispatch(ids[:, :TOPK], wts[:, :TOPK], t)
        h_pad = jnp.concatenate([h2, jnp.zeros((1, d), BF16)], axis=0)
        yblk = _experts(blk_e, nused, h_pad[tok_buf], expert_w_gate[l].astype(BF16),
                        expert_w_up[l].astype(BF16), expert_w_down[l].astype(BF16))
        x2 = x2 + jnp.zeros((t + 1, d), F32).at[tok_buf].add(yblk * w_buf[:, None])[:t]

    return _final_norm(x2, final_norm_w, tm=tm).reshape(bsz, s, d)
```

```python
import functools
import math

import jax
import jax.numpy as jnp
from jax import lax
from jax.experimental import pallas as pl
from jax.experimental.pallas import tpu as pltpu

GRID_W = 64
ROPE_THETA = 10000.0
NORM_EPS = 1e-6
GDN_HEADS = 4
GDN_DK = 128
GDN_DV = 128
GDN_CONV = 5
GDN_CHUNK = 64
DIFF_HEADS = 4
DIFF_DQK = 64
GQA_HEADS = 8
GQA_KV = 2
GQA_DH = 64
N_GROUPS = 4
EXPERTS_PER_GROUP = 8
N_EXPERTS = N_GROUPS * EXPERTS_PER_GROUP
TOPK = 2
MOE_BLOCK = 256

LANES = 128
VMEM_LIMIT = 56 * 1024 * 1024

COL_GATE = 0
COL_QKV = 3072
COL_Z = 4608
COL_DQ = 5120
COL_DK = 5632
COL_DV = 6144
COL_CQ = 6656
COL_CK = 7168
COL_CV = 7424
N_MAIN = 7680

LOG2E = math.log2(math.e)
ATTN_TQ = 512
ATTN_TK = 512

HI = lax.Precision.HIGHEST
F32 = jnp.float32
BF16 = jnp.bfloat16


def _cparams(sem):
    return pltpu.CompilerParams(dimension_semantics=sem, vmem_limit_bytes=VMEM_LIMIT)


def _sigmoid(x):
    return 1.0 / (1.0 + jnp.exp(-x))


def _norm_proj_kernel(x_ref, nw_ref, w_ref, o_ref, *, exact):
    x = x_ref[...]
    h = x * lax.rsqrt(jnp.mean(x * x, axis=-1, keepdims=True) + NORM_EPS) * nw_ref[...]
    if exact:
        o_ref[...] = jnp.dot(h, w_ref[...], precision=HI, preferred_element_type=F32).astype(o_ref.dtype)
    else:
        o_ref[...] = jnp.dot(h.astype(BF16), w_ref[...], preferred_element_type=F32).astype(o_ref.dtype)


def _norm_proj(x2d, norm_w, w, out_dtype, *, exact, tm, tn):
    t, d = x2d.shape
    n = w.shape[1]
    return pl.pallas_call(
        functools.partial(_norm_proj_kernel, exact=exact),
        out_shape=jax.ShapeDtypeStruct((t, n), out_dtype),
        grid=(n // tn, t // tm),
        in_specs=[pl.BlockSpec((tm, d), lambda j, i: (i, 0)),
                  pl.BlockSpec((1, d), lambda j, i: (0, 0)),
                  pl.BlockSpec((d, tn), lambda j, i: (0, j))],
        out_specs=pl.BlockSpec((tm, tn), lambda j, i: (i, j)),
        compiler_params=_cparams(("arbitrary", "arbitrary")),
        name="norm_proj_exact" if exact else "norm_proj",
    )(x2d, norm_w.reshape(1, d), w)


HALO = 16


def _rot_half(x, half):
    lane = lax.broadcasted_iota(jnp.int32, x.shape, 1)
    first = (lane % (2 * half)) < half
    return jnp.where(first, pltpu.roll(x, LANES - half, 1), pltpu.roll(x, half, 1))


def _group_sumsq(x, width):
    x2 = x * x
    if width == LANES:
        return jnp.sum(x2, axis=-1, keepdims=True)
    lane = lax.broadcasted_iota(jnp.int32, x.shape, 1)
    lo = lane < width
    s_lo = jnp.sum(jnp.where(lo, x2, 0.0), axis=-1, keepdims=True)
    s_hi = jnp.sum(jnp.where(lo, 0.0, x2), axis=-1, keepdims=True)
    return jnp.where(lo, s_lo, s_hi)


def _aug_slab(x, m):
    lane = lax.broadcasted_iota(jnp.int32, x.shape, 1)
    half = LANES // 2
    keep = (lane < half) if m == 0 else (lane >= half)
    one = jnp.where(lane == (1 - m) * half, 1.0, 0.0).astype(x.dtype)
    return jnp.where(keep, x, one)


def _prep_kernel(qkv_ref, prev_ref, next_ref, dq_ref, dk_ref, dv_ref, cq_ref, ck_ref, cv_ref,
                 convw_ref, cos1_ref, sin1_ref, cos2_ref, sin2_ref, qnw_ref, knw_ref,
                 gq_ref, gk_ref, gv_ref, dqo_ref, dko_ref, dvo_ref, cqo_ref, cko_ref, cvo_ref, *, ts):
    i = pl.program_id(1)
    n = pl.num_programs(1)
    cur = qkv_ref[0].astype(F32)
    prev = jnp.where(i > 0, prev_ref[0].astype(F32), 0.0)
    nxt = jnp.where(i < n - 1, next_ref[0].astype(F32), 0.0)
    ext = jnp.concatenate([prev, cur, nxt], axis=0)
    pad = GDN_CONV // 2
    acc = jnp.zeros_like(cur)
    for j in range(GDN_CONV):
        off = HALO - pad + j
        acc = acc + ext[off:off + ts, :] * convw_ref[j:j + 1, :]
    act = acc * _sigmoid(acc)
    nqk = GDN_HEADS * GDN_DK
    for h in range(GDN_HEADS):
        sl = slice(h * GDN_DK, (h + 1) * GDN_DK)
        qh = act[:, sl]
        gq_ref[0, :, sl] = (qh * lax.rsqrt(_group_sumsq(qh, LANES) + NORM_EPS) * (GDN_DK ** -0.5)).astype(BF16)
        kh = act[:, nqk + h * GDN_DK: nqk + (h + 1) * GDN_DK]
        gk_ref[0, :, sl] = (kh * lax.rsqrt(_group_sumsq(kh, LANES) + NORM_EPS)).astype(BF16)
    gv_ref[0] = act[:, 2 * nqk:].astype(BF16)
    cos1, sin1 = cos1_ref[...], sin1_ref[...]
    lane = lax.broadcasted_iota(jnp.int32, (ts, LANES), 1)
    ones_col = jnp.where(lane == 0, 1.0, 0.0).astype(BF16)
    for p in range(DIFF_HEADS):
        sl = slice(p * LANES, (p + 1) * LANES)
        xq = dq_ref[0, :, sl].astype(F32)
        dqo_ref[0, :, sl] = ((xq * cos1 + _rot_half(xq, DIFF_DQK // 2) * sin1) * (DIFF_DQK ** -0.5 * LOG2E)).astype(BF16)
        xk = dk_ref[0, :, sl].astype(F32)
        xk = (xk * cos1 + _rot_half(xk, DIFF_DQK // 2) * sin1).astype(BF16)
        for m in range(2):
            dko_ref[0, :, (2 * p + m) * LANES:(2 * p + m + 1) * LANES] = _aug_slab(xk, m)
        dvo_ref[0, :, 2 * p * LANES:(2 * p + 1) * LANES] = dv_ref[0, :, sl]
        dvo_ref[0, :, (2 * p + 1) * LANES:(2 * p + 2) * LANES] = ones_col
    cos2, sin2 = cos2_ref[...], sin2_ref[...]
    for p in range(GQA_HEADS * GQA_DH // LANES):
        sl = slice(p * LANES, (p + 1) * LANES)
        xq = cq_ref[0, :, sl].astype(F32)
        xq = xq * lax.rsqrt(_group_sumsq(xq, GQA_DH) * (1.0 / GQA_DH) + NORM_EPS) * qnw_ref[...]
        cqo_ref[0, :, sl] = ((xq * cos2 + _rot_half(xq, GQA_DH // 4) * sin2) * (GQA_DH ** -0.5 * LOG2E)).astype(BF16)
    for p in range(2):
        sl = slice(p * LANES, (p + 1) * LANES)
        xk = ck_ref[0, :, sl].astype(F32)
        xk = xk * lax.rsqrt(_group_sumsq(xk, GQA_DH) * (1.0 / GQA_DH) + NORM_EPS) * knw_ref[...]
        xk = (xk * cos2 + _rot_half(xk, GQA_DH // 4) * sin2).astype(BF16)
        xv = cv_ref[0, :, sl]
        for m in range(2):
            c = p if m == 0 else 1 - p
            osl = slice((2 * c + m) * LANES, (2 * c + m + 1) * LANES)
            cko_ref[0, :, osl] = _aug_slab(xk, m)
            cvo_ref[0, :, osl] = _aug_slab(xv, m)


def _prep(main3, conv_w, cos1, sin1, cos2, sin2, qnw, knw, *, ts):
    b, s, _ = main3.shape
    nt = s // ts
    hb = ts // HALO
    last = s // HALO - 1
    row = lambda w: pl.BlockSpec((1, w), lambda bi, i: (0, 0))
    tab = pl.BlockSpec((ts, LANES), lambda bi, i: (i, 0))
    col = lambda w, off: pl.BlockSpec((1, ts, w), lambda bi, i: (bi, i, off // w))
    out = lambda w: pl.BlockSpec((1, ts, w), lambda bi, i: (bi, i, 0))
    widths = (512, 512, 512, 512, 1024, 1024, 512, 512, 512)
    return pl.pallas_call(
        functools.partial(_prep_kernel, ts=ts),
        out_shape=tuple(jax.ShapeDtypeStruct((b, s, w), BF16) for w in widths),
        grid=(b, nt),
        in_specs=[
            col(1536, COL_QKV),
            pl.BlockSpec((1, HALO, 1536), lambda bi, i: (bi, jnp.maximum(i * hb - 1, 0), COL_QKV // 1536)),
            pl.BlockSpec((1, HALO, 1536), lambda bi, i: (bi, jnp.minimum((i + 1) * hb, last), COL_QKV // 1536)),
            col(512, COL_DQ), col(512, COL_DK), col(512, COL_DV), col(512, COL_CQ), col(256, COL_CK), col(256, COL_CV),
            pl.BlockSpec((8, 1536), lambda bi, i: (0, 0)),
            tab, tab, tab, tab, row(LANES), row(LANES),
        ],
        out_specs=tuple(out(w) for w in widths),
        compiler_params=_cparams(("arbitrary", "arbitrary")),
        name="mixer_prep",
    )(*([main3] * 9), conv_w, cos1, sin1, cos2, sin2, qnw, knw)


GDN_G = 8
GDN_ROWS = GDN_HEADS * GDN_CHUNK


def _stack_heads(x):
    return jnp.concatenate([x[:, h * LANES:(h + 1) * LANES] for h in range(GDN_HEADS)], axis=0)


def _row_to_col(row, eye):
    return jnp.sum(jnp.where(eye, row, 0.0), axis=1, keepdims=True)


def _gdn_kernel(a_ref, b_ref, alog_ref, dtb_ref, q_ref, k_ref, v_ref, o_ref,
                state_ref, gc_ref, gt_ref, beta_ref):
    d = pl.program_id(0)
    blk = pl.program_id(2)
    sgn = 1 - 2 * d
    n = GDN_ROWS
    c = GDN_CHUNK

    @pl.when(blk == 0)
    def _():
        state_ref[...] = jnp.zeros_like(state_ref)

    ri = lax.broadcasted_iota(jnp.int32, (n, n), 0)
    ci = lax.broadcasted_iota(jnp.int32, (n, n), 1)
    same = (ri // c) == (ci // c)
    eye = ri == ci
    after = same & ((ri - ci) * sgn > 0)
    incl = same & ((ri - ci) * sgn >= 0)

    x = a_ref[0, 0] + dtb_ref[0]
    softplus = jnp.maximum(x, 0.0) + jnp.log(1.0 + jnp.exp(-jnp.abs(x)))
    g = -jnp.exp(alog_ref[0]) * softplus
    beta_ref[...] = _sigmoid(b_ref[0, 0])
    cum_m = jnp.where(same & ((ci - ri) * sgn >= 0), 1.0, 0.0)
    gc_ref[...] = jnp.dot(g, cum_m, precision=HI, preferred_element_type=F32)
    ti = lax.broadcasted_iota(jnp.int32, (n, GDN_HEADS * LANES), 0)
    tj = lax.broadcasted_iota(jnp.int32, (n, GDN_HEADS * LANES), 1)
    tot_m = jnp.where((ti // c) == (tj // LANES), 1.0, 0.0)
    gt_ref[...] = jnp.dot(g, tot_m, precision=HI, preferred_element_type=F32)

    def chunk(j, carry):
        cc = j + d * (GDN_G - 1 - 2 * j)
        r0 = pl.multiple_of(cc * c, c)
        gc_row = gc_ref[pl.ds(cc, 1), :]
        beta_row = beta_ref[pl.ds(cc, 1), :]
        gt_row = gt_ref[pl.ds(cc, 1), :]
        gc_col = _row_to_col(gc_row, eye)
        beta_col = _row_to_col(beta_row, eye)
        k_st = _stack_heads(k_ref[0, pl.ds(r0, c), :]).astype(F32)
        q_st = _stack_heads(q_ref[0, pl.ds(r0, c), :]).astype(F32)
        v_st = _stack_heads(v_ref[0, pl.ds(r0, c), :]).astype(F32)
        egc = jnp.exp(gc_col)
        decay = jnp.exp(jnp.minimum(gc_col - gc_row, 0.0))
        kb = k_st * beta_col
        k_bf = k_st.astype(BF16)
        kk = lax.dot_general(kb.astype(BF16), k_bf, (((1,), (1,)), ((), ())), preferred_element_type=F32)
        neg_a = jnp.where(after, -(kk * decay), 0.0)
        t_m = jnp.where(eye, 1.0, 0.0) + neg_a
        p_m = neg_a
        for _ in range(int(math.log2(c)) - 1):
            p_bf = p_m.astype(BF16)
            p_m = jnp.dot(p_bf, p_bf, preferred_element_type=F32)
            t_m = t_m + jnp.dot(t_m.astype(BF16), p_m.astype(BF16), preferred_element_type=F32)
        rhs = jnp.concatenate([v_st * beta_col, kb * egc], axis=1).astype(BF16)
        sol = jnp.dot(t_m.astype(BF16), rhs, preferred_element_type=F32)
        u_st, w_st = sol[:, :LANES], sol[:, LANES:]
        qk = lax.dot_general(q_st.astype(BF16), k_bf, (((1,), (1,)), ((), ())), preferred_element_type=F32)
        intra = jnp.where(incl, qk * decay, 0.0).astype(BF16)
        q_dec = (q_st * egc).astype(BF16)
        vn, oq = [], []
        for h in range(GDN_HEADS):
            rs = slice(h * c, (h + 1) * c)
            s_h = state_ref[h].astype(BF16)
            vn.append(u_st[rs] - jnp.dot(w_st[rs].astype(BF16), s_h, preferred_element_type=F32))
            oq.append(jnp.dot(q_dec[rs], s_h, preferred_element_type=F32))
        vn_st = jnp.concatenate(vn, axis=0)
        o_st = jnp.concatenate(oq, axis=0) + jnp.dot(intra, vn_st.astype(BF16), preferred_element_type=F32)
        for h in range(GDN_HEADS):
            rs = slice(h * c, (h + 1) * c)
            gt_h = gt_row[:, h * LANES:(h + 1) * LANES]
            k_dec = (k_st[rs] * jnp.exp(gt_h[:, :1] - gc_col[rs])).astype(BF16)
            upd = lax.dot_general(k_dec, vn[h].astype(BF16), (((0,), (0,)), ((), ())), preferred_element_type=F32)
            state_ref[h] = state_ref[h] * jnp.exp(gt_h) + upd
            o_ref[0, 0, pl.ds(r0, c), h * LANES:(h + 1) * LANES] = o_st[rs]
        return carry

    lax.fori_loop(0, GDN_G, chunk, 0)


def _gdn(a_rows, b_rows, alog_row, dtb_row, gq, gk, gv):
    b, s, _ = gq.shape
    nb = s // (GDN_G * GDN_CHUNK)
    ts = GDN_G * GDN_CHUNK

    def blk_of(d, i):
        return i + d * (nb - 1 - 2 * i)

    tok = pl.BlockSpec((1, ts, 512), lambda d, bi, i: (bi, blk_of(d, i), 0))
    rows = pl.BlockSpec((1, 1, GDN_G, GDN_ROWS), lambda d, bi, i: (d, bi, blk_of(d, i), 0))
    par = pl.BlockSpec((1, 1, GDN_ROWS), lambda d, bi, i: (d, 0, 0))
    return pl.pallas_call(
        _gdn_kernel,
        out_shape=jax.ShapeDtypeStruct((2, b, s, 512), F32),
        grid=(2, b, nb),
        in_specs=[rows, rows, par, par, tok, tok, tok],
        out_specs=pl.BlockSpec((1, 1, ts, 512), lambda d, bi, i: (d, bi, blk_of(d, i), 0)),
        scratch_shapes=[pltpu.VMEM((GDN_HEADS, GDN_DK, GDN_DV), F32),
                        pltpu.VMEM((GDN_G, GDN_ROWS), F32),
                        pltpu.VMEM((GDN_G, GDN_HEADS * LANES), F32),
                        pltpu.VMEM((GDN_G, GDN_ROWS), F32)],
        compiler_params=_cparams(("arbitrary", "arbitrary", "arbitrary")),
        name="gdn_chunked",
    )(a_rows, b_rows, alog_row, dtb_row, gq, gk, gv)


def _attn_kernel(q_ref, k0_ref, k1_ref, v0_ref, v1_ref, lam_ref, nw_ref, o_ref, acc_ref, *, mode, tk, lambda_init):
    s_len = k0_ref.shape[1]
    tq = q_ref.shape[1]
    nv = v0_ref.shape[2]
    half = LANES // 2
    q = q_ref[0]
    lane = lax.broadcasted_iota(jnp.int32, q.shape, 1)
    lo = lane < half
    keep = (lo, lane >= half)
    stab = (lane == half, lane == 0)
    zero = jnp.zeros_like(q)
    krefs = (k0_ref, k1_ref)
    vrefs = (v0_ref, v1_ref)
    nchunks = s_len // tk
    dn = (((1,), (1,)), ((), ()))

    def kchunk(m, ci):
        return krefs[m][0, pl.ds(pl.multiple_of(ci * tk, tk), tk), :]

    def vchunk(m, ci):
        return vrefs[m][0, pl.ds(pl.multiple_of(ci * tk, tk), tk), :]

    qm, qa = [], []
    for m in range(2):
        qm.append(jnp.where(keep[m], q, zero))
        mx = jnp.max(lax.dot_general(qm[m], kchunk(m, 0), dn, preferred_element_type=F32), axis=-1, keepdims=True)
        qa.append(jnp.where(stab[m], (-mx).astype(BF16), qm[m]))

    def fast(ci, acc):
        out = []
        for m in range(2):
            sc = lax.dot_general(qa[m], kchunk(m, ci), dn, preferred_element_type=F32)
            out.append(acc[m] + jnp.dot(jnp.exp2(sc).astype(BF16), vchunk(m, ci), preferred_element_type=F32))
        return tuple(out)

    acc = lax.fori_loop(0, nchunks, fast, tuple(jnp.zeros((tq, nv), F32) for _ in range(2)))
    nonfinite = jnp.float32(0.0)
    for m in range(2):
        acc_ref[m] = acc[m]
        nonfinite = nonfinite + jnp.sum(jnp.where(jnp.isfinite(acc[m]), 0.0, 1.0))

    @pl.when(nonfinite > 0.0)
    def _():
        def slow(ci, carry):
            out = []
            for m in range(2):
                m_i, a_i = carry[m]
                sc = lax.dot_general(qm[m], kchunk(m, ci), dn, preferred_element_type=F32)
                m_new = jnp.maximum(m_i, jnp.max(sc, axis=-1, keepdims=True))
                p = jnp.exp2(sc - m_new).astype(BF16)
                out.append((m_new, jnp.exp2(m_i - m_new) * a_i + jnp.dot(p, vchunk(m, ci), preferred_element_type=F32)))
            return tuple(out)

        init = tuple((jnp.full((tq, 1), -jnp.inf, F32), jnp.zeros((tq, nv), F32)) for _ in range(2))
        res = lax.fori_loop(0, nchunks, slow, init)
        for m in range(2):
            acc_ref[m] = res[m][1]

    if mode == "diff":
        o0 = acc_ref[0, :, :LANES] / acc_ref[0, :, LANES:LANES + 1]
        o1 = acc_ref[1, :, :LANES] / acc_ref[1, :, LANES:LANES + 1]
    else:
        o0 = acc_ref[0] / acc_ref[0, :, half:half + 1]
        o1 = acc_ref[1] / acc_ref[1, :, 0:1]
    if mode == "diff":
        lv = lam_ref[...]
        lam = (jnp.exp(jnp.sum(lv[0:1] * lv[1:2], axis=-1, keepdims=True))
               - jnp.exp(jnp.sum(lv[2:3] * lv[3:4], axis=-1, keepdims=True)) + lambda_init)
        o = o0 - lam * o1
        o = o * lax.rsqrt(jnp.mean(o * o, axis=-1, keepdims=True) + NORM_EPS) * nw_ref[...] * (1.0 - lambda_init)
    else:
        o = jnp.where(lo, o0, o1)
    o_ref[0] = o.astype(o_ref.dtype)


def _attention(q, k_arr, v_arr, lam_vecs, norm_w, *, mode, tq, tk, lambda_init=0.0):
    b, s, w = q.shape
    slabs = w // LANES
    if mode == "diff":
        nv = 2 * LANES
        k_col = lambda p, m: 2 * p + m
        v_col = lambda p, m: p
    else:
        nv = LANES
        k_col = v_col = lambda p, m: 2 * (p // 2) + m
    kspec = lambda m: pl.BlockSpec((1, s, LANES), lambda bi, p, i: (bi, 0, k_col(p, m)))
    vspec = lambda m: pl.BlockSpec((1, s, nv), lambda bi, p, i: (bi, 0, v_col(p, m)))
    return pl.pallas_call(
        functools.partial(_attn_kernel, mode=mode, tk=tk, lambda_init=lambda_init),
        out_shape=jax.ShapeDtypeStruct((b, s, w), BF16),
        grid=(b, slabs, s // tq),
        in_specs=[pl.BlockSpec((1, tq, LANES), lambda bi, p, i: (bi, i, p)),
                  kspec(0), kspec(1), vspec(0), vspec(1),
                  pl.BlockSpec((4, DIFF_DQK), lambda bi, p, i: (0, 0)),
                  pl.BlockSpec((1, LANES), lambda bi, p, i: (0, 0))],
        out_specs=pl.BlockSpec((1, tq, LANES), lambda bi, p, i: (bi, i, p)),
        scratch_shapes=[pltpu.VMEM((2, tq, nv), F32)],
        compiler_params=_cparams(("arbitrary", "arbitrary", "arbitrary")),
        name="attn_" + mode,
    )(q, k_arr, k_arr, v_arr, v_arr, lam_vecs, norm_w)


def _merge_kernel(og_ref, z_ref, g0_ref, g1_ref, g2_ref, yb_ref, yc_ref, x_ref,
                  wa_ref, wb_ref, wc_ref, wo_ref, gnw_ref, fnw_ref, rw_ref, rb_ref,
                  xo_ref, h_ref, id_ref, rwgt_ref):
    o = og_ref[0] + og_ref[1]
    parts = []
    for h in range(GDN_HEADS):
        oh = o[:, h * LANES:(h + 1) * LANES]
        parts.append(oh * lax.rsqrt(jnp.mean(oh * oh, axis=-1, keepdims=True) + NORM_EPS) * gnw_ref[...])
    z = z_ref[...].astype(F32)
    ya = (jnp.concatenate(parts, axis=1) * (z * _sigmoid(z))).astype(BF16)
    merged = _sigmoid(g0_ref[...].astype(F32)) * jnp.dot(ya, wa_ref[...], preferred_element_type=F32)
    merged = merged + _sigmoid(g1_ref[...].astype(F32)) * jnp.dot(yb_ref[...], wb_ref[...], preferred_element_type=F32)
    merged = merged + _sigmoid(g2_ref[...].astype(F32)) * jnp.dot(yc_ref[...], wc_ref[...], preferred_element_type=F32)
    xn = x_ref[...] + jnp.dot(merged.astype(BF16), wo_ref[...], preferred_element_type=F32)
    xo_ref[...] = xn
    hf = xn * lax.rsqrt(jnp.mean(xn * xn, axis=-1, keepdims=True) + NORM_EPS) * fnw_ref[...]
    h_ref[...] = hf.astype(BF16)
    logits = jnp.dot(hf, rw_ref[...], precision=HI, preferred_element_type=F32) + rb_ref[...]
    lane = lax.broadcasted_iota(jnp.int32, logits.shape, 1)
    big = jnp.int32(LANES)
    ninf = -jnp.inf
    glog = jnp.where(lane < N_GROUPS, logits, ninf)
    gmax = jnp.max(glog, axis=-1, keepdims=True)
    gidx = jnp.min(jnp.where(glog == gmax, lane, big), axis=-1, keepdims=True)
    gp = 1.0 / jnp.sum(jnp.exp(glog - gmax), axis=-1, keepdims=True)
    e = lane - N_GROUPS
    sel = (e >= 0) & (e < N_EXPERTS) & ((e // EXPERTS_PER_GROUP) == gidx)
    elog = jnp.where(sel, logits, ninf)
    m1 = jnp.max(elog, axis=-1, keepdims=True)
    i1 = jnp.min(jnp.where(elog == m1, lane, big), axis=-1, keepdims=True)
    elog2 = jnp.where(lane == i1, ninf, elog)
    m2 = jnp.max(elog2, axis=-1, keepdims=True)
    i2 = jnp.min(jnp.where(elog2 == m2, lane, big), axis=-1, keepdims=True)
    e2 = jnp.exp(m2 - m1)
    w1 = 1.0 / (1.0 + e2)
    w2 = e2 * w1
    id_ref[...] = jnp.where(lane == 0, i1 - N_GROUPS, jnp.where(lane == 1, i2 - N_GROUPS, 0))
    rwgt_ref[...] = jnp.where(lane == 0, gp * w1, jnp.where(lane == 1, gp * w2, 0.0))


def _merge(og, main2, yb, yc, x2d, wa, wb, wc, wo, gnw, fnw, rw, rb, *, tm):
    t, d = x2d.shape
    full = lambda shp: pl.BlockSpec(shp, lambda i: tuple(0 for _ in shp))
    return pl.pallas_call(
        _merge_kernel,
        out_shape=(jax.ShapeDtypeStruct((t, d), F32), jax.ShapeDtypeStruct((t, d), BF16),
                   jax.ShapeDtypeStruct((t, LANES), jnp.int32), jax.ShapeDtypeStruct((t, LANES), F32)),
        grid=(t // tm,),
        in_specs=[pl.BlockSpec((2, tm, 512), lambda i: (0, i, 0)),
                  pl.BlockSpec((tm, 512), lambda i: (i, COL_Z // 512)),
                  pl.BlockSpec((tm, d), lambda i: (i, 0)),
                  pl.BlockSpec((tm, d), lambda i: (i, 1)),
                  pl.BlockSpec((tm, d), lambda i: (i, 2)),
                  pl.BlockSpec((tm, 512), lambda i: (i, 0)),
                  pl.BlockSpec((tm, 512), lambda i: (i, 0)),
                  pl.BlockSpec((tm, d), lambda i: (i, 0)),
                  full((512, d)), full((512, d)), full((512, d)), full((d, d)),
                  full((1, LANES)), full((1, d)), full((d, LANES)), full((1, LANES))],
        out_specs=(pl.BlockSpec((tm, d), lambda i: (i, 0)), pl.BlockSpec((tm, d), lambda i: (i, 0)),
                   pl.BlockSpec((tm, LANES), lambda i: (i, 0)), pl.BlockSpec((tm, LANES), lambda i: (i, 0))),
        compiler_params=_cparams(("arbitrary",)),
        name="merge_router",
    )(og, main2, main2, main2, main2, yb, yc, x2d, wa, wb, wc, wo, gnw, fnw, rw, rb)


def _expert_kernel(blk_e_ref, nused_ref, x_ref, w1_ref, w3_ref, w2_ref, o_ref):
    i = pl.program_id(0)

    @pl.when(i < nused_ref[0])
    def _():
        x = x_ref[...]
        a = jnp.dot(x, w1_ref[0], preferred_element_type=F32)
        u = jnp.dot(x, w3_ref[0], preferred_element_type=F32)
        hmid = (a * _sigmoid(a) * u).astype(BF16)
        o_ref[...] = jnp.dot(hmid, w2_ref[0], preferred_element_type=F32).astype(o_ref.dtype)

    @pl.when(i >= nused_ref[0])
    def _():
        o_ref[...] = jnp.zeros_like(o_ref)


def _experts(blk_e, nused, xb, w1, w3, w2):
    p_len, d = xb.shape
    ff = w1.shape[2]
    nblk = p_len // MOE_BLOCK
    return pl.pallas_call(
        _expert_kernel,
        out_shape=jax.ShapeDtypeStruct((p_len, d), BF16),
        grid_spec=pltpu.PrefetchScalarGridSpec(
            num_scalar_prefetch=2,
            grid=(nblk,),
            in_specs=[pl.BlockSpec((MOE_BLOCK, d), lambda i, be, nu: (i, 0)),
                      pl.BlockSpec((1, d, ff), lambda i, be, nu: (be[i], 0, 0)),
                      pl.BlockSpec((1, d, ff), lambda i, be, nu: (be[i], 0, 0)),
                      pl.BlockSpec((1, ff, d), lambda i, be, nu: (be[i], 0, 0))],
            out_specs=pl.BlockSpec((MOE_BLOCK, d), lambda i, be, nu: (i, 0)),
        ),
        compiler_params=_cparams(("arbitrary",)),
        name="expert_mlp",
    )(blk_e, nused, xb, w1, w3, w2)


def _combine_kernel(x_ref, y0_ref, y1_ref, w_ref, nw_ref, o_ref, *, final):
    w = w_ref[...]
    x = x_ref[...] + w[:, 0:1] * y0_ref[...].astype(F32) + w[:, 1:2] * y1_ref[...].astype(F32)
    if final:
        x = x * lax.rsqrt(jnp.mean(x * x, axis=-1, keepdims=True) + NORM_EPS) * nw_ref[...]
    o_ref[...] = x


def _combine(x2d, y0, y1, wts, norm_w, *, final, tm):
    t, d = x2d.shape
    tile = pl.BlockSpec((tm, d), lambda i: (i, 0))
    return pl.pallas_call(
        functools.partial(_combine_kernel, final=final),
        out_shape=jax.ShapeDtypeStruct((t, d), F32),
        grid=(t // tm,),
        in_specs=[tile, tile, tile, pl.BlockSpec((tm, LANES), lambda i: (i, 0)), pl.BlockSpec((1, d), lambda i: (0, 0))],
        out_specs=tile,
        compiler_params=_cparams(("arbitrary",)),
        name="moe_combine",
    )(x2d, y0, y1, wts, norm_w.reshape(1, d))


def _rope_tables(pos, dim):
    inv = 1.0 / (ROPE_THETA ** (jnp.arange(0, dim, 2, dtype=F32) / dim))
    ang = pos.astype(F32)[:, None] * inv[None, :]
    ang = jnp.concatenate([ang, ang], axis=-1)
    return jnp.cos(ang), jnp.sin(ang)


def _signed_sin(sin):
    half = sin.shape[-1] // 2
    return jnp.concatenate([-sin[:, :half], sin[:, half:]], axis=-1)


def _layout_w_in(w):
    o = 0
    parts = {}
    for name, size in (("qkv", 1536), ("z", 512), ("b", 8), ("a", 8), ("dq", 512), ("dk", 512), ("dv", 512),
                       ("cq", 512), ("ck", 128), ("cv", 128), ("gate", 3072)):
        parts[name] = w[:, o:o + size]
        o += size
    swap = lambda m: jnp.concatenate([m[:, 64:], m[:, :64]], axis=1)
    main = jnp.concatenate([parts["gate"], parts["qkv"], parts["z"], parts["dq"], parts["dk"], parts["dv"],
                            parts["cq"], parts["ck"], swap(parts["ck"]), parts["cv"], swap(parts["cv"])], axis=1)
    ba = jnp.concatenate([parts["b"], parts["a"], jnp.zeros((w.shape[0], LANES - 16), w.dtype)], axis=1)
    return main.astype(BF16), ba


def _rows_layout(t, bsz, s):
    nc = s // GDN_CHUNK
    t = t.reshape(bsz, nc, GDN_CHUNK, 2, GDN_HEADS)
    return jnp.transpose(t, (3, 0, 1, 4, 2)).reshape(2, bsz, nc, GDN_ROWS)


def _moe_dispatch(ids, t):
    a = t * TOPK
    p_len = ((a + N_EXPERTS * (MOE_BLOCK - 1) + MOE_BLOCK - 1) // MOE_BLOCK) * MOE_BLOCK
    n_blocks = p_len // MOE_BLOCK
    flat_e = ids.reshape(-1)
    iota_a = jnp.arange(a, dtype=jnp.int32)
    skey = jnp.sort(flat_e * a + iota_a)
    order = skey % a
    se = skey // a
    experts = jnp.arange(N_EXPERTS, dtype=jnp.int32)
    counts = jnp.sum((flat_e[:, None] == experts[None, :]).astype(jnp.int32), axis=0)
    start = jnp.cumsum(counts) - counts
    pcounts = ((counts + MOE_BLOCK - 1) // MOE_BLOCK) * MOE_BLOCK
    pend = jnp.cumsum(pcounts)
    pstart = pend - pcounts
    dest_sorted = pstart[se] + (iota_a - start[se])
    blk_first = jnp.arange(n_blocks, dtype=jnp.int32) * MOE_BLOCK
    blk_e = jnp.minimum(jnp.sum((pend[None, :] <= blk_first[:, None]).astype(jnp.int32), axis=1), N_EXPERTS - 1)
    row = jnp.arange(p_len, dtype=jnp.int32)
    row_e = jnp.repeat(blk_e, MOE_BLOCK)
    j = row - pstart[row_e]
    valid = j < counts[row_e]
    tok_buf = jnp.where(valid, order[jnp.minimum(start[row_e] + j, a - 1)] // TOPK, t)
    _, dest = lax.sort((order, dest_sorted), num_keys=1)
    nused = (pend[-1] // MOE_BLOCK).astype(jnp.int32).reshape(1)
    return tok_buf, dest.reshape(t, TOPK), blk_e, nused


def kernel(x, attn_norm_w, w_in, gdn_conv_w, gdn_a_log, gdn_dt_bias, gdn_norm_w, diff_lambda, diff_norm_w,
           gqa_q_norm_w, gqa_k_norm_w, w_branch_a, w_branch_b, w_branch_c, w_out, ffn_norm_w,
           router_group_w, router_group_b, router_expert_w, router_expert_b,
           expert_w_gate, expert_w_up, expert_w_down, final_norm_w):
    bsz, s, d = x.shape
    t = bsz * s
    depth = w_in.shape[0]
    tm = min(512, t)
    ts = min(512, s)

    rows = s // GRID_W
    row = jnp.broadcast_to(jnp.arange(rows)[:, None], (rows, GRID_W)).reshape(s)
    col = jnp.broadcast_to(jnp.arange(GRID_W)[None, :], (rows, GRID_W)).reshape(s)
    c1, s1 = _rope_tables(jnp.arange(s), DIFF_DQK)
    cr, sr = _rope_tables(row, GQA_DH // 2)
    cc, sc = _rope_tables(col, GQA_DH // 2)
    cos1 = jnp.tile(c1, (1, 2))
    sin1 = jnp.tile(_signed_sin(s1), (1, 2))
    cos2 = jnp.tile(jnp.concatenate([cr, cc], axis=-1), (1, 2))
    sin2 = jnp.tile(jnp.concatenate([_signed_sin(sr), _signed_sin(sc)], axis=-1), (1, 2))

    x2 = x.reshape(t, d)
    for l in range(depth):
        lambda_init = 0.8 - 0.6 * math.exp(-0.3 * l)
        w_main, w_ba = _layout_w_in(w_in[l])
        main2 = _norm_proj(x2, attn_norm_w[l], w_main, BF16, exact=False, tm=tm, tn=1536)
        ba = _norm_proj(x2, attn_norm_w[l], w_ba, F32, exact=True, tm=tm, tn=LANES)
        main3 = main2.reshape(bsz, s, N_MAIN)

        conv_w = jnp.concatenate([gdn_conv_w[l], jnp.zeros((8 - GDN_CONV, gdn_conv_w.shape[2]), F32)], axis=0)
        qnw = jnp.tile(gqa_q_norm_w[l], 2).reshape(1, LANES)
        knw = jnp.tile(gqa_k_norm_w[l], 2).reshape(1, LANES)
        gq, gk, gv, dq, dk, dv, cq, ck, cv = _prep(main3, conv_w, cos1, sin1, cos2, sin2, qnw, knw, ts=ts)

        b_rows = _rows_layout(ba[:, 0:8], bsz, s)
        a_rows = _rows_layout(ba[:, 8:16], bsz, s)
        alog_row = jnp.repeat(gdn_a_log[l], GDN_CHUNK, axis=1).reshape(2, 1, GDN_ROWS)
        dtb_row = jnp.repeat(gdn_dt_bias[l], GDN_CHUNK, axis=1).reshape(2, 1, GDN_ROWS)
        og = _gdn(a_rows, b_rows, alog_row, dtb_row, gq, gk, gv)

        nw_diff = diff_norm_w[l].reshape(1, LANES)
        yb = _attention(dq, dk, dv, diff_lambda[l], nw_diff, mode="diff",
                        tq=min(ATTN_TQ, s), tk=min(ATTN_TK, s), lambda_init=lambda_init)
        yc = _attention(cq, ck, cv, diff_lambda[l], nw_diff, mode="gqa",
                        tq=min(ATTN_TQ, s), tk=min(ATTN_TK, s))

        rw = jnp.concatenate([router_group_w[l], router_expert_w[l],
                              jnp.zeros((d, LANES - N_GROUPS - N_EXPERTS), F32)], axis=1)
        rb = jnp.concatenate([router_group_b[l], router_expert_b[l],
                              jnp.zeros((LANES - N_GROUPS - N_EXPERTS,), F32)]).reshape(1, LANES)
        x2, h2, ids, wts = _merge(og.reshape(2, t, 512), main2, yb.reshape(t, 512), yc.reshape(t, 512), x2,
                                  w_branch_a[l].astype(BF16), w_branch_b[l].astype(BF16),
                                  w_branch_c[l].astype(BF16), w_out[l].astype(BF16),
                                  gdn_norm_w[l].reshape(1, LANES), ffn_norm_w[l].reshape(1, d), rw, rb, tm=min(256, t))

        tok_buf, dest, blk_e, nused = _moe_dispatch(ids[:, :TOPK], t)
        h_pad = jnp.concatenate([h2, jnp.zeros((1, d), BF16)], axis=0)
        yblk = _experts(blk_e, nused, h_pad[tok_buf], expert_w_gate[l].astype(BF16),
                        expert_w_up[l].astype(BF16), expert_w_down[l].astype(BF16))
        x2 = _combine(x2, yblk[dest[:, 0]], yblk[dest[:, 1]], wts, final_norm_w, final=(l == depth - 1), tm=tm)

    return x2.reshape(bsz, s, d)
```

```python
import functools
import math

import jax
import jax.numpy as jnp
from jax import lax
from jax.experimental import pallas as pl
from jax.experimental.pallas import tpu as pltpu

GRID_W = 64
ROPE_THETA = 10000.0
NORM_EPS = 1e-6
GDN_HEADS = 4
GDN_DK = 128
GDN_DV = 128
GDN_CONV = 5
GDN_CHUNK = 64
DIFF_HEADS = 4
DIFF_DQK = 64
GQA_HEADS = 8
GQA_KV = 2
GQA_DH = 64
N_GROUPS = 4
EXPERTS_PER_GROUP = 8
N_EXPERTS = N_GROUPS * EXPERTS_PER_GROUP
TOPK = 2
MOE_BLOCK = 256

LANES = 128
VMEM_LIMIT = 56 * 1024 * 1024

COL_GATE = 0
COL_QKV = 3072
COL_Z = 4608
COL_DQ = 5120
COL_DK = 5632
COL_DV = 6144
COL_CQ = 6656
COL_CK = 7168
COL_CV = 7424
N_MAIN = 7680

LOG2E = math.log2(math.e)
ATTN_TQ = 512
ATTN_TK = 512

HI = lax.Precision.HIGHEST
F32 = jnp.float32
BF16 = jnp.bfloat16


def _cparams(sem):
    return pltpu.CompilerParams(dimension_semantics=sem, vmem_limit_bytes=VMEM_LIMIT)


def _sigmoid(x):
    return 1.0 / (1.0 + jnp.exp(-x))


def _norm_proj_kernel(x_ref, nw_ref, w_ref, o_ref, *, exact):
    x = x_ref[...]
    h = x * lax.rsqrt(jnp.mean(x * x, axis=-1, keepdims=True) + NORM_EPS) * nw_ref[...]
    if exact:
        o_ref[...] = jnp.dot(h, w_ref[...], precision=HI, preferred_element_type=F32).astype(o_ref.dtype)
    else:
        o_ref[...] = jnp.dot(h.astype(BF16), w_ref[...], preferred_element_type=F32).astype(o_ref.dtype)


def _norm_proj(x2d, norm_w, w, out_dtype, *, exact, tm, tn):
    t, d = x2d.shape
    n = w.shape[1]
    return pl.pallas_call(
        functools.partial(_norm_proj_kernel, exact=exact),
        out_shape=jax.ShapeDtypeStruct((t, n), out_dtype),
        grid=(n // tn, t // tm),
        in_specs=[pl.BlockSpec((tm, d), lambda j, i: (i, 0)),
                  pl.BlockSpec((1, d), lambda j, i: (0, 0)),
                  pl.BlockSpec((d, tn), lambda j, i: (0, j))],
        out_specs=pl.BlockSpec((tm, tn), lambda j, i: (i, j)),
        compiler_params=_cparams(("arbitrary", "arbitrary")),
        name="norm_proj_exact" if exact else "norm_proj",
    )(x2d, norm_w.reshape(1, d), w)


HALO = 16


def _rot_half(x, half):
    lane = lax.broadcasted_iota(jnp.int32, x.shape, 1)
    first = (lane % (2 * half)) < half
    return jnp.where(first, pltpu.roll(x, LANES - half, 1), pltpu.roll(x, half, 1))


def _group_sumsq(x, width):
    x2 = x * x
    if width == LANES:
        return jnp.sum(x2, axis=-1, keepdims=True)
    lane = lax.broadcasted_iota(jnp.int32, x.shape, 1)
    lo = lane < width
    s_lo = jnp.sum(jnp.where(lo, x2, 0.0), axis=-1, keepdims=True)
    s_hi = jnp.sum(jnp.where(lo, 0.0, x2), axis=-1, keepdims=True)
    return jnp.where(lo, s_lo, s_hi)


def _aug_slab(x, m):
    lane = lax.broadcasted_iota(jnp.int32, x.shape, 1)
    half = LANES // 2
    keep = (lane < half) if m == 0 else (lane >= half)
    one = jnp.where(lane == (1 - m) * half, 1.0, 0.0).astype(x.dtype)
    return jnp.where(keep, x, one)


def _prep_kernel(qkv_ref, prev_ref, next_ref, dq_ref, dk_ref, dv_ref, cq_ref, ck_ref, cv_ref,
                 convw_ref, cos1_ref, sin1_ref, cos2_ref, sin2_ref, qnw_ref, knw_ref,
                 gq_ref, gk_ref, gv_ref, dqo_ref, dko_ref, dvo_ref, cqo_ref, cko_ref, cvo_ref, *, ts):
    i = pl.program_id(1)
    n = pl.num_programs(1)
    cur = qkv_ref[0].astype(F32)
    prev = jnp.where(i > 0, prev_ref[0].astype(F32), 0.0)
    nxt = jnp.where(i < n - 1, next_ref[0].astype(F32), 0.0)
    ext = jnp.concatenate([prev, cur, nxt], axis=0)
    pad = GDN_CONV // 2
    acc = jnp.zeros_like(cur)
    for j in range(GDN_CONV):
        off = HALO - pad + j
        acc = acc + ext[off:off + ts, :] * convw_ref[j:j + 1, :]
    act = acc * _sigmoid(acc)
    nqk = GDN_HEADS * GDN_DK
    for h in range(GDN_HEADS):
        sl = slice(h * GDN_DK, (h + 1) * GDN_DK)
        qh = act[:, sl]
        gq_ref[0, :, sl] = (qh * lax.rsqrt(_group_sumsq(qh, LANES) + NORM_EPS) * (GDN_DK ** -0.5)).astype(BF16)
        kh = act[:, nqk + h * GDN_DK: nqk + (h + 1) * GDN_DK]
        gk_ref[0, :, sl] = (kh * lax.rsqrt(_group_sumsq(kh, LANES) + NORM_EPS)).astype(BF16)
    gv_ref[0] = act[:, 2 * nqk:].astype(BF16)
    cos1, sin1 = cos1_ref[...], sin1_ref[...]
    lane = lax.broadcasted_iota(jnp.int32, (ts, LANES), 1)
    ones_col = jnp.where(lane == 0, 1.0, 0.0).astype(BF16)
    for p in range(DIFF_HEADS):
        sl = slice(p * LANES, (p + 1) * LANES)
        xq = dq_ref[0, :, sl].astype(F32)
        dqo_ref[0, :, sl] = ((xq * cos1 + _rot_half(xq, DIFF_DQK // 2) * sin1) * (DIFF_DQK ** -0.5 * LOG2E)).astype(BF16)
        xk = dk_ref[0, :, sl].astype(F32)
        xk = (xk * cos1 + _rot_half(xk, DIFF_DQK // 2) * sin1).astype(BF16)
        for m in range(2):
            dko_ref[0, :, (2 * p + m) * LANES:(2 * p + m + 1) * LANES] = _aug_slab(xk, m)
        dvo_ref[0, :, 2 * p * LANES:(2 * p + 1) * LANES] = dv_ref[0, :, sl]
        dvo_ref[0, :, (2 * p + 1) * LANES:(2 * p + 2) * LANES] = ones_col
    cos2, sin2 = cos2_ref[...], sin2_ref[...]
    for p in range(GQA_HEADS * GQA_DH // LANES):
        sl = slice(p * LANES, (p + 1) * LANES)
        xq = cq_ref[0, :, sl].astype(F32)
        xq = xq * lax.rsqrt(_group_sumsq(xq, GQA_DH) * (1.0 / GQA_DH) + NORM_EPS) * qnw_ref[...]
        cqo_ref[0, :, sl] = ((xq * cos2 + _rot_half(xq, GQA_DH // 4) * sin2) * (GQA_DH ** -0.5 * LOG2E)).astype(BF16)
    for p in range(2):
        sl = slice(p * LANES, (p + 1) * LANES)
        xk = ck_ref[0, :, sl].astype(F32)
        xk = xk * lax.rsqrt(_group_sumsq(xk, GQA_DH) * (1.0 / GQA_DH) + NORM_EPS) * knw_ref[...]
        xk = (xk * cos2 + _rot_half(xk, GQA_DH // 4) * sin2).astype(BF16)
        xv = cv_ref[0, :, sl]
        for m in range(2):
            c = p if m == 0 else 1 - p
            osl = slice((2 * c + m) * LANES, (2 * c + m + 1) * LANES)
            cko_ref[0, :, osl] = _aug_slab(xk, m)
            cvo_ref[0, :, osl] = _aug_slab(xv, m)


def _prep(main3, conv_w, cos1, sin1, cos2, sin2, qnw, knw, *, ts):
    b, s, _ = main3.shape
    nt = s // ts
    hb = ts // HALO
    last = s // HALO - 1
    row = lambda w: pl.BlockSpec((1, w), lambda bi, i: (0, 0))
    tab = pl.BlockSpec((ts, LANES), lambda bi, i: (i, 0))
    col = lambda w, off: pl.BlockSpec((1, ts, w), lambda bi, i: (bi, i, off // w))
    out = lambda w: pl.BlockSpec((1, ts, w), lambda bi, i: (bi, i, 0))
    widths = (512, 512, 512, 512, 1024, 1024, 512, 512, 512)
    return pl.pallas_call(
        functools.partial(_prep_kernel, ts=ts),
        out_shape=tuple(jax.ShapeDtypeStruct((b, s, w), BF16) for w in widths),
        grid=(b, nt),
        in_specs=[
            col(1536, COL_QKV),
            pl.BlockSpec((1, HALO, 1536), lambda bi, i: (bi, jnp.maximum(i * hb - 1, 0), COL_QKV // 1536)),
            pl.BlockSpec((1, HALO, 1536), lambda bi, i: (bi, jnp.minimum((i + 1) * hb, last), COL_QKV // 1536)),
            col(512, COL_DQ), col(512, COL_DK), col(512, COL_DV), col(512, COL_CQ), col(256, COL_CK), col(256, COL_CV),
            pl.BlockSpec((8, 1536), lambda bi, i: (0, 0)),
            tab, tab, tab, tab, row(LANES), row(LANES),
        ],
        out_specs=tuple(out(w) for w in widths),
        compiler_params=_cparams(("arbitrary", "arbitrary")),
        name="mixer_prep",
    )(*([main3] * 9), conv_w, cos1, sin1, cos2, sin2, qnw, knw)


GDN_G = 8
GDN_NBATCH = 1
GDN_ROWS = GDN_HEADS * GDN_CHUNK


def _stack_heads(x):
    return jnp.concatenate([x[:, h * LANES:(h + 1) * LANES] for h in range(GDN_HEADS)], axis=0)


def _row_to_col(row, eye):
    return jnp.sum(jnp.where(eye, row, 0.0), axis=1, keepdims=True)


def _gdn_kernel(af_ref, ab_ref, bf_ref, bb_ref, alog_ref, dtb_ref, qf_ref, kf_ref, vf_ref, qb_ref, kb_ref, vb_ref,
                of_ref, ob_ref, state_ref, gc_ref, gt_ref, beta_ref, *, nbatch):
    blk = pl.program_id(1)
    n = GDN_ROWS
    c = GDN_CHUNK

    @pl.when(blk == 0)
    def _():
        state_ref[...] = jnp.zeros_like(state_ref)

    ri = lax.broadcasted_iota(jnp.int32, (n, n), 0)
    ci = lax.broadcasted_iota(jnp.int32, (n, n), 1)
    same = (ri // c) == (ci // c)
    eye = ri == ci
    ti = lax.broadcasted_iota(jnp.int32, (n, GDN_HEADS * LANES), 0)
    tj = lax.broadcasted_iota(jnp.int32, (n, GDN_HEADS * LANES), 1)
    tot_m = jnp.where((ti // c) == (tj // LANES), 1.0, 0.0)

    chains = []
    for d, (a_ref, b_ref, q_ref, k_ref, v_ref, o_ref) in enumerate(
            ((af_ref, bf_ref, qf_ref, kf_ref, vf_ref, of_ref), (ab_ref, bb_ref, qb_ref, kb_ref, vb_ref, ob_ref))):
        sgn = 1 - 2 * d
        after = same & ((ri - ci) * sgn > 0)
        incl = same & ((ri - ci) * sgn >= 0)
        cum_m = jnp.where(same & ((ci - ri) * sgn >= 0), 1.0, 0.0)
        for bi in range(nbatch):
            ch = d * nbatch + bi
            x = a_ref[0, bi] + dtb_ref[d]
            softplus = jnp.maximum(x, 0.0) + jnp.log(1.0 + jnp.exp(-jnp.abs(x)))
            g = -jnp.exp(alog_ref[d]) * softplus
            beta_ref[ch] = _sigmoid(b_ref[0, bi])
            gc_ref[ch] = jnp.dot(g, cum_m, precision=HI, preferred_element_type=F32)
            gt_ref[ch] = jnp.dot(g, tot_m, precision=HI, preferred_element_type=F32)
            chains.append((ch, d, bi, after, incl, q_ref, k_ref, v_ref, o_ref))

    def chunk(j, chain):
        ch, d, bi, after, incl, q_ref, k_ref, v_ref, o_ref = chain
        cc = j if d == 0 else GDN_G - 1 - j
        r0 = pl.multiple_of(cc * c, c)
        gc_row = gc_ref[ch, pl.ds(cc, 1), :]
        beta_row = beta_ref[ch, pl.ds(cc, 1), :]
        gt_row = gt_ref[ch, pl.ds(cc, 1), :]
        gc_col = _row_to_col(gc_row, eye)
        beta_col = _row_to_col(beta_row, eye)
        k_st = _stack_heads(k_ref[bi, pl.ds(r0, c), :]).astype(F32)
        q_st = _stack_heads(q_ref[bi, pl.ds(r0, c), :]).astype(F32)
        v_st = _stack_heads(v_ref[bi, pl.ds(r0, c), :]).astype(F32)
        egc = jnp.exp(gc_col)
        decay = jnp.exp(jnp.minimum(gc_col - gc_row, 0.0))
        kb = k_st * beta_col
        k_bf = k_st.astype(BF16)
        kk = lax.dot_general(kb.astype(BF16), k_bf, (((1,), (1,)), ((), ())), preferred_element_type=F32)
        neg_a = jnp.where(after, -(kk * decay), 0.0)
        t_m = jnp.where(eye, 1.0, 0.0) + neg_a
        p_m = neg_a
        for _ in range(int(math.log2(c)) - 1):
            p_bf = p_m.astype(BF16)
            p_m = jnp.dot(p_bf, p_bf, preferred_element_type=F32)
            t_m = t_m + jnp.dot(t_m.astype(BF16), p_m.astype(BF16), preferred_element_type=F32)
        rhs = jnp.concatenate([v_st * beta_col, kb * egc], axis=1).astype(BF16)
        sol = jnp.dot(t_m.astype(BF16), rhs, preferred_element_type=F32)
        u_st, w_st = sol[:, :LANES], sol[:, LANES:]
        qk = lax.dot_general(q_st.astype(BF16), k_bf, (((1,), (1,)), ((), ())), preferred_element_type=F32)
        intra = jnp.where(incl, qk * decay, 0.0).astype(BF16)
        q_dec = (q_st * egc).astype(BF16)
        vn, oq = [], []
        for h in range(GDN_HEADS):
            rs = slice(h * c, (h + 1) * c)
            s_h = state_ref[ch * GDN_HEADS + h].astype(BF16)
            vn.append(u_st[rs] - jnp.dot(w_st[rs].astype(BF16), s_h, preferred_element_type=F32))
            oq.append(jnp.dot(q_dec[rs], s_h, preferred_element_type=F32))
        vn_st = jnp.concatenate(vn, axis=0)
        o_st = jnp.concatenate(oq, axis=0) + jnp.dot(intra, vn_st.astype(BF16), preferred_element_type=F32)
        for h in range(GDN_HEADS):
            rs = slice(h * c, (h + 1) * c)
            gt_h = gt_row[:, h * LANES:(h + 1) * LANES]
            k_dec = (k_st[rs] * jnp.exp(gt_h[:, :1] - gc_col[rs])).astype(BF16)
            upd = lax.dot_general(k_dec, vn[h].astype(BF16), (((0,), (0,)), ((), ())), preferred_element_type=F32)
            state_ref[ch * GDN_HEADS + h] = state_ref[ch * GDN_HEADS + h] * jnp.exp(gt_h) + upd
            o_ref[bi, pl.ds(r0, c), h * LANES:(h + 1) * LANES] = o_st[rs]

    def step(j, carry):
        for chain in chains:
            chunk(j, chain)
        return carry

    lax.fori_loop(0, GDN_G, step, 0)


def _gdn(a_rows, b_rows, alog_row, dtb_row, gq, gk, gv, *, nbatch):
    b, s, _ = gq.shape
    nb = s // (GDN_G * GDN_CHUNK)
    ts = GDN_G * GDN_CHUNK
    nchain = 2 * nbatch
    tok_f = pl.BlockSpec((nbatch, ts, 512), lambda bi, i: (bi, i, 0))
    tok_b = pl.BlockSpec((nbatch, ts, 512), lambda bi, i: (bi, nb - 1 - i, 0))
    rows_f = pl.BlockSpec((1, nbatch, GDN_G, GDN_ROWS), lambda bi, i: (0, bi, i, 0))
    rows_b = pl.BlockSpec((1, nbatch, GDN_G, GDN_ROWS), lambda bi, i: (1, bi, nb - 1 - i, 0))
    par = pl.BlockSpec((2, 1, GDN_ROWS), lambda bi, i: (0, 0, 0))
    return pl.pallas_call(
        functools.partial(_gdn_kernel, nbatch=nbatch),
        out_shape=(jax.ShapeDtypeStruct((b, s, 512), F32), jax.ShapeDtypeStruct((b, s, 512), F32)),
        grid=(b // nbatch, nb),
        in_specs=[rows_f, rows_b, rows_f, rows_b, par, par, tok_f, tok_f, tok_f, tok_b, tok_b, tok_b],
        out_specs=(tok_f, tok_b),
        scratch_shapes=[pltpu.VMEM((nchain * GDN_HEADS, GDN_DK, GDN_DV), F32),
                        pltpu.VMEM((nchain, GDN_G, GDN_ROWS), F32),
                        pltpu.VMEM((nchain, GDN_G, GDN_HEADS * LANES), F32),
                        pltpu.VMEM((nchain, GDN_G, GDN_ROWS), F32)],
        compiler_params=_cparams(("arbitrary", "arbitrary")),
        name="gdn_chunked",
    )(a_rows, a_rows, b_rows, b_rows, alog_row, dtb_row, gq, gk, gv, gq, gk, gv)


def _attn_kernel(q_ref, k0_ref, k1_ref, v0_ref, v1_ref, lam_ref, nw_ref, o_ref, acc_ref, *, mode, tk, lambda_init):
    s_len = k0_ref.shape[1]
    tq = q_ref.shape[1]
    nv = v0_ref.shape[2]
    half = LANES // 2
    q = q_ref[0]
    lane = lax.broadcasted_iota(jnp.int32, q.shape, 1)
    lo = lane < half
    keep = (lo, lane >= half)
    stab = (lane == half, lane == 0)
    zero = jnp.zeros_like(q)
    krefs = (k0_ref, k1_ref)
    vrefs = (v0_ref, v1_ref)
    nchunks = s_len // tk
    dn = (((1,), (1,)), ((), ()))

    def kchunk(m, ci):
        return krefs[m][0, pl.ds(pl.multiple_of(ci * tk, tk), tk), :]

    def vchunk(m, ci):
        return vrefs[m][0, pl.ds(pl.multiple_of(ci * tk, tk), tk), :]

    qm, qa = [], []
    for m in range(2):
        qm.append(jnp.where(keep[m], q, zero))
        mx = jnp.max(lax.dot_general(qm[m], kchunk(m, 0), dn, preferred_element_type=F32), axis=-1, keepdims=True)
        qa.append(jnp.where(stab[m], (-mx).astype(BF16), qm[m]))

    def fast(ci, acc):
        out = []
        for m in range(2):
            sc = lax.dot_general(qa[m], kchunk(m, ci), dn, preferred_element_type=F32)
            out.append(acc[m] + jnp.dot(jnp.exp2(sc).astype(BF16), vchunk(m, ci), preferred_element_type=F32))
        return tuple(out)

    acc = lax.fori_loop(0, nchunks, fast, tuple(jnp.zeros((tq, nv), F32) for _ in range(2)))
    nonfinite = jnp.float32(0.0)
    for m in range(2):
        acc_ref[m] = acc[m]
        nonfinite = nonfinite + jnp.sum(jnp.where(jnp.isfinite(acc[m]), 0.0, 1.0))

    @pl.when(nonfinite > 0.0)
    def _():
        def slow(ci, carry):
            out = []
            for m in range(2):
                m_i, a_i = carry[m]
                sc = lax.dot_general(qm[m], kchunk(m, ci), dn, preferred_element_type=F32)
                m_new = jnp.maximum(m_i, jnp.max(sc, axis=-1, keepdims=True))
                p = jnp.exp2(sc - m_new).astype(BF16)
                out.append((m_new, jnp.exp2(m_i - m_new) * a_i + jnp.dot(p, vchunk(m, ci), preferred_element_type=F32)))
            return tuple(out)

        init = tuple((jnp.full((tq, 1), -jnp.inf, F32), jnp.zeros((tq, nv), F32)) for _ in range(2))
        res = lax.fori_loop(0, nchunks, slow, init)
        for m in range(2):
            acc_ref[m] = res[m][1]

    if mode == "diff":
        o0 = acc_ref[0, :, :LANES] / acc_ref[0, :, LANES:LANES + 1]
        o1 = acc_ref[1, :, :LANES] / acc_ref[1, :, LANES:LANES + 1]
    else:
        o0 = acc_ref[0] / acc_ref[0, :, half:half + 1]
        o1 = acc_ref[1] / acc_ref[1, :, 0:1]
    if mode == "diff":
        lv = lam_ref[...]
        lam = (jnp.exp(jnp.sum(lv[0:1] * lv[1:2], axis=-1, keepdims=True))
               - jnp.exp(jnp.sum(lv[2:3] * lv[3:4], axis=-1, keepdims=True)) + lambda_init)
        o = o0 - lam * o1
        o = o * lax.rsqrt(jnp.mean(o * o, axis=-1, keepdims=True) + NORM_EPS) * nw_ref[...] * (1.0 - lambda_init)
    else:
        o = jnp.where(lo, o0, o1)
    o_ref[0] = o.astype(o_ref.dtype)


def _attention(q, k_arr, v_arr, lam_vecs, norm_w, *, mode, tq, tk, lambda_init=0.0):
    b, s, w = q.shape
    slabs = w // LANES
    if mode == "diff":
        nv = 2 * LANES
        k_col = lambda p, m: 2 * p + m
        v_col = lambda p, m: p
    else:
        nv = LANES
        k_col = v_col = lambda p, m: 2 * (p // 2) + m
    kspec = lambda m: pl.BlockSpec((1, s, LANES), lambda bi, p, i: (bi, 0, k_col(p, m)))
    vspec = lambda m: pl.BlockSpec((1, s, nv), lambda bi, p, i: (bi, 0, v_col(p, m)))
    return pl.pallas_call(
        functools.partial(_attn_kernel, mode=mode, tk=tk, lambda_init=lambda_init),
        out_shape=jax.ShapeDtypeStruct((b, s, w), BF16),
        grid=(b, slabs, s // tq),
        in_specs=[pl.BlockSpec((1, tq, LANES), lambda bi, p, i: (bi, i, p)),
                  kspec(0), kspec(1), vspec(0), vspec(1),
                  pl.BlockSpec((4, DIFF_DQK), lambda bi, p, i: (0, 0)),
                  pl.BlockSpec((1, LANES), lambda bi, p, i: (0, 0))],
        out_specs=pl.BlockSpec((1, tq, LANES), lambda bi, p, i: (bi, i, p)),
        scratch_shapes=[pltpu.VMEM((2, tq, nv), F32)],
        compiler_params=_cparams(("arbitrary", "arbitrary", "arbitrary")),
        name="attn_" + mode,
    )(q, k_arr, k_arr, v_arr, v_arr, lam_vecs, norm_w)


def _merge_kernel(of_ref, ob_ref, z_ref, g0_ref, g1_ref, g2_ref, yb_ref, yc_ref, x_ref,
                  wa_ref, wb_ref, wc_ref, wo_ref, gnw_ref, fnw_ref, rw_ref, rb_ref,
                  xo_ref, h_ref, id_ref, rwgt_ref):
    o = of_ref[...] + ob_ref[...]
    parts = []
    for h in range(GDN_HEADS):
        oh = o[:, h * LANES:(h + 1) * LANES]
        parts.append(oh * lax.rsqrt(jnp.mean(oh * oh, axis=-1, keepdims=True) + NORM_EPS) * gnw_ref[...])
    z = z_ref[...].astype(F32)
    ya = (jnp.concatenate(parts, axis=1) * (z * _sigmoid(z))).astype(BF16)
    merged = _sigmoid(g0_ref[...].astype(F32)) * jnp.dot(ya, wa_ref[...], preferred_element_type=F32)
    merged = merged + _sigmoid(g1_ref[...].astype(F32)) * jnp.dot(yb_ref[...], wb_ref[...], preferred_element_type=F32)
    merged = merged + _sigmoid(g2_ref[...].astype(F32)) * jnp.dot(yc_ref[...], wc_ref[...], preferred_element_type=F32)
    xn = x_ref[...] + jnp.dot(merged.astype(BF16), wo_ref[...], preferred_element_type=F32)
    xo_ref[...] = xn
    hf = xn * lax.rsqrt(jnp.mean(xn * xn, axis=-1, keepdims=True) + NORM_EPS) * fnw_ref[...]
    h_ref[...] = hf.astype(BF16)
    logits = jnp.dot(hf, rw_ref[...], precision=HI, preferred_element_type=F32) + rb_ref[...]
    lane = lax.broadcasted_iota(jnp.int32, logits.shape, 1)
    big = jnp.int32(LANES)
    ninf = -jnp.inf
    glog = jnp.where(lane < N_GROUPS, logits, ninf)
    gmax = jnp.max(glog, axis=-1, keepdims=True)
    gidx = jnp.min(jnp.where(glog == gmax, lane, big), axis=-1, keepdims=True)
    gp = 1.0 / jnp.sum(jnp.exp(glog - gmax), axis=-1, keepdims=True)
    e = lane - N_GROUPS
    sel = (e >= 0) & (e < N_EXPERTS) & ((e // EXPERTS_PER_GROUP) == gidx)
    elog = jnp.where(sel, logits, ninf)
    m1 = jnp.max(elog, axis=-1, keepdims=True)
    i1 = jnp.min(jnp.where(elog == m1, lane, big), axis=-1, keepdims=True)
    elog2 = jnp.where(lane == i1, ninf, elog)
    m2 = jnp.max(elog2, axis=-1, keepdims=True)
    i2 = jnp.min(jnp.where(elog2 == m2, lane, big), axis=-1, keepdims=True)
    e2 = jnp.exp(m2 - m1)
    w1 = 1.0 / (1.0 + e2)
    w2 = e2 * w1
    id_ref[...] = jnp.where(lane == 0, i1 - N_GROUPS, jnp.where(lane == 1, i2 - N_GROUPS, 0))
    rwgt_ref[...] = jnp.where(lane == 0, gp * w1, jnp.where(lane == 1, gp * w2, 0.0))


def _merge(o_f, o_b, main2, yb, yc, x2d, wa, wb, wc, wo, gnw, fnw, rw, rb, *, tm):
    t, d = x2d.shape
    full = lambda shp: pl.BlockSpec(shp, lambda i: tuple(0 for _ in shp))
    return pl.pallas_call(
        _merge_kernel,
        out_shape=(jax.ShapeDtypeStruct((t, d), F32), jax.ShapeDtypeStruct((t, d), BF16),
                   jax.ShapeDtypeStruct((t, LANES), jnp.int32), jax.ShapeDtypeStruct((t, LANES), F32)),
        grid=(t // tm,),
        in_specs=[pl.BlockSpec((tm, 512), lambda i: (i, 0)),
                  pl.BlockSpec((tm, 512), lambda i: (i, 0)),
                  pl.BlockSpec((tm, 512), lambda i: (i, COL_Z // 512)),
                  pl.BlockSpec((tm, d), lambda i: (i, 0)),
                  pl.BlockSpec((tm, d), lambda i: (i, 1)),
                  pl.BlockSpec((tm, d), lambda i: (i, 2)),
                  pl.BlockSpec((tm, 512), lambda i: (i, 0)),
                  pl.BlockSpec((tm, 512), lambda i: (i, 0)),
                  pl.BlockSpec((tm, d), lambda i: (i, 0)),
                  full((512, d)), full((512, d)), full((512, d)), full((d, d)),
                  full((1, LANES)), full((1, d)), full((d, LANES)), full((1, LANES))],
        out_specs=(pl.BlockSpec((tm, d), lambda i: (i, 0)), pl.BlockSpec((tm, d), lambda i: (i, 0)),
                   pl.BlockSpec((tm, LANES), lambda i: (i, 0)), pl.BlockSpec((tm, LANES), lambda i: (i, 0))),
        compiler_params=_cparams(("arbitrary",)),
        name="merge_router",
    )(o_f, o_b, main2, main2, main2, main2, yb, yc, x2d, wa, wb, wc, wo, gnw, fnw, rw, rb)


def _expert_kernel(blk_e_ref, nused_ref, x_ref, w1_ref, w3_ref, w2_ref, o_ref):
    i = pl.program_id(0)

    @pl.when(i < nused_ref[0])
    def _():
        x = x_ref[...]
        a = jnp.dot(x, w1_ref[0], preferred_element_type=F32)
        u = jnp.dot(x, w3_ref[0], preferred_element_type=F32)
        hmid = (a * _sigmoid(a) * u).astype(BF16)
        o_ref[...] = jnp.dot(hmid, w2_ref[0], preferred_element_type=F32).astype(o_ref.dtype)

    @pl.when(i >= nused_ref[0])
    def _():
        o_ref[...] = jnp.zeros_like(o_ref)


def _experts(blk_e, nused, xb, w1, w3, w2):
    p_len, d = xb.shape
    ff = w1.shape[2]
    nblk = p_len // MOE_BLOCK
    return pl.pallas_call(
        _expert_kernel,
        out_shape=jax.ShapeDtypeStruct((p_len, d), BF16),
        grid_spec=pltpu.PrefetchScalarGridSpec(
            num_scalar_prefetch=2,
            grid=(nblk,),
            in_specs=[pl.BlockSpec((MOE_BLOCK, d), lambda i, be, nu: (i, 0)),
                      pl.BlockSpec((1, d, ff), lambda i, be, nu: (be[i], 0, 0)),
                      pl.BlockSpec((1, d, ff), lambda i, be, nu: (be[i], 0, 0)),
                      pl.BlockSpec((1, ff, d), lambda i, be, nu: (be[i], 0, 0))],
            out_specs=pl.BlockSpec((MOE_BLOCK, d), lambda i, be, nu: (i, 0)),
        ),
        compiler_params=_cparams(("arbitrary",)),
        name="expert_mlp",
    )(blk_e, nused, xb, w1, w3, w2)


def _combine_kernel(x_ref, y0_ref, y1_ref, w_ref, nw_ref, o_ref, *, final):
    w = w_ref[...]
    x = x_ref[...] + w[:, 0:1] * y0_ref[...].astype(F32) + w[:, 1:2] * y1_ref[...].astype(F32)
    if final:
        x = x * lax.rsqrt(jnp.mean(x * x, axis=-1, keepdims=True) + NORM_EPS) * nw_ref[...]
    o_ref[...] = x


def _combine(x2d, y0, y1, wts, norm_w, *, final, tm):
    t, d = x2d.shape
    tile = pl.BlockSpec((tm, d), lambda i: (i, 0))
    return pl.pallas_call(
        functools.partial(_combine_kernel, final=final),
        out_shape=jax.ShapeDtypeStruct((t, d), F32),
        grid=(t // tm,),
        in_specs=[tile, tile, tile, pl.BlockSpec((tm, LANES), lambda i: (i, 0)), pl.BlockSpec((1, d), lambda i: (0, 0))],
        out_specs=tile,
        compiler_params=_cparams(("arbitrary",)),
        name="moe_combine",
    )(x2d, y0, y1, wts, norm_w.reshape(1, d))


def _rope_tables(pos, dim):
    inv = 1.0 / (ROPE_THETA ** (jnp.arange(0, dim, 2, dtype=F32) / dim))
    ang = pos.astype(F32)[:, None] * inv[None, :]
    ang = jnp.concatenate([ang, ang], axis=-1)
    return jnp.cos(ang), jnp.sin(ang)


def _signed_sin(sin):
    half = sin.shape[-1] // 2
    return jnp.concatenate([-sin[:, :half], sin[:, half:]], axis=-1)


def _layout_w_in(w):
    o = 0
    parts = {}
    for name, size in (("qkv", 1536), ("z", 512), ("b", 8), ("a", 8), ("dq", 512), ("dk", 512), ("dv", 512),
                       ("cq", 512), ("ck", 128), ("cv", 128), ("gate", 3072)):
        parts[name] = w[:, o:o + size]
        o += size
    swap = lambda m: jnp.concatenate([m[:, 64:], m[:, :64]], axis=1)
    main = jnp.concatenate([parts["gate"], parts["qkv"], parts["z"], parts["dq"], parts["dk"], parts["dv"],
                            parts["cq"], parts["ck"], swap(parts["ck"]), parts["cv"], swap(parts["cv"])], axis=1)
    ba = jnp.concatenate([parts["b"], parts["a"], jnp.zeros((w.shape[0], LANES - 16), w.dtype)], axis=1)
    return main.astype(BF16), ba


def _rows_layout(t, bsz, s):
    nc = s // GDN_CHUNK
    t = t.reshape(bsz, nc, GDN_CHUNK, 2, GDN_HEADS)
    return jnp.transpose(t, (3, 0, 1, 4, 2)).reshape(2, bsz, nc, GDN_ROWS)


def _moe_dispatch(ids, t):
    a = t * TOPK
    p_len = ((a + N_EXPERTS * (MOE_BLOCK - 1) + MOE_BLOCK - 1) // MOE_BLOCK) * MOE_BLOCK
    n_blocks = p_len // MOE_BLOCK
    flat_e = ids.reshape(-1)
    iota_a = jnp.arange(a, dtype=jnp.int32)
    skey = jnp.sort(flat_e * a + iota_a)
    order = skey % a
    se = skey // a
    experts = jnp.arange(N_EXPERTS, dtype=jnp.int32)
    counts = jnp.sum((flat_e[:, None] == experts[None, :]).astype(jnp.int32), axis=0)
    start = jnp.cumsum(counts) - counts
    pcounts = ((counts + MOE_BLOCK - 1) // MOE_BLOCK) * MOE_BLOCK
    pend = jnp.cumsum(pcounts)
    pstart = pend - pcounts
    dest_sorted = pstart[se] + (iota_a - start[se])
    blk_first = jnp.arange(n_blocks, dtype=jnp.int32) * MOE_BLOCK
    blk_e = jnp.minimum(jnp.sum((pend[None, :] <= blk_first[:, None]).astype(jnp.int32), axis=1), N_EXPERTS - 1)
    row = jnp.arange(p_len, dtype=jnp.int32)
    row_e = jnp.repeat(blk_e, MOE_BLOCK)
    j = row - pstart[row_e]
    valid = j < counts[row_e]
    tok_buf = jnp.where(valid, order[jnp.minimum(start[row_e] + j, a - 1)] // TOPK, t)
    _, dest = lax.sort((order, dest_sorted), num_keys=1)
    nused = (pend[-1] // MOE_BLOCK).astype(jnp.int32).reshape(1)
    return tok_buf, dest.reshape(t, TOPK), blk_e, nused


def kernel(x, attn_norm_w, w_in, gdn_conv_w, gdn_a_log, gdn_dt_bias, gdn_norm_w, diff_lambda, diff_norm_w,
           gqa_q_norm_w, gqa_k_norm_w, w_branch_a, w_branch_b, w_branch_c, w_out, ffn_norm_w,
           router_group_w, router_group_b, router_expert_w, router_expert_b,
           expert_w_gate, expert_w_up, expert_w_down, final_norm_w):
    bsz, s, d = x.shape
    t = bsz * s
    depth = w_in.shape[0]
    tm = min(512, t)
    ts = min(512, s)

    rows = s // GRID_W
    row = jnp.broadcast_to(jnp.arange(rows)[:, None], (rows, GRID_W)).reshape(s)
    col = jnp.broadcast_to(jnp.arange(GRID_W)[None, :], (rows, GRID_W)).reshape(s)
    c1, s1 = _rope_tables(jnp.arange(s), DIFF_DQK)
    cr, sr = _rope_tables(row, GQA_DH // 2)
    cc, sc = _rope_tables(col, GQA_DH // 2)
    cos1 = jnp.tile(c1, (1, 2))
    sin1 = jnp.tile(_signed_sin(s1), (1, 2))
    cos2 = jnp.tile(jnp.concatenate([cr, cc], axis=-1), (1, 2))
    sin2 = jnp.tile(jnp.concatenate([_signed_sin(sr), _signed_sin(sc)], axis=-1), (1, 2))

    x2 = x.reshape(t, d)
    for l in range(depth):
        lambda_init = 0.8 - 0.6 * math.exp(-0.3 * l)
        w_main, w_ba = _layout_w_in(w_in[l])
        main2 = _norm_proj(x2, attn_norm_w[l], w_main, BF16, exact=False, tm=tm, tn=1536)
        ba = _norm_proj(x2, attn_norm_w[l], w_ba, F32, exact=True, tm=tm, tn=LANES)
        main3 = main2.reshape(bsz, s, N_MAIN)

        conv_w = jnp.concatenate([gdn_conv_w[l], jnp.zeros((8 - GDN_CONV, gdn_conv_w.shape[2]), F32)], axis=0)
        qnw = jnp.tile(gqa_q_norm_w[l], 2).reshape(1, LANES)
        knw = jnp.tile(gqa_k_norm_w[l], 2).reshape(1, LANES)
        gq, gk, gv, dq, dk, dv, cq, ck, cv = _prep(main3, conv_w, cos1, sin1, cos2, sin2, qnw, knw, ts=ts)

        b_rows = _rows_layout(ba[:, 0:8], bsz, s)
        a_rows = _rows_layout(ba[:, 8:16], bsz, s)
        alog_row = jnp.repeat(gdn_a_log[l], GDN_CHUNK, axis=1).reshape(2, 1, GDN_ROWS)
        dtb_row = jnp.repeat(gdn_dt_bias[l], GDN_CHUNK, axis=1).reshape(2, 1, GDN_ROWS)
        o_f, o_b = _gdn(a_rows, b_rows, alog_row, dtb_row, gq, gk, gv, nbatch=GDN_NBATCH if bsz % GDN_NBATCH == 0 else 1)

        nw_diff = diff_norm_w[l].reshape(1, LANES)
        yb = _attention(dq, dk, dv, diff_lambda[l], nw_diff, mode="diff",
                        tq=min(ATTN_TQ, s), tk=min(ATTN_TK, s), lambda_init=lambda_init)
        yc = _attention(cq, ck, cv, diff_lambda[l], nw_diff, mode="gqa",
                        tq=min(ATTN_TQ, s), tk=min(ATTN_TK, s))

        rw = jnp.concatenate([router_group_w[l], router_expert_w[l],
                              jnp.zeros((d, LANES - N_GROUPS - N_EXPERTS), F32)], axis=1)
        rb = jnp.concatenate([router_group_b[l], router_expert_b[l],
                              jnp.zeros((LANES - N_GROUPS - N_EXPERTS,), F32)]).reshape(1, LANES)
        x2, h2, ids, wts = _merge(o_f.reshape(t, 512), o_b.reshape(t, 512), main2, yb.reshape(t, 512), yc.reshape(t, 512), x2,
                                  w_branch_a[l].astype(BF16), w_branch_b[l].astype(BF16),
                                  w_branch_c[l].astype(BF16), w_out[l].astype(BF16),
                                  gdn_norm_w[l].reshape(1, LANES), ffn_norm_w[l].reshape(1, d), rw, rb, tm=min(256, t))

        tok_buf, dest, blk_e, nused = _moe_dispatch(ids[:, :TOPK], t)
        h_pad = jnp.concatenate([h2, jnp.zeros((1, d), BF16)], axis=0)
        yblk = _experts(blk_e, nused, h_pad[tok_buf], expert_w_gate[l].astype(BF16),
                        expert_w_up[l].astype(BF16), expert_w_down[l].astype(BF16))
        x2 = _combine(x2, yblk[dest[:, 0]], yblk[dest[:, 1]], wts, final_norm_w, final=(l == depth - 1), tm=tm)

    return x2.reshape(bsz, s, d)
```

```python
import functools
import math

import jax
import jax.numpy as jnp
from jax import lax
from jax.experimental import pallas as pl
from jax.experimental.pallas import tpu as pltpu

GRID_W = 64
ROPE_THETA = 10000.0
NORM_EPS = 1e-6
GDN_HEADS = 4
GDN_DK = 128
GDN_DV = 128
GDN_CONV = 5
GDN_CHUNK = 64
DIFF_HEADS = 4
DIFF_DQK = 64
GQA_HEADS = 8
GQA_KV = 2
GQA_DH = 64
N_GROUPS = 4
EXPERTS_PER_GROUP = 8
N_EXPERTS = N_GROUPS * EXPERTS_PER_GROUP
TOPK = 2
MOE_BLOCK = 256

LANES = 128
VMEM_LIMIT = 56 * 1024 * 1024

COL_GATE = 0
COL_QKV = 3072
COL_Z = 4608
COL_DQ = 5120
COL_DK = 5632
COL_DV = 6144
COL_CQ = 6656
COL_CK = 7168
COL_CV = 7424
N_MAIN = 7680

LOG2E = math.log2(math.e)
ATTN_TQ = 512
ATTN_TK = 512

HI = lax.Precision.HIGHEST
F32 = jnp.float32
BF16 = jnp.bfloat16


def _cparams(sem):
    return pltpu.CompilerParams(dimension_semantics=sem, vmem_limit_bytes=VMEM_LIMIT)


def _sigmoid(x):
    return 1.0 / (1.0 + jnp.exp(-x))


def _norm_proj_kernel(x_ref, nw_ref, w_ref, o_ref, *, exact):
    x = x_ref[...]
    h = x * lax.rsqrt(jnp.mean(x * x, axis=-1, keepdims=True) + NORM_EPS) * nw_ref[...]
    if exact:
        o_ref[...] = jnp.dot(h, w_ref[...], precision=HI, preferred_element_type=F32).astype(o_ref.dtype)
    else:
        o_ref[...] = jnp.dot(h.astype(BF16), w_ref[...], preferred_element_type=F32).astype(o_ref.dtype)


def _norm_proj(x2d, norm_w, w, out_dtype, *, exact, tm, tn):
    t, d = x2d.shape
    n = w.shape[1]
    return pl.pallas_call(
        functools.partial(_norm_proj_kernel, exact=exact),
        out_shape=jax.ShapeDtypeStruct((t, n), out_dtype),
        grid=(n // tn, t // tm),
        in_specs=[pl.BlockSpec((tm, d), lambda j, i: (i, 0)),
                  pl.BlockSpec((1, d), lambda j, i: (0, 0)),
                  pl.BlockSpec((d, tn), lambda j, i: (0, j))],
        out_specs=pl.BlockSpec((tm, tn), lambda j, i: (i, j)),
        compiler_params=_cparams(("arbitrary", "arbitrary")),
        name="norm_proj_exact" if exact else "norm_proj",
    )(x2d, norm_w.reshape(1, d), w)


HALO = 16


def _rot_half(x, half):
    lane = lax.broadcasted_iota(jnp.int32, x.shape, 1)
    first = (lane % (2 * half)) < half
    return jnp.where(first, pltpu.roll(x, LANES - half, 1), pltpu.roll(x, half, 1))


def _group_sumsq(x, width):
    x2 = x * x
    if width == LANES:
        return jnp.sum(x2, axis=-1, keepdims=True)
    lane = lax.broadcasted_iota(jnp.int32, x.shape, 1)
    lo = lane < width
    s_lo = jnp.sum(jnp.where(lo, x2, 0.0), axis=-1, keepdims=True)
    s_hi = jnp.sum(jnp.where(lo, 0.0, x2), axis=-1, keepdims=True)
    return jnp.where(lo, s_lo, s_hi)


def _aug_slab(x, m):
    lane = lax.broadcasted_iota(jnp.int32, x.shape, 1)
    half = LANES // 2
    keep = (lane < half) if m == 0 else (lane >= half)
    one = jnp.where(lane == (1 - m) * half, 1.0, 0.0).astype(x.dtype)
    return jnp.where(keep, x, one)


def _prep_kernel(qkv_ref, prev_ref, next_ref, dq_ref, dk_ref, dv_ref, cq_ref, ck_ref, cv_ref,
                 convw_ref, cos1_ref, sin1_ref, cos2_ref, sin2_ref, qnw_ref, knw_ref,
                 gq_ref, gk_ref, gv_ref, dqo_ref, dko_ref, dvo_ref, cqo_ref, cko_ref, cvo_ref, *, ts):
    i = pl.program_id(1)
    n = pl.num_programs(1)
    cur = qkv_ref[0].astype(F32)
    prev = jnp.where(i > 0, prev_ref[0].astype(F32), 0.0)
    nxt = jnp.where(i < n - 1, next_ref[0].astype(F32), 0.0)
    ext = jnp.concatenate([prev, cur, nxt], axis=0)
    pad = GDN_CONV // 2
    acc = jnp.zeros_like(cur)
    for j in range(GDN_CONV):
        off = HALO - pad + j
        acc = acc + ext[off:off + ts, :] * convw_ref[j:j + 1, :]
    act = acc * _sigmoid(acc)
    nqk = GDN_HEADS * GDN_DK
    for h in range(GDN_HEADS):
        sl = slice(h * GDN_DK, (h + 1) * GDN_DK)
        qh = act[:, sl]
        gq_ref[0, :, sl] = (qh * lax.rsqrt(_group_sumsq(qh, LANES) + NORM_EPS) * (GDN_DK ** -0.5)).astype(BF16)
        kh = act[:, nqk + h * GDN_DK: nqk + (h + 1) * GDN_DK]
        gk_ref[0, :, sl] = (kh * lax.rsqrt(_group_sumsq(kh, LANES) + NORM_EPS)).astype(BF16)
    gv_ref[0] = act[:, 2 * nqk:].astype(BF16)
    cos1, sin1 = cos1_ref[...], sin1_ref[...]
    lane = lax.broadcasted_iota(jnp.int32, (ts, LANES), 1)
    ones_col = jnp.where(lane == 0, 1.0, 0.0).astype(BF16)
    for p in range(DIFF_HEADS):
        sl = slice(p * LANES, (p + 1) * LANES)
        xq = dq_ref[0, :, sl].astype(F32)
        dqo_ref[0, :, sl] = ((xq * cos1 + _rot_half(xq, DIFF_DQK // 2) * sin1) * (DIFF_DQK ** -0.5 * LOG2E)).astype(BF16)
        xk = dk_ref[0, :, sl].astype(F32)
        xk = (xk * cos1 + _rot_half(xk, DIFF_DQK // 2) * sin1).astype(BF16)
        for m in range(2):
            dko_ref[0, :, (2 * p + m) * LANES:(2 * p + m + 1) * LANES] = _aug_slab(xk, m)
        dvo_ref[0, :, 2 * p * LANES:(2 * p + 1) * LANES] = dv_ref[0, :, sl]
        dvo_ref[0, :, (2 * p + 1) * LANES:(2 * p + 2) * LANES] = ones_col
    cos2, sin2 = cos2_ref[...], sin2_ref[...]
    for p in range(GQA_HEADS * GQA_DH // LANES):
        sl = slice(p * LANES, (p + 1) * LANES)
        xq = cq_ref[0, :, sl].astype(F32)
        xq = xq * lax.rsqrt(_group_sumsq(xq, GQA_DH) * (1.0 / GQA_DH) + NORM_EPS) * qnw_ref[...]
        cqo_ref[0, :, sl] = ((xq * cos2 + _rot_half(xq, GQA_DH // 4) * sin2) * (GQA_DH ** -0.5 * LOG2E)).astype(BF16)
    for p in range(2):
        sl = slice(p * LANES, (p + 1) * LANES)
        xk = ck_ref[0, :, sl].astype(F32)
        xk = xk * lax.rsqrt(_group_sumsq(xk, GQA_DH) * (1.0 / GQA_DH) + NORM_EPS) * knw_ref[...]
        xk = (xk * cos2 + _rot_half(xk, GQA_DH // 4) * sin2).astype(BF16)
        xv = cv_ref[0, :, sl]
        for m in range(2):
            c = p if m == 0 else 1 - p
            osl = slice((2 * c + m) * LANES, (2 * c + m + 1) * LANES)
            cko_ref[0, :, osl] = _aug_slab(xk, m)
            cvo_ref[0, :, osl] = _aug_slab(xv, m)


def _prep(main3, conv_w, cos1, sin1, cos2, sin2, qnw, knw, *, ts):
    b, s, _ = main3.shape
    nt = s // ts
    hb = ts // HALO
    last = s // HALO - 1
    row = lambda w: pl.BlockSpec((1, w), lambda bi, i: (0, 0))
    tab = pl.BlockSpec((ts, LANES), lambda bi, i: (i, 0))
    col = lambda w, off: pl.BlockSpec((1, ts, w), lambda bi, i: (bi, i, off // w))
    out = lambda w: pl.BlockSpec((1, ts, w), lambda bi, i: (bi, i, 0))
    widths = (512, 512, 512, 512, 1024, 1024, 512, 512, 512)
    return pl.pallas_call(
        functools.partial(_prep_kernel, ts=ts),
        out_shape=tuple(jax.ShapeDtypeStruct((b, s, w), BF16) for w in widths),
        grid=(b, nt),
        in_specs=[
            col(1536, COL_QKV),
            pl.BlockSpec((1, HALO, 1536), lambda bi, i: (bi, jnp.maximum(i * hb - 1, 0), COL_QKV // 1536)),
            pl.BlockSpec((1, HALO, 1536), lambda bi, i: (bi, jnp.minimum((i + 1) * hb, last), COL_QKV // 1536)),
            col(512, COL_DQ), col(512, COL_DK), col(512, COL_DV), col(512, COL_CQ), col(256, COL_CK), col(256, COL_CV),
            pl.BlockSpec((8, 1536), lambda bi, i: (0, 0)),
            tab, tab, tab, tab, row(LANES), row(LANES),
        ],
        out_specs=tuple(out(w) for w in widths),
        compiler_params=_cparams(("arbitrary", "arbitrary")),
        name="mixer_prep",
    )(*([main3] * 9), conv_w, cos1, sin1, cos2, sin2, qnw, knw)


GDN_G = 8
GDN_NBATCH = 2
GDN_ROWS = GDN_HEADS * GDN_CHUNK


def _stack_heads(x):
    return jnp.concatenate([x[:, h * LANES:(h + 1) * LANES] for h in range(GDN_HEADS)], axis=0)


def _row_to_col(row, eye):
    return jnp.sum(jnp.where(eye, row, 0.0), axis=1, keepdims=True)


def _gdn_kernel(af_ref, ab_ref, bf_ref, bb_ref, alog_ref, dtb_ref, qf_ref, kf_ref, vf_ref, qb_ref, kb_ref, vb_ref,
                of_ref, ob_ref, state_ref, gc_ref, gt_ref, beta_ref, *, nbatch):
    blk = pl.program_id(1)
    n = GDN_ROWS
    c = GDN_CHUNK

    @pl.when(blk == 0)
    def _():
        state_ref[...] = jnp.zeros_like(state_ref)

    ri = lax.broadcasted_iota(jnp.int32, (n, n), 0)
    ci = lax.broadcasted_iota(jnp.int32, (n, n), 1)
    same = (ri // c) == (ci // c)
    eye = ri == ci
    ti = lax.broadcasted_iota(jnp.int32, (n, GDN_HEADS * LANES), 0)
    tj = lax.broadcasted_iota(jnp.int32, (n, GDN_HEADS * LANES), 1)
    tot_m = jnp.where((ti // c) == (tj // LANES), 1.0, 0.0)

    chains = []
    for d, (a_ref, b_ref, q_ref, k_ref, v_ref, o_ref) in enumerate(
            ((af_ref, bf_ref, qf_ref, kf_ref, vf_ref, of_ref), (ab_ref, bb_ref, qb_ref, kb_ref, vb_ref, ob_ref))):
        sgn = 1 - 2 * d
        after = same & ((ri - ci) * sgn > 0)
        incl = same & ((ri - ci) * sgn >= 0)
        cum_m = jnp.where(same & ((ci - ri) * sgn >= 0), 1.0, 0.0)
        for bi in range(nbatch):
            ch = d * nbatch + bi
            x = a_ref[0, bi] + dtb_ref[d]
            softplus = jnp.maximum(x, 0.0) + jnp.log(1.0 + jnp.exp(-jnp.abs(x)))
            g = -jnp.exp(alog_ref[d]) * softplus
            beta_ref[ch] = _sigmoid(b_ref[0, bi])
            gc_ref[ch] = jnp.dot(g, cum_m, precision=HI, preferred_element_type=F32)
            gt_ref[ch] = jnp.dot(g, tot_m, precision=HI, preferred_element_type=F32)
            chains.append((ch, d, bi, after, incl, q_ref, k_ref, v_ref, o_ref))

    def chunk(j, chain):
        ch, d, bi, after, incl, q_ref, k_ref, v_ref, o_ref = chain
        cc = j if d == 0 else GDN_G - 1 - j
        r0 = pl.multiple_of(cc * c, c)
        gc_row = gc_ref[ch, pl.ds(cc, 1), :]
        beta_row = beta_ref[ch, pl.ds(cc, 1), :]
        gt_row = gt_ref[ch, pl.ds(cc, 1), :]
        gc_col = _row_to_col(gc_row, eye)
        beta_col = _row_to_col(beta_row, eye)
        k_st = _stack_heads(k_ref[bi, pl.ds(r0, c), :]).astype(F32)
        q_st = _stack_heads(q_ref[bi, pl.ds(r0, c), :]).astype(F32)
        v_st = _stack_heads(v_ref[bi, pl.ds(r0, c), :]).astype(F32)
        egc = jnp.exp(gc_col)
        decay = jnp.exp(jnp.minimum(gc_col - gc_row, 0.0))
        kb = k_st * beta_col
        k_bf = k_st.astype(BF16)
        kk = lax.dot_general(kb.astype(BF16), k_bf, (((1,), (1,)), ((), ())), preferred_element_type=F32)
        qk = lax.dot_general(q_st.astype(BF16), k_bf, (((1,), (1,)), ((), ())), preferred_element_type=F32)
        yield
        neg_a = jnp.where(after, -(kk * decay), 0.0)
        t_m = jnp.where(eye, 1.0, 0.0) + neg_a
        p_m = neg_a
        for _ in range(int(math.log2(c)) - 1):
            p_bf = p_m.astype(BF16)
            p_m = jnp.dot(p_bf, p_bf, preferred_element_type=F32)
            yield
            t_m = t_m + jnp.dot(t_m.astype(BF16), p_m.astype(BF16), preferred_element_type=F32)
            yield
        rhs = jnp.concatenate([v_st * beta_col, kb * egc], axis=1).astype(BF16)
        sol = jnp.dot(t_m.astype(BF16), rhs, preferred_element_type=F32)
        yield
        u_st, w_st = sol[:, :LANES], sol[:, LANES:]
        intra = jnp.where(incl, qk * decay, 0.0).astype(BF16)
        q_dec = (q_st * egc).astype(BF16)
        vn, oq = [], []
        for h in range(GDN_HEADS):
            rs = slice(h * c, (h + 1) * c)
            s_h = state_ref[ch * GDN_HEADS + h].astype(BF16)
            vn.append(u_st[rs] - jnp.dot(w_st[rs].astype(BF16), s_h, preferred_element_type=F32))
            oq.append(jnp.dot(q_dec[rs], s_h, preferred_element_type=F32))
        yield
        vn_st = jnp.concatenate(vn, axis=0)
        o_st = jnp.concatenate(oq, axis=0) + jnp.dot(intra, vn_st.astype(BF16), preferred_element_type=F32)
        for h in range(GDN_HEADS):
            rs = slice(h * c, (h + 1) * c)
            gt_h = gt_row[:, h * LANES:(h + 1) * LANES]
            k_dec = (k_st[rs] * jnp.exp(gt_h[:, :1] - gc_col[rs])).astype(BF16)
            upd = lax.dot_general(k_dec, vn[h].astype(BF16), (((0,), (0,)), ((), ())), preferred_element_type=F32)
            state_ref[ch * GDN_HEADS + h] = state_ref[ch * GDN_HEADS + h] * jnp.exp(gt_h) + upd
            o_ref[bi, pl.ds(r0, c), h * LANES:(h + 1) * LANES] = o_st[rs]

    def step(j, carry):
        active = [chunk(j, chain) for chain in chains]
        while active:
            active = [g for g in active if next(g, active) is not active]
        return carry

    lax.fori_loop(0, GDN_G, step, 0)


def _gdn(a_rows, b_rows, alog_row, dtb_row, gq, gk, gv, *, nbatch):
    b, s, _ = gq.shape
    nb = s // (GDN_G * GDN_CHUNK)
    ts = GDN_G * GDN_CHUNK
    nchain = 2 * nbatch
    tok_f = pl.BlockSpec((nbatch, ts, 512), lambda bi, i: (bi, i, 0))
    tok_b = pl.BlockSpec((nbatch, ts, 512), lambda bi, i: (bi, nb - 1 - i, 0))
    rows_f = pl.BlockSpec((1, nbatch, GDN_G, GDN_ROWS), lambda bi, i: (0, bi, i, 0))
    rows_b = pl.BlockSpec((1, nbatch, GDN_G, GDN_ROWS), lambda bi, i: (1, bi, nb - 1 - i, 0))
    par = pl.BlockSpec((2, 1, GDN_ROWS), lambda bi, i: (0, 0, 0))
    return pl.pallas_call(
        functools.partial(_gdn_kernel, nbatch=nbatch),
        out_shape=(jax.ShapeDtypeStruct((b, s, 512), F32), jax.ShapeDtypeStruct((b, s, 512), F32)),
        grid=(b // nbatch, nb),
        in_specs=[rows_f, rows_b, rows_f, rows_b, par, par, tok_f, tok_f, tok_f, tok_b, tok_b, tok_b],
        out_specs=(tok_f, tok_b),
        scratch_shapes=[pltpu.VMEM((nchain * GDN_HEADS, GDN_DK, GDN_DV), F32),
                        pltpu.VMEM((nchain, GDN_G, GDN_ROWS), F32),
                        pltpu.VMEM((nchain, GDN_G, GDN_HEADS * LANES), F32),
                        pltpu.VMEM((nchain, GDN_G, GDN_ROWS), F32)],
        compiler_params=_cparams(("arbitrary", "arbitrary")),
        name="gdn_chunked",
    )(a_rows, a_rows, b_rows, b_rows, alog_row, dtb_row, gq, gk, gv, gq, gk, gv)


def _attn_kernel(q_ref, k0_ref, k1_ref, v0_ref, v1_ref, lam_ref, nw_ref, o_ref, acc_ref, *, mode, tk, lambda_init):
    s_len = k0_ref.shape[1]
    tq = q_ref.shape[1]
    nv = v0_ref.shape[2]
    half = LANES // 2
    q = q_ref[0]
    lane = lax.broadcasted_iota(jnp.int32, q.shape, 1)
    lo = lane < half
    keep = (lo, lane >= half)
    stab = (lane == half, lane == 0)
    zero = jnp.zeros_like(q)
    krefs = (k0_ref, k1_ref)
    vrefs = (v0_ref, v1_ref)
    nchunks = s_len // tk
    dn = (((1,), (1,)), ((), ()))

    def kchunk(m, ci):
        return krefs[m][0, pl.ds(pl.multiple_of(ci * tk, tk), tk), :]

    def vchunk(m, ci):
        return vrefs[m][0, pl.ds(pl.multiple_of(ci * tk, tk), tk), :]

    qm, qa = [], []
    for m in range(2):
        qm.append(jnp.where(keep[m], q, zero))
        mx = jnp.max(lax.dot_general(qm[m], kchunk(m, 0), dn, preferred_element_type=F32), axis=-1, keepdims=True)
        qa.append(jnp.where(stab[m], (-mx).astype(BF16), qm[m]))

    def fast(ci, acc):
        sc = [lax.dot_general(qa[m], kchunk(m, ci), dn, preferred_element_type=F32) for m in range(2)]
        return tuple(acc[m] + jnp.dot(jnp.exp2(sc[m]).astype(BF16), vchunk(m, ci), preferred_element_type=F32)
                     for m in range(2))

    acc = lax.fori_loop(0, nchunks, fast, tuple(jnp.zeros((tq, nv), F32) for _ in range(2)))
    nonfinite = jnp.float32(0.0)
    for m in range(2):
        acc_ref[m] = acc[m]
        nonfinite = nonfinite + jnp.sum(jnp.where(jnp.isfinite(acc[m]), 0.0, 1.0))

    @pl.when(nonfinite > 0.0)
    def _():
        def slow(ci, carry):
            out = []
            for m in range(2):
                m_i, a_i = carry[m]
                sc = lax.dot_general(qm[m], kchunk(m, ci), dn, preferred_element_type=F32)
                m_new = jnp.maximum(m_i, jnp.max(sc, axis=-1, keepdims=True))
                p = jnp.exp2(sc - m_new).astype(BF16)
                out.append((m_new, jnp.exp2(m_i - m_new) * a_i + jnp.dot(p, vchunk(m, ci), preferred_element_type=F32)))
            return tuple(out)

        init = tuple((jnp.full((tq, 1), -jnp.inf, F32), jnp.zeros((tq, nv), F32)) for _ in range(2))
        res = lax.fori_loop(0, nchunks, slow, init)
        for m in range(2):
            acc_ref[m] = res[m][1]

    if mode == "diff":
        o0 = acc_ref[0, :, :LANES] / acc_ref[0, :, LANES:LANES + 1]
        o1 = acc_ref[1, :, :LANES] / acc_ref[1, :, LANES:LANES + 1]
    else:
        o0 = acc_ref[0] / acc_ref[0, :, half:half + 1]
        o1 = acc_ref[1] / acc_ref[1, :, 0:1]
    if mode == "diff":
        lv = lam_ref[...]
        lam = (jnp.exp(jnp.sum(lv[0:1] * lv[1:2], axis=-1, keepdims=True))
               - jnp.exp(jnp.sum(lv[2:3] * lv[3:4], axis=-1, keepdims=True)) + lambda_init)
        o = o0 - lam * o1
        o = o * lax.rsqrt(jnp.mean(o * o, axis=-1, keepdims=True) + NORM_EPS) * nw_ref[...] * (1.0 - lambda_init)
    else:
        o = jnp.where(lo, o0, o1)
    o_ref[0] = o.astype(o_ref.dtype)


def _attention(q, k_arr, v_arr, lam_vecs, norm_w, *, mode, tq, tk, lambda_init=0.0):
    b, s, w = q.shape
    slabs = w // LANES
    if mode == "diff":
        nv = 2 * LANES
        k_col = lambda p, m: 2 * p + m
        v_col = lambda p, m: p
    else:
        nv = LANES
        k_col = v_col = lambda p, m: 2 * (p // 2) + m
    kspec = lambda m: pl.BlockSpec((1, s, LANES), lambda bi, p, i: (bi, 0, k_col(p, m)))
    vspec = lambda m: pl.BlockSpec((1, s, nv), lambda bi, p, i: (bi, 0, v_col(p, m)))
    return pl.pallas_call(
        functools.partial(_attn_kernel, mode=mode, tk=tk, lambda_init=lambda_init),
        out_shape=jax.ShapeDtypeStruct((b, s, w), BF16),
        grid=(b, slabs, s // tq),
        in_specs=[pl.BlockSpec((1, tq, LANES), lambda bi, p, i: (bi, i, p)),
                  kspec(0), kspec(1), vspec(0), vspec(1),
                  pl.BlockSpec((4, DIFF_DQK), lambda bi, p, i: (0, 0)),
                  pl.BlockSpec((1, LANES), lambda bi, p, i: (0, 0))],
        out_specs=pl.BlockSpec((1, tq, LANES), lambda bi, p, i: (bi, i, p)),
        scratch_shapes=[pltpu.VMEM((2, tq, nv), F32)],
        compiler_params=_cparams(("arbitrary", "arbitrary", "arbitrary")),
        name="attn_" + mode,
    )(q, k_arr, k_arr, v_arr, v_arr, lam_vecs, norm_w)


def _merge_kernel(of_ref, ob_ref, z_ref, g0_ref, g1_ref, g2_ref, yb_ref, yc_ref, x_ref,
                  wa_ref, wb_ref, wc_ref, wo_ref, gnw_ref, fnw_ref, rw_ref, rb_ref,
                  xo_ref, h_ref, id_ref, rwgt_ref):
    o = of_ref[...] + ob_ref[...]
    parts = []
    for h in range(GDN_HEADS):
        oh = o[:, h * LANES:(h + 1) * LANES]
        parts.append(oh * lax.rsqrt(jnp.mean(oh * oh, axis=-1, keepdims=True) + NORM_EPS) * gnw_ref[...])
    z = z_ref[...].astype(F32)
    ya = (jnp.concatenate(parts, axis=1) * (z * _sigmoid(z))).astype(BF16)
    merged = _sigmoid(g0_ref[...].astype(F32)) * jnp.dot(ya, wa_ref[...], preferred_element_type=F32)
    merged = merged + _sigmoid(g1_ref[...].astype(F32)) * jnp.dot(yb_ref[...], wb_ref[...], preferred_element_type=F32)
    merged = merged + _sigmoid(g2_ref[...].astype(F32)) * jnp.dot(yc_ref[...], wc_ref[...], preferred_element_type=F32)
    xn = x_ref[...] + jnp.dot(merged.astype(BF16), wo_ref[...], preferred_element_type=F32)
    xo_ref[...] = xn
    hf = xn * lax.rsqrt(jnp.mean(xn * xn, axis=-1, keepdims=True) + NORM_EPS) * fnw_ref[...]
    h_ref[...] = hf.astype(BF16)
    logits = jnp.dot(hf, rw_ref[...], precision=HI, preferred_element_type=F32) + rb_ref[...]
    lane = lax.broadcasted_iota(jnp.int32, logits.shape, 1)
    big = jnp.int32(LANES)
    ninf = -jnp.inf
    glog = jnp.where(lane < N_GROUPS, logits, ninf)
    gmax = jnp.max(glog, axis=-1, keepdims=True)
    gidx = jnp.min(jnp.where(glog == gmax, lane, big), axis=-1, keepdims=True)
    gp = 1.0 / jnp.sum(jnp.exp(glog - gmax), axis=-1, keepdims=True)
    e = lane - N_GROUPS
    sel = (e >= 0) & (e < N_EXPERTS) & ((e // EXPERTS_PER_GROUP) == gidx)
    elog = jnp.where(sel, logits, ninf)
    m1 = jnp.max(elog, axis=-1, keepdims=True)
    i1 = jnp.min(jnp.where(elog == m1, lane, big), axis=-1, keepdims=True)
    elog2 = jnp.where(lane == i1, ninf, elog)
    m2 = jnp.max(elog2, axis=-1, keepdims=True)
    i2 = jnp.min(jnp.where(elog2 == m2, lane, big), axis=-1, keepdims=True)
    e2 = jnp.exp(m2 - m1)
    w1 = 1.0 / (1.0 + e2)
    w2 = e2 * w1
    id_ref[...] = jnp.where(lane == 0, i1 - N_GROUPS, jnp.where(lane == 1, i2 - N_GROUPS, 0))
    rwgt_ref[...] = jnp.where(lane == 0, gp * w1, jnp.where(lane == 1, gp * w2, 0.0))


def _merge(o_f, o_b, main2, yb, yc, x2d, wa, wb, wc, wo, gnw, fnw, rw, rb, *, tm):
    t, d = x2d.shape
    full = lambda shp: pl.BlockSpec(shp, lambda i: tuple(0 for _ in shp))
    return pl.pallas_call(
        _merge_kernel,
        out_shape=(jax.ShapeDtypeStruct((t, d), F32), jax.ShapeDtypeStruct((t, d), BF16),
                   jax.ShapeDtypeStruct((t, LANES), jnp.int32), jax.ShapeDtypeStruct((t, LANES), F32)),
        grid=(t // tm,),
        in_specs=[pl.BlockSpec((tm, 512), lambda i: (i, 0)),
                  pl.BlockSpec((tm, 512), lambda i: (i, 0)),
                  pl.BlockSpec((tm, 512), lambda i: (i, COL_Z // 512)),
                  pl.BlockSpec((tm, d), lambda i: (i, 0)),
                  pl.BlockSpec((tm, d), lambda i: (i, 1)),
                  pl.BlockSpec((tm, d), lambda i: (i, 2)),
                  pl.BlockSpec((tm, 512), lambda i: (i, 0)),
                  pl.BlockSpec((tm, 512), lambda i: (i, 0)),
                  pl.BlockSpec((tm, d), lambda i: (i, 0)),
                  full((512, d)), full((512, d)), full((512, d)), full((d, d)),
                  full((1, LANES)), full((1, d)), full((d, LANES)), full((1, LANES))],
        out_specs=(pl.BlockSpec((tm, d), lambda i: (i, 0)), pl.BlockSpec((tm, d), lambda i: (i, 0)),
                   pl.BlockSpec((tm, LANES), lambda i: (i, 0)), pl.BlockSpec((tm, LANES), lambda i: (i, 0))),
        compiler_params=_cparams(("arbitrary",)),
        name="merge_router",
    )(o_f, o_b, main2, main2, main2, main2, yb, yc, x2d, wa, wb, wc, wo, gnw, fnw, rw, rb)


def _expert_kernel(blk_e_ref, nused_ref, x_ref, w1_ref, w3_ref, w2_ref, o_ref):
    i = pl.program_id(0)

    @pl.when(i < nused_ref[0])
    def _():
        x = x_ref[...]
        a = jnp.dot(x, w1_ref[0], preferred_element_type=F32)
        u = jnp.dot(x, w3_ref[0], preferred_element_type=F32)
        hmid = (a * _sigmoid(a) * u).astype(BF16)
        o_ref[...] = jnp.dot(hmid, w2_ref[0], preferred_element_type=F32).astype(o_ref.dtype)

    @pl.when(i >= nused_ref[0])
    def _():
        o_ref[...] = jnp.zeros_like(o_ref)


def _experts(blk_e, nused, xb, w1, w3, w2):
    p_len, d = xb.shape
    ff = w1.shape[2]
    nblk = p_len // MOE_BLOCK
    return pl.pallas_call(
        _expert_kernel,
        out_shape=jax.ShapeDtypeStruct((p_len, d), BF16),
        grid_spec=pltpu.PrefetchScalarGridSpec(
            num_scalar_prefetch=2,
            grid=(nblk,),
            in_specs=[pl.BlockSpec((MOE_BLOCK, d), lambda i, be, nu: (i, 0)),
                      pl.BlockSpec((1, d, ff), lambda i, be, nu: (be[i], 0, 0)),
                      pl.BlockSpec((1, d, ff), lambda i, be, nu: (be[i], 0, 0)),
                      pl.BlockSpec((1, ff, d), lambda i, be, nu: (be[i], 0, 0))],
            out_specs=pl.BlockSpec((MOE_BLOCK, d), lambda i, be, nu: (i, 0)),
        ),
        compiler_params=_cparams(("arbitrary",)),
        name="expert_mlp",
    )(blk_e, nused, xb, w1, w3, w2)


def _combine_kernel(x_ref, y0_ref, y1_ref, w_ref, nw_ref, o_ref, *, final):
    w = w_ref[...]
    x = x_ref[...] + w[:, 0:1] * y0_ref[...].astype(F32) + w[:, 1:2] * y1_ref[...].astype(F32)
    if final:
        x = x * lax.rsqrt(jnp.mean(x * x, axis=-1, keepdims=True) + NORM_EPS) * nw_ref[...]
    o_ref[...] = x


def _combine(x2d, y0, y1, wts, norm_w, *, final, tm):
    t, d = x2d.shape
    tile = pl.BlockSpec((tm, d), lambda i: (i, 0))
    return pl.pallas_call(
        functools.partial(_combine_kernel, final=final),
        out_shape=jax.ShapeDtypeStruct((t, d), F32),
        grid=(t // tm,),
        in_specs=[tile, tile, tile, pl.BlockSpec((tm, LANES), lambda i: (i, 0)), pl.BlockSpec((1, d), lambda i: (0, 0))],
        out_specs=tile,
        compiler_params=_cparams(("arbitrary",)),
        name="moe_combine",
    )(x2d, y0, y1, wts, norm_w.reshape(1, d))


def _rope_tables(pos, dim):
    inv = 1.0 / (ROPE_THETA ** (jnp.arange(0, dim, 2, dtype=F32) / dim))
    ang = pos.astype(F32)[:, None] * inv[None, :]
    ang = jnp.concatenate([ang, ang], axis=-1)
    return jnp.cos(ang), jnp.sin(ang)


def _signed_sin(sin):
    half = sin.shape[-1] // 2
    return jnp.concatenate([-sin[:, :half], sin[:, half:]], axis=-1)


def _layout_w_in(w):
    o = 0
    parts = {}
    for name, size in (("qkv", 1536), ("z", 512), ("b", 8), ("a", 8), ("dq", 512), ("dk", 512), ("dv", 512),
                       ("cq", 512), ("ck", 128), ("cv", 128), ("gate", 3072)):
        parts[name] = w[:, o:o + size]
        o += size
    swap = lambda m: jnp.concatenate([m[:, 64:], m[:, :64]], axis=1)
    main = jnp.concatenate([parts["gate"], parts["qkv"], parts["z"], parts["dq"], parts["dk"], parts["dv"],
                            parts["cq"], parts["ck"], swap(parts["ck"]), parts["cv"], swap(parts["cv"])], axis=1)
    ba = jnp.concatenate([parts["b"], parts["a"], jnp.zeros((w.shape[0], LANES - 16), w.dtype)], axis=1)
    return main.astype(BF16), ba


def _rows_layout(t, bsz, s):
    nc = s // GDN_CHUNK
    t = t.reshape(bsz, nc, GDN_CHUNK, 2, GDN_HEADS)
    return jnp.transpose(t, (3, 0, 1, 4, 2)).reshape(2, bsz, nc, GDN_ROWS)


def _moe_dispatch(ids, t):
    a = t * TOPK
    p_len = ((a + N_EXPERTS * (MOE_BLOCK - 1) + MOE_BLOCK - 1) // MOE_BLOCK) * MOE_BLOCK
    n_blocks = p_len // MOE_BLOCK
    flat_e = ids.reshape(-1)
    iota_a = jnp.arange(a, dtype=jnp.int32)
    skey = jnp.sort(flat_e * a + iota_a)
    order = skey % a
    se = skey // a
    experts = jnp.arange(N_EXPERTS, dtype=jnp.int32)
    counts = jnp.sum((flat_e[:, None] == experts[None, :]).astype(jnp.int32), axis=0)
    start = jnp.cumsum(counts) - counts
    pcounts = ((counts + MOE_BLOCK - 1) // MOE_BLOCK) * MOE_BLOCK
    pend = jnp.cumsum(pcounts)
    pstart = pend - pcounts
    dest_sorted = pstart[se] + (iota_a - start[se])
    blk_first = jnp.arange(n_blocks, dtype=jnp.int32) * MOE_BLOCK
    blk_e = jnp.minimum(jnp.sum((pend[None, :] <= blk_first[:, None]).astype(jnp.int32), axis=1), N_EXPERTS - 1)
    row = jnp.arange(p_len, dtype=jnp.int32)
    row_e = jnp.repeat(blk_e, MOE_BLOCK)
    j = row - pstart[row_e]
    valid = j < counts[row_e]
    tok_buf = jnp.where(valid, order[jnp.minimum(start[row_e] + j, a - 1)] // TOPK, t)
    _, dest = lax.sort((order, dest_sorted), num_keys=1)
    nused = (pend[-1] // MOE_BLOCK).astype(jnp.int32).reshape(1)
    return tok_buf, dest.reshape(t, TOPK), blk_e, nused


def kernel(x, attn_norm_w, w_in, gdn_conv_w, gdn_a_log, gdn_dt_bias, gdn_norm_w, diff_lambda, diff_norm_w,
           gqa_q_norm_w, gqa_k_norm_w, w_branch_a, w_branch_b, w_branch_c, w_out, ffn_norm_w,
           router_group_w, router_group_b, router_expert_w, router_expert_b,
           expert_w_gate, expert_w_up, expert_w_down, final_norm_w):
    bsz, s, d = x.shape
    t = bsz * s
    depth = w_in.shape[0]
    tm = min(512, t)
    ts = min(512, s)

    rows = s // GRID_W
    row = jnp.broadcast_to(jnp.arange(rows)[:, None], (rows, GRID_W)).reshape(s)
    col = jnp.broadcast_to(jnp.arange(GRID_W)[None, :], (rows, GRID_W)).reshape(s)
    c1, s1 = _rope_tables(jnp.arange(s), DIFF_DQK)
    cr, sr = _rope_tables(row, GQA_DH // 2)
    cc, sc = _rope_tables(col, GQA_DH // 2)
    cos1 = jnp.tile(c1, (1, 2))
    sin1 = jnp.tile(_signed_sin(s1), (1, 2))
    cos2 = jnp.tile(jnp.concatenate([cr, cc], axis=-1), (1, 2))
    sin2 = jnp.tile(jnp.concatenate([_signed_sin(sr), _signed_sin(sc)], axis=-1), (1, 2))

    x2 = x.reshape(t, d)
    for l in range(depth):
        lambda_init = 0.8 - 0.6 * math.exp(-0.3 * l)
        w_main, w_ba = _layout_w_in(w_in[l])
        main2 = _norm_proj(x2, attn_norm_w[l], w_main, BF16, exact=False, tm=tm, tn=1536)
        ba = _norm_proj(x2, attn_norm_w[l], w_ba, F32, exact=True, tm=tm, tn=LANES)
        main3 = main2.reshape(bsz, s, N_MAIN)

        conv_w = jnp.concatenate([gdn_conv_w[l], jnp.zeros((8 - GDN_CONV, gdn_conv_w.shape[2]), F32)], axis=0)
        qnw = jnp.tile(gqa_q_norm_w[l], 2).reshape(1, LANES)
        knw = jnp.tile(gqa_k_norm_w[l], 2).reshape(1, LANES)
        gq, gk, gv, dq, dk, dv, cq, ck, cv = _prep(main3, conv_w, cos1, sin1, cos2, sin2, qnw, knw, ts=ts)

        b_rows = _rows_layout(ba[:, 0:8], bsz, s)
        a_rows = _rows_layout(ba[:, 8:16], bsz, s)
        alog_row = jnp.repeat(gdn_a_log[l], GDN_CHUNK, axis=1).reshape(2, 1, GDN_ROWS)
        dtb_row = jnp.repeat(gdn_dt_bias[l], GDN_CHUNK, axis=1).reshape(2, 1, GDN_ROWS)
        o_f, o_b = _gdn(a_rows, b_rows, alog_row, dtb_row, gq, gk, gv, nbatch=GDN_NBATCH if bsz % GDN_NBATCH == 0 else 1)

        nw_diff = diff_norm_w[l].reshape(1, LANES)
        yb = _attention(dq, dk, dv, diff_lambda[l], nw_diff, mode="diff",
                        tq=min(ATTN_TQ, s), tk=min(ATTN_TK, s), lambda_init=lambda_init)
        yc = _attention(cq, ck, cv, diff_lambda[l], nw_diff, mode="gqa",
                        tq=min(ATTN_TQ, s), tk=min(ATTN_TK, s))

        rw = jnp.concatenate([router_group_w[l], router_expert_w[l],
                              jnp.zeros((d, LANES - N_GROUPS - N_EXPERTS), F32)], axis=1)
        rb = jnp.concatenate([router_group_b[l], router_expert_b[l],
                              jnp.zeros((LANES - N_GROUPS - N_EXPERTS,), F32)]).reshape(1, LANES)
        x2, h2, ids, wts = _merge(o_f.reshape(t, 512), o_b.reshape(t, 512), main2, yb.reshape(t, 512), yc.reshape(t, 512), x2,
                                  w_branch_a[l].astype(BF16), w_branch_b[l].astype(BF16),
                                  w_branch_c[l].astype(BF16), w_out[l].astype(BF16),
                                  gdn_norm_w[l].reshape(1, LANES), ffn_norm_w[l].reshape(1, d), rw, rb, tm=min(256, t))

        tok_buf, dest, blk_e, nused = _moe_dispatch(ids[:, :TOPK], t)
        h_pad = jnp.concatenate([h2, jnp.zeros((1, d), BF16)], axis=0)
        yblk = _experts(blk_e, nused, h_pad[tok_buf], expert_w_gate[l].astype(BF16),
                        expert_w_up[l].astype(BF16), expert_w_down[l].astype(BF16))
        x2 = _combine(x2, yblk[dest[:, 0]], yblk[dest[:, 1]], wts, final_norm_w, final=(l == depth - 1), tm=tm)

    return x2.reshape(bsz, s, d)
```

```python
import functools
import math

import jax
import jax.numpy as jnp
from jax import lax
from jax.experimental import pallas as pl
from jax.experimental.pallas import tpu as pltpu

GRID_W = 64
ROPE_THETA = 10000.0
NORM_EPS = 1e-6
GDN_HEADS = 4
GDN_DK = 128
GDN_DV = 128
GDN_CONV = 5
GDN_CHUNK = 64
DIFF_HEADS = 4
DIFF_DQK = 64
GQA_HEADS = 8
GQA_KV = 2
GQA_DH = 64
N_GROUPS = 4
EXPERTS_PER_GROUP = 8
N_EXPERTS = N_GROUPS * EXPERTS_PER_GROUP
TOPK = 2
MOE_BLOCK = 256

LANES = 128
VMEM_LIMIT = 56 * 1024 * 1024

COL_GATE = 0
COL_QKV = 3072
COL_Z = 4608
COL_DQ = 5120
COL_DK = 5632
COL_DV = 6144
COL_CQ = 6656
COL_CK = 7168
COL_CV = 7424
N_MAIN = 7680

LOG2E = math.log2(math.e)
ATTN_TQ = 512
ATTN_TK = 1024

HI = lax.Precision.HIGHEST
F32 = jnp.float32
BF16 = jnp.bfloat16


def _cparams(sem):
    return pltpu.CompilerParams(dimension_semantics=sem, vmem_limit_bytes=VMEM_LIMIT)


def _sigmoid(x):
    return 1.0 / (1.0 + jnp.exp(-x))


def _norm_proj_kernel(x_ref, nw_ref, w_ref, o_ref, *, exact):
    x = x_ref[...]
    h = x * lax.rsqrt(jnp.mean(x * x, axis=-1, keepdims=True) + NORM_EPS) * nw_ref[...]
    if exact:
        o_ref[...] = jnp.dot(h, w_ref[...], precision=HI, preferred_element_type=F32).astype(o_ref.dtype)
    else:
        o_ref[...] = jnp.dot(h.astype(BF16), w_ref[...], preferred_element_type=F32).astype(o_ref.dtype)


def _norm_proj(x2d, norm_w, w, out_dtype, *, exact, tm, tn):
    t, d = x2d.shape
    n = w.shape[1]
    return pl.pallas_call(
        functools.partial(_norm_proj_kernel, exact=exact),
        out_shape=jax.ShapeDtypeStruct((t, n), out_dtype),
        grid=(n // tn, t // tm),
        in_specs=[pl.BlockSpec((tm, d), lambda j, i: (i, 0)),
                  pl.BlockSpec((1, d), lambda j, i: (0, 0)),
                  pl.BlockSpec((d, tn), lambda j, i: (0, j))],
        out_specs=pl.BlockSpec((tm, tn), lambda j, i: (i, j)),
        compiler_params=_cparams(("arbitrary", "arbitrary")),
        name="norm_proj_exact" if exact else "norm_proj",
    )(x2d, norm_w.reshape(1, d), w)


HALO = 16


def _rot_half(x, half):
    lane = lax.broadcasted_iota(jnp.int32, x.shape, 1)
    first = (lane % (2 * half)) < half
    return jnp.where(first, pltpu.roll(x, LANES - half, 1), pltpu.roll(x, half, 1))


def _group_sumsq(x, width):
    x2 = x * x
    if width == LANES:
        return jnp.sum(x2, axis=-1, keepdims=True)
    lane = lax.broadcasted_iota(jnp.int32, x.shape, 1)
    lo = lane < width
    s_lo = jnp.sum(jnp.where(lo, x2, 0.0), axis=-1, keepdims=True)
    s_hi = jnp.sum(jnp.where(lo, 0.0, x2), axis=-1, keepdims=True)
    return jnp.where(lo, s_lo, s_hi)


def _aug_slab(x, m):
    lane = lax.broadcasted_iota(jnp.int32, x.shape, 1)
    half = LANES // 2
    keep = (lane < half) if m == 0 else (lane >= half)
    one = jnp.where(lane == (1 - m) * half, 1.0, 0.0).astype(x.dtype)
    return jnp.where(keep, x, one)


def _prep_kernel(qkv_ref, prev_ref, next_ref, dq_ref, dk_ref, dv_ref, cq_ref, ck_ref, cv_ref,
                 convw_ref, cos1_ref, sin1_ref, cos2_ref, sin2_ref, qnw_ref, knw_ref,
                 gq_ref, gk_ref, gv_ref, dqo_ref, dko_ref, dvo_ref, cqo_ref, cko_ref, cvo_ref, *, ts):
    i = pl.program_id(1)
    n = pl.num_programs(1)
    cur = qkv_ref[0].astype(F32)
    prev = jnp.where(i > 0, prev_ref[0].astype(F32), 0.0)
    nxt = jnp.where(i < n - 1, next_ref[0].astype(F32), 0.0)
    ext = jnp.concatenate([prev, cur, nxt], axis=0)
    pad = GDN_CONV // 2
    acc = jnp.zeros_like(cur)
    for j in range(GDN_CONV):
        off = HALO - pad + j
        acc = acc + ext[off:off + ts, :] * convw_ref[j:j + 1, :]
    act = acc * _sigmoid(acc)
    nqk = GDN_HEADS * GDN_DK
    for h in range(GDN_HEADS):
        sl = slice(h * GDN_DK, (h + 1) * GDN_DK)
        qh = act[:, sl]
        gq_ref[0, :, sl] = (qh * lax.rsqrt(_group_sumsq(qh, LANES) + NORM_EPS) * (GDN_DK ** -0.5)).astype(BF16)
        kh = act[:, nqk + h * GDN_DK: nqk + (h + 1) * GDN_DK]
        gk_ref[0, :, sl] = (kh * lax.rsqrt(_group_sumsq(kh, LANES) + NORM_EPS)).astype(BF16)
    gv_ref[0] = act[:, 2 * nqk:].astype(BF16)
    cos1, sin1 = cos1_ref[...], sin1_ref[...]
    lane = lax.broadcasted_iota(jnp.int32, (ts, LANES), 1)
    ones_col = jnp.where(lane == 0, 1.0, 0.0).astype(BF16)
    for p in range(DIFF_HEADS):
        sl = slice(p * LANES, (p + 1) * LANES)
        xq = dq_ref[0, :, sl].astype(F32)
        dqo_ref[0, :, sl] = ((xq * cos1 + _rot_half(xq, DIFF_DQK // 2) * sin1) * (DIFF_DQK ** -0.5 * LOG2E)).astype(BF16)
        xk = dk_ref[0, :, sl].astype(F32)
        xk = (xk * cos1 + _rot_half(xk, DIFF_DQK // 2) * sin1).astype(BF16)
        for m in range(2):
            dko_ref[0, :, (2 * p + m) * LANES:(2 * p + m + 1) * LANES] = _aug_slab(xk, m)
        dvo_ref[0, :, 2 * p * LANES:(2 * p + 1) * LANES] = dv_ref[0, :, sl]
        dvo_ref[0, :, (2 * p + 1) * LANES:(2 * p + 2) * LANES] = ones_col
    cos2, sin2 = cos2_ref[...], sin2_ref[...]
    for p in range(GQA_HEADS * GQA_DH // LANES):
        sl = slice(p * LANES, (p + 1) * LANES)
        xq = cq_ref[0, :, sl].astype(F32)
        xq = xq * lax.rsqrt(_group_sumsq(xq, GQA_DH) * (1.0 / GQA_DH) + NORM_EPS) * qnw_ref[...]
        cqo_ref[0, :, sl] = ((xq * cos2 + _rot_half(xq, GQA_DH // 4) * sin2) * (GQA_DH ** -0.5 * LOG2E)).astype(BF16)
    for p in range(2):
        sl = slice(p * LANES, (p + 1) * LANES)
        xk = ck_ref[0, :, sl].astype(F32)
        xk = xk * lax.rsqrt(_group_sumsq(xk, GQA_DH) * (1.0 / GQA_DH) + NORM_EPS) * knw_ref[...]
        xk = (xk * cos2 + _rot_half(xk, GQA_DH // 4) * sin2).astype(BF16)
        xv = cv_ref[0, :, sl]
        for m in range(2):
            c = p if m == 0 else 1 - p
            osl = slice((2 * c + m) * LANES, (2 * c + m + 1) * LANES)
            cko_ref[0, :, osl] = _aug_slab(xk, m)
            cvo_ref[0, :, osl] = _aug_slab(xv, m)


def _prep(main3, conv_w, cos1, sin1, cos2, sin2, qnw, knw, *, ts):
    b, s, _ = main3.shape
    nt = s // ts
    hb = ts // HALO
    last = s // HALO - 1
    row = lambda w: pl.BlockSpec((1, w), lambda bi, i: (0, 0))
    tab = pl.BlockSpec((ts, LANES), lambda bi, i: (i, 0))
    col = lambda w, off: pl.BlockSpec((1, ts, w), lambda bi, i: (bi, i, off // w))
    out = lambda w: pl.BlockSpec((1, ts, w), lambda bi, i: (bi, i, 0))
    widths = (512, 512, 512, 512, 1024, 1024, 512, 512, 512)
    return pl.pallas_call(
        functools.partial(_prep_kernel, ts=ts),
        out_shape=tuple(jax.ShapeDtypeStruct((b, s, w), BF16) for w in widths),
        grid=(b, nt),
        in_specs=[
            col(1536, COL_QKV),
            pl.BlockSpec((1, HALO, 1536), lambda bi, i: (bi, jnp.maximum(i * hb - 1, 0), COL_QKV // 1536)),
            pl.BlockSpec((1, HALO, 1536), lambda bi, i: (bi, jnp.minimum((i + 1) * hb, last), COL_QKV // 1536)),
            col(512, COL_DQ), col(512, COL_DK), col(512, COL_DV), col(512, COL_CQ), col(256, COL_CK), col(256, COL_CV),
            pl.BlockSpec((8, 1536), lambda bi, i: (0, 0)),
            tab, tab, tab, tab, row(LANES), row(LANES),
        ],
        out_specs=tuple(out(w) for w in widths),
        compiler_params=_cparams(("arbitrary", "arbitrary")),
        name="mixer_prep",
    )(*([main3] * 9), conv_w, cos1, sin1, cos2, sin2, qnw, knw)


GDN_G = 8
GDN_NBATCH = 2
GDN_ROWS = GDN_HEADS * GDN_CHUNK


def _stack_heads(x):
    return jnp.concatenate([x[:, h * LANES:(h + 1) * LANES] for h in range(GDN_HEADS)], axis=0)


def _row_to_col(row, eye):
    return jnp.sum(jnp.where(eye, row, 0.0), axis=1, keepdims=True)


def _gdn_kernel(af_ref, ab_ref, bf_ref, bb_ref, alog_ref, dtb_ref, qf_ref, kf_ref, vf_ref, qb_ref, kb_ref, vb_ref,
                of_ref, ob_ref, state_ref, gc_ref, gt_ref, beta_ref, *, nbatch):
    blk = pl.program_id(1)
    n = GDN_ROWS
    c = GDN_CHUNK

    @pl.when(blk == 0)
    def _():
        state_ref[...] = jnp.zeros_like(state_ref)

    ri = lax.broadcasted_iota(jnp.int32, (n, n), 0)
    ci = lax.broadcasted_iota(jnp.int32, (n, n), 1)
    same = (ri // c) == (ci // c)
    eye = ri == ci
    ti = lax.broadcasted_iota(jnp.int32, (n, GDN_HEADS * LANES), 0)
    tj = lax.broadcasted_iota(jnp.int32, (n, GDN_HEADS * LANES), 1)
    tot_m = jnp.where((ti // c) == (tj // LANES), 1.0, 0.0)

    chains = []
    for d, (a_ref, b_ref, q_ref, k_ref, v_ref, o_ref) in enumerate(
            ((af_ref, bf_ref, qf_ref, kf_ref, vf_ref, of_ref), (ab_ref, bb_ref, qb_ref, kb_ref, vb_ref, ob_ref))):
        sgn = 1 - 2 * d
        after = same & ((ri - ci) * sgn > 0)
        incl = same & ((ri - ci) * sgn >= 0)
        cum_m = jnp.where(same & ((ci - ri) * sgn >= 0), 1.0, 0.0)
        for bi in range(nbatch):
            ch = d * nbatch + bi
            x = a_ref[0, bi] + dtb_ref[d]
            softplus = jnp.maximum(x, 0.0) + jnp.log(1.0 + jnp.exp(-jnp.abs(x)))
            g = -jnp.exp(alog_ref[d]) * softplus
            beta_ref[ch] = _sigmoid(b_ref[0, bi])
            gc_ref[ch] = jnp.dot(g, cum_m, precision=HI, preferred_element_type=F32)
            gt_ref[ch] = jnp.dot(g, tot_m, precision=HI, preferred_element_type=F32)
            chains.append((ch, d, bi, after, incl, q_ref, k_ref, v_ref, o_ref))

    def chunk(j, chain):
        ch, d, bi, after, incl, q_ref, k_ref, v_ref, o_ref = chain
        cc = j if d == 0 else GDN_G - 1 - j
        r0 = pl.multiple_of(cc * c, c)
        gc_row = gc_ref[ch, pl.ds(cc, 1), :]
        beta_row = beta_ref[ch, pl.ds(cc, 1), :]
        gt_row = gt_ref[ch, pl.ds(cc, 1), :]
        gc_col = _row_to_col(gc_row, eye)
        beta_col = _row_to_col(beta_row, eye)
        k_st = _stack_heads(k_ref[bi, pl.ds(r0, c), :]).astype(F32)
        q_st = _stack_heads(q_ref[bi, pl.ds(r0, c), :]).astype(F32)
        v_st = _stack_heads(v_ref[bi, pl.ds(r0, c), :]).astype(F32)
        egc = jnp.exp(gc_col)
        decay = jnp.exp(jnp.minimum(gc_col - gc_row, 0.0))
        kb = k_st * beta_col
        k_bf = k_st.astype(BF16)
        kk = lax.dot_general(kb.astype(BF16), k_bf, (((1,), (1,)), ((), ())), preferred_element_type=F32)
        qk = lax.dot_general(q_st.astype(BF16), k_bf, (((1,), (1,)), ((), ())), preferred_element_type=F32)
        yield
        neg_a = jnp.where(after, -(kk * decay), 0.0)
        t_m = jnp.where(eye, 1.0, 0.0) + neg_a
        p_m = neg_a
        for _ in range(int(math.log2(c)) - 1):
            p_bf = p_m.astype(BF16)
            p_m = jnp.dot(p_bf, p_bf, preferred_element_type=F32)
            yield
            t_m = t_m + jnp.dot(t_m.astype(BF16), p_m.astype(BF16), preferred_element_type=F32)
            yield
        rhs = jnp.concatenate([v_st * beta_col, kb * egc], axis=1).astype(BF16)
        sol = jnp.dot(t_m.astype(BF16), rhs, preferred_element_type=F32)
        yield
        u_st, w_st = sol[:, :LANES], sol[:, LANES:]
        intra = jnp.where(incl, qk * decay, 0.0).astype(BF16)
        q_dec = (q_st * egc).astype(BF16)
        vn, oq = [], []
        for h in range(GDN_HEADS):
            rs = slice(h * c, (h + 1) * c)
            s_h = state_ref[ch * GDN_HEADS + h].astype(BF16)
            vn.append(u_st[rs] - jnp.dot(w_st[rs].astype(BF16), s_h, preferred_element_type=F32))
            oq.append(jnp.dot(q_dec[rs], s_h, preferred_element_type=F32))
        yield
        vn_st = jnp.concatenate(vn, axis=0)
        o_st = jnp.concatenate(oq, axis=0) + jnp.dot(intra, vn_st.astype(BF16), preferred_element_type=F32)
        for h in range(GDN_HEADS):
            rs = slice(h * c, (h + 1) * c)
            gt_h = gt_row[:, h * LANES:(h + 1) * LANES]
            k_dec = (k_st[rs] * jnp.exp(gt_h[:, :1] - gc_col[rs])).astype(BF16)
            upd = lax.dot_general(k_dec, vn[h].astype(BF16), (((0,), (0,)), ((), ())), preferred_element_type=F32)
            state_ref[ch * GDN_HEADS + h] = state_ref[ch * GDN_HEADS + h] * jnp.exp(gt_h) + upd
            o_ref[bi, pl.ds(r0, c), h * LANES:(h + 1) * LANES] = o_st[rs]

    def step(j, carry):
        active = [chunk(j, chain) for chain in chains]
        while active:
            active = [g for g in active if next(g, active) is not active]
        return carry

    lax.fori_loop(0, GDN_G, step, 0)


def _gdn(a_rows, b_rows, alog_row, dtb_row, gq, gk, gv, *, nbatch):
    b, s, _ = gq.shape
    nb = s // (GDN_G * GDN_CHUNK)
    ts = GDN_G * GDN_CHUNK
    nchain = 2 * nbatch
    tok_f = pl.BlockSpec((nbatch, ts, 512), lambda bi, i: (bi, i, 0))
    tok_b = pl.BlockSpec((nbatch, ts, 512), lambda bi, i: (bi, nb - 1 - i, 0))
    rows_f = pl.BlockSpec((1, nbatch, GDN_G, GDN_ROWS), lambda bi, i: (0, bi, i, 0))
    rows_b = pl.BlockSpec((1, nbatch, GDN_G, GDN_ROWS), lambda bi, i: (1, bi, nb - 1 - i, 0))
    par = pl.BlockSpec((2, 1, GDN_ROWS), lambda bi, i: (0, 0, 0))
    return pl.pallas_call(
        functools.partial(_gdn_kernel, nbatch=nbatch),
        out_shape=(jax.ShapeDtypeStruct((b, s, 512), F32), jax.ShapeDtypeStruct((b, s, 512), F32)),
        grid=(b // nbatch, nb),
        in_specs=[rows_f, rows_b, rows_f, rows_b, par, par, tok_f, tok_f, tok_f, tok_b, tok_b, tok_b],
        out_specs=(tok_f, tok_b),
        scratch_shapes=[pltpu.VMEM((nchain * GDN_HEADS, GDN_DK, GDN_DV), F32),
                        pltpu.VMEM((nchain, GDN_G, GDN_ROWS), F32),
                        pltpu.VMEM((nchain, GDN_G, GDN_HEADS * LANES), F32),
                        pltpu.VMEM((nchain, GDN_G, GDN_ROWS), F32)],
        compiler_params=_cparams(("arbitrary", "arbitrary")),
        name="gdn_chunked",
    )(a_rows, a_rows, b_rows, b_rows, alog_row, dtb_row, gq, gk, gv, gq, gk, gv)


def _attn_kernel(q_ref, k0_ref, k1_ref, v0_ref, v1_ref, lam_ref, nw_ref, o_ref, acc_ref, *, mode, tk, lambda_init):
    s_len = k0_ref.shape[1]
    tq = q_ref.shape[1]
    nv = v0_ref.shape[2]
    half = LANES // 2
    q = q_ref[0]
    lane = lax.broadcasted_iota(jnp.int32, q.shape, 1)
    lo = lane < half
    keep = (lo, lane >= half)
    stab = (lane == half, lane == 0)
    zero = jnp.zeros_like(q)
    krefs = (k0_ref, k1_ref)
    vrefs = (v0_ref, v1_ref)
    nchunks = s_len // tk
    dn = (((1,), (1,)), ((), ()))

    def kchunk(m, ci):
        return krefs[m][0, pl.ds(pl.multiple_of(ci * tk, tk), tk), :]

    def vchunk(m, ci):
        return vrefs[m][0, pl.ds(pl.multiple_of(ci * tk, tk), tk), :]

    qm, qa = [], []
    for m in range(2):
        qm.append(jnp.where(keep[m], q, zero))
        mx = jnp.max(lax.dot_general(qm[m], kchunk(m, 0), dn, preferred_element_type=F32), axis=-1, keepdims=True)
        qa.append(jnp.where(stab[m], (-mx).astype(BF16), qm[m]))

    def fast(ci, acc):
        sc = [lax.dot_general(qa[m], kchunk(m, ci), dn, preferred_element_type=F32) for m in range(2)]
        return tuple(acc[m] + jnp.dot(jnp.exp2(sc[m]).astype(BF16), vchunk(m, ci), preferred_element_type=F32)
                     for m in range(2))

    acc = lax.fori_loop(0, nchunks, fast, tuple(jnp.zeros((tq, nv), F32) for _ in range(2)))
    nonfinite = jnp.float32(0.0)
    for m in range(2):
        acc_ref[m] = acc[m]
        nonfinite = nonfinite + jnp.sum(jnp.where(jnp.isfinite(acc[m]), 0.0, 1.0))

    @pl.when(nonfinite > 0.0)
    def _():
        def slow(ci, carry):
            out = []
            for m in range(2):
                m_i, a_i = carry[m]
                sc = lax.dot_general(qm[m], kchunk(m, ci), dn, preferred_element_type=F32)
                m_new = jnp.maximum(m_i, jnp.max(sc, axis=-1, keepdims=True))
                p = jnp.exp2(sc - m_new).astype(BF16)
                out.append((m_new, jnp.exp2(m_i - m_new) * a_i + jnp.dot(p, vchunk(m, ci), preferred_element_type=F32)))
            return tuple(out)

        init = tuple((jnp.full((tq, 1), -jnp.inf, F32), jnp.zeros((tq, nv), F32)) for _ in range(2))
        res = lax.fori_loop(0, nchunks, slow, init)
        for m in range(2):
            acc_ref[m] = res[m][1]

    if mode == "diff":
        o0 = acc_ref[0, :, :LANES] / acc_ref[0, :, LANES:LANES + 1]
        o1 = acc_ref[1, :, :LANES] / acc_ref[1, :, LANES:LANES + 1]
    else:
        o0 = acc_ref[0] / acc_ref[0, :, half:half + 1]
        o1 = acc_ref[1] / acc_ref[1, :, 0:1]
    if mode == "diff":
        lv = lam_ref[...]
        lam = (jnp.exp(jnp.sum(lv[0:1] * lv[1:2], axis=-1, keepdims=True))
               - jnp.exp(jnp.sum(lv[2:3] * lv[3:4], axis=-1, keepdims=True)) + lambda_init)
        o = o0 - lam * o1
        o = o * lax.rsqrt(jnp.mean(o * o, axis=-1, keepdims=True) + NORM_EPS) * nw_ref[...] * (1.0 - lambda_init)
    else:
        o = jnp.where(lo, o0, o1)
    o_ref[0] = o.astype(o_ref.dtype)


def _attention(q, k_arr, v_arr, lam_vecs, norm_w, *, mode, tq, tk, lambda_init=0.0):
    b, s, w = q.shape
    slabs = w // LANES
    if mode == "diff":
        nv = 2 * LANES
        k_col = lambda p, m: 2 * p + m
        v_col = lambda p, m: p
    else:
        nv = LANES
        k_col = v_col = lambda p, m: 2 * (p // 2) + m
    kspec = lambda m: pl.BlockSpec((1, s, LANES), lambda bi, p, i: (bi, 0, k_col(p, m)))
    vspec = lambda m: pl.BlockSpec((1, s, nv), lambda bi, p, i: (bi, 0, v_col(p, m)))
    return pl.pallas_call(
        functools.partial(_attn_kernel, mode=mode, tk=tk, lambda_init=lambda_init),
        out_shape=jax.ShapeDtypeStruct((b, s, w), BF16),
        grid=(b, slabs, s // tq),
        in_specs=[pl.BlockSpec((1, tq, LANES), lambda bi, p, i: (bi, i, p)),
                  kspec(0), kspec(1), vspec(0), vspec(1),
                  pl.BlockSpec((4, DIFF_DQK), lambda bi, p, i: (0, 0)),
                  pl.BlockSpec((1, LANES), lambda bi, p, i: (0, 0))],
        out_specs=pl.BlockSpec((1, tq, LANES), lambda bi, p, i: (bi, i, p)),
        scratch_shapes=[pltpu.VMEM((2, tq, nv), F32)],
        compiler_params=_cparams(("arbitrary", "arbitrary", "arbitrary")),
        name="attn_" + mode,
    )(q, k_arr, k_arr, v_arr, v_arr, lam_vecs, norm_w)


def _merge_kernel(of_ref, ob_ref, z_ref, g0_ref, g1_ref, g2_ref, yb_ref, yc_ref, x_ref,
                  wa_ref, wb_ref, wc_ref, wo_ref, gnw_ref, fnw_ref, rw_ref, rb_ref,
                  xo_ref, h_ref, id_ref, rwgt_ref):
    o = of_ref[...] + ob_ref[...]
    parts = []
    for h in range(GDN_HEADS):
        oh = o[:, h * LANES:(h + 1) * LANES]
        parts.append(oh * lax.rsqrt(jnp.mean(oh * oh, axis=-1, keepdims=True) + NORM_EPS) * gnw_ref[...])
    z = z_ref[...].astype(F32)
    ya = (jnp.concatenate(parts, axis=1) * (z * _sigmoid(z))).astype(BF16)
    merged = _sigmoid(g0_ref[...].astype(F32)) * jnp.dot(ya, wa_ref[...], preferred_element_type=F32)
    merged = merged + _sigmoid(g1_ref[...].astype(F32)) * jnp.dot(yb_ref[...], wb_ref[...], preferred_element_type=F32)
    merged = merged + _sigmoid(g2_ref[...].astype(F32)) * jnp.dot(yc_ref[...], wc_ref[...], preferred_element_type=F32)
    xn = x_ref[...] + jnp.dot(merged.astype(BF16), wo_ref[...], preferred_element_type=F32)
    xo_ref[...] = xn
    hf = xn * lax.rsqrt(jnp.mean(xn * xn, axis=-1, keepdims=True) + NORM_EPS) * fnw_ref[...]
    h_ref[...] = hf.astype(BF16)
    logits = jnp.dot(hf, rw_ref[...], precision=HI, preferred_element_type=F32) + rb_ref[...]
    lane = lax.broadcasted_iota(jnp.int32, logits.shape, 1)
    big = jnp.int32(LANES)
    ninf = -jnp.inf
    glog = jnp.where(lane < N_GROUPS, logits, ninf)
    gmax = jnp.max(glog, axis=-1, keepdims=True)
    gidx = jnp.min(jnp.where(glog == gmax, lane, big), axis=-1, keepdims=True)
    gp = 1.0 / jnp.sum(jnp.exp(glog - gmax), axis=-1, keepdims=True)
    e = lane - N_GROUPS
    sel = (e >= 0) & (e < N_EXPERTS) & ((e // EXPERTS_PER_GROUP) == gidx)
    elog = jnp.where(sel, logits, ninf)
    m1 = jnp.max(elog, axis=-1, keepdims=True)
    i1 = jnp.min(jnp.where(elog == m1, lane, big), axis=-1, keepdims=True)
    elog2 = jnp.where(lane == i1, ninf, elog)
    m2 = jnp.max(elog2, axis=-1, keepdims=True)
    i2 = jnp.min(jnp.where(elog2 == m2, lane, big), axis=-1, keepdims=True)
    e2 = jnp.exp(m2 - m1)
    w1 = 1.0 / (1.0 + e2)
    w2 = e2 * w1
    id_ref[...] = jnp.where(lane == 0, i1 - N_GROUPS, jnp.where(lane == 1, i2 - N_GROUPS, 0))
    rwgt_ref[...] = jnp.where(lane == 0, gp * w1, jnp.where(lane == 1, gp * w2, 0.0))


def _merge(o_f, o_b, main2, yb, yc, x2d, wa, wb, wc, wo, gnw, fnw, rw, rb, *, tm):
    t, d = x2d.shape
    full = lambda shp: pl.BlockSpec(shp, lambda i: tuple(0 for _ in shp))
    return pl.pallas_call(
        _merge_kernel,
        out_shape=(jax.ShapeDtypeStruct((t, d), F32), jax.ShapeDtypeStruct((t, d), BF16),
                   jax.ShapeDtypeStruct((t, LANES), jnp.int32), jax.ShapeDtypeStruct((t, LANES), F32)),
        grid=(t // tm,),
        in_specs=[pl.BlockSpec((tm, 512), lambda i: (i, 0)),
                  pl.BlockSpec((tm, 512), lambda i: (i, 0)),
                  pl.BlockSpec((tm, 512), lambda i: (i, COL_Z // 512)),
                  pl.BlockSpec((tm, d), lambda i: (i, 0)),
                  pl.BlockSpec((tm, d), lambda i: (i, 1)),
                  pl.BlockSpec((tm, d), lambda i: (i, 2)),
                  pl.BlockSpec((tm, 512), lambda i: (i, 0)),
                  pl.BlockSpec((tm, 512), lambda i: (i, 0)),
                  pl.BlockSpec((tm, d), lambda i: (i, 0)),
                  full((512, d)), full((512, d)), full((512, d)), full((d, d)),
                  full((1, LANES)), full((1, d)), full((d, LANES)), full((1, LANES))],
        out_specs=(pl.BlockSpec((tm, d), lambda i: (i, 0)), pl.BlockSpec((tm, d), lambda i: (i, 0)),
                   pl.BlockSpec((tm, LANES), lambda i: (i, 0)), pl.BlockSpec((tm, LANES), lambda i: (i, 0))),
        compiler_params=_cparams(("arbitrary",)),
        name="merge_router",
    )(o_f, o_b, main2, main2, main2, main2, yb, yc, x2d, wa, wb, wc, wo, gnw, fnw, rw, rb)


def _expert_kernel(blk_e_ref, nused_ref, x_ref, w1_ref, w3_ref, w2_ref, o_ref):
    i = pl.program_id(0)

    @pl.when(i < nused_ref[0])
    def _():
        x = x_ref[...]
        a = jnp.dot(x, w1_ref[0], preferred_element_type=F32)
        u = jnp.dot(x, w3_ref[0], preferred_element_type=F32)
        hmid = (a * _sigmoid(a) * u).astype(BF16)
        o_ref[...] = jnp.dot(hmid, w2_ref[0], preferred_element_type=F32).astype(o_ref.dtype)

    @pl.when(i >= nused_ref[0])
    def _():
        o_ref[...] = jnp.zeros_like(o_ref)


def _experts(blk_e, nused, xb, w1, w3, w2):
    p_len, d = xb.shape
    ff = w1.shape[2]
    nblk = p_len // MOE_BLOCK
    return pl.pallas_call(
        _expert_kernel,
        out_shape=jax.ShapeDtypeStruct((p_len, d), BF16),
        grid_spec=pltpu.PrefetchScalarGridSpec(
            num_scalar_prefetch=2,
            grid=(nblk,),
            in_specs=[pl.BlockSpec((MOE_BLOCK, d), lambda i, be, nu: (i, 0)),
                      pl.BlockSpec((1, d, ff), lambda i, be, nu: (be[i], 0, 0)),
                      pl.BlockSpec((1, d, ff), lambda i, be, nu: (be[i], 0, 0)),
                      pl.BlockSpec((1, ff, d), lambda i, be, nu: (be[i], 0, 0))],
            out_specs=pl.BlockSpec((MOE_BLOCK, d), lambda i, be, nu: (i, 0)),
        ),
        compiler_params=_cparams(("arbitrary",)),
        name="expert_mlp",
    )(blk_e, nused, xb, w1, w3, w2)


def _combine_kernel(x_ref, y0_ref, y1_ref, w_ref, nw_ref, o_ref, *, final):
    w = w_ref[...]
    x = x_ref[...] + w[:, 0:1] * y0_ref[...].astype(F32) + w[:, 1:2] * y1_ref[...].astype(F32)
    if final:
        x = x * lax.rsqrt(jnp.mean(x * x, axis=-1, keepdims=True) + NORM_EPS) * nw_ref[...]
    o_ref[...] = x


def _combine(x2d, y0, y1, wts, norm_w, *, final, tm):
    t, d = x2d.shape
    tile = pl.BlockSpec((tm, d), lambda i: (i, 0))
    return pl.pallas_call(
        functools.partial(_combine_kernel, final=final),
        out_shape=jax.ShapeDtypeStruct((t, d), F32),
        grid=(t // tm,),
        in_specs=[tile, tile, tile, pl.BlockSpec((tm, LANES), lambda i: (i, 0)), pl.BlockSpec((1, d), lambda i: (0, 0))],
        out_specs=tile,
        compiler_params=_cparams(("arbitrary",)),
        name="moe_combine",
    )(x2d, y0, y1, wts, norm_w.reshape(1, d))


def _rope_tables(pos, dim):
    inv = 1.0 / (ROPE_THETA ** (jnp.arange(0, dim, 2, dtype=F32) / dim))
    ang = pos.astype(F32)[:, None] * inv[None, :]
    ang = jnp.concatenate([ang, ang], axis=-1)
    return jnp.cos(ang), jnp.sin(ang)


def _signed_sin(sin):
    half = sin.shape[-1] // 2
    return jnp.concatenate([-sin[:, :half], sin[:, half:]], axis=-1)


def _layout_w_in(w):
    o = 0
    parts = {}
    for name, size in (("qkv", 1536), ("z", 512), ("b", 8), ("a", 8), ("dq", 512), ("dk", 512), ("dv", 512),
                       ("cq", 512), ("ck", 128), ("cv", 128), ("gate", 3072)):
        parts[name] = w[:, o:o + size]
        o += size
    swap = lambda m: jnp.concatenate([m[:, 64:], m[:, :64]], axis=1)
    main = jnp.concatenate([parts["gate"], parts["qkv"], parts["z"], parts["dq"], parts["dk"], parts["dv"],
                            parts["cq"], parts["ck"], swap(parts["ck"]), parts["cv"], swap(parts["cv"])], axis=1)
    ba = jnp.concatenate([parts["b"], parts["a"], jnp.zeros((w.shape[0], LANES - 16), w.dtype)], axis=1)
    return main.astype(BF16), ba


def _rows_layout(t, bsz, s):
    nc = s // GDN_CHUNK
    t = t.reshape(bsz, nc, GDN_CHUNK, 2, GDN_HEADS)
    return jnp.transpose(t, (3, 0, 1, 4, 2)).reshape(2, bsz, nc, GDN_ROWS)


def _moe_dispatch(ids, t):
    a = t * TOPK
    p_len = ((a + N_EXPERTS * (MOE_BLOCK - 1) + MOE_BLOCK - 1) // MOE_BLOCK) * MOE_BLOCK
    n_blocks = p_len // MOE_BLOCK
    flat_e = ids.reshape(-1)
    iota_a = jnp.arange(a, dtype=jnp.int32)
    skey = jnp.sort(flat_e * a + iota_a)
    order = skey % a
    se = skey // a
    experts = jnp.arange(N_EXPERTS, dtype=jnp.int32)
    counts = jnp.sum((flat_e[:, None] == experts[None, :]).astype(jnp.int32), axis=0)
    start = jnp.cumsum(counts) - counts
    pcounts = ((counts + MOE_BLOCK - 1) // MOE_BLOCK) * MOE_BLOCK
    pend = jnp.cumsum(pcounts)
    pstart = pend - pcounts
    dest_sorted = pstart[se] + (iota_a - start[se])
    blk_first = jnp.arange(n_blocks, dtype=jnp.int32) * MOE_BLOCK
    blk_e = jnp.minimum(jnp.sum((pend[None, :] <= blk_first[:, None]).astype(jnp.int32), axis=1), N_EXPERTS - 1)
    row = jnp.arange(p_len, dtype=jnp.int32)
    row_e = jnp.repeat(blk_e, MOE_BLOCK)
    j = row - pstart[row_e]
    valid = j < counts[row_e]
    tok_buf = jnp.where(valid, order[jnp.minimum(start[row_e] + j, a - 1)] // TOPK, t)
    _, dest = lax.sort((order, dest_sorted), num_keys=1)
    nused = (pend[-1] // MOE_BLOCK).astype(jnp.int32).reshape(1)
    return tok_buf, dest.reshape(t, TOPK), blk_e, nused


def kernel(x, attn_norm_w, w_in, gdn_conv_w, gdn_a_log, gdn_dt_bias, gdn_norm_w, diff_lambda, diff_norm_w,
           gqa_q_norm_w, gqa_k_norm_w, w_branch_a, w_branch_b, w_branch_c, w_out, ffn_norm_w,
           router_group_w, router_group_b, router_expert_w, router_expert_b,
           expert_w_gate, expert_w_up, expert_w_down, final_norm_w):
    bsz, s, d = x.shape
    t = bsz * s
    depth = w_in.shape[0]
    tm = min(512, t)
    ts = min(512, s)

    rows = s // GRID_W
    row = jnp.broadcast_to(jnp.arange(rows)[:, None], (rows, GRID_W)).reshape(s)
    col = jnp.broadcast_to(jnp.arange(GRID_W)[None, :], (rows, GRID_W)).reshape(s)
    c1, s1 = _rope_tables(jnp.arange(s), DIFF_DQK)
    cr, sr = _rope_tables(row, GQA_DH // 2)
    cc, sc = _rope_tables(col, GQA_DH // 2)
    cos1 = jnp.tile(c1, (1, 2))
    sin1 = jnp.tile(_signed_sin(s1), (1, 2))
    cos2 = jnp.tile(jnp.concatenate([cr, cc], axis=-1), (1, 2))
    sin2 = jnp.tile(jnp.concatenate([_signed_sin(sr), _signed_sin(sc)], axis=-1), (1, 2))

    x2 = x.reshape(t, d)
    for l in range(depth):
        lambda_init = 0.8 - 0.6 * math.exp(-0.3 * l)
        w_main, w_ba = _layout_w_in(w_in[l])
        main2 = _norm_proj(x2, attn_norm_w[l], w_main, BF16, exact=False, tm=tm, tn=1536)
        ba = _norm_proj(x2, attn_norm_w[l], w_ba, F32, exact=True, tm=tm, tn=LANES)
        main3 = main2.reshape(bsz, s, N_MAIN)

        conv_w = jnp.concatenate([gdn_conv_w[l], jnp.zeros((8 - GDN_CONV, gdn_conv_w.shape[2]), F32)], axis=0)
        qnw = jnp.tile(gqa_q_norm_w[l], 2).reshape(1, LANES)
        knw = jnp.tile(gqa_k_norm_w[l], 2).reshape(1, LANES)
        gq, gk, gv, dq, dk, dv, cq, ck, cv = _prep(main3, conv_w, cos1, sin1, cos2, sin2, qnw, knw, ts=ts)

        b_rows = _rows_layout(ba[:, 0:8], bsz, s)
        a_rows = _rows_layout(ba[:, 8:16], bsz, s)
        alog_row = jnp.repeat(gdn_a_log[l], GDN_CHUNK, axis=1).reshape(2, 1, GDN_ROWS)
        dtb_row = jnp.repeat(gdn_dt_bias[l], GDN_CHUNK, axis=1).reshape(2, 1, GDN_ROWS)
        o_f, o_b = _gdn(a_rows, b_rows, alog_row, dtb_row, gq, gk, gv, nbatch=GDN_NBATCH if bsz % GDN_NBATCH == 0 else 1)

        nw_diff = diff_norm_w[l].reshape(1, LANES)
        yb = _attention(dq, dk, dv, diff_lambda[l], nw_diff, mode="diff",
                        tq=min(ATTN_TQ, s), tk=min(ATTN_TK, s), lambda_init=lambda_init)
        yc = _attention(cq, ck, cv, diff_lambda[l], nw_diff, mode="gqa",
                        tq=min(ATTN_TQ, s), tk=min(ATTN_TK, s))

        rw = jnp.concatenate([router_group_w[l], router_expert_w[l],
                              jnp.zeros((d, LANES - N_GROUPS - N_EXPERTS), F32)], axis=1)
        rb = jnp.concatenate([router_group_b[l], router_expert_b[l],
                              jnp.zeros((LANES - N_GROUPS - N_EXPERTS,), F32)]).reshape(1, LANES)
        x2, h2, ids, wts = _merge(o_f.reshape(t, 512), o_b.reshape(t, 512), main2, yb.reshape(t, 512), yc.reshape(t, 512), x2,
                                  w_branch_a[l].astype(BF16), w_branch_b[l].astype(BF16),
                                  w_branch_c[l].astype(BF16), w_out[l].astype(BF16),
                                  gdn_norm_w[l].reshape(1, LANES), ffn_norm_w[l].reshape(1, d), rw, rb, tm=min(256, t))

        tok_buf, dest, blk_e, nused = _moe_dispatch(ids[:, :TOPK], t)
        h_pad = jnp.concatenate([h2, jnp.zeros((1, d), BF16)], axis=0)
        yblk = _experts(blk_e, nused, h_pad[tok_buf], expert_w_gate[l].astype(BF16),
                        expert_w_up[l].astype(BF16), expert_w_down[l].astype(BF16))
        x2 = _combine(x2, yblk[dest[:, 0]], yblk[dest[:, 1]], wts, final_norm_w, final=(l == depth - 1), tm=tm)

    return x2.reshape(bsz, s, d)
```

```python
import functools
import math

import jax
import jax.numpy as jnp
from jax import lax
from jax.experimental import pallas as pl
from jax.experimental.pallas import tpu as pltpu

GRID_W = 64
ROPE_THETA = 10000.0
NORM_EPS = 1e-6
GDN_HEADS = 4
GDN_DK = 128
GDN_DV = 128
GDN_CONV = 5
GDN_CHUNK = 64
DIFF_HEADS = 4
DIFF_DQK = 64
GQA_HEADS = 8
GQA_KV = 2
GQA_DH = 64
N_GROUPS = 4
EXPERTS_PER_GROUP = 8
N_EXPERTS = N_GROUPS * EXPERTS_PER_GROUP
TOPK = 2
MOE_BLOCK = 256

LANES = 128
VMEM_LIMIT = 56 * 1024 * 1024

COL_GATE = 0
COL_QKV = 3072
COL_Z = 4608
COL_DQ = 5120
COL_DK = 5632
COL_DV = 6144
COL_CQ = 6656
COL_CK = 7168
COL_CV = 7424
N_MAIN = 7680

LOG2E = math.log2(math.e)
ATTN_TQ = 512
ATTN_TK = 2048

HI = lax.Precision.HIGHEST
F32 = jnp.float32
BF16 = jnp.bfloat16


def _cparams(sem):
    return pltpu.CompilerParams(dimension_semantics=sem, vmem_limit_bytes=VMEM_LIMIT)


def _sigmoid(x):
    return 1.0 / (1.0 + jnp.exp(-x))


def _norm_proj_kernel(x_ref, nw_ref, w_ref, o_ref, *, exact):
    x = x_ref[...]
    h = x * lax.rsqrt(jnp.mean(x * x, axis=-1, keepdims=True) + NORM_EPS) * nw_ref[...]
    if exact:
        o_ref[...] = jnp.dot(h, w_ref[...], precision=HI, preferred_element_type=F32).astype(o_ref.dtype)
    else:
        o_ref[...] = jnp.dot(h.astype(BF16), w_ref[...], preferred_element_type=F32).astype(o_ref.dtype)


def _norm_proj(x2d, norm_w, w, out_dtype, *, exact, tm, tn):
    t, d = x2d.shape
    n = w.shape[1]
    return pl.pallas_call(
        functools.partial(_norm_proj_kernel, exact=exact),
        out_shape=jax.ShapeDtypeStruct((t, n), out_dtype),
        grid=(n // tn, t // tm),
        in_specs=[pl.BlockSpec((tm, d), lambda j, i: (i, 0)),
                  pl.BlockSpec((1, d), lambda j, i: (0, 0)),
                  pl.BlockSpec((d, tn), lambda j, i: (0, j))],
        out_specs=pl.BlockSpec((tm, tn), lambda j, i: (i, j)),
        compiler_params=_cparams(("arbitrary", "arbitrary")),
        name="norm_proj_exact" if exact else "norm_proj",
    )(x2d, norm_w.reshape(1, d), w)


HALO = 16


def _rot_half(x, half):
    lane = lax.broadcasted_iota(jnp.int32, x.shape, 1)
    first = (lane % (2 * half)) < half
    return jnp.where(first, pltpu.roll(x, LANES - half, 1), pltpu.roll(x, half, 1))


def _group_sumsq(x, width):
    x2 = x * x
    if width == LANES:
        return jnp.sum(x2, axis=-1, keepdims=True)
    lane = lax.broadcasted_iota(jnp.int32, x.shape, 1)
    lo = lane < width
    s_lo = jnp.sum(jnp.where(lo, x2, 0.0), axis=-1, keepdims=True)
    s_hi = jnp.sum(jnp.where(lo, 0.0, x2), axis=-1, keepdims=True)
    return jnp.where(lo, s_lo, s_hi)


def _aug_slab(x, m):
    lane = lax.broadcasted_iota(jnp.int32, x.shape, 1)
    half = LANES // 2
    keep = (lane < half) if m == 0 else (lane >= half)
    one = jnp.where(lane == (1 - m) * half, 1.0, 0.0).astype(x.dtype)
    return jnp.where(keep, x, one)


def _prep_kernel(qkv_ref, prev_ref, next_ref, dq_ref, dk_ref, dv_ref, cq_ref, ck_ref, cv_ref,
                 convw_ref, cos1_ref, sin1_ref, cos2_ref, sin2_ref, qnw_ref, knw_ref,
                 gq_ref, gk_ref, gv_ref, dqo_ref, dko_ref, dvo_ref, cqo_ref, cko_ref, cvo_ref, *, ts):
    i = pl.program_id(1)
    n = pl.num_programs(1)
    cur = qkv_ref[0].astype(F32)
    prev = jnp.where(i > 0, prev_ref[0].astype(F32), 0.0)
    nxt = jnp.where(i < n - 1, next_ref[0].astype(F32), 0.0)
    ext = jnp.concatenate([prev, cur, nxt], axis=0)
    pad = GDN_CONV // 2
    acc = jnp.zeros_like(cur)
    for j in range(GDN_CONV):
        off = HALO - pad + j
        acc = acc + ext[off:off + ts, :] * convw_ref[j:j + 1, :]
    act = acc * _sigmoid(acc)
    nqk = GDN_HEADS * GDN_DK
    for h in range(GDN_HEADS):
        sl = slice(h * GDN_DK, (h + 1) * GDN_DK)
        qh = act[:, sl]
        gq_ref[0, :, sl] = (qh * lax.rsqrt(_group_sumsq(qh, LANES) + NORM_EPS) * (GDN_DK ** -0.5)).astype(BF16)
        kh = act[:, nqk + h * GDN_DK: nqk + (h + 1) * GDN_DK]
        gk_ref[0, :, sl] = (kh * lax.rsqrt(_group_sumsq(kh, LANES) + NORM_EPS)).astype(BF16)
    gv_ref[0] = act[:, 2 * nqk:].astype(BF16)
    cos1, sin1 = cos1_ref[...], sin1_ref[...]
    lane = lax.broadcasted_iota(jnp.int32, (ts, LANES), 1)
    ones_col = jnp.where(lane == 0, 1.0, 0.0).astype(BF16)
    for p in range(DIFF_HEADS):
        sl = slice(p * LANES, (p + 1) * LANES)
        xq = dq_ref[0, :, sl].astype(F32)
        dqo_ref[0, :, sl] = ((xq * cos1 + _rot_half(xq, DIFF_DQK // 2) * sin1) * (DIFF_DQK ** -0.5 * LOG2E)).astype(BF16)
        xk = dk_ref[0, :, sl].astype(F32)
        xk = (xk * cos1 + _rot_half(xk, DIFF_DQK // 2) * sin1).astype(BF16)
        for m in range(2):
            dko_ref[0, :, (2 * p + m) * LANES:(2 * p + m + 1) * LANES] = _aug_slab(xk, m)
        dvo_ref[0, :, 2 * p * LANES:(2 * p + 1) * LANES] = dv_ref[0, :, sl]
        dvo_ref[0, :, (2 * p + 1) * LANES:(2 * p + 2) * LANES] = ones_col
    cos2, sin2 = cos2_ref[...], sin2_ref[...]
    for p in range(GQA_HEADS * GQA_DH // LANES):
        sl = slice(p * LANES, (p + 1) * LANES)
        xq = cq_ref[0, :, sl].astype(F32)
        xq = xq * lax.rsqrt(_group_sumsq(xq, GQA_DH) * (1.0 / GQA_DH) + NORM_EPS) * qnw_ref[...]
        cqo_ref[0, :, sl] = ((xq * cos2 + _rot_half(xq, GQA_DH // 4) * sin2) * (GQA_DH ** -0.5 * LOG2E)).astype(BF16)
    for p in range(2):
        sl = slice(p * LANES, (p + 1) * LANES)
        xk = ck_ref[0, :, sl].astype(F32)
        xk = xk * lax.rsqrt(_group_sumsq(xk, GQA_DH) * (1.0 / GQA_DH) + NORM_EPS) * knw_ref[...]
        xk = (xk * cos2 + _rot_half(xk, GQA_DH // 4) * sin2).astype(BF16)
        xv = cv_ref[0, :, sl]
        for m in range(2):
            c = p if m == 0 else 1 - p
            osl = slice((2 * c + m) * LANES, (2 * c + m + 1) * LANES)
            cko_ref[0, :, osl] = _aug_slab(xk, m)
            cvo_ref[0, :, osl] = _aug_slab(xv, m)


def _prep(main3, conv_w, cos1, sin1, cos2, sin2, qnw, knw, *, ts):
    b, s, _ = main3.shape
    nt = s // ts
    hb = ts // HALO
    last = s // HALO - 1
    row = lambda w: pl.BlockSpec((1, w), lambda bi, i: (0, 0))
    tab = pl.BlockSpec((ts, LANES), lambda bi, i: (i, 0))
    col = lambda w, off: pl.BlockSpec((1, ts, w), lambda bi, i: (bi, i, off // w))
    out = lambda w: pl.BlockSpec((1, ts, w), lambda bi, i: (bi, i, 0))
    widths = (512, 512, 512, 512, 1024, 1024, 512, 512, 512)
    return pl.pallas_call(
        functools.partial(_prep_kernel, ts=ts),
        out_shape=tuple(jax.ShapeDtypeStruct((b, s, w), BF16) for w in widths),
        grid=(b, nt),
        in_specs=[
            col(1536, COL_QKV),
            pl.BlockSpec((1, HALO, 1536), lambda bi, i: (bi, jnp.maximum(i * hb - 1, 0), COL_QKV // 1536)),
            pl.BlockSpec((1, HALO, 1536), lambda bi, i: (bi, jnp.minimum((i + 1) * hb, last), COL_QKV // 1536)),
            col(512, COL_DQ), col(512, COL_DK), col(512, COL_DV), col(512, COL_CQ), col(256, COL_CK), col(256, COL_CV),
            pl.BlockSpec((8, 1536), lambda bi, i: (0, 0)),
            tab, tab, tab, tab, row(LANES), row(LANES),
        ],
        out_specs=tuple(out(w) for w in widths),
        compiler_params=_cparams(("arbitrary", "arbitrary")),
        name="mixer_prep",
    )(*([main3] * 9), conv_w, cos1, sin1, cos2, sin2, qnw, knw)


GDN_G = 8
GDN_NBATCH = 2
GDN_ROWS = GDN_HEADS * GDN_CHUNK


def _stack_heads(x):
    return jnp.concatenate([x[:, h * LANES:(h + 1) * LANES] for h in range(GDN_HEADS)], axis=0)


def _row_to_col(row, eye):
    return jnp.sum(jnp.where(eye, row, 0.0), axis=1, keepdims=True)


def _gdn_kernel(af_ref, ab_ref, bf_ref, bb_ref, alog_ref, dtb_ref, qf_ref, kf_ref, vf_ref, qb_ref, kb_ref, vb_ref,
                of_ref, ob_ref, state_ref, gc_ref, gt_ref, beta_ref, *, nbatch):
    blk = pl.program_id(1)
    n = GDN_ROWS
    c = GDN_CHUNK

    @pl.when(blk == 0)
    def _():
        state_ref[...] = jnp.zeros_like(state_ref)

    ri = lax.broadcasted_iota(jnp.int32, (n, n), 0)
    ci = lax.broadcasted_iota(jnp.int32, (n, n), 1)
    same = (ri // c) == (ci // c)
    eye = ri == ci
    ti = lax.broadcasted_iota(jnp.int32, (n, GDN_HEADS * LANES), 0)
    tj = lax.broadcasted_iota(jnp.int32, (n, GDN_HEADS * LANES), 1)
    tot_m = jnp.where((ti // c) == (tj // LANES), 1.0, 0.0)

    chains = []
    for d, (a_ref, b_ref, q_ref, k_ref, v_ref, o_ref) in enumerate(
            ((af_ref, bf_ref, qf_ref, kf_ref, vf_ref, of_ref), (ab_ref, bb_ref, qb_ref, kb_ref, vb_ref, ob_ref))):
        sgn = 1 - 2 * d
        after = same & ((ri - ci) * sgn > 0)
        incl = same & ((ri - ci) * sgn >= 0)
        cum_m = jnp.where(same & ((ci - ri) * sgn >= 0), 1.0, 0.0)
        for bi in range(nbatch):
            ch = d * nbatch + bi
            x = a_ref[0, bi] + dtb_ref[d]
            softplus = jnp.maximum(x, 0.0) + jnp.log(1.0 + jnp.exp(-jnp.abs(x)))
            g = -jnp.exp(alog_ref[d]) * softplus
            beta_ref[ch] = _sigmoid(b_ref[0, bi])
            gc_ref[ch] = jnp.dot(g, cum_m, precision=HI, preferred_element_type=F32)
            gt_ref[ch] = jnp.dot(g, tot_m, precision=HI, preferred_element_type=F32)
            chains.append((ch, d, bi, after, incl, q_ref, k_ref, v_ref, o_ref))

    def chunk(j, chain):
        ch, d, bi, after, incl, q_ref, k_ref, v_ref, o_ref = chain
        cc = j if d == 0 else GDN_G - 1 - j
        r0 = pl.multiple_of(cc * c, c)
        gc_row = gc_ref[ch, pl.ds(cc, 1), :]
        beta_row = beta_ref[ch, pl.ds(cc, 1), :]
        gt_row = gt_ref[ch, pl.ds(cc, 1), :]
        gc_col = _row_to_col(gc_row, eye)
        beta_col = _row_to_col(beta_row, eye)
        k_st = _stack_heads(k_ref[bi, pl.ds(r0, c), :]).astype(F32)
        q_st = _stack_heads(q_ref[bi, pl.ds(r0, c), :]).astype(F32)
        v_st = _stack_heads(v_ref[bi, pl.ds(r0, c), :]).astype(F32)
        egc = jnp.exp(gc_col)
        decay = jnp.exp(jnp.minimum(gc_col - gc_row, 0.0))
        kb = k_st * beta_col
        k_bf = k_st.astype(BF16)
        kk = lax.dot_general(kb.astype(BF16), k_bf, (((1,), (1,)), ((), ())), preferred_element_type=F32)
        qk = lax.dot_general(q_st.astype(BF16), k_bf, (((1,), (1,)), ((), ())), preferred_element_type=F32)
        yield
        neg_a = jnp.where(after, -(kk * decay), 0.0)
        t_m = jnp.where(eye, 1.0, 0.0) + neg_a
        p_m = neg_a
        for _ in range(int(math.log2(c)) - 1):
            p_bf = p_m.astype(BF16)
            p_m = jnp.dot(p_bf, p_bf, preferred_element_type=F32)
            yield
            t_m = t_m + jnp.dot(t_m.astype(BF16), p_m.astype(BF16), preferred_element_type=F32)
            yield
        rhs = jnp.concatenate([v_st * beta_col, kb * egc], axis=1).astype(BF16)
        sol = jnp.dot(t_m.astype(BF16), rhs, preferred_element_type=F32)
        yield
        u_st, w_st = sol[:, :LANES], sol[:, LANES:]
        intra = jnp.where(incl, qk * decay, 0.0).astype(BF16)
        q_dec = (q_st * egc).astype(BF16)
        vn, oq = [], []
        for h in range(GDN_HEADS):
            rs = slice(h * c, (h + 1) * c)
            s_h = state_ref[ch * GDN_HEADS + h].astype(BF16)
            vn.append(u_st[rs] - jnp.dot(w_st[rs].astype(BF16), s_h, preferred_element_type=F32))
            oq.append(jnp.dot(q_dec[rs], s_h, preferred_element_type=F32))
        yield
        vn_st = jnp.concatenate(vn, axis=0)
        o_st = jnp.concatenate(oq, axis=0) + jnp.dot(intra, vn_st.astype(BF16), preferred_element_type=F32)
        for h in range(GDN_HEADS):
            rs = slice(h * c, (h + 1) * c)
            gt_h = gt_row[:, h * LANES:(h + 1) * LANES]
            k_dec = (k_st[rs] * jnp.exp(gt_h[:, :1] - gc_col[rs])).astype(BF16)
            upd = lax.dot_general(k_dec, vn[h].astype(BF16), (((0,), (0,)), ((), ())), preferred_element_type=F32)
            state_ref[ch * GDN_HEADS + h] = state_ref[ch * GDN_HEADS + h] * jnp.exp(gt_h) + upd
            o_ref[bi, pl.ds(r0, c), h * LANES:(h + 1) * LANES] = o_st[rs]

    def step(j, carry):
        active = [chunk(j, chain) for chain in chains]
        while active:
            active = [g for g in active if next(g, active) is not active]
        return carry

    lax.fori_loop(0, GDN_G, step, 0)


def _gdn(a_rows, b_rows, alog_row, dtb_row, gq, gk, gv, *, nbatch):
    b, s, _ = gq.shape
    nb = s // (GDN_G * GDN_CHUNK)
    ts = GDN_G * GDN_CHUNK
    nchain = 2 * nbatch
    tok_f = pl.BlockSpec((nbatch, ts, 512), lambda bi, i: (bi, i, 0))
    tok_b = pl.BlockSpec((nbatch, ts, 512), lambda bi, i: (bi, nb - 1 - i, 0))
    rows_f = pl.BlockSpec((1, nbatch, GDN_G, GDN_ROWS), lambda bi, i: (0, bi, i, 0))
    rows_b = pl.BlockSpec((1, nbatch, GDN_G, GDN_ROWS), lambda bi, i: (1, bi, nb - 1 - i, 0))
    par = pl.BlockSpec((2, 1, GDN_ROWS), lambda bi, i: (0, 0, 0))
    return pl.pallas_call(
        functools.partial(_gdn_kernel, nbatch=nbatch),
        out_shape=(jax.ShapeDtypeStruct((b, s, 512), F32), jax.ShapeDtypeStruct((b, s, 512), F32)),
        grid=(b // nbatch, nb),
        in_specs=[rows_f, rows_b, rows_f, rows_b, par, par, tok_f, tok_f, tok_f, tok_b, tok_b, tok_b],
        out_specs=(tok_f, tok_b),
        scratch_shapes=[pltpu.VMEM((nchain * GDN_HEADS, GDN_DK, GDN_DV), F32),
                        pltpu.VMEM((nchain, GDN_G, GDN_ROWS), F32),
                        pltpu.VMEM((nchain, GDN_G, GDN_HEADS * LANES), F32),
                        pltpu.VMEM((nchain, GDN_G, GDN_ROWS), F32)],
        compiler_params=_cparams(("arbitrary", "arbitrary")),
        name="gdn_chunked",
    )(a_rows, a_rows, b_rows, b_rows, alog_row, dtb_row, gq, gk, gv, gq, gk, gv)


def _attn_kernel(q_ref, k0_ref, k1_ref, v0_ref, v1_ref, lam_ref, nw_ref, o_ref, acc_ref, *, mode, tk, lambda_init):
    s_len = k0_ref.shape[1]
    tq = q_ref.shape[1]
    nv = v0_ref.shape[2]
    half = LANES // 2
    q = q_ref[0]
    lane = lax.broadcasted_iota(jnp.int32, q.shape, 1)
    lo = lane < half
    keep = (lo, lane >= half)
    stab = (lane == half, lane == 0)
    zero = jnp.zeros_like(q)
    krefs = (k0_ref, k1_ref)
    vrefs = (v0_ref, v1_ref)
    nchunks = s_len // tk
    dn = (((1,), (1,)), ((), ()))

    def kchunk(m, ci):
        return krefs[m][0, pl.ds(pl.multiple_of(ci * tk, tk), tk), :]

    def vchunk(m, ci):
        return vrefs[m][0, pl.ds(pl.multiple_of(ci * tk, tk), tk), :]

    qm, qa = [], []
    for m in range(2):
        qm.append(jnp.where(keep[m], q, zero))
        mx = jnp.max(lax.dot_general(qm[m], kchunk(m, 0), dn, preferred_element_type=F32), axis=-1, keepdims=True)
        qa.append(jnp.where(stab[m], (-mx).astype(BF16), qm[m]))

    def fast(ci, acc):
        sc = [lax.dot_general(qa[m], kchunk(m, ci), dn, preferred_element_type=F32) for m in range(2)]
        return tuple(acc[m] + jnp.dot(jnp.exp2(sc[m]).astype(BF16), vchunk(m, ci), preferred_element_type=F32)
                     for m in range(2))

    acc = lax.fori_loop(0, nchunks, fast, tuple(jnp.zeros((tq, nv), F32) for _ in range(2)))
    nonfinite = jnp.float32(0.0)
    for m in range(2):
        acc_ref[m] = acc[m]
        nonfinite = nonfinite + jnp.sum(jnp.where(jnp.isfinite(acc[m]), 0.0, 1.0))

    @pl.when(nonfinite > 0.0)
    def _():
        def slow(ci, carry):
            out = []
            for m in range(2):
                m_i, a_i = carry[m]
                sc = lax.dot_general(qm[m], kchunk(m, ci), dn, preferred_element_type=F32)
                m_new = jnp.maximum(m_i, jnp.max(sc, axis=-1, keepdims=True))
                p = jnp.exp2(sc - m_new).astype(BF16)
                out.append((m_new, jnp.exp2(m_i - m_new) * a_i + jnp.dot(p, vchunk(m, ci), preferred_element_type=F32)))
            return tuple(out)

        init = tuple((jnp.full((tq, 1), -jnp.inf, F32), jnp.zeros((tq, nv), F32)) for _ in range(2))
        res = lax.fori_loop(0, nchunks, slow, init)
        for m in range(2):
            acc_ref[m] = res[m][1]

    if mode == "diff":
        o0 = acc_ref[0, :, :LANES] / acc_ref[0, :, LANES:LANES + 1]
        o1 = acc_ref[1, :, :LANES] / acc_ref[1, :, LANES:LANES + 1]
    else:
        o0 = acc_ref[0] / acc_ref[0, :, half:half + 1]
        o1 = acc_ref[1] / acc_ref[1, :, 0:1]
    if mode == "diff":
        lv = lam_ref[...]
        lam = (jnp.exp(jnp.sum(lv[0:1] * lv[1:2], axis=-1, keepdims=True))
               - jnp.exp(jnp.sum(lv[2:3] * lv[3:4], axis=-1, keepdims=True)) + lambda_init)
        o = o0 - lam * o1
        o = o * lax.rsqrt(jnp.mean(o * o, axis=-1, keepdims=True) + NORM_EPS) * nw_ref[...] * (1.0 - lambda_init)
    else:
        o = jnp.where(lo, o0, o1)
    o_ref[0] = o.astype(o_ref.dtype)


def _attention(q, k_arr, v_arr, lam_vecs, norm_w, *, mode, tq, tk, lambda_init=0.0):
    b, s, w = q.shape
    slabs = w // LANES
    if mode == "diff":
        nv = 2 * LANES
        k_col = lambda p, m: 2 * p + m
        v_col = lambda p, m: p
    else:
        nv = LANES
        k_col = v_col = lambda p, m: 2 * (p // 2) + m
    kspec = lambda m: pl.BlockSpec((1, s, LANES), lambda bi, p, i: (bi, 0, k_col(p, m)))
    vspec = lambda m: pl.BlockSpec((1, s, nv), lambda bi, p, i: (bi, 0, v_col(p, m)))
    return pl.pallas_call(
        functools.partial(_attn_kernel, mode=mode, tk=tk, lambda_init=lambda_init),
        out_shape=jax.ShapeDtypeStruct((b, s, w), BF16),
        grid=(b, slabs, s // tq),
        in_specs=[pl.BlockSpec((1, tq, LANES), lambda bi, p, i: (bi, i, p)),
                  kspec(0), kspec(1), vspec(0), vspec(1),
                  pl.BlockSpec((4, DIFF_DQK), lambda bi, p, i: (0, 0)),
                  pl.BlockSpec((1, LANES), lambda bi, p, i: (0, 0))],
        out_specs=pl.BlockSpec((1, tq, LANES), lambda bi, p, i: (bi, i, p)),
        scratch_shapes=[pltpu.VMEM((2, tq, nv), F32)],
        compiler_params=_cparams(("arbitrary", "arbitrary", "arbitrary")),
        name="attn_" + mode,
    )(q, k_arr, k_arr, v_arr, v_arr, lam_vecs, norm_w)


def _merge_kernel(of_ref, ob_ref, z_ref, g0_ref, g1_ref, g2_ref, yb_ref, yc_ref, x_ref,
                  wa_ref, wb_ref, wc_ref, wo_ref, gnw_ref, fnw_ref, rw_ref, rb_ref,
                  xo_ref, h_ref, id_ref, rwgt_ref):
    o = of_ref[...] + ob_ref[...]
    parts = []
    for h in range(GDN_HEADS):
        oh = o[:, h * LANES:(h + 1) * LANES]
        parts.append(oh * lax.rsqrt(jnp.mean(oh * oh, axis=-1, keepdims=True) + NORM_EPS) * gnw_ref[...])
    z = z_ref[...].astype(F32)
    ya = (jnp.concatenate(parts, axis=1) * (z * _sigmoid(z))).astype(BF16)
    merged = _sigmoid(g0_ref[...].astype(F32)) * jnp.dot(ya, wa_ref[...], preferred_element_type=F32)
    merged = merged + _sigmoid(g1_ref[...].astype(F32)) * jnp.dot(yb_ref[...], wb_ref[...], preferred_element_type=F32)
    merged = merged + _sigmoid(g2_ref[...].astype(F32)) * jnp.dot(yc_ref[...], wc_ref[...], preferred_element_type=F32)
    xn = x_ref[...] + jnp.dot(merged.astype(BF16), wo_ref[...], preferred_element_type=F32)
    xo_ref[...] = xn
    hf = xn * lax.rsqrt(jnp.mean(xn * xn, axis=-1, keepdims=True) + NORM_EPS) * fnw_ref[...]
    h_ref[...] = hf.astype(BF16)
    logits = jnp.dot(hf, rw_ref[...], precision=HI, preferred_element_type=F32) + rb_ref[...]
    lane = lax.broadcasted_iota(jnp.int32, logits.shape, 1)
    big = jnp.int32(LANES)
    ninf = -jnp.inf
    glog = jnp.where(lane < N_GROUPS, logits, ninf)
    gmax = jnp.max(glog, axis=-1, keepdims=True)
    gidx = jnp.min(jnp.where(glog == gmax, lane, big), axis=-1, keepdims=True)
    gp = 1.0 / jnp.sum(jnp.exp(glog - gmax), axis=-1, keepdims=True)
    e = lane - N_GROUPS
    sel = (e >= 0) & (e < N_EXPERTS) & ((e // EXPERTS_PER_GROUP) == gidx)
    elog = jnp.where(sel, logits, ninf)
    m1 = jnp.max(elog, axis=-1, keepdims=True)
    i1 = jnp.min(jnp.where(elog == m1, lane, big), axis=-1, keepdims=True)
    elog2 = jnp.where(lane == i1, ninf, elog)
    m2 = jnp.max(elog2, axis=-1, keepdims=True)
    i2 = jnp.min(jnp.where(elog2 == m2, lane, big), axis=-1, keepdims=True)
    e2 = jnp.exp(m2 - m1)
    w1 = 1.0 / (1.0 + e2)
    w2 = e2 * w1
    id_ref[...] = jnp.where(lane == 0, i1 - N_GROUPS, jnp.where(lane == 1, i2 - N_GROUPS, 0))
    rwgt_ref[...] = jnp.where(lane == 0, gp * w1, jnp.where(lane == 1, gp * w2, 0.0))


def _merge(o_f, o_b, main2, yb, yc, x2d, wa, wb, wc, wo, gnw, fnw, rw, rb, *, tm):
    t, d = x2d.shape
    full = lambda shp: pl.BlockSpec(shp, lambda i: tuple(0 for _ in shp))
    return pl.pallas_call(
        _merge_kernel,
        out_shape=(jax.ShapeDtypeStruct((t, d), F32), jax.ShapeDtypeStruct((t, d), BF16),
                   jax.ShapeDtypeStruct((t, LANES), jnp.int32), jax.ShapeDtypeStruct((t, LANES), F32)),
        grid=(t // tm,),
        in_specs=[pl.BlockSpec((tm, 512), lambda i: (i, 0)),
                  pl.BlockSpec((tm, 512), lambda i: (i, 0)),
                  pl.BlockSpec((tm, 512), lambda i: (i, COL_Z // 512)),
                  pl.BlockSpec((tm, d), lambda i: (i, 0)),
                  pl.BlockSpec((tm, d), lambda i: (i, 1)),
                  pl.BlockSpec((tm, d), lambda i: (i, 2)),
                  pl.BlockSpec((tm, 512), lambda i: (i, 0)),
                  pl.BlockSpec((tm, 512), lambda i: (i, 0)),
                  pl.BlockSpec((tm, d), lambda i: (i, 0)),
                  full((512, d)), full((512, d)), full((512, d)), full((d, d)),
                  full((1, LANES)), full((1, d)), full((d, LANES)), full((1, LANES))],
        out_specs=(pl.BlockSpec((tm, d), lambda i: (i, 0)), pl.BlockSpec((tm, d), lambda i: (i, 0)),
                   pl.BlockSpec((tm, LANES), lambda i: (i, 0)), pl.BlockSpec((tm, LANES), lambda i: (i, 0))),
        compiler_params=_cparams(("arbitrary",)),
        name="merge_router",
    )(o_f, o_b, main2, main2, main2, main2, yb, yc, x2d, wa, wb, wc, wo, gnw, fnw, rw, rb)


def _expert_kernel(blk_e_ref, nused_ref, x_ref, w1_ref, w3_ref, w2_ref, o_ref):
    i = pl.program_id(0)

    @pl.when(i < nused_ref[0])
    def _():
        x = x_ref[...]
        a = jnp.dot(x, w1_ref[0], preferred_element_type=F32)
        u = jnp.dot(x, w3_ref[0], preferred_element_type=F32)
        hmid = (a * _sigmoid(a) * u).astype(BF16)
        o_ref[...] = jnp.dot(hmid, w2_ref[0], preferred_element_type=F32).astype(o_ref.dtype)

    @pl.when(i >= nused_ref[0])
    def _():
        o_ref[...] = jnp.zeros_like(o_ref)


def _experts(blk_e, nused, xb, w1, w3, w2):
    p_len, d = xb.shape
    ff = w1.shape[2]
    nblk = p_len // MOE_BLOCK
    return pl.pallas_call(
        _expert_kernel,
        out_shape=jax.ShapeDtypeStruct((p_len, d), BF16),
        grid_spec=pltpu.PrefetchScalarGridSpec(
            num_scalar_prefetch=2,
            grid=(nblk,),
            in_specs=[pl.BlockSpec((MOE_BLOCK, d), lambda i, be, nu: (i, 0)),
                      pl.BlockSpec((1, d, ff), lambda i, be, nu: (be[i], 0, 0)),
                      pl.BlockSpec((1, d, ff), lambda i, be, nu: (be[i], 0, 0)),
                      pl.BlockSpec((1, ff, d), lambda i, be, nu: (be[i], 0, 0))],
            out_specs=pl.BlockSpec((MOE_BLOCK, d), lambda i, be, nu: (i, 0)),
        ),
        compiler_params=_cparams(("arbitrary",)),
        name="expert_mlp",
    )(blk_e, nused, xb, w1, w3, w2)


def _combine_kernel(x_ref, y0_ref, y1_ref, w_ref, nw_ref, o_ref, *, final):
    w = w_ref[...]
    x = x_ref[...] + w[:, 0:1] * y0_ref[...].astype(F32) + w[:, 1:2] * y1_ref[...].astype(F32)
    if final:
        x = x * lax.rsqrt(jnp.mean(x * x, axis=-1, keepdims=True) + NORM_EPS) * nw_ref[...]
    o_ref[...] = x


def _combine(x2d, y0, y1, wts, norm_w, *, final, tm):
    t, d = x2d.shape
    tile = pl.BlockSpec((tm, d), lambda i: (i, 0))
    return pl.pallas_call(
        functools.partial(_combine_kernel, final=final),
        out_shape=jax.ShapeDtypeStruct((t, d), F32),
        grid=(t // tm,),
        in_specs=[tile, tile, tile, pl.BlockSpec((tm, LANES), lambda i: (i, 0)), pl.BlockSpec((1, d), lambda i: (0, 0))],
        out_specs=tile,
        compiler_params=_cparams(("arbitrary",)),
        name="moe_combine",
    )(x2d, y0, y1, wts, norm_w.reshape(1, d))


def _rope_tables(pos, dim):
    inv = 1.0 / (ROPE_THETA ** (jnp.arange(0, dim, 2, dtype=F32) / dim))
    ang = pos.astype(F32)[:, None] * inv[None, :]
    ang = jnp.concatenate([ang, ang], axis=-1)
    return jnp.cos(ang), jnp.sin(ang)


def _signed_sin(sin):
    half = sin.shape[-1] // 2
    return jnp.concatenate([-sin[:, :half], sin[:, half:]], axis=-1)


def _layout_w_in(w):
    o = 0
    parts = {}
    for name, size in (("qkv", 1536), ("z", 512), ("b", 8), ("a", 8), ("dq", 512), ("dk", 512), ("dv", 512),
                       ("cq", 512), ("ck", 128), ("cv", 128), ("gate", 3072)):
        parts[name] = w[:, o:o + size]
        o += size
    swap = lambda m: jnp.concatenate([m[:, 64:], m[:, :64]], axis=1)
    main = jnp.concatenate([parts["gate"], parts["qkv"], parts["z"], parts["dq"], parts["dk"], parts["dv"],
                            parts["cq"], parts["ck"], swap(parts["ck"]), parts["cv"], swap(parts["cv"])], axis=1)
    ba = jnp.concatenate([parts["b"], parts["a"], jnp.zeros((w.shape[0], LANES - 16), w.dtype)], axis=1)
    return main.astype(BF16), ba


def _rows_layout(t, bsz, s):
    nc = s // GDN_CHUNK
    t = t.reshape(bsz, nc, GDN_CHUNK, 2, GDN_HEADS)
    return jnp.transpose(t, (3, 0, 1, 4, 2)).reshape(2, bsz, nc, GDN_ROWS)


def _moe_dispatch(ids, t):
    a = t * TOPK
    p_len = ((a + N_EXPERTS * (MOE_BLOCK - 1) + MOE_BLOCK - 1) // MOE_BLOCK) * MOE_BLOCK
    n_blocks = p_len // MOE_BLOCK
    flat_e = ids.reshape(-1)
    iota_a = jnp.arange(a, dtype=jnp.int32)
    skey = jnp.sort(flat_e * a + iota_a)
    order = skey % a
    se = skey // a
    experts = jnp.arange(N_EXPERTS, dtype=jnp.int32)
    counts = jnp.sum((flat_e[:, None] == experts[None, :]).astype(jnp.int32), axis=0)
    start = jnp.cumsum(counts) - counts
    pcounts = ((counts + MOE_BLOCK - 1) // MOE_BLOCK) * MOE_BLOCK
    pend = jnp.cumsum(pcounts)
    pstart = pend - pcounts
    dest_sorted = pstart[se] + (iota_a - start[se])
    blk_first = jnp.arange(n_blocks, dtype=jnp.int32) * MOE_BLOCK
    blk_e = jnp.minimum(jnp.sum((pend[None, :] <= blk_first[:, None]).astype(jnp.int32), axis=1), N_EXPERTS - 1)
    row = jnp.arange(p_len, dtype=jnp.int32)
    row_e = jnp.repeat(blk_e, MOE_BLOCK)
    j = row - pstart[row_e]
    valid = j < counts[row_e]
    tok_buf = jnp.where(valid, order[jnp.minimum(start[row_e] + j, a - 1)] // TOPK, t)
    _, dest = lax.sort((order, dest_sorted), num_keys=1)
    nused = (pend[-1] // MOE_BLOCK).astype(jnp.int32).reshape(1)
    return tok_buf, dest.reshape(t, TOPK), blk_e, nused


def kernel(x, attn_norm_w, w_in, gdn_conv_w, gdn_a_log, gdn_dt_bias, gdn_norm_w, diff_lambda, diff_norm_w,
           gqa_q_norm_w, gqa_k_norm_w, w_branch_a, w_branch_b, w_branch_c, w_out, ffn_norm_w,
           router_group_w, router_group_b, router_expert_w, router_expert_b,
           expert_w_gate, expert_w_up, expert_w_down, final_norm_w):
    bsz, s, d = x.shape
    t = bsz * s
    depth = w_in.shape[0]
    tm = min(512, t)
    ts = min(512, s)

    rows = s // GRID_W
    row = jnp.broadcast_to(jnp.arange(rows)[:, None], (rows, GRID_W)).reshape(s)
    col = jnp.broadcast_to(jnp.arange(GRID_W)[None, :], (rows, GRID_W)).reshape(s)
    c1, s1 = _rope_tables(jnp.arange(s), DIFF_DQK)
    cr, sr = _rope_tables(row, GQA_DH // 2)
    cc, sc = _rope_tables(col, GQA_DH // 2)
    cos1 = jnp.tile(c1, (1, 2))
    sin1 = jnp.tile(_signed_sin(s1), (1, 2))
    cos2 = jnp.tile(jnp.concatenate([cr, cc], axis=-1), (1, 2))
    sin2 = jnp.tile(jnp.concatenate([_signed_sin(sr), _signed_sin(sc)], axis=-1), (1, 2))

    x2 = x.reshape(t, d)
    for l in range(depth):
        lambda_init = 0.8 - 0.6 * math.exp(-0.3 * l)
        w_main, w_ba = _layout_w_in(w_in[l])
        main2 = _norm_proj(x2, attn_norm_w[l], w_main, BF16, exact=False, tm=tm, tn=1536)
        ba = _norm_proj(x2, attn_norm_w[l], w_ba, F32, exact=True, tm=tm, tn=LANES)
        main3 = main2.reshape(bsz, s, N_MAIN)

        conv_w = jnp.concatenate([gdn_conv_w[l], jnp.zeros((8 - GDN_CONV, gdn_conv_w.shape[2]), F32)], axis=0)
        qnw = jnp.tile(gqa_q_norm_w[l], 2).reshape(1, LANES)
        knw = jnp.tile(gqa_k_norm_w[l], 2).reshape(1, LANES)
        gq, gk, gv, dq, dk, dv, cq, ck, cv = _prep(main3, conv_w, cos1, sin1, cos2, sin2, qnw, knw, ts=ts)

        b_rows = _rows_layout(ba[:, 0:8], bsz, s)
        a_rows = _rows_layout(ba[:, 8:16], bsz, s)
        alog_row = jnp.repeat(gdn_a_log[l], GDN_CHUNK, axis=1).reshape(2, 1, GDN_ROWS)
        dtb_row = jnp.repeat(gdn_dt_bias[l], GDN_CHUNK, axis=1).reshape(2, 1, GDN_ROWS)
        o_f, o_b = _gdn(a_rows, b_rows, alog_row, dtb_row, gq, gk, gv, nbatch=GDN_NBATCH if bsz % GDN_NBATCH == 0 else 1)

        nw_diff = diff_norm_w[l].reshape(1, LANES)
        yb = _attention(dq, dk, dv, diff_lambda[l], nw_diff, mode="diff",
                        tq=min(ATTN_TQ, s), tk=min(ATTN_TK, s), lambda_init=lambda_init)
        yc = _attention(cq, ck, cv, diff_lambda[l], nw_diff, mode="gqa",
                        tq=min(ATTN_TQ, s), tk=min(ATTN_TK, s))

        rw = jnp.concatenate([router_group_w[l], router_expert_w[l],
                              jnp.zeros((d, LANES - N_GROUPS - N_EXPERTS), F32)], axis=1)
        rb = jnp.concatenate([router_group_b[l], router_expert_b[l],
                              jnp.zeros((LANES - N_GROUPS - N_EXPERTS,), F32)]).reshape(1, LANES)
        x2, h2, ids, wts = _merge(o_f.reshape(t, 512), o_b.reshape(t, 512), main2, yb.reshape(t, 512), yc.reshape(t, 512), x2,
                                  w_branch_a[l].astype(BF16), w_branch_b[l].astype(BF16),
                                  w_branch_c[l].astype(BF16), w_out[l].astype(BF16),
                                  gdn_norm_w[l].reshape(1, LANES), ffn_norm_w[l].reshape(1, d), rw, rb, tm=min(256, t))

        tok_buf, dest, blk_e, nused = _moe_dispatch(ids[:, :TOPK], t)
        h_pad = jnp.concatenate([h2, jnp.zeros((1, d), BF16)], axis=0)
        yblk = _experts(blk_e, nused, h_pad[tok_buf], expert_w_gate[l].astype(BF16),
                        expert_w_up[l].astype(BF16), expert_w_down[l].astype(BF16))
        x2 = _combine(x2, yblk[dest[:, 0]], yblk[dest[:, 1]], wts, final_norm_w, final=(l == depth - 1), tm=tm)

    return x2.reshape(bsz, s, d)
```

```python
import functools
import math

import jax
import jax.numpy as jnp
from jax import lax
from jax.experimental import pallas as pl
from jax.experimental.pallas import tpu as pltpu

GRID_W = 64
ROPE_THETA = 10000.0
NORM_EPS = 1e-6
GDN_HEADS = 4
GDN_DK = 128
GDN_DV = 128
GDN_CONV = 5
GDN_CHUNK = 64
DIFF_HEADS = 4
DIFF_DQK = 64
GQA_HEADS = 8
GQA_KV = 2
GQA_DH = 64
N_GROUPS = 4
EXPERTS_PER_GROUP = 8
N_EXPERTS = N_GROUPS * EXPERTS_PER_GROUP
TOPK = 2
MOE_BLOCK = 256

LANES = 128
VMEM_LIMIT = 56 * 1024 * 1024

COL_GATE = 0
COL_QKV = 3072
COL_Z = 4608
COL_DQ = 5120
COL_DK = 5632
COL_DV = 6144
COL_CQ = 6656
COL_CK = 7168
COL_CV = 7424
N_MAIN = 7680

LOG2E = math.log2(math.e)
ATTN_TQ = 512
ATTN_TK = 2048
ATTN_STAB_KEYS = 256

HI = lax.Precision.HIGHEST
F32 = jnp.float32
BF16 = jnp.bfloat16


def _cparams(sem):
    return pltpu.CompilerParams(dimension_semantics=sem, vmem_limit_bytes=VMEM_LIMIT)


def _sigmoid(x):
    return 1.0 / (1.0 + jnp.exp(-x))


def _norm_proj_kernel(x_ref, nw_ref, w_ref, o_ref, *, exact):
    x = x_ref[...]
    h = x * lax.rsqrt(jnp.mean(x * x, axis=-1, keepdims=True) + NORM_EPS) * nw_ref[...]
    if exact:
        o_ref[...] = jnp.dot(h, w_ref[...], precision=HI, preferred_element_type=F32).astype(o_ref.dtype)
    else:
        o_ref[...] = jnp.dot(h.astype(BF16), w_ref[...], preferred_element_type=F32).astype(o_ref.dtype)


def _norm_proj(x2d, norm_w, w, out_dtype, *, exact, tm, tn):
    t, d = x2d.shape
    n = w.shape[1]
    return pl.pallas_call(
        functools.partial(_norm_proj_kernel, exact=exact),
        out_shape=jax.ShapeDtypeStruct((t, n), out_dtype),
        grid=(n // tn, t // tm),
        in_specs=[pl.BlockSpec((tm, d), lambda j, i: (i, 0)),
                  pl.BlockSpec((1, d), lambda j, i: (0, 0)),
                  pl.BlockSpec((d, tn), lambda j, i: (0, j))],
        out_specs=pl.BlockSpec((tm, tn), lambda j, i: (i, j)),
        compiler_params=_cparams(("arbitrary", "arbitrary")),
        name="norm_proj_exact" if exact else "norm_proj",
    )(x2d, norm_w.reshape(1, d), w)


HALO = 16


def _rot_half(x, half):
    lane = lax.broadcasted_iota(jnp.int32, x.shape, 1)
    first = (lane % (2 * half)) < half
    return jnp.where(first, pltpu.roll(x, LANES - half, 1), pltpu.roll(x, half, 1))


def _group_sumsq(x, width):
    x2 = x * x
    if width == LANES:
        return jnp.sum(x2, axis=-1, keepdims=True)
    lane = lax.broadcasted_iota(jnp.int32, x.shape, 1)
    lo = lane < width
    s_lo = jnp.sum(jnp.where(lo, x2, 0.0), axis=-1, keepdims=True)
    s_hi = jnp.sum(jnp.where(lo, 0.0, x2), axis=-1, keepdims=True)
    return jnp.where(lo, s_lo, s_hi)


def _aug_slab(x, m):
    lane = lax.broadcasted_iota(jnp.int32, x.shape, 1)
    half = LANES // 2
    keep = (lane < half) if m == 0 else (lane >= half)
    one = jnp.where(lane == (1 - m) * half, 1.0, 0.0).astype(x.dtype)
    return jnp.where(keep, x, one)


def _prep_kernel(qkv_ref, prev_ref, next_ref, dq_ref, dk_ref, dv_ref, cq_ref, ck_ref, cv_ref,
                 convw_ref, cos1_ref, sin1_ref, cos2_ref, sin2_ref, qnw_ref, knw_ref,
                 gq_ref, gk_ref, gv_ref, dqo_ref, dko_ref, dvo_ref, cqo_ref, cko_ref, cvo_ref, *, ts):
    i = pl.program_id(1)
    n = pl.num_programs(1)
    cur = qkv_ref[0].astype(F32)
    prev = jnp.where(i > 0, prev_ref[0].astype(F32), 0.0)
    nxt = jnp.where(i < n - 1, next_ref[0].astype(F32), 0.0)
    ext = jnp.concatenate([prev, cur, nxt], axis=0)
    pad = GDN_CONV // 2
    acc = jnp.zeros_like(cur)
    for j in range(GDN_CONV):
        off = HALO - pad + j
        acc = acc + ext[off:off + ts, :] * convw_ref[j:j + 1, :]
    act = acc * _sigmoid(acc)
    nqk = GDN_HEADS * GDN_DK
    for h in range(GDN_HEADS):
        sl = slice(h * GDN_DK, (h + 1) * GDN_DK)
        qh = act[:, sl]
        gq_ref[0, :, sl] = (qh * lax.rsqrt(_group_sumsq(qh, LANES) + NORM_EPS) * (GDN_DK ** -0.5)).astype(BF16)
        kh = act[:, nqk + h * GDN_DK: nqk + (h + 1) * GDN_DK]
        gk_ref[0, :, sl] = (kh * lax.rsqrt(_group_sumsq(kh, LANES) + NORM_EPS)).astype(BF16)
    gv_ref[0] = act[:, 2 * nqk:].astype(BF16)
    cos1, sin1 = cos1_ref[...], sin1_ref[...]
    lane = lax.broadcasted_iota(jnp.int32, (ts, LANES), 1)
    ones_col = jnp.where(lane == 0, 1.0, 0.0).astype(BF16)
    for p in range(DIFF_HEADS):
        sl = slice(p * LANES, (p + 1) * LANES)
        xq = dq_ref[0, :, sl].astype(F32)
        dqo_ref[0, :, sl] = ((xq * cos1 + _rot_half(xq, DIFF_DQK // 2) * sin1) * (DIFF_DQK ** -0.5 * LOG2E)).astype(BF16)
        xk = dk_ref[0, :, sl].astype(F32)
        xk = (xk * cos1 + _rot_half(xk, DIFF_DQK // 2) * sin1).astype(BF16)
        for m in range(2):
            dko_ref[0, :, (2 * p + m) * LANES:(2 * p + m + 1) * LANES] = _aug_slab(xk, m)
        dvo_ref[0, :, 2 * p * LANES:(2 * p + 1) * LANES] = dv_ref[0, :, sl]
        dvo_ref[0, :, (2 * p + 1) * LANES:(2 * p + 2) * LANES] = ones_col
    cos2, sin2 = cos2_ref[...], sin2_ref[...]
    for p in range(GQA_HEADS * GQA_DH // LANES):
        sl = slice(p * LANES, (p + 1) * LANES)
        xq = cq_ref[0, :, sl].astype(F32)
        xq = xq * lax.rsqrt(_group_sumsq(xq, GQA_DH) * (1.0 / GQA_DH) + NORM_EPS) * qnw_ref[...]
        cqo_ref[0, :, sl] = ((xq * cos2 + _rot_half(xq, GQA_DH // 4) * sin2) * (GQA_DH ** -0.5 * LOG2E)).astype(BF16)
    for p in range(2):
        sl = slice(p * LANES, (p + 1) * LANES)
        xk = ck_ref[0, :, sl].astype(F32)
        xk = xk * lax.rsqrt(_group_sumsq(xk, GQA_DH) * (1.0 / GQA_DH) + NORM_EPS) * knw_ref[...]
        xk = (xk * cos2 + _rot_half(xk, GQA_DH // 4) * sin2).astype(BF16)
        xv = cv_ref[0, :, sl]
        for m in range(2):
            c = p if m == 0 else 1 - p
            osl = slice((2 * c + m) * LANES, (2 * c + m + 1) * LANES)
            cko_ref[0, :, osl] = _aug_slab(xk, m)
            cvo_ref[0, :, osl] = _aug_slab(xv, m)


def _prep(main3, conv_w, cos1, sin1, cos2, sin2, qnw, knw, *, ts):
    b, s, _ = main3.shape
    nt = s // ts
    hb = ts // HALO
    last = s // HALO - 1
    row = lambda w: pl.BlockSpec((1, w), lambda bi, i: (0, 0))
    tab = pl.BlockSpec((ts, LANES), lambda bi, i: (i, 0))
    col = lambda w, off: pl.BlockSpec((1, ts, w), lambda bi, i: (bi, i, off // w))
    out = lambda w: pl.BlockSpec((1, ts, w), lambda bi, i: (bi, i, 0))
    widths = (512, 512, 512, 512, 1024, 1024, 512, 512, 512)
    return pl.pallas_call(
        functools.partial(_prep_kernel, ts=ts),
        out_shape=tuple(jax.ShapeDtypeStruct((b, s, w), BF16) for w in widths),
        grid=(b, nt),
        in_specs=[
            col(1536, COL_QKV),
            pl.BlockSpec((1, HALO, 1536), lambda bi, i: (bi, jnp.maximum(i * hb - 1, 0), COL_QKV // 1536)),
            pl.BlockSpec((1, HALO, 1536), lambda bi, i: (bi, jnp.minimum((i + 1) * hb, last), COL_QKV // 1536)),
            col(512, COL_DQ), col(512, COL_DK), col(512, COL_DV), col(512, COL_CQ), col(256, COL_CK), col(256, COL_CV),
            pl.BlockSpec((8, 1536), lambda bi, i: (0, 0)),
            tab, tab, tab, tab, row(LANES), row(LANES),
        ],
        out_specs=tuple(out(w) for w in widths),
        compiler_params=_cparams(("arbitrary", "arbitrary")),
        name="mixer_prep",
    )(*([main3] * 9), conv_w, cos1, sin1, cos2, sin2, qnw, knw)


GDN_G = 8
GDN_NBATCH = 2
GDN_ROWS = GDN_HEADS * GDN_CHUNK


def _stack_heads(x):
    return jnp.concatenate([x[:, h * LANES:(h + 1) * LANES] for h in range(GDN_HEADS)], axis=0)


def _row_to_col(row, eye):
    return jnp.sum(jnp.where(eye, row, 0.0), axis=1, keepdims=True)


def _gdn_kernel(af_ref, ab_ref, bf_ref, bb_ref, alog_ref, dtb_ref, qf_ref, kf_ref, vf_ref, qb_ref, kb_ref, vb_ref,
                of_ref, ob_ref, state_ref, gc_ref, gt_ref, beta_ref, *, nbatch):
    blk = pl.program_id(1)
    n = GDN_ROWS
    c = GDN_CHUNK

    @pl.when(blk == 0)
    def _():
        state_ref[...] = jnp.zeros_like(state_ref)

    ri = lax.broadcasted_iota(jnp.int32, (n, n), 0)
    ci = lax.broadcasted_iota(jnp.int32, (n, n), 1)
    same = (ri // c) == (ci // c)
    eye = ri == ci
    ti = lax.broadcasted_iota(jnp.int32, (n, GDN_HEADS * LANES), 0)
    tj = lax.broadcasted_iota(jnp.int32, (n, GDN_HEADS * LANES), 1)
    tot_m = jnp.where((ti // c) == (tj // LANES), 1.0, 0.0)

    chains = []
    for d, (a_ref, b_ref, q_ref, k_ref, v_ref, o_ref) in enumerate(
            ((af_ref, bf_ref, qf_ref, kf_ref, vf_ref, of_ref), (ab_ref, bb_ref, qb_ref, kb_ref, vb_ref, ob_ref))):
        sgn = 1 - 2 * d
        after = same & ((ri - ci) * sgn > 0)
        incl = same & ((ri - ci) * sgn >= 0)
        cum_m = jnp.where(same & ((ci - ri) * sgn >= 0), 1.0, 0.0)
        for bi in range(nbatch):
            ch = d * nbatch + bi
            x = a_ref[0, bi] + dtb_ref[d]
            softplus = jnp.maximum(x, 0.0) + jnp.log(1.0 + jnp.exp(-jnp.abs(x)))
            g = -jnp.exp(alog_ref[d]) * softplus
            beta_ref[ch] = _sigmoid(b_ref[0, bi])
            gc_ref[ch] = jnp.dot(g, cum_m, precision=HI, preferred_element_type=F32)
            gt_ref[ch] = jnp.dot(g, tot_m, precision=HI, preferred_element_type=F32)
            chains.append((ch, d, bi, after, incl, q_ref, k_ref, v_ref, o_ref))

    def chunk(j, chain):
        ch, d, bi, after, incl, q_ref, k_ref, v_ref, o_ref = chain
        cc = j if d == 0 else GDN_G - 1 - j
        r0 = pl.multiple_of(cc * c, c)
        gc_row = gc_ref[ch, pl.ds(cc, 1), :]
        beta_row = beta_ref[ch, pl.ds(cc, 1), :]
        gt_row = gt_ref[ch, pl.ds(cc, 1), :]
        gc_col = _row_to_col(gc_row, eye)
        beta_col = _row_to_col(beta_row, eye)
        k_st = _stack_heads(k_ref[bi, pl.ds(r0, c), :]).astype(F32)
        q_st = _stack_heads(q_ref[bi, pl.ds(r0, c), :]).astype(F32)
        v_st = _stack_heads(v_ref[bi, pl.ds(r0, c), :]).astype(F32)
        egc = jnp.exp(gc_col)
        decay = jnp.exp(jnp.minimum(gc_col - gc_row, 0.0))
        kb = k_st * beta_col
        k_bf = k_st.astype(BF16)
        kk = lax.dot_general(kb.astype(BF16), k_bf, (((1,), (1,)), ((), ())), preferred_element_type=F32)
        qk = lax.dot_general(q_st.astype(BF16), k_bf, (((1,), (1,)), ((), ())), preferred_element_type=F32)
        yield
        neg_a = jnp.where(after, -(kk * decay), 0.0)
        t_m = jnp.where(eye, 1.0, 0.0) + neg_a
        p_m = neg_a
        for _ in range(int(math.log2(c)) - 1):
            p_bf = p_m.astype(BF16)
            p_m = jnp.dot(p_bf, p_bf, preferred_element_type=F32)
            yield
            t_m = t_m + jnp.dot(t_m.astype(BF16), p_m.astype(BF16), preferred_element_type=F32)
            yield
        rhs = jnp.concatenate([v_st * beta_col, kb * egc], axis=1).astype(BF16)
        sol = jnp.dot(t_m.astype(BF16), rhs, preferred_element_type=F32)
        yield
        u_st, w_st = sol[:, :LANES], sol[:, LANES:]
        intra = jnp.where(incl, qk * decay, 0.0).astype(BF16)
        q_dec = (q_st * egc).astype(BF16)
        vn, oq = [], []
        for h in range(GDN_HEADS):
            rs = slice(h * c, (h + 1) * c)
            s_h = state_ref[ch * GDN_HEADS + h].astype(BF16)
            vn.append(u_st[rs] - jnp.dot(w_st[rs].astype(BF16), s_h, preferred_element_type=F32))
            oq.append(jnp.dot(q_dec[rs], s_h, preferred_element_type=F32))
        yield
        vn_st = jnp.concatenate(vn, axis=0)
        o_st = jnp.concatenate(oq, axis=0) + jnp.dot(intra, vn_st.astype(BF16), preferred_element_type=F32)
        for h in range(GDN_HEADS):
            rs = slice(h * c, (h + 1) * c)
            gt_h = gt_row[:, h * LANES:(h + 1) * LANES]
            k_dec = (k_st[rs] * jnp.exp(gt_h[:, :1] - gc_col[rs])).astype(BF16)
            upd = lax.dot_general(k_dec, vn[h].astype(BF16), (((0,), (0,)), ((), ())), preferred_element_type=F32)
            state_ref[ch * GDN_HEADS + h] = state_ref[ch * GDN_HEADS + h] * jnp.exp(gt_h) + upd
            o_ref[bi, pl.ds(r0, c), h * LANES:(h + 1) * LANES] = o_st[rs]

    def step(j, carry):
        active = [chunk(j, chain) for chain in chains]
        while active:
            active = [g for g in active if next(g, active) is not active]
        return carry

    lax.fori_loop(0, GDN_G, step, 0)


def _gdn(a_rows, b_rows, alog_row, dtb_row, gq, gk, gv, *, nbatch):
    b, s, _ = gq.shape
    nb = s // (GDN_G * GDN_CHUNK)
    ts = GDN_G * GDN_CHUNK
    nchain = 2 * nbatch
    tok_f = pl.BlockSpec((nbatch, ts, 512), lambda bi, i: (bi, i, 0))
    tok_b = pl.BlockSpec((nbatch, ts, 512), lambda bi, i: (bi, nb - 1 - i, 0))
    rows_f = pl.BlockSpec((1, nbatch, GDN_G, GDN_ROWS), lambda bi, i: (0, bi, i, 0))
    rows_b = pl.BlockSpec((1, nbatch, GDN_G, GDN_ROWS), lambda bi, i: (1, bi, nb - 1 - i, 0))
    par = pl.BlockSpec((2, 1, GDN_ROWS), lambda bi, i: (0, 0, 0))
    return pl.pallas_call(
        functools.partial(_gdn_kernel, nbatch=nbatch),
        out_shape=(jax.ShapeDtypeStruct((b, s, 512), F32), jax.ShapeDtypeStruct((b, s, 512), F32)),
        grid=(b // nbatch, nb),
        in_specs=[rows_f, rows_b, rows_f, rows_b, par, par, tok_f, tok_f, tok_f, tok_b, tok_b, tok_b],
        out_specs=(tok_f, tok_b),
        scratch_shapes=[pltpu.VMEM((nchain * GDN_HEADS, GDN_DK, GDN_DV), F32),
                        pltpu.VMEM((nchain, GDN_G, GDN_ROWS), F32),
                        pltpu.VMEM((nchain, GDN_G, GDN_HEADS * LANES), F32),
                        pltpu.VMEM((nchain, GDN_G, GDN_ROWS), F32)],
        compiler_params=_cparams(("arbitrary", "arbitrary")),
        name="gdn_chunked",
    )(a_rows, a_rows, b_rows, b_rows, alog_row, dtb_row, gq, gk, gv, gq, gk, gv)


def _attn_kernel(q_ref, k0_ref, k1_ref, v0_ref, v1_ref, lam_ref, nw_ref, o_ref, acc_ref, *, mode, tk, lambda_init):
    s_len = k0_ref.shape[1]
    tq = q_ref.shape[1]
    nv = v0_ref.shape[2]
    half = LANES // 2
    q = q_ref[0]
    lane = lax.broadcasted_iota(jnp.int32, q.shape, 1)
    lo = lane < half
    keep = (lo, lane >= half)
    stab = (lane == half, lane == 0)
    zero = jnp.zeros_like(q)
    krefs = (k0_ref, k1_ref)
    vrefs = (v0_ref, v1_ref)
    nchunks = s_len // tk
    dn = (((1,), (1,)), ((), ()))

    def kchunk(m, ci):
        return krefs[m][0, pl.ds(pl.multiple_of(ci * tk, tk), tk), :]

    def vchunk(m, ci):
        return vrefs[m][0, pl.ds(pl.multiple_of(ci * tk, tk), tk), :]

    qm, qa = [], []
    for m in range(2):
        qm.append(jnp.where(keep[m], q, zero))
        k_first = krefs[m][0, 0:min(ATTN_STAB_KEYS, s_len), :]
        mx = jnp.max(lax.dot_general(qm[m], k_first, dn, preferred_element_type=F32), axis=-1, keepdims=True)
        qa.append(jnp.where(stab[m], (-mx).astype(BF16), qm[m]))

    def fast(ci, acc):
        sc = [lax.dot_general(qa[m], kchunk(m, ci), dn, preferred_element_type=F32) for m in range(2)]
        return tuple(acc[m] + jnp.dot(jnp.exp2(sc[m]).astype(BF16), vchunk(m, ci), preferred_element_type=F32)
                     for m in range(2))

    acc = lax.fori_loop(0, nchunks, fast, tuple(jnp.zeros((tq, nv), F32) for _ in range(2)))
    nonfinite = jnp.float32(0.0)
    for m in range(2):
        acc_ref[m] = acc[m]
        nonfinite = nonfinite + jnp.sum(jnp.where(jnp.isfinite(acc[m]), 0.0, 1.0))

    @pl.when(nonfinite > 0.0)
    def _():
        def slow(ci, carry):
            out = []
            for m in range(2):
                m_i, a_i = carry[m]
                sc = lax.dot_general(qm[m], kchunk(m, ci), dn, preferred_element_type=F32)
                m_new = jnp.maximum(m_i, jnp.max(sc, axis=-1, keepdims=True))
                p = jnp.exp2(sc - m_new).astype(BF16)
                out.append((m_new, jnp.exp2(m_i - m_new) * a_i + jnp.dot(p, vchunk(m, ci), preferred_element_type=F32)))
            return tuple(out)

        init = tuple((jnp.full((tq, 1), -jnp.inf, F32), jnp.zeros((tq, nv), F32)) for _ in range(2))
        res = lax.fori_loop(0, nchunks, slow, init)
        for m in range(2):
            acc_ref[m] = res[m][1]

    if mode == "diff":
        o0 = acc_ref[0, :, :LANES] / acc_ref[0, :, LANES:LANES + 1]
        o1 = acc_ref[1, :, :LANES] / acc_ref[1, :, LANES:LANES + 1]
    else:
        o0 = acc_ref[0] / acc_ref[0, :, half:half + 1]
        o1 = acc_ref[1] / acc_ref[1, :, 0:1]
    if mode == "diff":
        lv = lam_ref[...]
        lam = (jnp.exp(jnp.sum(lv[0:1] * lv[1:2], axis=-1, keepdims=True))
               - jnp.exp(jnp.sum(lv[2:3] * lv[3:4], axis=-1, keepdims=True)) + lambda_init)
        o = o0 - lam * o1
        o = o * lax.rsqrt(jnp.mean(o * o, axis=-1, keepdims=True) + NORM_EPS) * nw_ref[...] * (1.0 - lambda_init)
    else:
        o = jnp.where(lo, o0, o1)
    o_ref[0] = o.astype(o_ref.dtype)


def _attention(q, k_arr, v_arr, lam_vecs, norm_w, *, mode, tq, tk, lambda_init=0.0):
    b, s, w = q.shape
    slabs = w // LANES
    if mode == "diff":
        nv = 2 * LANES
        k_col = lambda p, m: 2 * p + m
        v_col = lambda p, m: p
    else:
        nv = LANES
        k_col = v_col = lambda p, m: 2 * (p // 2) + m
    kspec = lambda m: pl.BlockSpec((1, s, LANES), lambda bi, p, i: (bi, 0, k_col(p, m)))
    vspec = lambda m: pl.BlockSpec((1, s, nv), lambda bi, p, i: (bi, 0, v_col(p, m)))
    return pl.pallas_call(
        functools.partial(_attn_kernel, mode=mode, tk=tk, lambda_init=lambda_init),
        out_shape=jax.ShapeDtypeStruct((b, s, w), BF16),
        grid=(b, slabs, s // tq),
        in_specs=[pl.BlockSpec((1, tq, LANES), lambda bi, p, i: (bi, i, p)),
                  kspec(0), kspec(1), vspec(0), vspec(1),
                  pl.BlockSpec((4, DIFF_DQK), lambda bi, p, i: (0, 0)),
                  pl.BlockSpec((1, LANES), lambda bi, p, i: (0, 0))],
        out_specs=pl.BlockSpec((1, tq, LANES), lambda bi, p, i: (bi, i, p)),
        scratch_shapes=[pltpu.VMEM((2, tq, nv), F32)],
        compiler_params=_cparams(("arbitrary", "arbitrary", "arbitrary")),
        name="attn_" + mode,
    )(q, k_arr, k_arr, v_arr, v_arr, lam_vecs, norm_w)


def _merge_kernel(of_ref, ob_ref, z_ref, g0_ref, g1_ref, g2_ref, yb_ref, yc_ref, x_ref,
                  wa_ref, wb_ref, wc_ref, wo_ref, gnw_ref, fnw_ref, rw_ref, rb_ref,
                  xo_ref, h_ref, id_ref, rwgt_ref):
    o = of_ref[...] + ob_ref[...]
    parts = []
    for h in range(GDN_HEADS):
        oh = o[:, h * LANES:(h + 1) * LANES]
        parts.append(oh * lax.rsqrt(jnp.mean(oh * oh, axis=-1, keepdims=True) + NORM_EPS) * gnw_ref[...])
    z = z_ref[...].astype(F32)
    ya = (jnp.concatenate(parts, axis=1) * (z * _sigmoid(z))).astype(BF16)
    merged = _sigmoid(g0_ref[...].astype(F32)) * jnp.dot(ya, wa_ref[...], preferred_element_type=F32)
    merged = merged + _sigmoid(g1_ref[...].astype(F32)) * jnp.dot(yb_ref[...], wb_ref[...], preferred_element_type=F32)
    merged = merged + _sigmoid(g2_ref[...].astype(F32)) * jnp.dot(yc_ref[...], wc_ref[...], preferred_element_type=F32)
    xn = x_ref[...] + jnp.dot(merged.astype(BF16), wo_ref[...], preferred_element_type=F32)
    xo_ref[...] = xn
    hf = xn * lax.rsqrt(jnp.mean(xn * xn, axis=-1, keepdims=True) + NORM_EPS) * fnw_ref[...]
    h_ref[...] = hf.astype(BF16)
    logits = jnp.dot(hf, rw_ref[...], precision=HI, preferred_element_type=F32) + rb_ref[...]
    lane = lax.broadcasted_iota(jnp.int32, logits.shape, 1)
    big = jnp.int32(LANES)
    ninf = -jnp.inf
    glog = jnp.where(lane < N_GROUPS, logits, ninf)
    gmax = jnp.max(glog, axis=-1, keepdims=True)
    gidx = jnp.min(jnp.where(glog == gmax, lane, big), axis=-1, keepdims=True)
    gp = 1.0 / jnp.sum(jnp.exp(glog - gmax), axis=-1, keepdims=True)
    e = lane - N_GROUPS
    sel = (e >= 0) & (e < N_EXPERTS) & ((e // EXPERTS_PER_GROUP) == gidx)
    elog = jnp.where(sel, logits, ninf)
    m1 = jnp.max(elog, axis=-1, keepdims=True)
    i1 = jnp.min(jnp.where(elog == m1, lane, big), axis=-1, keepdims=True)
    elog2 = jnp.where(lane == i1, ninf, elog)
    m2 = jnp.max(elog2, axis=-1, keepdims=True)
    i2 = jnp.min(jnp.where(elog2 == m2, lane, big), axis=-1, keepdims=True)
    e2 = jnp.exp(m2 - m1)
    w1 = 1.0 / (1.0 + e2)
    w2 = e2 * w1
    id_ref[...] = jnp.where(lane == 0, i1 - N_GROUPS, jnp.where(lane == 1, i2 - N_GROUPS, 0))
    rwgt_ref[...] = jnp.where(lane == 0, gp * w1, jnp.where(lane == 1, gp * w2, 0.0))


def _merge(o_f, o_b, main2, yb, yc, x2d, wa, wb, wc, wo, gnw, fnw, rw, rb, *, tm):
    t, d = x2d.shape
    full = lambda shp: pl.BlockSpec(shp, lambda i: tuple(0 for _ in shp))
    return pl.pallas_call(
        _merge_kernel,
        out_shape=(jax.ShapeDtypeStruct((t, d), F32), jax.ShapeDtypeStruct((t, d), BF16),
                   jax.ShapeDtypeStruct((t, LANES), jnp.int32), jax.ShapeDtypeStruct((t, LANES), F32)),
        grid=(t // tm,),
        in_specs=[pl.BlockSpec((tm, 512), lambda i: (i, 0)),
                  pl.BlockSpec((tm, 512), lambda i: (i, 0)),
                  pl.BlockSpec((tm, 512), lambda i: (i, COL_Z // 512)),
                  pl.BlockSpec((tm, d), lambda i: (i, 0)),
                  pl.BlockSpec((tm, d), lambda i: (i, 1)),
                  pl.BlockSpec((tm, d), lambda i: (i, 2)),
                  pl.BlockSpec((tm, 512), lambda i: (i, 0)),
                  pl.BlockSpec((tm, 512), lambda i: (i, 0)),
                  pl.BlockSpec((tm, d), lambda i: (i, 0)),
                  full((512, d)), full((512, d)), full((512, d)), full((d, d)),
                  full((1, LANES)), full((1, d)), full((d, LANES)), full((1, LANES))],
        out_specs=(pl.BlockSpec((tm, d), lambda i: (i, 0)), pl.BlockSpec((tm, d), lambda i: (i, 0)),
                   pl.BlockSpec((tm, LANES), lambda i: (i, 0)), pl.BlockSpec((tm, LANES), lambda i: (i, 0))),
        compiler_params=_cparams(("arbitrary",)),
        name="merge_router",
    )(o_f, o_b, main2, main2, main2, main2, yb, yc, x2d, wa, wb, wc, wo, gnw, fnw, rw, rb)


def _expert_kernel(blk_e_ref, nused_ref, x_ref, w1_ref, w3_ref, w2_ref, o_ref):
    i = pl.program_id(0)

    @pl.when(i < nused_ref[0])
    def _():
        x = x_ref[...]
        a = jnp.dot(x, w1_ref[0], preferred_element_type=F32)
        u = jnp.dot(x, w3_ref[0], preferred_element_type=F32)
        hmid = (a * _sigmoid(a) * u).astype(BF16)
        o_ref[...] = jnp.dot(hmid, w2_ref[0], preferred_element_type=F32).astype(o_ref.dtype)

    @pl.when(i >= nused_ref[0])
    def _():
        o_ref[...] = jnp.zeros_like(o_ref)


def _experts(blk_e, nused, xb, w1, w3, w2):
    p_len, d = xb.shape
    ff = w1.shape[2]
    nblk = p_len // MOE_BLOCK
    return pl.pallas_call(
        _expert_kernel,
        out_shape=jax.ShapeDtypeStruct((p_len, d), BF16),
        grid_spec=pltpu.PrefetchScalarGridSpec(
            num_scalar_prefetch=2,
            grid=(nblk,),
            in_specs=[pl.BlockSpec((MOE_BLOCK, d), lambda i, be, nu: (i, 0)),
                      pl.BlockSpec((1, d, ff), lambda i, be, nu: (be[i], 0, 0)),
                      pl.BlockSpec((1, d, ff), lambda i, be, nu: (be[i], 0, 0)),
                      pl.BlockSpec((1, ff, d), lambda i, be, nu: (be[i], 0, 0))],
            out_specs=pl.BlockSpec((MOE_BLOCK, d), lambda i, be, nu: (i, 0)),
        ),
        compiler_params=_cparams(("arbitrary",)),
        name="expert_mlp",
    )(blk_e, nused, xb, w1, w3, w2)


def _combine_kernel(x_ref, y0_ref, y1_ref, w_ref, nw_ref, o_ref, *, final):
    w = w_ref[...]
    x = x_ref[...] + w[:, 0:1] * y0_ref[...].astype(F32) + w[:, 1:2] * y1_ref[...].astype(F32)
    if final:
        x = x * lax.rsqrt(jnp.mean(x * x, axis=-1, keepdims=True) + NORM_EPS) * nw_ref[...]
    o_ref[...] = x


def _combine(x2d, y0, y1, wts, norm_w, *, final, tm):
    t, d = x2d.shape
    tile = pl.BlockSpec((tm, d), lambda i: (i, 0))
    return pl.pallas_call(
        functools.partial(_combine_kernel, final=final),
        out_shape=jax.ShapeDtypeStruct((t, d), F32),
        grid=(t // tm,),
        in_specs=[tile, tile, tile, pl.BlockSpec((tm, LANES), lambda i: (i, 0)), pl.BlockSpec((1, d), lambda i: (0, 0))],
        out_specs=tile,
        compiler_params=_cparams(("arbitrary",)),
        name="moe_combine",
    )(x2d, y0, y1, wts, norm_w.reshape(1, d))


def _rope_tables(pos, dim):
    inv = 1.0 / (ROPE_THETA ** (jnp.arange(0, dim, 2, dtype=F32) / dim))
    ang = pos.astype(F32)[:, None] * inv[None, :]
    ang = jnp.concatenate([ang, ang], axis=-1)
    return jnp.cos(ang), jnp.sin(ang)


def _signed_sin(sin):
    half = sin.shape[-1] // 2
    return jnp.concatenate([-sin[:, :half], sin[:, half:]], axis=-1)


def _layout_w_in(w):
    o = 0
    parts = {}
    for name, size in (("qkv", 1536), ("z", 512), ("b", 8), ("a", 8), ("dq", 512), ("dk", 512), ("dv", 512),
                       ("cq", 512), ("ck", 128), ("cv", 128), ("gate", 3072)):
        parts[name] = w[:, o:o + size]
        o += size
    swap = lambda m: jnp.concatenate([m[:, 64:], m[:, :64]], axis=1)
    main = jnp.concatenate([parts["gate"], parts["qkv"], parts["z"], parts["dq"], parts["dk"], parts["dv"],
                            parts["cq"], parts["ck"], swap(parts["ck"]), parts["cv"], swap(parts["cv"])], axis=1)
    ba = jnp.concatenate([parts["b"], parts["a"], jnp.zeros((w.shape[0], LANES - 16), w.dtype)], axis=1)
    return main.astype(BF16), ba


def _rows_layout(t, bsz, s):
    nc = s // GDN_CHUNK
    t = t.reshape(bsz, nc, GDN_CHUNK, 2, GDN_HEADS)
    return jnp.transpose(t, (3, 0, 1, 4, 2)).reshape(2, bsz, nc, GDN_ROWS)


def _moe_dispatch(ids, t):
    a = t * TOPK
    p_len = ((a + N_EXPERTS * (MOE_BLOCK - 1) + MOE_BLOCK - 1) // MOE_BLOCK) * MOE_BLOCK
    n_blocks = p_len // MOE_BLOCK
    flat_e = ids.reshape(-1)
    iota_a = jnp.arange(a, dtype=jnp.int32)
    skey = jnp.sort(flat_e * a + iota_a)
    order = skey % a
    se = skey // a
    experts = jnp.arange(N_EXPERTS, dtype=jnp.int32)
    counts = jnp.sum((flat_e[:, None] == experts[None, :]).astype(jnp.int32), axis=0)
    start = jnp.cumsum(counts) - counts
    pcounts = ((counts + MOE_BLOCK - 1) // MOE_BLOCK) * MOE_BLOCK
    pend = jnp.cumsum(pcounts)
    pstart = pend - pcounts
    dest_sorted = pstart[se] + (iota_a - start[se])
    blk_first = jnp.arange(n_blocks, dtype=jnp.int32) * MOE_BLOCK
    blk_e = jnp.minimum(jnp.sum((pend[None, :] <= blk_first[:, None]).astype(jnp.int32), axis=1), N_EXPERTS - 1)
    row = jnp.arange(p_len, dtype=jnp.int32)
    row_e = jnp.repeat(blk_e, MOE_BLOCK)
    j = row - pstart[row_e]
    valid = j < counts[row_e]
    tok_buf = jnp.where(valid, order[jnp.minimum(start[row_e] + j, a - 1)] // TOPK, t)
    _, dest = lax.sort((order, dest_sorted), num_keys=1)
    nused = (pend[-1] // MOE_BLOCK).astype(jnp.int32).reshape(1)
    return tok_buf, dest.reshape(t, TOPK), blk_e, nused


def kernel(x, attn_norm_w, w_in, gdn_conv_w, gdn_a_log, gdn_dt_bias, gdn_norm_w, diff_lambda, diff_norm_w,
           gqa_q_norm_w, gqa_k_norm_w, w_branch_a, w_branch_b, w_branch_c, w_out, ffn_norm_w,
           router_group_w, router_group_b, router_expert_w, router_expert_b,
           expert_w_gate, expert_w_up, expert_w_down, final_norm_w):
    bsz, s, d = x.shape
    t = bsz * s
    depth = w_in.shape[0]
    tm = min(512, t)
    ts = min(512, s)

    rows = s // GRID_W
    row = jnp.broadcast_to(jnp.arange(rows)[:, None], (rows, GRID_W)).reshape(s)
    col = jnp.broadcast_to(jnp.arange(GRID_W)[None, :], (rows, GRID_W)).reshape(s)
    c1, s1 = _rope_tables(jnp.arange(s), DIFF_DQK)
    cr, sr = _rope_tables(row, GQA_DH // 2)
    cc, sc = _rope_tables(col, GQA_DH // 2)
    cos1 = jnp.tile(c1, (1, 2))
    sin1 = jnp.tile(_signed_sin(s1), (1, 2))
    cos2 = jnp.tile(jnp.concatenate([cr, cc], axis=-1), (1, 2))
    sin2 = jnp.tile(jnp.concatenate([_signed_sin(sr), _signed_sin(sc)], axis=-1), (1, 2))

    x2 = x.reshape(t, d)
    for l in range(depth):
        lambda_init = 0.8 - 0.6 * math.exp(-0.3 * l)
        w_main, w_ba = _layout_w_in(w_in[l])
        main2 = _norm_proj(x2, attn_norm_w[l], w_main, BF16, exact=False, tm=tm, tn=1536)
        ba = _norm_proj(x2, attn_norm_w[l], w_ba, F32, exact=True, tm=tm, tn=LANES)
        main3 = main2.reshape(bsz, s, N_MAIN)

        conv_w = jnp.concatenate([gdn_conv_w[l], jnp.zeros((8 - GDN_CONV, gdn_conv_w.shape[2]), F32)], axis=0)
        qnw = jnp.tile(gqa_q_norm_w[l], 2).reshape(1, LANES)
        knw = jnp.tile(gqa_k_norm_w[l], 2).reshape(1, LANES)
        gq, gk, gv, dq, dk, dv, cq, ck, cv = _prep(main3, conv_w, cos1, sin1, cos2, sin2, qnw, knw, ts=ts)

        b_rows = _rows_layout(ba[:, 0:8], bsz, s)
        a_rows = _rows_layout(ba[:, 8:16], bsz, s)
        alog_row = jnp.repeat(gdn_a_log[l], GDN_CHUNK, axis=1).reshape(2, 1, GDN_ROWS)
        dtb_row = jnp.repeat(gdn_dt_bias[l], GDN_CHUNK, axis=1).reshape(2, 1, GDN_ROWS)
        o_f, o_b = _gdn(a_rows, b_rows, alog_row, dtb_row, gq, gk, gv, nbatch=GDN_NBATCH if bsz % GDN_NBATCH == 0 else 1)

        nw_diff = diff_norm_w[l].reshape(1, LANES)
        yb = _attention(dq, dk, dv, diff_lambda[l], nw_diff, mode="diff",
                        tq=min(ATTN_TQ, s), tk=min(ATTN_TK, s), lambda_init=lambda_init)
        yc = _attention(cq, ck, cv, diff_lambda[l], nw_diff, mode="gqa",
                        tq=min(ATTN_TQ, s), tk=min(ATTN_TK, s))

        rw = jnp.concatenate([router_group_w[l], router_expert_w[l],
                              jnp.zeros((d, LANES - N_GROUPS - N_EXPERTS), F32)], axis=1)
        rb = jnp.concatenate([router_group_b[l], router_expert_b[l],
                              jnp.zeros((LANES - N_GROUPS - N_EXPERTS,), F32)]).reshape(1, LANES)
        x2, h2, ids, wts = _merge(o_f.reshape(t, 512), o_b.reshape(t, 512), main2, yb.reshape(t, 512), yc.reshape(t, 512), x2,
                                  w_branch_a[l].astype(BF16), w_branch_b[l].astype(BF16),
                                  w_branch_c[l].astype(BF16), w_out[l].astype(BF16),
                                  gdn_norm_w[l].reshape(1, LANES), ffn_norm_w[l].reshape(1, d), rw, rb, tm=min(256, t))

        tok_buf, dest, blk_e, nused = _moe_dispatch(ids[:, :TOPK], t)
        h_pad = jnp.concatenate([h2, jnp.zeros((1, d), BF16)], axis=0)
        yblk = _experts(blk_e, nused, h_pad[tok_buf], expert_w_gate[l].astype(BF16),
                        expert_w_up[l].astype(BF16), expert_w_down[l].astype(BF16))
        x2 = _combine(x2, yblk[dest[:, 0]], yblk[dest[:, 1]], wts, final_norm_w, final=(l == depth - 1), tm=tm)

    return x2.reshape(bsz, s, d)
```

```python
import functools
import math

import jax
import jax.numpy as jnp
from jax import lax
from jax.experimental import pallas as pl
from jax.experimental.pallas import tpu as pltpu

GRID_W = 64
ROPE_THETA = 10000.0
NORM_EPS = 1e-6
GDN_HEADS = 4
GDN_DK = 128
GDN_DV = 128
GDN_CONV = 5
GDN_CHUNK = 64
DIFF_HEADS = 4
DIFF_DQK = 64
GQA_HEADS = 8
GQA_KV = 2
GQA_DH = 64
N_GROUPS = 4
EXPERTS_PER_GROUP = 8
N_EXPERTS = N_GROUPS * EXPERTS_PER_GROUP
TOPK = 2
MOE_BLOCK = 256

LANES = 128
VMEM_LIMIT = 56 * 1024 * 1024

COL_GATE = 0
COL_QKV = 3072
COL_Z = 4608
COL_DQ = 5120
COL_DK = 5632
COL_DV = 6144
COL_CQ = 6656
COL_CK = 7168
COL_CV = 7424
N_MAIN = 7680

LOG2E = math.log2(math.e)
ATTN_TQ = 512
ATTN_TK = 2048
ATTN_STAB_KEYS = 256
VT_PAD = 16

HI = lax.Precision.HIGHEST
F32 = jnp.float32
BF16 = jnp.bfloat16


def _cparams(sem):
    return pltpu.CompilerParams(dimension_semantics=sem, vmem_limit_bytes=VMEM_LIMIT)


def _sigmoid(x):
    return 1.0 / (1.0 + jnp.exp(-x))


def _norm_proj_kernel(x_ref, nw_ref, w_ref, o_ref, *, exact):
    x = x_ref[...]
    h = x * lax.rsqrt(jnp.mean(x * x, axis=-1, keepdims=True) + NORM_EPS) * nw_ref[...]
    if exact:
        o_ref[...] = jnp.dot(h, w_ref[...], precision=HI, preferred_element_type=F32).astype(o_ref.dtype)
    else:
        o_ref[...] = jnp.dot(h.astype(BF16), w_ref[...], preferred_element_type=F32).astype(o_ref.dtype)


def _norm_proj(x2d, norm_w, w, out_dtype, *, exact, tm, tn):
    t, d = x2d.shape
    n = w.shape[1]
    return pl.pallas_call(
        functools.partial(_norm_proj_kernel, exact=exact),
        out_shape=jax.ShapeDtypeStruct((t, n), out_dtype),
        grid=(n // tn, t // tm),
        in_specs=[pl.BlockSpec((tm, d), lambda j, i: (i, 0)),
                  pl.BlockSpec((1, d), lambda j, i: (0, 0)),
                  pl.BlockSpec((d, tn), lambda j, i: (0, j))],
        out_specs=pl.BlockSpec((tm, tn), lambda j, i: (i, j)),
        compiler_params=_cparams(("arbitrary", "arbitrary")),
        name="norm_proj_exact" if exact else "norm_proj",
    )(x2d, norm_w.reshape(1, d), w)


HALO = 16


def _rot_half(x, half):
    lane = lax.broadcasted_iota(jnp.int32, x.shape, 1)
    first = (lane % (2 * half)) < half
    return jnp.where(first, pltpu.roll(x, LANES - half, 1), pltpu.roll(x, half, 1))


def _group_sumsq(x, width):
    x2 = x * x
    if width == LANES:
        return jnp.sum(x2, axis=-1, keepdims=True)
    lane = lax.broadcasted_iota(jnp.int32, x.shape, 1)
    lo = lane < width
    s_lo = jnp.sum(jnp.where(lo, x2, 0.0), axis=-1, keepdims=True)
    s_hi = jnp.sum(jnp.where(lo, 0.0, x2), axis=-1, keepdims=True)
    return jnp.where(lo, s_lo, s_hi)


def _aug_slab(x, m):
    lane = lax.broadcasted_iota(jnp.int32, x.shape, 1)
    half = LANES // 2
    keep = (lane < half) if m == 0 else (lane >= half)
    one = jnp.where(lane == (1 - m) * half, 1.0, 0.0).astype(x.dtype)
    return jnp.where(keep, x, one)


def _prep_kernel(qkv_ref, prev_ref, next_ref, dq_ref, dk_ref, dv_ref, cq_ref, ck_ref, cv_ref,
                 convw_ref, cos1_ref, sin1_ref, cos2_ref, sin2_ref, qnw_ref, knw_ref,
                 gq_ref, gk_ref, gv_ref, dqo_ref, dko_ref, dvo_ref, cqo_ref, cko_ref, cvo_ref, *, ts):
    i = pl.program_id(1)
    n = pl.num_programs(1)
    cur = qkv_ref[0].astype(F32)
    prev = jnp.where(i > 0, prev_ref[0].astype(F32), 0.0)
    nxt = jnp.where(i < n - 1, next_ref[0].astype(F32), 0.0)
    ext = jnp.concatenate([prev, cur, nxt], axis=0)
    pad = GDN_CONV // 2
    acc = jnp.zeros_like(cur)
    for j in range(GDN_CONV):
        off = HALO - pad + j
        acc = acc + ext[off:off + ts, :] * convw_ref[j:j + 1, :]
    act = acc * _sigmoid(acc)
    nqk = GDN_HEADS * GDN_DK
    for h in range(GDN_HEADS):
        sl = slice(h * GDN_DK, (h + 1) * GDN_DK)
        qh = act[:, sl]
        gq_ref[0, :, sl] = (qh * lax.rsqrt(_group_sumsq(qh, LANES) + NORM_EPS) * (GDN_DK ** -0.5)).astype(BF16)
        kh = act[:, nqk + h * GDN_DK: nqk + (h + 1) * GDN_DK]
        gk_ref[0, :, sl] = (kh * lax.rsqrt(_group_sumsq(kh, LANES) + NORM_EPS)).astype(BF16)
    gv_ref[0] = act[:, 2 * nqk:].astype(BF16)
    cos1, sin1 = cos1_ref[...], sin1_ref[...]
    ones_rows = jnp.where(lax.broadcasted_iota(jnp.int32, (VT_PAD, ts), 0) == 0, 1.0, 0.0).astype(BF16)
    for p in range(DIFF_HEADS):
        sl = slice(p * LANES, (p + 1) * LANES)
        xq = dq_ref[0, :, sl].astype(F32)
        dqo_ref[0, :, sl] = ((xq * cos1 + _rot_half(xq, DIFF_DQK // 2) * sin1) * (DIFF_DQK ** -0.5 * LOG2E)).astype(BF16)
        xk = dk_ref[0, :, sl].astype(F32)
        xk = (xk * cos1 + _rot_half(xk, DIFF_DQK // 2) * sin1).astype(BF16)
        for m in range(2):
            dko_ref[0, :, (2 * p + m) * LANES:(2 * p + m + 1) * LANES] = _aug_slab(xk, m)
        r0 = p * (LANES + VT_PAD)
        dvo_ref[0, r0:r0 + LANES, :] = dv_ref[0, :, sl].astype(F32).T.astype(BF16)
        dvo_ref[0, r0 + LANES:r0 + LANES + VT_PAD, :] = ones_rows
    cos2, sin2 = cos2_ref[...], sin2_ref[...]
    for p in range(GQA_HEADS * GQA_DH // LANES):
        sl = slice(p * LANES, (p + 1) * LANES)
        xq = cq_ref[0, :, sl].astype(F32)
        xq = xq * lax.rsqrt(_group_sumsq(xq, GQA_DH) * (1.0 / GQA_DH) + NORM_EPS) * qnw_ref[...]
        cqo_ref[0, :, sl] = ((xq * cos2 + _rot_half(xq, GQA_DH // 4) * sin2) * (GQA_DH ** -0.5 * LOG2E)).astype(BF16)
    for p in range(2):
        sl = slice(p * LANES, (p + 1) * LANES)
        xk = ck_ref[0, :, sl].astype(F32)
        xk = xk * lax.rsqrt(_group_sumsq(xk, GQA_DH) * (1.0 / GQA_DH) + NORM_EPS) * knw_ref[...]
        xk = (xk * cos2 + _rot_half(xk, GQA_DH // 4) * sin2).astype(BF16)
        for m in range(2):
            c = p if m == 0 else 1 - p
            cko_ref[0, :, (2 * c + m) * LANES:(2 * c + m + 1) * LANES] = _aug_slab(xk, m)
    vt = cv_ref[0].astype(F32).T.astype(BF16)
    for c in range(GQA_KV):
        r0 = c * (GQA_DH + VT_PAD)
        cvo_ref[0, r0:r0 + GQA_DH, :] = vt[c * GQA_DH:(c + 1) * GQA_DH]
        cvo_ref[0, r0 + GQA_DH:r0 + GQA_DH + VT_PAD, :] = ones_rows


def _prep(main3, conv_w, cos1, sin1, cos2, sin2, qnw, knw, *, ts):
    b, s, _ = main3.shape
    nt = s // ts
    hb = ts // HALO
    last = s // HALO - 1
    row = lambda w: pl.BlockSpec((1, w), lambda bi, i: (0, 0))
    tab = pl.BlockSpec((ts, LANES), lambda bi, i: (i, 0))
    col = lambda w, off: pl.BlockSpec((1, ts, w), lambda bi, i: (bi, i, off // w))
    out = lambda w: pl.BlockSpec((1, ts, w), lambda bi, i: (bi, i, 0))
    outs = [("tok", 512), ("tok", 512), ("tok", 512), ("tok", 512), ("tok", 1024),
            ("rows", DIFF_HEADS * (LANES + VT_PAD)), ("tok", 512), ("tok", 512), ("rows", GQA_KV * (GQA_DH + VT_PAD))]
    specs = tuple(out(w) if kind == "tok" else pl.BlockSpec((1, w, ts), lambda bi, i: (bi, 0, i)) for kind, w in outs)
    shapes = tuple(jax.ShapeDtypeStruct((b, s, w) if kind == "tok" else (b, w, s), BF16) for kind, w in outs)
    return pl.pallas_call(
        functools.partial(_prep_kernel, ts=ts),
        out_shape=shapes,
        grid=(b, nt),
        in_specs=[
            col(1536, COL_QKV),
            pl.BlockSpec((1, HALO, 1536), lambda bi, i: (bi, jnp.maximum(i * hb - 1, 0), COL_QKV // 1536)),
            pl.BlockSpec((1, HALO, 1536), lambda bi, i: (bi, jnp.minimum((i + 1) * hb, last), COL_QKV // 1536)),
            col(512, COL_DQ), col(512, COL_DK), col(512, COL_DV), col(512, COL_CQ), col(256, COL_CK), col(128, COL_CV),
            pl.BlockSpec((8, 1536), lambda bi, i: (0, 0)),
            tab, tab, tab, tab, row(LANES), row(LANES),
        ],
        out_specs=specs,
        compiler_params=_cparams(("arbitrary", "arbitrary")),
        name="mixer_prep",
    )(*([main3] * 9), conv_w, cos1, sin1, cos2, sin2, qnw, knw)


GDN_G = 8
GDN_NBATCH = 2
GDN_ROWS = GDN_HEADS * GDN_CHUNK


def _stack_heads(x):
    return jnp.concatenate([x[:, h * LANES:(h + 1) * LANES] for h in range(GDN_HEADS)], axis=0)


def _row_to_col(row, eye):
    return jnp.sum(jnp.where(eye, row, 0.0), axis=1, keepdims=True)


def _gdn_kernel(af_ref, ab_ref, bf_ref, bb_ref, alog_ref, dtb_ref, qf_ref, kf_ref, vf_ref, qb_ref, kb_ref, vb_ref,
                of_ref, ob_ref, state_ref, gc_ref, gt_ref, beta_ref, *, nbatch):
    blk = pl.program_id(1)
    n = GDN_ROWS
    c = GDN_CHUNK

    @pl.when(blk == 0)
    def _():
        state_ref[...] = jnp.zeros_like(state_ref)

    ri = lax.broadcasted_iota(jnp.int32, (n, n), 0)
    ci = lax.broadcasted_iota(jnp.int32, (n, n), 1)
    same = (ri // c) == (ci // c)
    eye = ri == ci
    ti = lax.broadcasted_iota(jnp.int32, (n, GDN_HEADS * LANES), 0)
    tj = lax.broadcasted_iota(jnp.int32, (n, GDN_HEADS * LANES), 1)
    tot_m = jnp.where((ti // c) == (tj // LANES), 1.0, 0.0)

    chains = []
    for d, (a_ref, b_ref, q_ref, k_ref, v_ref, o_ref) in enumerate(
            ((af_ref, bf_ref, qf_ref, kf_ref, vf_ref, of_ref), (ab_ref, bb_ref, qb_ref, kb_ref, vb_ref, ob_ref))):
        sgn = 1 - 2 * d
        after = same & ((ri - ci) * sgn > 0)
        incl = same & ((ri - ci) * sgn >= 0)
        cum_m = jnp.where(same & ((ci - ri) * sgn >= 0), 1.0, 0.0)
        for bi in range(nbatch):
            ch = d * nbatch + bi
            x = a_ref[0, bi] + dtb_ref[d]
            softplus = jnp.maximum(x, 0.0) + jnp.log(1.0 + jnp.exp(-jnp.abs(x)))
            g = -jnp.exp(alog_ref[d]) * softplus
            beta_ref[ch] = _sigmoid(b_ref[0, bi])
            gc_ref[ch] = jnp.dot(g, cum_m, precision=HI, preferred_element_type=F32)
            gt_ref[ch] = jnp.dot(g, tot_m, precision=HI, preferred_element_type=F32)
            chains.append((ch, d, bi, after, incl, q_ref, k_ref, v_ref, o_ref))

    def chunk(j, chain):
        ch, d, bi, after, incl, q_ref, k_ref, v_ref, o_ref = chain
        cc = j if d == 0 else GDN_G - 1 - j
        r0 = pl.multiple_of(cc * c, c)
        gc_row = gc_ref[ch, pl.ds(cc, 1), :]
        beta_row = beta_ref[ch, pl.ds(cc, 1), :]
        gt_row = gt_ref[ch, pl.ds(cc, 1), :]
        gc_col = _row_to_col(gc_row, eye)
        beta_col = _row_to_col(beta_row, eye)
        k_st = _stack_heads(k_ref[bi, pl.ds(r0, c), :]).astype(F32)
        q_st = _stack_heads(q_ref[bi, pl.ds(r0, c), :]).astype(F32)
        v_st = _stack_heads(v_ref[bi, pl.ds(r0, c), :]).astype(F32)
        egc = jnp.exp(gc_col)
        decay = jnp.exp(jnp.minimum(gc_col - gc_row, 0.0))
        kb = k_st * beta_col
        k_bf = k_st.astype(BF16)
        kk = lax.dot_general(kb.astype(BF16), k_bf, (((1,), (1,)), ((), ())), preferred_element_type=F32)
        qk = lax.dot_general(q_st.astype(BF16), k_bf, (((1,), (1,)), ((), ())), preferred_element_type=F32)
        yield
        neg_a = jnp.where(after, -(kk * decay), 0.0)
        t_m = jnp.where(eye, 1.0, 0.0) + neg_a
        p_m = neg_a
        for _ in range(int(math.log2(c)) - 1):
            p_bf = p_m.astype(BF16)
            p_m = jnp.dot(p_bf, p_bf, preferred_element_type=F32)
            yield
            t_m = t_m + jnp.dot(t_m.astype(BF16), p_m.astype(BF16), preferred_element_type=F32)
            yield
        rhs = jnp.concatenate([v_st * beta_col, kb * egc], axis=1).astype(BF16)
        sol = jnp.dot(t_m.astype(BF16), rhs, preferred_element_type=F32)
        yield
        u_st, w_st = sol[:, :LANES], sol[:, LANES:]
        intra = jnp.where(incl, qk * decay, 0.0).astype(BF16)
        q_dec = (q_st * egc).astype(BF16)
        vn, oq = [], []
        for h in range(GDN_HEADS):
            rs = slice(h * c, (h + 1) * c)
            s_h = state_ref[ch * GDN_HEADS + h].astype(BF16)
            vn.append(u_st[rs] - jnp.dot(w_st[rs].astype(BF16), s_h, preferred_element_type=F32))
            oq.append(jnp.dot(q_dec[rs], s_h, preferred_element_type=F32))
        yield
        vn_st = jnp.concatenate(vn, axis=0)
        o_st = jnp.concatenate(oq, axis=0) + jnp.dot(intra, vn_st.astype(BF16), preferred_element_type=F32)
        for h in range(GDN_HEADS):
            rs = slice(h * c, (h + 1) * c)
            gt_h = gt_row[:, h * LANES:(h + 1) * LANES]
            k_dec = (k_st[rs] * jnp.exp(gt_h[:, :1] - gc_col[rs])).astype(BF16)
            upd = lax.dot_general(k_dec, vn[h].astype(BF16), (((0,), (0,)), ((), ())), preferred_element_type=F32)
            state_ref[ch * GDN_HEADS + h] = state_ref[ch * GDN_HEADS + h] * jnp.exp(gt_h) + upd
            o_ref[bi, pl.ds(r0, c), h * LANES:(h + 1) * LANES] = o_st[rs]

    def step(j, carry):
        active = [chunk(j, chain) for chain in chains]
        while active:
            active = [g for g in active if next(g, active) is not active]
        return carry

    lax.fori_loop(0, GDN_G, step, 0)


def _gdn(a_rows, b_rows, alog_row, dtb_row, gq, gk, gv, *, nbatch):
    b, s, _ = gq.shape
    nb = s // (GDN_G * GDN_CHUNK)
    ts = GDN_G * GDN_CHUNK
    nchain = 2 * nbatch
    tok_f = pl.BlockSpec((nbatch, ts, 512), lambda bi, i: (bi, i, 0))
    tok_b = pl.BlockSpec((nbatch, ts, 512), lambda bi, i: (bi, nb - 1 - i, 0))
    rows_f = pl.BlockSpec((1, nbatch, GDN_G, GDN_ROWS), lambda bi, i: (0, bi, i, 0))
    rows_b = pl.BlockSpec((1, nbatch, GDN_G, GDN_ROWS), lambda bi, i: (1, bi, nb - 1 - i, 0))
    par = pl.BlockSpec((2, 1, GDN_ROWS), lambda bi, i: (0, 0, 0))
    return pl.pallas_call(
        functools.partial(_gdn_kernel, nbatch=nbatch),
        out_shape=(jax.ShapeDtypeStruct((b, s, 512), F32), jax.ShapeDtypeStruct((b, s, 512), F32)),
        grid=(b // nbatch, nb),
        in_specs=[rows_f, rows_b, rows_f, rows_b, par, par, tok_f, tok_f, tok_f, tok_b, tok_b, tok_b],
        out_specs=(tok_f, tok_b),
        scratch_shapes=[pltpu.VMEM((nchain * GDN_HEADS, GDN_DK, GDN_DV), F32),
                        pltpu.VMEM((nchain, GDN_G, GDN_ROWS), F32),
                        pltpu.VMEM((nchain, GDN_G, GDN_HEADS * LANES), F32),
                        pltpu.VMEM((nchain, GDN_G, GDN_ROWS), F32)],
        compiler_params=_cparams(("arbitrary", "arbitrary")),
        name="gdn_chunked",
    )(a_rows, a_rows, b_rows, b_rows, alog_row, dtb_row, gq, gk, gv, gq, gk, gv)


def _attn_kernel(q_ref, k0_ref, k1_ref, vt_ref, lam_ref, nw_ref, o_ref, acc_ref, *, mode, tk, lambda_init):
    s_len = k0_ref.shape[1]
    tq = q_ref.shape[1]
    rows = vt_ref.shape[1]
    dv = rows - VT_PAD
    half = LANES // 2
    q = q_ref[0]
    lane = lax.broadcasted_iota(jnp.int32, q.shape, 1)
    keep = (lane < half, lane >= half)
    stab = (lane == half, lane == 0)
    zero = jnp.zeros_like(q)
    krefs = (k0_ref, k1_ref)
    nchunks = s_len // tk
    dn = (((1,), (1,)), ((), ()))

    def kchunk(m, ci):
        return krefs[m][0, pl.ds(pl.multiple_of(ci * tk, tk), tk), :]

    def vchunk(ci):
        return vt_ref[0, :, pl.ds(pl.multiple_of(ci * tk, tk), tk)]

    qm, qa = [], []
    for m in range(2):
        qm.append(jnp.where(keep[m], q, zero))
        k_first = krefs[m][0, 0:min(ATTN_STAB_KEYS, s_len), :]
        mx = jnp.max(lax.dot_general(qm[m], k_first, dn, preferred_element_type=F32), axis=-1, keepdims=True)
        qa.append(jnp.where(stab[m], (-mx).astype(BF16), qm[m]))

    def fast(ci, acc):
        st = [lax.dot_general(kchunk(m, ci), qa[m], dn, preferred_element_type=F32) for m in range(2)]
        vt = vchunk(ci)
        return tuple(acc[m] + jnp.dot(vt, jnp.exp2(st[m]).astype(BF16), preferred_element_type=F32) for m in range(2))

    acc = lax.fori_loop(0, nchunks, fast, tuple(jnp.zeros((rows, tq), F32) for _ in range(2)))
    nonfinite = jnp.float32(0.0)
    for m in range(2):
        acc_ref[m] = acc[m]
        nonfinite = nonfinite + jnp.sum(jnp.where(jnp.isfinite(acc[m]), 0.0, 1.0))

    @pl.when(nonfinite > 0.0)
    def _():
        def slow(ci, carry):
            out = []
            vt = vchunk(ci)
            for m in range(2):
                m_i, a_i = carry[m]
                st = lax.dot_general(kchunk(m, ci), qm[m], dn, preferred_element_type=F32)
                m_new = jnp.maximum(m_i, jnp.max(st, axis=0, keepdims=True))
                p = jnp.exp2(st - m_new).astype(BF16)
                out.append((m_new, jnp.exp2(m_i - m_new) * a_i + jnp.dot(vt, p, preferred_element_type=F32)))
            return tuple(out)

        init = tuple((jnp.full((1, tq), -jnp.inf, F32), jnp.zeros((rows, tq), F32)) for _ in range(2))
        res = lax.fori_loop(0, nchunks, slow, init)
        for m in range(2):
            acc_ref[m] = res[m][1]

    o0 = acc_ref[0, 0:dv, :] / acc_ref[0, dv:dv + 1, :]
    o1 = acc_ref[1, 0:dv, :] / acc_ref[1, dv:dv + 1, :]
    if mode == "diff":
        lv = lam_ref[...]
        lam = (jnp.exp(jnp.sum(lv[0:1] * lv[1:2], axis=-1, keepdims=True))
               - jnp.exp(jnp.sum(lv[2:3] * lv[3:4], axis=-1, keepdims=True)) + lambda_init)
        ot = o0 - lam * o1
        ot = ot * lax.rsqrt(jnp.mean(ot * ot, axis=0, keepdims=True) + NORM_EPS) * nw_ref[...] * (1.0 - lambda_init)
    else:
        ot = jnp.concatenate([o0, o1], axis=0)
    o_ref[0] = ot.T.astype(o_ref.dtype)


def _attention(q, k_arr, vt_arr, lam_vecs, norm_w, *, mode, tq, tk, lambda_init=0.0):
    b, s, w = q.shape
    slabs = w // LANES
    if mode == "diff":
        rows = LANES + VT_PAD
        k_col = lambda p, m: 2 * p + m
        v_grp = lambda p: p
    else:
        rows = GQA_DH + VT_PAD
        k_col = lambda p, m: 2 * (p // 2) + m
        v_grp = lambda p: p // 2
    kspec = lambda m: pl.BlockSpec((1, s, LANES), lambda bi, p, i: (bi, 0, k_col(p, m)))
    return pl.pallas_call(
        functools.partial(_attn_kernel, mode=mode, tk=tk, lambda_init=lambda_init),
        out_shape=jax.ShapeDtypeStruct((b, s, w), BF16),
        grid=(b, slabs, s // tq),
        in_specs=[pl.BlockSpec((1, tq, LANES), lambda bi, p, i: (bi, i, p)),
                  kspec(0), kspec(1),
                  pl.BlockSpec((1, rows, s), lambda bi, p, i: (bi, v_grp(p), 0)),
                  pl.BlockSpec((4, DIFF_DQK), lambda bi, p, i: (0, 0)),
                  pl.BlockSpec((LANES, 1), lambda bi, p, i: (0, 0))],
        out_specs=pl.BlockSpec((1, tq, LANES), lambda bi, p, i: (bi, i, p)),
        scratch_shapes=[pltpu.VMEM((2, rows, tq), F32)],
        compiler_params=_cparams(("arbitrary", "arbitrary", "arbitrary")),
        name="attn_" + mode,
    )(q, k_arr, k_arr, vt_arr, lam_vecs, norm_w)


def _merge_kernel(of_ref, ob_ref, z_ref, g0_ref, g1_ref, g2_ref, yb_ref, yc_ref, x_ref,
                  wa_ref, wb_ref, wc_ref, wo_ref, gnw_ref, fnw_ref, rw_ref, rb_ref,
                  xo_ref, h_ref, id_ref, rwgt_ref):
    o = of_ref[...] + ob_ref[...]
    parts = []
    for h in range(GDN_HEADS):
        oh = o[:, h * LANES:(h + 1) * LANES]
        parts.append(oh * lax.rsqrt(jnp.mean(oh * oh, axis=-1, keepdims=True) + NORM_EPS) * gnw_ref[...])
    z = z_ref[...].astype(F32)
    ya = (jnp.concatenate(parts, axis=1) * (z * _sigmoid(z))).astype(BF16)
    merged = _sigmoid(g0_ref[...].astype(F32)) * jnp.dot(ya, wa_ref[...], preferred_element_type=F32)
    merged = merged + _sigmoid(g1_ref[...].astype(F32)) * jnp.dot(yb_ref[...], wb_ref[...], preferred_element_type=F32)
    merged = merged + _sigmoid(g2_ref[...].astype(F32)) * jnp.dot(yc_ref[...], wc_ref[...], preferred_element_type=F32)
    xn = x_ref[...] + jnp.dot(merged.astype(BF16), wo_ref[...], preferred_element_type=F32)
    xo_ref[...] = xn
    hf = xn * lax.rsqrt(jnp.mean(xn * xn, axis=-1, keepdims=True) + NORM_EPS) * fnw_ref[...]
    h_ref[...] = hf.astype(BF16)
    logits = jnp.dot(hf, rw_ref[...], precision=HI, preferred_element_type=F32) + rb_ref[...]
    lane = lax.broadcasted_iota(jnp.int32, logits.shape, 1)
    big = jnp.int32(LANES)
    ninf = -jnp.inf
    glog = jnp.where(lane < N_GROUPS, logits, ninf)
    gmax = jnp.max(glog, axis=-1, keepdims=True)
    gidx = jnp.min(jnp.where(glog == gmax, lane, big), axis=-1, keepdims=True)
    gp = 1.0 / jnp.sum(jnp.exp(glog - gmax), axis=-1, keepdims=True)
    e = lane - N_GROUPS
    sel = (e >= 0) & (e < N_EXPERTS) & ((e // EXPERTS_PER_GROUP) == gidx)
    elog = jnp.where(sel, logits, ninf)
    m1 = jnp.max(elog, axis=-1, keepdims=True)
    i1 = jnp.min(jnp.where(elog == m1, lane, big), axis=-1, keepdims=True)
    elog2 = jnp.where(lane == i1, ninf, elog)
    m2 = jnp.max(elog2, axis=-1, keepdims=True)
    i2 = jnp.min(jnp.where(elog2 == m2, lane, big), axis=-1, keepdims=True)
    e2 = jnp.exp(m2 - m1)
    w1 = 1.0 / (1.0 + e2)
    w2 = e2 * w1
    id_ref[...] = jnp.where(lane == 0, i1 - N_GROUPS, jnp.where(lane == 1, i2 - N_GROUPS, 0))
    rwgt_ref[...] = jnp.where(lane == 0, gp * w1, jnp.where(lane == 1, gp * w2, 0.0))


def _merge(o_f, o_b, main2, yb, yc, x2d, wa, wb, wc, wo, gnw, fnw, rw, rb, *, tm):
    t, d = x2d.shape
    full = lambda shp: pl.BlockSpec(shp, lambda i: tuple(0 for _ in shp))
    return pl.pallas_call(
        _merge_kernel,
        out_shape=(jax.ShapeDtypeStruct((t, d), F32), jax.ShapeDtypeStruct((t, d), BF16),
                   jax.ShapeDtypeStruct((t, LANES), jnp.int32), jax.ShapeDtypeStruct((t, LANES), F32)),
        grid=(t // tm,),
        in_specs=[pl.BlockSpec((tm, 512), lambda i: (i, 0)),
                  pl.BlockSpec((tm, 512), lambda i: (i, 0)),
                  pl.BlockSpec((tm, 512), lambda i: (i, COL_Z // 512)),
                  pl.BlockSpec((tm, d), lambda i: (i, 0)),
                  pl.BlockSpec((tm, d), lambda i: (i, 1)),
                  pl.BlockSpec((tm, d), lambda i: (i, 2)),
                  pl.BlockSpec((tm, 512), lambda i: (i, 0)),
                  pl.BlockSpec((tm, 512), lambda i: (i, 0)),
                  pl.BlockSpec((tm, d), lambda i: (i, 0)),
                  full((512, d)), full((512, d)), full((512, d)), full((d, d)),
                  full((1, LANES)), full((1, d)), full((d, LANES)), full((1, LANES))],
        out_specs=(pl.BlockSpec((tm, d), lambda i: (i, 0)), pl.BlockSpec((tm, d), lambda i: (i, 0)),
                   pl.BlockSpec((tm, LANES), lambda i: (i, 0)), pl.BlockSpec((tm, LANES), lambda i: (i, 0))),
        compiler_params=_cparams(("arbitrary",)),
        name="merge_router",
    )(o_f, o_b, main2, main2, main2, main2, yb, yc, x2d, wa, wb, wc, wo, gnw, fnw, rw, rb)


def _expert_kernel(blk_e_ref, nused_ref, x_ref, w1_ref, w3_ref, w2_ref, o_ref):
    i = pl.program_id(0)

    @pl.when(i < nused_ref[0])
    def _():
        x = x_ref[...]
        a = jnp.dot(x, w1_ref[0], preferred_element_type=F32)
        u = jnp.dot(x, w3_ref[0], preferred_element_type=F32)
        hmid = (a * _sigmoid(a) * u).astype(BF16)
        o_ref[...] = jnp.dot(hmid, w2_ref[0], preferred_element_type=F32).astype(o_ref.dtype)

    @pl.when(i >= nused_ref[0])
    def _():
        o_ref[...] = jnp.zeros_like(o_ref)


def _experts(blk_e, nused, xb, w1, w3, w2):
    p_len, d = xb.shape
    ff = w1.shape[2]
    nblk = p_len // MOE_BLOCK
    return pl.pallas_call(
        _expert_kernel,
        out_shape=jax.ShapeDtypeStruct((p_len, d), BF16),
        grid_spec=pltpu.PrefetchScalarGridSpec(
            num_scalar_prefetch=2,
            grid=(nblk,),
            in_specs=[pl.BlockSpec((MOE_BLOCK, d), lambda i, be, nu: (i, 0)),
                      pl.BlockSpec((1, d, ff), lambda i, be, nu: (be[i], 0, 0)),
                      pl.BlockSpec((1, d, ff), lambda i, be, nu: (be[i], 0, 0)),
                      pl.BlockSpec((1, ff, d), lambda i, be, nu: (be[i], 0, 0))],
            out_specs=pl.BlockSpec((MOE_BLOCK, d), lambda i, be, nu: (i, 0)),
        ),
        compiler_params=_cparams(("arbitrary",)),
        name="expert_mlp",
    )(blk_e, nused, xb, w1, w3, w2)


def _combine_kernel(x_ref, y0_ref, y1_ref, w_ref, nw_ref, o_ref, *, final):
    w = w_ref[...]
    x = x_ref[...] + w[:, 0:1] * y0_ref[...].astype(F32) + w[:, 1:2] * y1_ref[...].astype(F32)
    if final:
        x = x * lax.rsqrt(jnp.mean(x * x, axis=-1, keepdims=True) + NORM_EPS) * nw_ref[...]
    o_ref[...] = x


def _combine(x2d, y0, y1, wts, norm_w, *, final, tm):
    t, d = x2d.shape
    tile = pl.BlockSpec((tm, d), lambda i: (i, 0))
    return pl.pallas_call(
        functools.partial(_combine_kernel, final=final),
        out_shape=jax.ShapeDtypeStruct((t, d), F32),
        grid=(t // tm,),
        in_specs=[tile, tile, tile, pl.BlockSpec((tm, LANES), lambda i: (i, 0)), pl.BlockSpec((1, d), lambda i: (0, 0))],
        out_specs=tile,
        compiler_params=_cparams(("arbitrary",)),
        name="moe_combine",
    )(x2d, y0, y1, wts, norm_w.reshape(1, d))


def _rope_tables(pos, dim):
    inv = 1.0 / (ROPE_THETA ** (jnp.arange(0, dim, 2, dtype=F32) / dim))
    ang = pos.astype(F32)[:, None] * inv[None, :]
    ang = jnp.concatenate([ang, ang], axis=-1)
    return jnp.cos(ang), jnp.sin(ang)


def _signed_sin(sin):
    half = sin.shape[-1] // 2
    return jnp.concatenate([-sin[:, :half], sin[:, half:]], axis=-1)


def _layout_w_in(w):
    o = 0
    parts = {}
    for name, size in (("qkv", 1536), ("z", 512), ("b", 8), ("a", 8), ("dq", 512), ("dk", 512), ("dv", 512),
                       ("cq", 512), ("ck", 128), ("cv", 128), ("gate", 3072)):
        parts[name] = w[:, o:o + size]
        o += size
    swap = lambda m: jnp.concatenate([m[:, 64:], m[:, :64]], axis=1)
    main = jnp.concatenate([parts["gate"], parts["qkv"], parts["z"], parts["dq"], parts["dk"], parts["dv"],
                            parts["cq"], parts["ck"], swap(parts["ck"]), parts["cv"], swap(parts["cv"])], axis=1)
    ba = jnp.concatenate([parts["b"], parts["a"], jnp.zeros((w.shape[0], LANES - 16), w.dtype)], axis=1)
    return main.astype(BF16), ba


def _rows_layout(t, bsz, s):
    nc = s // GDN_CHUNK
    t = t.reshape(bsz, nc, GDN_CHUNK, 2, GDN_HEADS)
    return jnp.transpose(t, (3, 0, 1, 4, 2)).reshape(2, bsz, nc, GDN_ROWS)


def _moe_dispatch(ids, t):
    a = t * TOPK
    p_len = ((a + N_EXPERTS * (MOE_BLOCK - 1) + MOE_BLOCK - 1) // MOE_BLOCK) * MOE_BLOCK
    n_blocks = p_len // MOE_BLOCK
    flat_e = ids.reshape(-1)
    iota_a = jnp.arange(a, dtype=jnp.int32)
    skey = jnp.sort(flat_e * a + iota_a)
    order = skey % a
    se = skey // a
    experts = jnp.arange(N_EXPERTS, dtype=jnp.int32)
    counts = jnp.sum((flat_e[:, None] == experts[None, :]).astype(jnp.int32), axis=0)
    start = jnp.cumsum(counts) - counts
    pcounts = ((counts + MOE_BLOCK - 1) // MOE_BLOCK) * MOE_BLOCK
    pend = jnp.cumsum(pcounts)
    pstart = pend - pcounts
    dest_sorted = pstart[se] + (iota_a - start[se])
    blk_first = jnp.arange(n_blocks, dtype=jnp.int32) * MOE_BLOCK
    blk_e = jnp.minimum(jnp.sum((pend[None, :] <= blk_first[:, None]).astype(jnp.int32), axis=1), N_EXPERTS - 1)
    row = jnp.arange(p_len, dtype=jnp.int32)
    row_e = jnp.repeat(blk_e, MOE_BLOCK)
    j = row - pstart[row_e]
    valid = j < counts[row_e]
    tok_buf = jnp.where(valid, order[jnp.minimum(start[row_e] + j, a - 1)] // TOPK, t)
    _, dest = lax.sort((order, dest_sorted), num_keys=1)
    nused = (pend[-1] // MOE_BLOCK).astype(jnp.int32).reshape(1)
    return tok_buf, dest.reshape(t, TOPK), blk_e, nused


def kernel(x, attn_norm_w, w_in, gdn_conv_w, gdn_a_log, gdn_dt_bias, gdn_norm_w, diff_lambda, diff_norm_w,
           gqa_q_norm_w, gqa_k_norm_w, w_branch_a, w_branch_b, w_branch_c, w_out, ffn_norm_w,
           router_group_w, router_group_b, router_expert_w, router_expert_b,
           expert_w_gate, expert_w_up, expert_w_down, final_norm_w):
    bsz, s, d = x.shape
    t = bsz * s
    depth = w_in.shape[0]
    tm = min(512, t)
    ts = min(512, s)

    rows = s // GRID_W
    row = jnp.broadcast_to(jnp.arange(rows)[:, None], (rows, GRID_W)).reshape(s)
    col = jnp.broadcast_to(jnp.arange(GRID_W)[None, :], (rows, GRID_W)).reshape(s)
    c1, s1 = _rope_tables(jnp.arange(s), DIFF_DQK)
    cr, sr = _rope_tables(row, GQA_DH // 2)
    cc, sc = _rope_tables(col, GQA_DH // 2)
    cos1 = jnp.tile(c1, (1, 2))
    sin1 = jnp.tile(_signed_sin(s1), (1, 2))
    cos2 = jnp.tile(jnp.concatenate([cr, cc], axis=-1), (1, 2))
    sin2 = jnp.tile(jnp.concatenate([_signed_sin(sr), _signed_sin(sc)], axis=-1), (1, 2))

    x2 = x.reshape(t, d)
    for l in range(depth):
        lambda_init = 0.8 - 0.6 * math.exp(-0.3 * l)
        w_main, w_ba = _layout_w_in(w_in[l])
        main2 = _norm_proj(x2, attn_norm_w[l], w_main, BF16, exact=False, tm=tm, tn=1536)
        ba = _norm_proj(x2, attn_norm_w[l], w_ba, F32, exact=True, tm=tm, tn=LANES)
        main3 = main2.reshape(bsz, s, N_MAIN)

        conv_w = jnp.concatenate([gdn_conv_w[l], jnp.zeros((8 - GDN_CONV, gdn_conv_w.shape[2]), F32)], axis=0)
        qnw = jnp.tile(gqa_q_norm_w[l], 2).reshape(1, LANES)
        knw = jnp.tile(gqa_k_norm_w[l], 2).reshape(1, LANES)
        gq, gk, gv, dq, dk, dv, cq, ck, cv = _prep(main3, conv_w, cos1, sin1, cos2, sin2, qnw, knw, ts=ts)

        b_rows = _rows_layout(ba[:, 0:8], bsz, s)
        a_rows = _rows_layout(ba[:, 8:16], bsz, s)
        alog_row = jnp.repeat(gdn_a_log[l], GDN_CHUNK, axis=1).reshape(2, 1, GDN_ROWS)
        dtb_row = jnp.repeat(gdn_dt_bias[l], GDN_CHUNK, axis=1).reshape(2, 1, GDN_ROWS)
        o_f, o_b = _gdn(a_rows, b_rows, alog_row, dtb_row, gq, gk, gv, nbatch=GDN_NBATCH if bsz % GDN_NBATCH == 0 else 1)

        nw_diff = diff_norm_w[l].reshape(LANES, 1)
        yb = _attention(dq, dk, dv, diff_lambda[l], nw_diff, mode="diff",
                        tq=min(ATTN_TQ, s), tk=min(ATTN_TK, s), lambda_init=lambda_init)
        yc = _attention(cq, ck, cv, diff_lambda[l], nw_diff, mode="gqa",
                        tq=min(ATTN_TQ, s), tk=min(ATTN_TK, s))

        rw = jnp.concatenate([router_group_w[l], router_expert_w[l],
                              jnp.zeros((d, LANES - N_GROUPS - N_EXPERTS), F32)], axis=1)
        rb = jnp.concatenate([router_group_b[l], router_expert_b[l],
                              jnp.zeros((LANES - N_GROUPS - N_EXPERTS,), F32)]).reshape(1, LANES)
        x2, h2, ids, wts = _merge(o_f.reshape(t, 512), o_b.reshape(t, 512), main2, yb.reshape(t, 512), yc.reshape(t, 512), x2,
                                  w_branch_a[l].astype(BF16), w_branch_b[l].astype(BF16),
                                  w_branch_c[l].astype(BF16), w_out[l].astype(BF16),
                                  gdn_norm_w[l].reshape(1, LANES), ffn_norm_w[l].reshape(1, d), rw, rb, tm=min(256, t))

        tok_buf, dest, blk_e, nused = _moe_dispatch(ids[:, :TOPK], t)
        h_pad = jnp.concatenate([h2, jnp.zeros((1, d), BF16)], axis=0)
        yblk = _experts(blk_e, nused, h_pad[tok_buf], expert_w_gate[l].astype(BF16),
                        expert_w_up[l].astype(BF16), expert_w_down[l].astype(BF16))
        x2 = _combine(x2, yblk[dest[:, 0]], yblk[dest[:, 1]], wts, final_norm_w, final=(l == depth - 1), tm=tm)

    return x2.reshape(bsz, s, d)
```

```python
import functools
import math

import jax
import jax.numpy as jnp
from jax import lax
from jax.experimental import pallas as pl
from jax.experimental.pallas import tpu as pltpu

GRID_W = 64
ROPE_THETA = 10000.0
NORM_EPS = 1e-6
GDN_HEADS = 4
GDN_DK = 128
GDN_DV = 128
GDN_CONV = 5
GDN_CHUNK = 64
DIFF_HEADS = 4
DIFF_DQK = 64
GQA_HEADS = 8
GQA_KV = 2
GQA_DH = 64
N_GROUPS = 4
EXPERTS_PER_GROUP = 8
N_EXPERTS = N_GROUPS * EXPERTS_PER_GROUP
TOPK = 2
MOE_BLOCK = 256

LANES = 128
VMEM_LIMIT = 56 * 1024 * 1024

COL_GATE = 0
COL_QKV = 3072
COL_Z = 4608
COL_DQ = 5120
COL_DK = 5632
COL_DV = 6144
COL_CQ = 6656
COL_CK = 7168
COL_CV = 7424
N_MAIN = 7680

LOG2E = math.log2(math.e)
ATTN_TQ = 1024
ATTN_TK = 1024
ATTN_STAB_KEYS = 256
VT_PAD = 16

HI = lax.Precision.HIGHEST
F32 = jnp.float32
BF16 = jnp.bfloat16


def _cparams(sem):
    return pltpu.CompilerParams(dimension_semantics=sem, vmem_limit_bytes=VMEM_LIMIT)


def _sigmoid(x):
    return 1.0 / (1.0 + jnp.exp(-x))


def _norm_proj_kernel(x_ref, nw_ref, w_ref, o_ref, *, exact):
    x = x_ref[...]
    h = x * lax.rsqrt(jnp.mean(x * x, axis=-1, keepdims=True) + NORM_EPS) * nw_ref[...]
    if exact:
        o_ref[...] = jnp.dot(h, w_ref[...], precision=HI, preferred_element_type=F32).astype(o_ref.dtype)
    else:
        o_ref[...] = jnp.dot(h.astype(BF16), w_ref[...], preferred_element_type=F32).astype(o_ref.dtype)


def _norm_proj(x2d, norm_w, w, out_dtype, *, exact, tm, tn):
    t, d = x2d.shape
    n = w.shape[1]
    return pl.pallas_call(
        functools.partial(_norm_proj_kernel, exact=exact),
        out_shape=jax.ShapeDtypeStruct((t, n), out_dtype),
        grid=(n // tn, t // tm),
        in_specs=[pl.BlockSpec((tm, d), lambda j, i: (i, 0)),
                  pl.BlockSpec((1, d), lambda j, i: (0, 0)),
                  pl.BlockSpec((d, tn), lambda j, i: (0, j))],
        out_specs=pl.BlockSpec((tm, tn), lambda j, i: (i, j)),
        compiler_params=_cparams(("arbitrary", "arbitrary")),
        name="norm_proj_exact" if exact else "norm_proj",
    )(x2d, norm_w.reshape(1, d), w)


HALO = 16


def _rot_half(x, half):
    lane = lax.broadcasted_iota(jnp.int32, x.shape, 1)
    first = (lane % (2 * half)) < half
    return jnp.where(first, pltpu.roll(x, LANES - half, 1), pltpu.roll(x, half, 1))


def _group_sumsq(x, width):
    x2 = x * x
    if width == LANES:
        return jnp.sum(x2, axis=-1, keepdims=True)
    lane = lax.broadcasted_iota(jnp.int32, x.shape, 1)
    lo = lane < width
    s_lo = jnp.sum(jnp.where(lo, x2, 0.0), axis=-1, keepdims=True)
    s_hi = jnp.sum(jnp.where(lo, 0.0, x2), axis=-1, keepdims=True)
    return jnp.where(lo, s_lo, s_hi)


def _aug_slab(x, m):
    lane = lax.broadcasted_iota(jnp.int32, x.shape, 1)
    half = LANES // 2
    keep = (lane < half) if m == 0 else (lane >= half)
    one = jnp.where(lane == (1 - m) * half, 1.0, 0.0).astype(x.dtype)
    return jnp.where(keep, x, one)


def _prep_kernel(qkv_ref, prev_ref, next_ref, dq_ref, dk_ref, dv_ref, cq_ref, ck_ref, cv_ref,
                 convw_ref, cos1_ref, sin1_ref, cos2_ref, sin2_ref, qnw_ref, knw_ref,
                 gq_ref, gk_ref, gv_ref, dqo_ref, dko_ref, dvo_ref, cqo_ref, cko_ref, cvo_ref, *, ts):
    i = pl.program_id(1)
    n = pl.num_programs(1)
    cur = qkv_ref[0].astype(F32)
    prev = jnp.where(i > 0, prev_ref[0].astype(F32), 0.0)
    nxt = jnp.where(i < n - 1, next_ref[0].astype(F32), 0.0)
    ext = jnp.concatenate([prev, cur, nxt], axis=0)
    pad = GDN_CONV // 2
    acc = jnp.zeros_like(cur)
    for j in range(GDN_CONV):
        off = HALO - pad + j
        acc = acc + ext[off:off + ts, :] * convw_ref[j:j + 1, :]
    act = acc * _sigmoid(acc)
    nqk = GDN_HEADS * GDN_DK
    for h in range(GDN_HEADS):
        sl = slice(h * GDN_DK, (h + 1) * GDN_DK)
        qh = act[:, sl]
        gq_ref[0, :, sl] = (qh * lax.rsqrt(_group_sumsq(qh, LANES) + NORM_EPS) * (GDN_DK ** -0.5)).astype(BF16)
        kh = act[:, nqk + h * GDN_DK: nqk + (h + 1) * GDN_DK]
        gk_ref[0, :, sl] = (kh * lax.rsqrt(_group_sumsq(kh, LANES) + NORM_EPS)).astype(BF16)
    gv_ref[0] = act[:, 2 * nqk:].astype(BF16)
    cos1, sin1 = cos1_ref[...], sin1_ref[...]
    ones_rows = jnp.where(lax.broadcasted_iota(jnp.int32, (VT_PAD, ts), 0) == 0, 1.0, 0.0).astype(BF16)
    for p in range(DIFF_HEADS):
        sl = slice(p * LANES, (p + 1) * LANES)
        xq = dq_ref[0, :, sl].astype(F32)
        dqo_ref[0, :, sl] = ((xq * cos1 + _rot_half(xq, DIFF_DQK // 2) * sin1) * (DIFF_DQK ** -0.5 * LOG2E)).astype(BF16)
        xk = dk_ref[0, :, sl].astype(F32)
        xk = (xk * cos1 + _rot_half(xk, DIFF_DQK // 2) * sin1).astype(BF16)
        for m in range(2):
            dko_ref[0, :, (2 * p + m) * LANES:(2 * p + m + 1) * LANES] = _aug_slab(xk, m)
        r0 = p * (LANES + VT_PAD)
        dvo_ref[0, r0:r0 + LANES, :] = dv_ref[0, :, sl].astype(F32).T.astype(BF16)
        dvo_ref[0, r0 + LANES:r0 + LANES + VT_PAD, :] = ones_rows
    cos2, sin2 = cos2_ref[...], sin2_ref[...]
    for p in range(GQA_HEADS * GQA_DH // LANES):
        sl = slice(p * LANES, (p + 1) * LANES)
        xq = cq_ref[0, :, sl].astype(F32)
        xq = xq * lax.rsqrt(_group_sumsq(xq, GQA_DH) * (1.0 / GQA_DH) + NORM_EPS) * qnw_ref[...]
        cqo_ref[0, :, sl] = ((xq * cos2 + _rot_half(xq, GQA_DH // 4) * sin2) * (GQA_DH ** -0.5 * LOG2E)).astype(BF16)
    for p in range(2):
        sl = slice(p * LANES, (p + 1) * LANES)
        xk = ck_ref[0, :, sl].astype(F32)
        xk = xk * lax.rsqrt(_group_sumsq(xk, GQA_DH) * (1.0 / GQA_DH) + NORM_EPS) * knw_ref[...]
        xk = (xk * cos2 + _rot_half(xk, GQA_DH // 4) * sin2).astype(BF16)
        for m in range(2):
            c = p if m == 0 else 1 - p
            cko_ref[0, :, (2 * c + m) * LANES:(2 * c + m + 1) * LANES] = _aug_slab(xk, m)
    vt = cv_ref[0].astype(F32).T.astype(BF16)
    for c in range(GQA_KV):
        r0 = c * (GQA_DH + VT_PAD)
        cvo_ref[0, r0:r0 + GQA_DH, :] = vt[c * GQA_DH:(c + 1) * GQA_DH]
        cvo_ref[0, r0 + GQA_DH:r0 + GQA_DH + VT_PAD, :] = ones_rows


def _prep(main3, conv_w, cos1, sin1, cos2, sin2, qnw, knw, *, ts):
    b, s, _ = main3.shape
    nt = s // ts
    hb = ts // HALO
    last = s // HALO - 1
    row = lambda w: pl.BlockSpec((1, w), lambda bi, i: (0, 0))
    tab = pl.BlockSpec((ts, LANES), lambda bi, i: (i, 0))
    col = lambda w, off: pl.BlockSpec((1, ts, w), lambda bi, i: (bi, i, off // w))
    out = lambda w: pl.BlockSpec((1, ts, w), lambda bi, i: (bi, i, 0))
    outs = [("tok", 512), ("tok", 512), ("tok", 512), ("tok", 512), ("tok", 1024),
            ("rows", DIFF_HEADS * (LANES + VT_PAD)), ("tok", 512), ("tok", 512), ("rows", GQA_KV * (GQA_DH + VT_PAD))]
    specs = tuple(out(w) if kind == "tok" else pl.BlockSpec((1, w, ts), lambda bi, i: (bi, 0, i)) for kind, w in outs)
    shapes = tuple(jax.ShapeDtypeStruct((b, s, w) if kind == "tok" else (b, w, s), BF16) for kind, w in outs)
    return pl.pallas_call(
        functools.partial(_prep_kernel, ts=ts),
        out_shape=shapes,
        grid=(b, nt),
        in_specs=[
            col(1536, COL_QKV),
            pl.BlockSpec((1, HALO, 1536), lambda bi, i: (bi, jnp.maximum(i * hb - 1, 0), COL_QKV // 1536)),
            pl.BlockSpec((1, HALO, 1536), lambda bi, i: (bi, jnp.minimum((i + 1) * hb, last), COL_QKV // 1536)),
            col(512, COL_DQ), col(512, COL_DK), col(512, COL_DV), col(512, COL_CQ), col(256, COL_CK), col(128, COL_CV),
            pl.BlockSpec((8, 1536), lambda bi, i: (0, 0)),
            tab, tab, tab, tab, row(LANES), row(LANES),
        ],
        out_specs=specs,
        compiler_params=_cparams(("arbitrary", "arbitrary")),
        name="mixer_prep",
    )(*([main3] * 9), conv_w, cos1, sin1, cos2, sin2, qnw, knw)


GDN_G = 8
GDN_NBATCH = 2
GDN_ROWS = GDN_HEADS * GDN_CHUNK


def _stack_heads(x):
    return jnp.concatenate([x[:, h * LANES:(h + 1) * LANES] for h in range(GDN_HEADS)], axis=0)


def _row_to_col(row, eye):
    return jnp.sum(jnp.where(eye, row, 0.0), axis=1, keepdims=True)


def _gdn_kernel(af_ref, ab_ref, bf_ref, bb_ref, alog_ref, dtb_ref, qf_ref, kf_ref, vf_ref, qb_ref, kb_ref, vb_ref,
                of_ref, ob_ref, state_ref, gc_ref, gt_ref, beta_ref, *, nbatch):
    blk = pl.program_id(1)
    n = GDN_ROWS
    c = GDN_CHUNK

    @pl.when(blk == 0)
    def _():
        state_ref[...] = jnp.zeros_like(state_ref)

    ri = lax.broadcasted_iota(jnp.int32, (n, n), 0)
    ci = lax.broadcasted_iota(jnp.int32, (n, n), 1)
    same = (ri // c) == (ci // c)
    eye = ri == ci
    ti = lax.broadcasted_iota(jnp.int32, (n, GDN_HEADS * LANES), 0)
    tj = lax.broadcasted_iota(jnp.int32, (n, GDN_HEADS * LANES), 1)
    tot_m = jnp.where((ti // c) == (tj // LANES), 1.0, 0.0)

    chains = []
    for d, (a_ref, b_ref, q_ref, k_ref, v_ref, o_ref) in enumerate(
            ((af_ref, bf_ref, qf_ref, kf_ref, vf_ref, of_ref), (ab_ref, bb_ref, qb_ref, kb_ref, vb_ref, ob_ref))):
        sgn = 1 - 2 * d
        after = same & ((ri - ci) * sgn > 0)
        incl = same & ((ri - ci) * sgn >= 0)
        cum_m = jnp.where(same & ((ci - ri) * sgn >= 0), 1.0, 0.0)
        for bi in range(nbatch):
            ch = d * nbatch + bi
            x = a_ref[0, bi] + dtb_ref[d]
            softplus = jnp.maximum(x, 0.0) + jnp.log(1.0 + jnp.exp(-jnp.abs(x)))
            g = -jnp.exp(alog_ref[d]) * softplus
            beta_ref[ch] = _sigmoid(b_ref[0, bi])
            gc_ref[ch] = jnp.dot(g, cum_m, precision=HI, preferred_element_type=F32)
            gt_ref[ch] = jnp.dot(g, tot_m, precision=HI, preferred_element_type=F32)
            chains.append((ch, d, bi, after, incl, q_ref, k_ref, v_ref, o_ref))

    def chunk(j, chain):
        ch, d, bi, after, incl, q_ref, k_ref, v_ref, o_ref = chain
        cc = j if d == 0 else GDN_G - 1 - j
        r0 = pl.multiple_of(cc * c, c)
        gc_row = gc_ref[ch, pl.ds(cc, 1), :]
        beta_row = beta_ref[ch, pl.ds(cc, 1), :]
        gt_row = gt_ref[ch, pl.ds(cc, 1), :]
        gc_col = _row_to_col(gc_row, eye)
        beta_col = _row_to_col(beta_row, eye)
        k_st = _stack_heads(k_ref[bi, pl.ds(r0, c), :]).astype(F32)
        q_st = _stack_heads(q_ref[bi, pl.ds(r0, c), :]).astype(F32)
        v_st = _stack_heads(v_ref[bi, pl.ds(r0, c), :]).astype(F32)
        egc = jnp.exp(gc_col)
        decay = jnp.exp(jnp.minimum(gc_col - gc_row, 0.0))
        kb = k_st * beta_col
        k_bf = k_st.astype(BF16)
        kk = lax.dot_general(kb.astype(BF16), k_bf, (((1,), (1,)), ((), ())), preferred_element_type=F32)
        qk = lax.dot_general(q_st.astype(BF16), k_bf, (((1,), (1,)), ((), ())), preferred_element_type=F32)
        yield
        neg_a = jnp.where(after, -(kk * decay), 0.0)
        t_m = jnp.where(eye, 1.0, 0.0) + neg_a
        p_m = neg_a
        for _ in range(int(math.log2(c)) - 1):
            p_bf = p_m.astype(BF16)
            p_m = jnp.dot(p_bf, p_bf, preferred_element_type=F32)
            yield
            t_m = t_m + jnp.dot(t_m.astype(BF16), p_m.astype(BF16), preferred_element_type=F32)
            yield
        rhs = jnp.concatenate([v_st * beta_col, kb * egc], axis=1).astype(BF16)
        sol = jnp.dot(t_m.astype(BF16), rhs, preferred_element_type=F32)
        yield
        u_st, w_st = sol[:, :LANES], sol[:, LANES:]
        intra = jnp.where(incl, qk * decay, 0.0).astype(BF16)
        q_dec = (q_st * egc).astype(BF16)
        vn, oq = [], []
        for h in range(GDN_HEADS):
            rs = slice(h * c, (h + 1) * c)
            s_h = state_ref[ch * GDN_HEADS + h].astype(BF16)
            vn.append(u_st[rs] - jnp.dot(w_st[rs].astype(BF16), s_h, preferred_element_type=F32))
            oq.append(jnp.dot(q_dec[rs], s_h, preferred_element_type=F32))
        yield
        vn_st = jnp.concatenate(vn, axis=0)
        o_st = jnp.concatenate(oq, axis=0) + jnp.dot(intra, vn_st.astype(BF16), preferred_element_type=F32)
        for h in range(GDN_HEADS):
            rs = slice(h * c, (h + 1) * c)
            gt_h = gt_row[:, h * LANES:(h + 1) * LANES]
            k_dec = (k_st[rs] * jnp.exp(gt_h[:, :1] - gc_col[rs])).astype(BF16)
            upd = lax.dot_general(k_dec, vn[h].astype(BF16), (((0,), (0,)), ((), ())), preferred_element_type=F32)
            state_ref[ch * GDN_HEADS + h] = state_ref[ch * GDN_HEADS + h] * jnp.exp(gt_h) + upd
            o_ref[bi, pl.ds(r0, c), h * LANES:(h + 1) * LANES] = o_st[rs]

    def step(j, carry):
        active = [chunk(j, chain) for chain in chains]
        while active:
            active = [g for g in active if next(g, active) is not active]
        return carry

    lax.fori_loop(0, GDN_G, step, 0)


def _gdn(a_rows, b_rows, alog_row, dtb_row, gq, gk, gv, *, nbatch):
    b, s, _ = gq.shape
    nb = s // (GDN_G * GDN_CHUNK)
    ts = GDN_G * GDN_CHUNK
    nchain = 2 * nbatch
    tok_f = pl.BlockSpec((nbatch, ts, 512), lambda bi, i: (bi, i, 0))
    tok_b = pl.BlockSpec((nbatch, ts, 512), lambda bi, i: (bi, nb - 1 - i, 0))
    rows_f = pl.BlockSpec((1, nbatch, GDN_G, GDN_ROWS), lambda bi, i: (0, bi, i, 0))
    rows_b = pl.BlockSpec((1, nbatch, GDN_G, GDN_ROWS), lambda bi, i: (1, bi, nb - 1 - i, 0))
    par = pl.BlockSpec((2, 1, GDN_ROWS), lambda bi, i: (0, 0, 0))
    return pl.pallas_call(
        functools.partial(_gdn_kernel, nbatch=nbatch),
        out_shape=(jax.ShapeDtypeStruct((b, s, 512), F32), jax.ShapeDtypeStruct((b, s, 512), F32)),
        grid=(b // nbatch, nb),
        in_specs=[rows_f, rows_b, rows_f, rows_b, par, par, tok_f, tok_f, tok_f, tok_b, tok_b, tok_b],
        out_specs=(tok_f, tok_b),
        scratch_shapes=[pltpu.VMEM((nchain * GDN_HEADS, GDN_DK, GDN_DV), F32),
                        pltpu.VMEM((nchain, GDN_G, GDN_ROWS), F32),
                        pltpu.VMEM((nchain, GDN_G, GDN_HEADS * LANES), F32),
                        pltpu.VMEM((nchain, GDN_G, GDN_ROWS), F32)],
        compiler_params=_cparams(("arbitrary", "arbitrary")),
        name="gdn_chunked",
    )(a_rows, a_rows, b_rows, b_rows, alog_row, dtb_row, gq, gk, gv, gq, gk, gv)


def _attn_kernel(q_ref, k0_ref, k1_ref, vt_ref, lam_ref, nw_ref, o_ref, acc_ref, *, mode, tk, lambda_init):
    s_len = k0_ref.shape[1]
    tq = q_ref.shape[1]
    rows = vt_ref.shape[1]
    dv = rows - VT_PAD
    half = LANES // 2
    q = q_ref[0]
    lane = lax.broadcasted_iota(jnp.int32, q.shape, 1)
    keep = (lane < half, lane >= half)
    stab = (lane == half, lane == 0)
    zero = jnp.zeros_like(q)
    krefs = (k0_ref, k1_ref)
    nchunks = s_len // tk
    dn = (((1,), (1,)), ((), ()))

    def kchunk(m, ci):
        return krefs[m][0, pl.ds(pl.multiple_of(ci * tk, tk), tk), :]

    def vchunk(ci):
        return vt_ref[0, :, pl.ds(pl.multiple_of(ci * tk, tk), tk)]

    qm, qa = [], []
    for m in range(2):
        qm.append(jnp.where(keep[m], q, zero))
        k_first = krefs[m][0, 0:min(ATTN_STAB_KEYS, s_len), :]
        mx = jnp.max(lax.dot_general(qm[m], k_first, dn, preferred_element_type=F32), axis=-1, keepdims=True)
        qa.append(jnp.where(stab[m], (-mx).astype(BF16), qm[m]))

    def fast(ci, acc):
        st = [lax.dot_general(kchunk(m, ci), qa[m], dn, preferred_element_type=F32) for m in range(2)]
        vt = vchunk(ci)
        return tuple(acc[m] + jnp.dot(vt, jnp.exp2(st[m]).astype(BF16), preferred_element_type=F32) for m in range(2))

    acc = lax.fori_loop(0, nchunks, fast, tuple(jnp.zeros((rows, tq), F32) for _ in range(2)))
    nonfinite = jnp.float32(0.0)
    for m in range(2):
        acc_ref[m] = acc[m]
        nonfinite = nonfinite + jnp.sum(jnp.where(jnp.isfinite(acc[m]), 0.0, 1.0))

    @pl.when(nonfinite > 0.0)
    def _():
        def slow(ci, carry):
            out = []
            vt = vchunk(ci)
            for m in range(2):
                m_i, a_i = carry[m]
                st = lax.dot_general(kchunk(m, ci), qm[m], dn, preferred_element_type=F32)
                m_new = jnp.maximum(m_i, jnp.max(st, axis=0, keepdims=True))
                p = jnp.exp2(st - m_new).astype(BF16)
                out.append((m_new, jnp.exp2(m_i - m_new) * a_i + jnp.dot(vt, p, preferred_element_type=F32)))
            return tuple(out)

        init = tuple((jnp.full((1, tq), -jnp.inf, F32), jnp.zeros((rows, tq), F32)) for _ in range(2))
        res = lax.fori_loop(0, nchunks, slow, init)
        for m in range(2):
            acc_ref[m] = res[m][1]

    o0 = acc_ref[0, 0:dv, :] / acc_ref[0, dv:dv + 1, :]
    o1 = acc_ref[1, 0:dv, :] / acc_ref[1, dv:dv + 1, :]
    if mode == "diff":
        lv = lam_ref[...]
        lam = (jnp.exp(jnp.sum(lv[0:1] * lv[1:2], axis=-1, keepdims=True))
               - jnp.exp(jnp.sum(lv[2:3] * lv[3:4], axis=-1, keepdims=True)) + lambda_init)
        ot = o0 - lam * o1
        ot = ot * lax.rsqrt(jnp.mean(ot * ot, axis=0, keepdims=True) + NORM_EPS) * nw_ref[...] * (1.0 - lambda_init)
    else:
        ot = jnp.concatenate([o0, o1], axis=0)
    o_ref[0] = ot.T.astype(o_ref.dtype)


def _attention(q, k_arr, vt_arr, lam_vecs, norm_w, *, mode, tq, tk, lambda_init=0.0):
    b, s, w = q.shape
    slabs = w // LANES
    if mode == "diff":
        rows = LANES + VT_PAD
        k_col = lambda p, m: 2 * p + m
        v_grp = lambda p: p
    else:
        rows = GQA_DH + VT_PAD
        k_col = lambda p, m: 2 * (p // 2) + m
        v_grp = lambda p: p // 2
    kspec = lambda m: pl.BlockSpec((1, s, LANES), lambda bi, p, i: (bi, 0, k_col(p, m)))
    return pl.pallas_call(
        functools.partial(_attn_kernel, mode=mode, tk=tk, lambda_init=lambda_init),
        out_shape=jax.ShapeDtypeStruct((b, s, w), BF16),
        grid=(b, slabs, s // tq),
        in_specs=[pl.BlockSpec((1, tq, LANES), lambda bi, p, i: (bi, i, p)),
                  kspec(0), kspec(1),
                  pl.BlockSpec((1, rows, s), lambda bi, p, i: (bi, v_grp(p), 0)),
                  pl.BlockSpec((4, DIFF_DQK), lambda bi, p, i: (0, 0)),
                  pl.BlockSpec((LANES, 1), lambda bi, p, i: (0, 0))],
        out_specs=pl.BlockSpec((1, tq, LANES), lambda bi, p, i: (bi, i, p)),
        scratch_shapes=[pltpu.VMEM((2, rows, tq), F32)],
        compiler_params=_cparams(("arbitrary", "arbitrary", "arbitrary")),
        name="attn_" + mode,
    )(q, k_arr, k_arr, vt_arr, lam_vecs, norm_w)


def _merge_kernel(of_ref, ob_ref, z_ref, g0_ref, g1_ref, g2_ref, yb_ref, yc_ref, x_ref,
                  wa_ref, wb_ref, wc_ref, wo_ref, gnw_ref, fnw_ref, rw_ref, rb_ref,
                  xo_ref, h_ref, id_ref, rwgt_ref):
    o = of_ref[...] + ob_ref[...]
    parts = []
    for h in range(GDN_HEADS):
        oh = o[:, h * LANES:(h + 1) * LANES]
        parts.append(oh * lax.rsqrt(jnp.mean(oh * oh, axis=-1, keepdims=True) + NORM_EPS) * gnw_ref[...])
    z = z_ref[...].astype(F32)
    ya = (jnp.concatenate(parts, axis=1) * (z * _sigmoid(z))).astype(BF16)
    merged = _sigmoid(g0_ref[...].astype(F32)) * jnp.dot(ya, wa_ref[...], preferred_element_type=F32)
    merged = merged + _sigmoid(g1_ref[...].astype(F32)) * jnp.dot(yb_ref[...], wb_ref[...], preferred_element_type=F32)
    merged = merged + _sigmoid(g2_ref[...].astype(F32)) * jnp.dot(yc_ref[...], wc_ref[...], preferred_element_type=F32)
    xn = x_ref[...] + jnp.dot(merged.astype(BF16), wo_ref[...], preferred_element_type=F32)
    xo_ref[...] = xn
    hf = xn * lax.rsqrt(jnp.mean(xn * xn, axis=-1, keepdims=True) + NORM_EPS) * fnw_ref[...]
    h_ref[...] = hf.astype(BF16)
    logits = jnp.dot(hf, rw_ref[...], precision=HI, preferred_element_type=F32) + rb_ref[...]
    lane = lax.broadcasted_iota(jnp.int32, logits.shape, 1)
    big = jnp.int32(LANES)
    ninf = -jnp.inf
    glog = jnp.where(lane < N_GROUPS, logits, ninf)
    gmax = jnp.max(glog, axis=-1, keepdims=True)
    gidx = jnp.min(jnp.where(glog == gmax, lane, big), axis=-1, keepdims=True)
    gp = 1.0 / jnp.sum(jnp.exp(glog - gmax), axis=-1, keepdims=True)
    e = lane - N_GROUPS
    sel = (e >= 0) & (e < N_EXPERTS) & ((e // EXPERTS_PER_GROUP) == gidx)
    elog = jnp.where(sel, logits, ninf)
    m1 = jnp.max(elog, axis=-1, keepdims=True)
    i1 = jnp.min(jnp.where(elog == m1, lane, big), axis=-1, keepdims=True)
    elog2 = jnp.where(lane == i1, ninf, elog)
    m2 = jnp.max(elog2, axis=-1, keepdims=True)
    i2 = jnp.min(jnp.where(elog2 == m2, lane, big), axis=-1, keepdims=True)
    e2 = jnp.exp(m2 - m1)
    w1 = 1.0 / (1.0 + e2)
    w2 = e2 * w1
    id_ref[...] = jnp.where(lane == 0, i1 - N_GROUPS, jnp.where(lane == 1, i2 - N_GROUPS, 0))
    rwgt_ref[...] = jnp.where(lane == 0, gp * w1, jnp.where(lane == 1, gp * w2, 0.0))


def _merge(o_f, o_b, main2, yb, yc, x2d, wa, wb, wc, wo, gnw, fnw, rw, rb, *, tm):
    t, d = x2d.shape
    full = lambda shp: pl.BlockSpec(shp, lambda i: tuple(0 for _ in shp))
    return pl.pallas_call(
        _merge_kernel,
        out_shape=(jax.ShapeDtypeStruct((t, d), F32), jax.ShapeDtypeStruct((t, d), BF16),
                   jax.ShapeDtypeStruct((t, LANES), jnp.int32), jax.ShapeDtypeStruct((t, LANES), F32)),
        grid=(t // tm,),
        in_specs=[pl.BlockSpec((tm, 512), lambda i: (i, 0)),
                  pl.BlockSpec((tm, 512), lambda i: (i, 0)),
                  pl.BlockSpec((tm, 512), lambda i: (i, COL_Z // 512)),
                  pl.BlockSpec((tm, d), lambda i: (i, 0)),
                  pl.BlockSpec((tm, d), lambda i: (i, 1)),
                  pl.BlockSpec((tm, d), lambda i: (i, 2)),
                  pl.BlockSpec((tm, 512), lambda i: (i, 0)),
                  pl.BlockSpec((tm, 512), lambda i: (i, 0)),
                  pl.BlockSpec((tm, d), lambda i: (i, 0)),
                  full((512, d)), full((512, d)), full((512, d)), full((d, d)),
                  full((1, LANES)), full((1, d)), full((d, LANES)), full((1, LANES))],
        out_specs=(pl.BlockSpec((tm, d), lambda i: (i, 0)), pl.BlockSpec((tm, d), lambda i: (i, 0)),
                   pl.BlockSpec((tm, LANES), lambda i: (i, 0)), pl.BlockSpec((tm, LANES), lambda i: (i, 0))),
        compiler_params=_cparams(("arbitrary",)),
        name="merge_router",
    )(o_f, o_b, main2, main2, main2, main2, yb, yc, x2d, wa, wb, wc, wo, gnw, fnw, rw, rb)


def _expert_kernel(blk_e_ref, nused_ref, x_ref, w1_ref, w3_ref, w2_ref, o_ref):
    i = pl.program_id(0)

    @pl.when(i < nused_ref[0])
    def _():
        x = x_ref[...]
        a = jnp.dot(x, w1_ref[0], preferred_element_type=F32)
        u = jnp.dot(x, w3_ref[0], preferred_element_type=F32)
        hmid = (a * _sigmoid(a) * u).astype(BF16)
        o_ref[...] = jnp.dot(hmid, w2_ref[0], preferred_element_type=F32).astype(o_ref.dtype)

    @pl.when(i >= nused_ref[0])
    def _():
        o_ref[...] = jnp.zeros_like(o_ref)


def _experts(blk_e, nused, xb, w1, w3, w2):
    p_len, d = xb.shape
    ff = w1.shape[2]
    nblk = p_len // MOE_BLOCK
    return pl.pallas_call(
        _expert_kernel,
        out_shape=jax.ShapeDtypeStruct((p_len, d), BF16),
        grid_spec=pltpu.PrefetchScalarGridSpec(
            num_scalar_prefetch=2,
            grid=(nblk,),
            in_specs=[pl.BlockSpec((MOE_BLOCK, d), lambda i, be, nu: (i, 0)),
                      pl.BlockSpec((1, d, ff), lambda i, be, nu: (be[i], 0, 0)),
                      pl.BlockSpec((1, d, ff), lambda i, be, nu: (be[i], 0, 0)),
                      pl.BlockSpec((1, ff, d), lambda i, be, nu: (be[i], 0, 0))],
            out_specs=pl.BlockSpec((MOE_BLOCK, d), lambda i, be, nu: (i, 0)),
        ),
        compiler_params=_cparams(("arbitrary",)),
        name="expert_mlp",
    )(blk_e, nused, xb, w1, w3, w2)


def _combine_kernel(x_ref, y0_ref, y1_ref, w_ref, nw_ref, o_ref, *, final):
    w = w_ref[...]
    x = x_ref[...] + w[:, 0:1] * y0_ref[...].astype(F32) + w[:, 1:2] * y1_ref[...].astype(F32)
    if final:
        x = x * lax.rsqrt(jnp.mean(x * x, axis=-1, keepdims=True) + NORM_EPS) * nw_ref[...]
    o_ref[...] = x


def _combine(x2d, y0, y1, wts, norm_w, *, final, tm):
    t, d = x2d.shape
    tile = pl.BlockSpec((tm, d), lambda i: (i, 0))
    return pl.pallas_call(
        functools.partial(_combine_kernel, final=final),
        out_shape=jax.ShapeDtypeStruct((t, d), F32),
        grid=(t // tm,),
        in_specs=[tile, tile, tile, pl.BlockSpec((tm, LANES), lambda i: (i, 0)), pl.BlockSpec((1, d), lambda i: (0, 0))],
        out_specs=tile,
        compiler_params=_cparams(("arbitrary",)),
        name="moe_combine",
    )(x2d, y0, y1, wts, norm_w.reshape(1, d))


def _rope_tables(pos, dim):
    inv = 1.0 / (ROPE_THETA ** (jnp.arange(0, dim, 2, dtype=F32) / dim))
    ang = pos.astype(F32)[:, None] * inv[None, :]
    ang = jnp.concatenate([ang, ang], axis=-1)
    return jnp.cos(ang), jnp.sin(ang)


def _signed_sin(sin):
    half = sin.shape[-1] // 2
    return jnp.concatenate([-sin[:, :half], sin[:, half:]], axis=-1)


def _layout_w_in(w):
    o = 0
    parts = {}
    for name, size in (("qkv", 1536), ("z", 512), ("b", 8), ("a", 8), ("dq", 512), ("dk", 512), ("dv", 512),
                       ("cq", 512), ("ck", 128), ("cv", 128), ("gate", 3072)):
        parts[name] = w[:, o:o + size]
        o += size
    swap = lambda m: jnp.concatenate([m[:, 64:], m[:, :64]], axis=1)
    main = jnp.concatenate([parts["gate"], parts["qkv"], parts["z"], parts["dq"], parts["dk"], parts["dv"],
                            parts["cq"], parts["ck"], swap(parts["ck"]), parts["cv"], swap(parts["cv"])], axis=1)
    ba = jnp.concatenate([parts["b"], parts["a"], jnp.zeros((w.shape[0], LANES - 16), w.dtype)], axis=1)
    return main.astype(BF16), ba


def _rows_layout(t, bsz, s):
    nc = s // GDN_CHUNK
    t = t.reshape(bsz, nc, GDN_CHUNK, 2, GDN_HEADS)
    return jnp.transpose(t, (3, 0, 1, 4, 2)).reshape(2, bsz, nc, GDN_ROWS)


def _moe_dispatch(ids, t):
    a = t * TOPK
    p_len = ((a + N_EXPERTS * (MOE_BLOCK - 1) + MOE_BLOCK - 1) // MOE_BLOCK) * MOE_BLOCK
    n_blocks = p_len // MOE_BLOCK
    flat_e = ids.reshape(-1)
    iota_a = jnp.arange(a, dtype=jnp.int32)
    skey = jnp.sort(flat_e * a + iota_a)
    order = skey % a
    se = skey // a
    experts = jnp.arange(N_EXPERTS, dtype=jnp.int32)
    counts = jnp.sum((flat_e[:, None] == experts[None, :]).astype(jnp.int32), axis=0)
    start = jnp.cumsum(counts) - counts
    pcounts = ((counts + MOE_BLOCK - 1) // MOE_BLOCK) * MOE_BLOCK
    pend = jnp.cumsum(pcounts)
    pstart = pend - pcounts
    dest_sorted = pstart[se] + (iota_a - start[se])
    blk_first = jnp.arange(n_blocks, dtype=jnp.int32) * MOE_BLOCK
    blk_e = jnp.minimum(jnp.sum((pend[None, :] <= blk_first[:, None]).astype(jnp.int32), axis=1), N_EXPERTS - 1)
    row = jnp.arange(p_len, dtype=jnp.int32)
    row_e = jnp.repeat(blk_e, MOE_BLOCK)
    j = row - pstart[row_e]
    valid = j < counts[row_e]
    tok_buf = jnp.where(valid, order[jnp.minimum(start[row_e] + j, a - 1)] // TOPK, t)
    _, dest = lax.sort((order, dest_sorted), num_keys=1)
    nused = (pend[-1] // MOE_BLOCK).astype(jnp.int32).reshape(1)
    return tok_buf, dest.reshape(t, TOPK), blk_e, nused


def kernel(x, attn_norm_w, w_in, gdn_conv_w, gdn_a_log, gdn_dt_bias, gdn_norm_w, diff_lambda, diff_norm_w,
           gqa_q_norm_w, gqa_k_norm_w, w_branch_a, w_branch_b, w_branch_c, w_out, ffn_norm_w,
           router_group_w, router_group_b, router_expert_w, router_expert_b,
           expert_w_gate, expert_w_up, expert_w_down, final_norm_w):
    bsz, s, d = x.shape
    t = bsz * s
    depth = w_in.shape[0]
    tm = min(512, t)
    ts = min(512, s)

    rows = s // GRID_W
    row = jnp.broadcast_to(jnp.arange(rows)[:, None], (rows, GRID_W)).reshape(s)
    col = jnp.broadcast_to(jnp.arange(GRID_W)[None, :], (rows, GRID_W)).reshape(s)
    c1, s1 = _rope_tables(jnp.arange(s), DIFF_DQK)
    cr, sr = _rope_tables(row, GQA_DH // 2)
    cc, sc = _rope_tables(col, GQA_DH // 2)
    cos1 = jnp.tile(c1, (1, 2))
    sin1 = jnp.tile(_signed_sin(s1), (1, 2))
    cos2 = jnp.tile(jnp.concatenate([cr, cc], axis=-1), (1, 2))
    sin2 = jnp.tile(jnp.concatenate([_signed_sin(sr), _signed_sin(sc)], axis=-1), (1, 2))

    x2 = x.reshape(t, d)
    for l in range(depth):
        lambda_init = 0.8 - 0.6 * math.exp(-0.3 * l)
        w_main, w_ba = _layout_w_in(w_in[l])
        main2 = _norm_proj(x2, attn_norm_w[l], w_main, BF16, exact=False, tm=tm, tn=1536)
        ba = _norm_proj(x2, attn_norm_w[l], w_ba, F32, exact=True, tm=tm, tn=LANES)
        main3 = main2.reshape(bsz, s, N_MAIN)

        conv_w = jnp.concatenate([gdn_conv_w[l], jnp.zeros((8 - GDN_CONV, gdn_conv_w.shape[2]), F32)], axis=0)
        qnw = jnp.tile(gqa_q_norm_w[l], 2).reshape(1, LANES)
        knw = jnp.tile(gqa_k_norm_w[l], 2).reshape(1, LANES)
        gq, gk, gv, dq, dk, dv, cq, ck, cv = _prep(main3, conv_w, cos1, sin1, cos2, sin2, qnw, knw, ts=ts)

        b_rows = _rows_layout(ba[:, 0:8], bsz, s)
        a_rows = _rows_layout(ba[:, 8:16], bsz, s)
        alog_row = jnp.repeat(gdn_a_log[l], GDN_CHUNK, axis=1).reshape(2, 1, GDN_ROWS)
        dtb_row = jnp.repeat(gdn_dt_bias[l], GDN_CHUNK, axis=1).reshape(2, 1, GDN_ROWS)
        o_f, o_b = _gdn(a_rows, b_rows, alog_row, dtb_row, gq, gk, gv, nbatch=GDN_NBATCH if bsz % GDN_NBATCH == 0 else 1)

        nw_diff = diff_norm_w[l].reshape(LANES, 1)
        yb = _attention(dq, dk, dv, diff_lambda[l], nw_diff, mode="diff",
                        tq=min(ATTN_TQ, s), tk=min(ATTN_TK, s), lambda_init=lambda_init)
        yc = _attention(cq, ck, cv, diff_lambda[l], nw_diff, mode="gqa",
                        tq=min(ATTN_TQ, s), tk=min(ATTN_TK, s))

        rw = jnp.concatenate([router_group_w[l], router_expert_w[l],
                              jnp.zeros((d, LANES - N_GROUPS - N_EXPERTS), F32)], axis=1)
        rb = jnp.concatenate([router_group_b[l], router_expert_b[l],
                              jnp.zeros((LANES - N_GROUPS - N_EXPERTS,), F32)]).reshape(1, LANES)
        x2, h2, ids, wts = _merge(o_f.reshape(t, 512), o_b.reshape(t, 512), main2, yb.reshape(t, 512), yc.reshape(t, 512), x2,
                                  w_branch_a[l].astype(BF16), w_branch_b[l].astype(BF16),
                                  w_branch_c[l].astype(BF16), w_out[l].astype(BF16),
                                  gdn_norm_w[l].reshape(1, LANES), ffn_norm_w[l].reshape(1, d), rw, rb, tm=min(256, t))

        tok_buf, dest, blk_e, nused = _moe_dispatch(ids[:, :TOPK], t)
        h_pad = jnp.concatenate([h2, jnp.zeros((1, d), BF16)], axis=0)
        yblk = _experts(blk_e, nused, h_pad[tok_buf], expert_w_gate[l].astype(BF16),
                        expert_w_up[l].astype(BF16), expert_w_down[l].astype(BF16))
        x2 = _combine(x2, yblk[dest[:, 0]], yblk[dest[:, 1]], wts, final_norm_w, final=(l == depth - 1), tm=tm)

    return x2.reshape(bsz, s, d)
```

```python
import functools
import math

import jax
import jax.numpy as jnp
from jax import lax
from jax.experimental import pallas as pl
from jax.experimental.pallas import tpu as pltpu

GRID_W = 64
ROPE_THETA = 10000.0
NORM_EPS = 1e-6
GDN_HEADS = 4
GDN_DK = 128
GDN_DV = 128
GDN_CONV = 5
GDN_CHUNK = 64
DIFF_HEADS = 4
DIFF_DQK = 64
GQA_HEADS = 8
GQA_KV = 2
GQA_DH = 64
N_GROUPS = 4
EXPERTS_PER_GROUP = 8
N_EXPERTS = N_GROUPS * EXPERTS_PER_GROUP
TOPK = 2
MOE_BLOCK = 256

LANES = 128
VMEM_LIMIT = 56 * 1024 * 1024

COL_GATE = 0
COL_QKV = 3072
COL_Z = 4608
COL_DQ = 5120
COL_DK = 5632
COL_DV = 6144
COL_CQ = 6656
COL_CK = 7168
COL_CV = 7424
N_MAIN = 7680

LOG2E = math.log2(math.e)
ATTN_TQ = 1024
ATTN_TK = 1024
ATTN_STAB_KEYS = 256
VT_PAD = 16

HI = lax.Precision.HIGHEST
F32 = jnp.float32
BF16 = jnp.bfloat16


def _cparams(sem):
    return pltpu.CompilerParams(dimension_semantics=sem, vmem_limit_bytes=VMEM_LIMIT)


def _sigmoid(x):
    return 1.0 / (1.0 + jnp.exp(-x))


def _norm_proj_kernel(x_ref, nw_ref, w_ref, o_ref, *, exact):
    x = x_ref[...]
    h = x * lax.rsqrt(jnp.mean(x * x, axis=-1, keepdims=True) + NORM_EPS) * nw_ref[...]
    if exact:
        o_ref[...] = jnp.dot(h, w_ref[...], precision=HI, preferred_element_type=F32).astype(o_ref.dtype)
    else:
        o_ref[...] = jnp.dot(h.astype(BF16), w_ref[...], preferred_element_type=F32).astype(o_ref.dtype)


def _norm_proj(x2d, norm_w, w, out_dtype, *, exact, tm, tn):
    t, d = x2d.shape
    n = w.shape[1]
    return pl.pallas_call(
        functools.partial(_norm_proj_kernel, exact=exact),
        out_shape=jax.ShapeDtypeStruct((t, n), out_dtype),
        grid=(n // tn, t // tm),
        in_specs=[pl.BlockSpec((tm, d), lambda j, i: (i, 0)),
                  pl.BlockSpec((1, d), lambda j, i: (0, 0)),
                  pl.BlockSpec((d, tn), lambda j, i: (0, j))],
        out_specs=pl.BlockSpec((tm, tn), lambda j, i: (i, j)),
        compiler_params=_cparams(("arbitrary", "arbitrary")),
        name="norm_proj_exact" if exact else "norm_proj",
    )(x2d, norm_w.reshape(1, d), w)


HALO = 16


def _rot_half(x, half):
    lane = lax.broadcasted_iota(jnp.int32, x.shape, 1)
    first = (lane % (2 * half)) < half
    return jnp.where(first, pltpu.roll(x, LANES - half, 1), pltpu.roll(x, half, 1))


def _group_sumsq(x, width):
    x2 = x * x
    if width == LANES:
        return jnp.sum(x2, axis=-1, keepdims=True)
    lane = lax.broadcasted_iota(jnp.int32, x.shape, 1)
    lo = lane < width
    s_lo = jnp.sum(jnp.where(lo, x2, 0.0), axis=-1, keepdims=True)
    s_hi = jnp.sum(jnp.where(lo, 0.0, x2), axis=-1, keepdims=True)
    return jnp.where(lo, s_lo, s_hi)


def _aug_slab(x, m):
    lane = lax.broadcasted_iota(jnp.int32, x.shape, 1)
    half = LANES // 2
    keep = (lane < half) if m == 0 else (lane >= half)
    one = jnp.where(lane == (1 - m) * half, 1.0, 0.0).astype(x.dtype)
    return jnp.where(keep, x, one)


def _prep_kernel(qkv_ref, prev_ref, next_ref, dq_ref, dk_ref, dv_ref, cq_ref, ck_ref, cv_ref,
                 convw_ref, cos1_ref, sin1_ref, cos2_ref, sin2_ref, qnw_ref, knw_ref,
                 gq_ref, gk_ref, gv_ref, dqo_ref, dko_ref, dvo_ref, cqo_ref, cko_ref, cvo_ref, *, ts):
    i = pl.program_id(1)
    n = pl.num_programs(1)
    cur = qkv_ref[0].astype(F32)
    prev = jnp.where(i > 0, prev_ref[0].astype(F32), 0.0)
    nxt = jnp.where(i < n - 1, next_ref[0].astype(F32), 0.0)
    ext = jnp.concatenate([prev, cur, nxt], axis=0)
    pad = GDN_CONV // 2
    acc = jnp.zeros_like(cur)
    for j in range(GDN_CONV):
        off = HALO - pad + j
        acc = acc + ext[off:off + ts, :] * convw_ref[j:j + 1, :]
    act = acc * _sigmoid(acc)
    nqk = GDN_HEADS * GDN_DK
    for h in range(GDN_HEADS):
        sl = slice(h * GDN_DK, (h + 1) * GDN_DK)
        qh = act[:, sl]
        gq_ref[0, :, sl] = (qh * lax.rsqrt(_group_sumsq(qh, LANES) + NORM_EPS) * (GDN_DK ** -0.5)).astype(BF16)
        kh = act[:, nqk + h * GDN_DK: nqk + (h + 1) * GDN_DK]
        gk_ref[0, :, sl] = (kh * lax.rsqrt(_group_sumsq(kh, LANES) + NORM_EPS)).astype(BF16)
    gv_ref[0] = act[:, 2 * nqk:].astype(BF16)
    cos1, sin1 = cos1_ref[...], sin1_ref[...]
    ones_rows = jnp.where(lax.broadcasted_iota(jnp.int32, (VT_PAD, ts), 0) == 0, 1.0, 0.0).astype(BF16)
    for p in range(DIFF_HEADS):
        sl = slice(p * LANES, (p + 1) * LANES)
        xq = dq_ref[0, :, sl].astype(F32)
        dqo_ref[0, :, sl] = ((xq * cos1 + _rot_half(xq, DIFF_DQK // 2) * sin1) * (DIFF_DQK ** -0.5 * LOG2E)).astype(BF16)
        xk = dk_ref[0, :, sl].astype(F32)
        xk = (xk * cos1 + _rot_half(xk, DIFF_DQK // 2) * sin1).astype(BF16)
        for m in range(2):
            dko_ref[0, :, (2 * p + m) * LANES:(2 * p + m + 1) * LANES] = _aug_slab(xk, m)
        r0 = p * (LANES + VT_PAD)
        dvo_ref[0, r0:r0 + LANES, :] = dv_ref[0, :, sl].astype(F32).T.astype(BF16)
        dvo_ref[0, r0 + LANES:r0 + LANES + VT_PAD, :] = ones_rows
    cos2, sin2 = cos2_ref[...], sin2_ref[...]
    for p in range(GQA_HEADS * GQA_DH // LANES):
        sl = slice(p * LANES, (p + 1) * LANES)
        xq = cq_ref[0, :, sl].astype(F32)
        xq = xq * lax.rsqrt(_group_sumsq(xq, GQA_DH) * (1.0 / GQA_DH) + NORM_EPS) * qnw_ref[...]
        cqo_ref[0, :, sl] = ((xq * cos2 + _rot_half(xq, GQA_DH // 4) * sin2) * (GQA_DH ** -0.5 * LOG2E)).astype(BF16)
    for p in range(2):
        sl = slice(p * LANES, (p + 1) * LANES)
        xk = ck_ref[0, :, sl].astype(F32)
        xk = xk * lax.rsqrt(_group_sumsq(xk, GQA_DH) * (1.0 / GQA_DH) + NORM_EPS) * knw_ref[...]
        xk = (xk * cos2 + _rot_half(xk, GQA_DH // 4) * sin2).astype(BF16)
        for m in range(2):
            c = p if m == 0 else 1 - p
            cko_ref[0, :, (2 * c + m) * LANES:(2 * c + m + 1) * LANES] = _aug_slab(xk, m)
    vt = cv_ref[0].astype(F32).T.astype(BF16)
    for c in range(GQA_KV):
        r0 = c * (GQA_DH + VT_PAD)
        cvo_ref[0, r0:r0 + GQA_DH, :] = vt[c * GQA_DH:(c + 1) * GQA_DH]
        cvo_ref[0, r0 + GQA_DH:r0 + GQA_DH + VT_PAD, :] = ones_rows


def _prep(main3, conv_w, cos1, sin1, cos2, sin2, qnw, knw, *, ts):
    b, s, _ = main3.shape
    nt = s // ts
    hb = ts // HALO
    last = s // HALO - 1
    row = lambda w: pl.BlockSpec((1, w), lambda bi, i: (0, 0))
    tab = pl.BlockSpec((ts, LANES), lambda bi, i: (i, 0))
    col = lambda w, off: pl.BlockSpec((1, ts, w), lambda bi, i: (bi, i, off // w))
    out = lambda w: pl.BlockSpec((1, ts, w), lambda bi, i: (bi, i, 0))
    outs = [("tok", 512), ("tok", 512), ("tok", 512), ("tok", 512), ("tok", 1024),
            ("rows", DIFF_HEADS * (LANES + VT_PAD)), ("tok", 512), ("tok", 512), ("rows", GQA_KV * (GQA_DH + VT_PAD))]
    specs = tuple(out(w) if kind == "tok" else pl.BlockSpec((1, w, ts), lambda bi, i: (bi, 0, i)) for kind, w in outs)
    shapes = tuple(jax.ShapeDtypeStruct((b, s, w) if kind == "tok" else (b, w, s), BF16) for kind, w in outs)
    return pl.pallas_call(
        functools.partial(_prep_kernel, ts=ts),
        out_shape=shapes,
        grid=(b, nt),
        in_specs=[
            col(1536, COL_QKV),
            pl.BlockSpec((1, HALO, 1536), lambda bi, i: (bi, jnp.maximum(i * hb - 1, 0), COL_QKV // 1536)),
            pl.BlockSpec((1, HALO, 1536), lambda bi, i: (bi, jnp.minimum((i + 1) * hb, last), COL_QKV // 1536)),
            col(512, COL_DQ), col(512, COL_DK), col(512, COL_DV), col(512, COL_CQ), col(256, COL_CK), col(128, COL_CV),
            pl.BlockSpec((8, 1536), lambda bi, i: (0, 0)),
            tab, tab, tab, tab, row(LANES), row(LANES),
        ],
        out_specs=specs,
        compiler_params=_cparams(("arbitrary", "arbitrary")),
        name="mixer_prep",
    )(*([main3] * 9), conv_w, cos1, sin1, cos2, sin2, qnw, knw)


GDN_G = 8
GDN_NBATCH = 2
GDN_ROWS = GDN_HEADS * GDN_CHUNK


def _stack_heads(x):
    return jnp.concatenate([x[:, h * LANES:(h + 1) * LANES] for h in range(GDN_HEADS)], axis=0)


def _row_to_col(row, eye):
    return jnp.sum(jnp.where(eye, row, 0.0), axis=1, keepdims=True)


def _gdn_kernel(af_ref, ab_ref, bf_ref, bb_ref, alog_ref, dtb_ref, qf_ref, kf_ref, vf_ref, qb_ref, kb_ref, vb_ref,
                of_ref, ob_ref, state_ref, gc_ref, gt_ref, beta_ref, *, nbatch):
    blk = pl.program_id(1)
    n = GDN_ROWS
    c = GDN_CHUNK

    @pl.when(blk == 0)
    def _():
        state_ref[...] = jnp.zeros_like(state_ref)

    ri = lax.broadcasted_iota(jnp.int32, (n, n), 0)
    ci = lax.broadcasted_iota(jnp.int32, (n, n), 1)
    same = (ri // c) == (ci // c)
    eye = ri == ci
    ti = lax.broadcasted_iota(jnp.int32, (n, GDN_HEADS * LANES), 0)
    tj = lax.broadcasted_iota(jnp.int32, (n, GDN_HEADS * LANES), 1)
    tot_m = jnp.where((ti // c) == (tj // LANES), 1.0, 0.0)

    chains = []
    for d, (a_ref, b_ref, q_ref, k_ref, v_ref, o_ref) in enumerate(
            ((af_ref, bf_ref, qf_ref, kf_ref, vf_ref, of_ref), (ab_ref, bb_ref, qb_ref, kb_ref, vb_ref, ob_ref))):
        sgn = 1 - 2 * d
        after = same & ((ri - ci) * sgn > 0)
        incl = same & ((ri - ci) * sgn >= 0)
        cum_m = jnp.where(same & ((ci - ri) * sgn >= 0), 1.0, 0.0)
        for bi in range(nbatch):
            ch = d * nbatch + bi
            x = a_ref[0, bi] + dtb_ref[d]
            softplus = jnp.maximum(x, 0.0) + jnp.log(1.0 + jnp.exp(-jnp.abs(x)))
            g = -jnp.exp(alog_ref[d]) * softplus
            beta_ref[ch] = _sigmoid(b_ref[0, bi])
            gc_ref[ch] = jnp.dot(g, cum_m, precision=HI, preferred_element_type=F32)
            gt_ref[ch] = jnp.dot(g, tot_m, precision=HI, preferred_element_type=F32)
            chains.append((ch, d, bi, after, incl, q_ref, k_ref, v_ref, o_ref))

    def chunk(j, chain):
        ch, d, bi, after, incl, q_ref, k_ref, v_ref, o_ref = chain
        cc = j if d == 0 else GDN_G - 1 - j
        r0 = pl.multiple_of(cc * c, c)
        gc_row = gc_ref[ch, pl.ds(cc, 1), :]
        beta_row = beta_ref[ch, pl.ds(cc, 1), :]
        gt_row = gt_ref[ch, pl.ds(cc, 1), :]
        gc_col = _row_to_col(gc_row, eye)
        beta_col = _row_to_col(beta_row, eye)
        k_st = _stack_heads(k_ref[bi, pl.ds(r0, c), :]).astype(F32)
        q_st = _stack_heads(q_ref[bi, pl.ds(r0, c), :]).astype(F32)
        v_st = _stack_heads(v_ref[bi, pl.ds(r0, c), :]).astype(F32)
        egc = jnp.exp(gc_col)
        decay = jnp.exp(jnp.minimum(gc_col - gc_row, 0.0))
        kb = k_st * beta_col
        k_bf = k_st.astype(BF16)
        kk = lax.dot_general(kb.astype(BF16), k_bf, (((1,), (1,)), ((), ())), preferred_element_type=F32)
        qk = lax.dot_general(q_st.astype(BF16), k_bf, (((1,), (1,)), ((), ())), preferred_element_type=F32)
        yield
        neg_a = jnp.where(after, -(kk * decay), 0.0)
        t_m = jnp.where(eye, 1.0, 0.0) + neg_a
        p_m = neg_a
        for _ in range(int(math.log2(c)) - 1):
            p_bf = p_m.astype(BF16)
            p_m = jnp.dot(p_bf, p_bf, preferred_element_type=F32)
            yield
            t_m = t_m + jnp.dot(t_m.astype(BF16), p_m.astype(BF16), preferred_element_type=F32)
            yield
        rhs = jnp.concatenate([v_st * beta_col, kb * egc], axis=1).astype(BF16)
        sol = jnp.dot(t_m.astype(BF16), rhs, preferred_element_type=F32)
        yield
        u_st, w_st = sol[:, :LANES], sol[:, LANES:]
        intra = jnp.where(incl, qk * decay, 0.0).astype(BF16)
        q_dec = (q_st * egc).astype(BF16)
        vn, oq = [], []
        for h in range(GDN_HEADS):
            rs = slice(h * c, (h + 1) * c)
            s_h = state_ref[ch * GDN_HEADS + h].astype(BF16)
            vn.append(u_st[rs] - jnp.dot(w_st[rs].astype(BF16), s_h, preferred_element_type=F32))
            oq.append(jnp.dot(q_dec[rs], s_h, preferred_element_type=F32))
        yield
        vn_st = jnp.concatenate(vn, axis=0)
        o_st = jnp.concatenate(oq, axis=0) + jnp.dot(intra, vn_st.astype(BF16), preferred_element_type=F32)
        for h in range(GDN_HEADS):
            rs = slice(h * c, (h + 1) * c)
            gt_h = gt_row[:, h * LANES:(h + 1) * LANES]
            k_dec = (k_st[rs] * jnp.exp(gt_h[:, :1] - gc_col[rs])).astype(BF16)
            upd = lax.dot_general(k_dec, vn[h].astype(BF16), (((0,), (0,)), ((), ())), preferred_element_type=F32)
            state_ref[ch * GDN_HEADS + h] = state_ref[ch * GDN_HEADS + h] * jnp.exp(gt_h) + upd
            o_ref[bi, pl.ds(r0, c), h * LANES:(h + 1) * LANES] = o_st[rs]

    def step(j, carry):
        active = [chunk(j, chain) for chain in chains]
        while active:
            active = [g for g in active if next(g, active) is not active]
        return carry

    lax.fori_loop(0, GDN_G, step, 0)


def _gdn(a_rows, b_rows, alog_row, dtb_row, gq, gk, gv, *, nbatch):
    b, s, _ = gq.shape
    nb = s // (GDN_G * GDN_CHUNK)
    ts = GDN_G * GDN_CHUNK
    nchain = 2 * nbatch
    tok_f = pl.BlockSpec((nbatch, ts, 512), lambda bi, i: (bi, i, 0))
    tok_b = pl.BlockSpec((nbatch, ts, 512), lambda bi, i: (bi, nb - 1 - i, 0))
    rows_f = pl.BlockSpec((1, nbatch, GDN_G, GDN_ROWS), lambda bi, i: (0, bi, i, 0))
    rows_b = pl.BlockSpec((1, nbatch, GDN_G, GDN_ROWS), lambda bi, i: (1, bi, nb - 1 - i, 0))
    par = pl.BlockSpec((2, 1, GDN_ROWS), lambda bi, i: (0, 0, 0))
    return pl.pallas_call(
        functools.partial(_gdn_kernel, nbatch=nbatch),
        out_shape=(jax.ShapeDtypeStruct((b, s, 512), F32), jax.ShapeDtypeStruct((b, s, 512), F32)),
        grid=(b // nbatch, nb),
        in_specs=[rows_f, rows_b, rows_f, rows_b, par, par, tok_f, tok_f, tok_f, tok_b, tok_b, tok_b],
        out_specs=(tok_f, tok_b),
        scratch_shapes=[pltpu.VMEM((nchain * GDN_HEADS, GDN_DK, GDN_DV), F32),
                        pltpu.VMEM((nchain, GDN_G, GDN_ROWS), F32),
                        pltpu.VMEM((nchain, GDN_G, GDN_HEADS * LANES), F32),
                        pltpu.VMEM((nchain, GDN_G, GDN_ROWS), F32)],
        compiler_params=_cparams(("arbitrary", "arbitrary")),
        name="gdn_chunked",
    )(a_rows, a_rows, b_rows, b_rows, alog_row, dtb_row, gq, gk, gv, gq, gk, gv)


def _attn_kernel(q_ref, k0_ref, k1_ref, vt_ref, lam_ref, nw_ref, o_ref, acc_ref, *, mode, tk, lambda_init):
    s_len = k0_ref.shape[1]
    tq = q_ref.shape[1]
    rows = vt_ref.shape[1]
    dv = rows - VT_PAD
    half = LANES // 2
    q = q_ref[0]
    lane = lax.broadcasted_iota(jnp.int32, q.shape, 1)
    keep = (lane < half, lane >= half)
    stab = (lane == half, lane == 0)
    zero = jnp.zeros_like(q)
    krefs = (k0_ref, k1_ref)
    nchunks = s_len // tk
    dn = (((1,), (1,)), ((), ()))

    def kchunk(m, ci):
        return krefs[m][0, pl.ds(pl.multiple_of(ci * tk, tk), tk), :]

    def vchunk(ci):
        return vt_ref[0, :, pl.ds(pl.multiple_of(ci * tk, tk), tk)]

    qm, qa = [], []
    for m in range(2):
        qm.append(jnp.where(keep[m], q, zero))
        k_first = krefs[m][0, 0:min(ATTN_STAB_KEYS, s_len), :]
        mx = jnp.max(lax.dot_general(qm[m], k_first, dn, preferred_element_type=F32), axis=-1, keepdims=True)
        qa.append(jnp.where(stab[m], (-mx).astype(BF16), qm[m]))

    def fast(ci, acc):
        st = [lax.dot_general(kchunk(m, ci), qa[m], dn, preferred_element_type=F32) for m in range(2)]
        vt = vchunk(ci)
        return tuple(acc[m] + jnp.dot(vt, jnp.exp2(st[m]).astype(BF16), preferred_element_type=F32) for m in range(2))

    acc = lax.fori_loop(0, nchunks, fast, tuple(jnp.zeros((rows, tq), F32) for _ in range(2)))
    nonfinite = jnp.float32(0.0)
    for m in range(2):
        acc_ref[m] = acc[m]
        nonfinite = nonfinite + jnp.sum(jnp.where(jnp.isfinite(acc[m]), 0.0, 1.0))

    @pl.when(nonfinite > 0.0)
    def _():
        def slow(ci, carry):
            out = []
            vt = vchunk(ci)
            for m in range(2):
                m_i, a_i = carry[m]
                st = lax.dot_general(kchunk(m, ci), qm[m], dn, preferred_element_type=F32)
                m_new = jnp.maximum(m_i, jnp.max(st, axis=0, keepdims=True))
                p = jnp.exp2(st - m_new).astype(BF16)
                out.append((m_new, jnp.exp2(m_i - m_new) * a_i + jnp.dot(vt, p, preferred_element_type=F32)))
            return tuple(out)

        init = tuple((jnp.full((1, tq), -jnp.inf, F32), jnp.zeros((rows, tq), F32)) for _ in range(2))
        res = lax.fori_loop(0, nchunks, slow, init)
        for m in range(2):
            acc_ref[m] = res[m][1]

    o0 = acc_ref[0, 0:dv, :] / acc_ref[0, dv:dv + 1, :]
    o1 = acc_ref[1, 0:dv, :] / acc_ref[1, dv:dv + 1, :]
    if mode == "diff":
        lv = lam_ref[...]
        lam = (jnp.exp(jnp.sum(lv[0:1] * lv[1:2], axis=-1, keepdims=True))
               - jnp.exp(jnp.sum(lv[2:3] * lv[3:4], axis=-1, keepdims=True)) + lambda_init)
        ot = o0 - lam * o1
        ot = ot * lax.rsqrt(jnp.mean(ot * ot, axis=0, keepdims=True) + NORM_EPS) * nw_ref[...] * (1.0 - lambda_init)
    else:
        ot = jnp.concatenate([o0, o1], axis=0)
    o_ref[0] = ot.T.astype(o_ref.dtype)


def _attention(q, k_arr, vt_arr, lam_vecs, norm_w, *, mode, tq, tk, lambda_init=0.0):
    b, s, w = q.shape
    slabs = w // LANES
    if mode == "diff":
        rows = LANES + VT_PAD
        k_col = lambda p, m: 2 * p + m
        v_grp = lambda p: p
    else:
        rows = GQA_DH + VT_PAD
        k_col = lambda p, m: 2 * (p // 2) + m
        v_grp = lambda p: p // 2
    kspec = lambda m: pl.BlockSpec((1, s, LANES), lambda bi, p, i: (bi, 0, k_col(p, m)))
    return pl.pallas_call(
        functools.partial(_attn_kernel, mode=mode, tk=tk, lambda_init=lambda_init),
        out_shape=jax.ShapeDtypeStruct((b, s, w), BF16),
        grid=(b, slabs, s // tq),
        in_specs=[pl.BlockSpec((1, tq, LANES), lambda bi, p, i: (bi, i, p)),
                  kspec(0), kspec(1),
                  pl.BlockSpec((1, rows, s), lambda bi, p, i: (bi, v_grp(p), 0)),
                  pl.BlockSpec((4, DIFF_DQK), lambda bi, p, i: (0, 0)),
                  pl.BlockSpec((LANES, 1), lambda bi, p, i: (0, 0))],
        out_specs=pl.BlockSpec((1, tq, LANES), lambda bi, p, i: (bi, i, p)),
        scratch_shapes=[pltpu.VMEM((2, rows, tq), F32)],
        compiler_params=_cparams(("arbitrary", "arbitrary", "arbitrary")),
        name="attn_" + mode,
    )(q, k_arr, k_arr, vt_arr, lam_vecs, norm_w)


def _merge_kernel(of_ref, ob_ref, z_ref, g0_ref, g1_ref, g2_ref, yb_ref, yc_ref, x_ref,
                  wa_ref, wb_ref, wc_ref, wo_ref, gnw_ref, fnw_ref, rw_ref, rb_ref,
                  xo_ref, h_ref, id_ref, rwgt_ref):
    o = of_ref[...] + ob_ref[...]
    parts = []
    for h in range(GDN_HEADS):
        oh = o[:, h * LANES:(h + 1) * LANES]
        parts.append(oh * lax.rsqrt(jnp.mean(oh * oh, axis=-1, keepdims=True) + NORM_EPS) * gnw_ref[...])
    z = z_ref[...].astype(F32)
    ya = (jnp.concatenate(parts, axis=1) * (z * _sigmoid(z))).astype(BF16)
    merged = _sigmoid(g0_ref[...].astype(F32)) * jnp.dot(ya, wa_ref[...], preferred_element_type=F32)
    merged = merged + _sigmoid(g1_ref[...].astype(F32)) * jnp.dot(yb_ref[...], wb_ref[...], preferred_element_type=F32)
    merged = merged + _sigmoid(g2_ref[...].astype(F32)) * jnp.dot(yc_ref[...], wc_ref[...], preferred_element_type=F32)
    xn = x_ref[...] + jnp.dot(merged.astype(BF16), wo_ref[...], preferred_element_type=F32)
    xo_ref[...] = xn
    hf = xn * lax.rsqrt(jnp.mean(xn * xn, axis=-1, keepdims=True) + NORM_EPS) * fnw_ref[...]
    h_ref[...] = hf.astype(BF16)
    logits = jnp.dot(hf, rw_ref[...], precision=HI, preferred_element_type=F32) + rb_ref[...]
    lane = lax.broadcasted_iota(jnp.int32, logits.shape, 1)
    big = jnp.int32(LANES)
    ninf = -jnp.inf
    glog = jnp.where(lane < N_GROUPS, logits, ninf)
    gmax = jnp.max(glog, axis=-1, keepdims=True)
    gidx = jnp.min(jnp.where(glog == gmax, lane, big), axis=-1, keepdims=True)
    gp = 1.0 / jnp.sum(jnp.exp(glog - gmax), axis=-1, keepdims=True)
    e = lane - N_GROUPS
    sel = (e >= 0) & (e < N_EXPERTS) & ((e // EXPERTS_PER_GROUP) == gidx)
    elog = jnp.where(sel, logits, ninf)
    m1 = jnp.max(elog, axis=-1, keepdims=True)
    i1 = jnp.min(jnp.where(elog == m1, lane, big), axis=-1, keepdims=True)
    elog2 = jnp.where(lane == i1, ninf, elog)
    m2 = jnp.max(elog2, axis=-1, keepdims=True)
    i2 = jnp.min(jnp.where(elog2 == m2, lane, big), axis=-1, keepdims=True)
    e2 = jnp.exp(m2 - m1)
    w1 = 1.0 / (1.0 + e2)
    w2 = e2 * w1
    id_ref[...] = jnp.where(lane == 0, i1 - N_GROUPS, jnp.where(lane == 1, i2 - N_GROUPS, 0))
    rwgt_ref[...] = jnp.where(lane == 0, gp * w1, jnp.where(lane == 1, gp * w2, 0.0))


def _merge(o_f, o_b, main2, yb, yc, x2d, wa, wb, wc, wo, gnw, fnw, rw, rb, *, tm):
    t, d = x2d.shape
    full = lambda shp: pl.BlockSpec(shp, lambda i: tuple(0 for _ in shp))
    return pl.pallas_call(
        _merge_kernel,
        out_shape=(jax.ShapeDtypeStruct((t, d), F32), jax.ShapeDtypeStruct((t, d), BF16),
                   jax.ShapeDtypeStruct((t, LANES), jnp.int32), jax.ShapeDtypeStruct((t, LANES), F32)),
        grid=(t // tm,),
        in_specs=[pl.BlockSpec((tm, 512), lambda i: (i, 0)),
                  pl.BlockSpec((tm, 512), lambda i: (i, 0)),
                  pl.BlockSpec((tm, 512), lambda i: (i, COL_Z // 512)),
                  pl.BlockSpec((tm, d), lambda i: (i, 0)),
                  pl.BlockSpec((tm, d), lambda i: (i, 1)),
                  pl.BlockSpec((tm, d), lambda i: (i, 2)),
                  pl.BlockSpec((tm, 512), lambda i: (i, 0)),
                  pl.BlockSpec((tm, 512), lambda i: (i, 0)),
                  pl.BlockSpec((tm, d), lambda i: (i, 0)),
                  full((512, d)), full((512, d)), full((512, d)), full((d, d)),
                  full((1, LANES)), full((1, d)), full((d, LANES)), full((1, LANES))],
        out_specs=(pl.BlockSpec((tm, d), lambda i: (i, 0)), pl.BlockSpec((tm, d), lambda i: (i, 0)),
                   pl.BlockSpec((tm, LANES), lambda i: (i, 0)), pl.BlockSpec((tm, LANES), lambda i: (i, 0))),
        compiler_params=_cparams(("arbitrary",)),
        name="merge_router",
    )(o_f, o_b, main2, main2, main2, main2, yb, yc, x2d, wa, wb, wc, wo, gnw, fnw, rw, rb)


def _expert_kernel(blk_e_ref, nused_ref, x_ref, w1_ref, w3_ref, w2_ref, o_ref, w1b_ref, w3b_ref, w2b_ref):
    i = pl.program_id(0)

    @pl.when((i == 0) | (blk_e_ref[i] != blk_e_ref[jnp.maximum(i - 1, 0)]))
    def _():
        w1b_ref[...] = w1_ref[0, 0].astype(BF16)
        w3b_ref[...] = w3_ref[0, 0].astype(BF16)
        w2b_ref[...] = w2_ref[0, 0].astype(BF16)

    @pl.when(i < nused_ref[0])
    def _():
        x = x_ref[...]
        a = jnp.dot(x, w1b_ref[...], preferred_element_type=F32)
        u = jnp.dot(x, w3b_ref[...], preferred_element_type=F32)
        hmid = (a * _sigmoid(a) * u).astype(BF16)
        o_ref[...] = jnp.dot(hmid, w2b_ref[...], preferred_element_type=F32).astype(o_ref.dtype)

    @pl.when(i >= nused_ref[0])
    def _():
        o_ref[...] = jnp.zeros_like(o_ref)


def _experts(blk_e, nused, xb, w1, w3, w2, *, layer):
    p_len, d = xb.shape
    ff = w1.shape[3]
    nblk = p_len // MOE_BLOCK
    return pl.pallas_call(
        _expert_kernel,
        out_shape=jax.ShapeDtypeStruct((p_len, d), BF16),
        grid_spec=pltpu.PrefetchScalarGridSpec(
            num_scalar_prefetch=2,
            grid=(nblk,),
            in_specs=[pl.BlockSpec((MOE_BLOCK, d), lambda i, be, nu: (i, 0)),
                      pl.BlockSpec((1, 1, d, ff), lambda i, be, nu: (layer, be[i], 0, 0)),
                      pl.BlockSpec((1, 1, d, ff), lambda i, be, nu: (layer, be[i], 0, 0)),
                      pl.BlockSpec((1, 1, ff, d), lambda i, be, nu: (layer, be[i], 0, 0))],
            out_specs=pl.BlockSpec((MOE_BLOCK, d), lambda i, be, nu: (i, 0)),
            scratch_shapes=[pltpu.VMEM((d, ff), BF16), pltpu.VMEM((d, ff), BF16), pltpu.VMEM((ff, d), BF16)],
        ),
        compiler_params=_cparams(("arbitrary",)),
        name="expert_mlp",
    )(blk_e, nused, xb, w1, w3, w2)


def _combine_kernel(x_ref, y0_ref, y1_ref, w_ref, nw_ref, o_ref, *, final):
    w = w_ref[...]
    x = x_ref[...] + w[:, 0:1] * y0_ref[...].astype(F32) + w[:, 1:2] * y1_ref[...].astype(F32)
    if final:
        x = x * lax.rsqrt(jnp.mean(x * x, axis=-1, keepdims=True) + NORM_EPS) * nw_ref[...]
    o_ref[...] = x


def _combine(x2d, y0, y1, wts, norm_w, *, final, tm):
    t, d = x2d.shape
    tile = pl.BlockSpec((tm, d), lambda i: (i, 0))
    return pl.pallas_call(
        functools.partial(_combine_kernel, final=final),
        out_shape=jax.ShapeDtypeStruct((t, d), F32),
        grid=(t // tm,),
        in_specs=[tile, tile, tile, pl.BlockSpec((tm, LANES), lambda i: (i, 0)), pl.BlockSpec((1, d), lambda i: (0, 0))],
        out_specs=tile,
        compiler_params=_cparams(("arbitrary",)),
        name="moe_combine",
    )(x2d, y0, y1, wts, norm_w.reshape(1, d))


def _rope_tables(pos, dim):
    inv = 1.0 / (ROPE_THETA ** (jnp.arange(0, dim, 2, dtype=F32) / dim))
    ang = pos.astype(F32)[:, None] * inv[None, :]
    ang = jnp.concatenate([ang, ang], axis=-1)
    return jnp.cos(ang), jnp.sin(ang)


def _signed_sin(sin):
    half = sin.shape[-1] // 2
    return jnp.concatenate([-sin[:, :half], sin[:, half:]], axis=-1)


def _layout_w_in(w):
    o = 0
    parts = {}
    for name, size in (("qkv", 1536), ("z", 512), ("b", 8), ("a", 8), ("dq", 512), ("dk", 512), ("dv", 512),
                       ("cq", 512), ("ck", 128), ("cv", 128), ("gate", 3072)):
        parts[name] = w[:, o:o + size]
        o += size
    swap = lambda m: jnp.concatenate([m[:, 64:], m[:, :64]], axis=1)
    main = jnp.concatenate([parts["gate"], parts["qkv"], parts["z"], parts["dq"], parts["dk"], parts["dv"],
                            parts["cq"], parts["ck"], swap(parts["ck"]), parts["cv"], swap(parts["cv"])], axis=1)
    ba = jnp.concatenate([parts["b"], parts["a"], jnp.zeros((w.shape[0], LANES - 16), w.dtype)], axis=1)
    return main.astype(BF16), ba


def _rows_layout(t, bsz, s):
    nc = s // GDN_CHUNK
    t = t.reshape(bsz, nc, GDN_CHUNK, 2, GDN_HEADS)
    return jnp.transpose(t, (3, 0, 1, 4, 2)).reshape(2, bsz, nc, GDN_ROWS)


def _moe_dispatch(ids, t):
    a = t * TOPK
    p_len = ((a + N_EXPERTS * (MOE_BLOCK - 1) + MOE_BLOCK - 1) // MOE_BLOCK) * MOE_BLOCK
    n_blocks = p_len // MOE_BLOCK
    flat_e = ids.reshape(-1)
    iota_a = jnp.arange(a, dtype=jnp.int32)
    skey = jnp.sort(flat_e * a + iota_a)
    order = skey % a
    se = skey // a
    experts = jnp.arange(N_EXPERTS, dtype=jnp.int32)
    counts = jnp.sum((flat_e[:, None] == experts[None, :]).astype(jnp.int32), axis=0)
    start = jnp.cumsum(counts) - counts
    pcounts = ((counts + MOE_BLOCK - 1) // MOE_BLOCK) * MOE_BLOCK
    pend = jnp.cumsum(pcounts)
    pstart = pend - pcounts
    dest_sorted = pstart[se] + (iota_a - start[se])
    blk_first = jnp.arange(n_blocks, dtype=jnp.int32) * MOE_BLOCK
    blk_e = jnp.minimum(jnp.sum((pend[None, :] <= blk_first[:, None]).astype(jnp.int32), axis=1), N_EXPERTS - 1)
    row = jnp.arange(p_len, dtype=jnp.int32)
    row_e = jnp.repeat(blk_e, MOE_BLOCK)
    j = row - pstart[row_e]
    valid = j < counts[row_e]
    tok_buf = jnp.where(valid, order[jnp.minimum(start[row_e] + j, a - 1)] // TOPK, t)
    _, dest = lax.sort((order, dest_sorted), num_keys=1)
    nused = (pend[-1] // MOE_BLOCK).astype(jnp.int32).reshape(1)
    return tok_buf, dest.reshape(t, TOPK), blk_e, nused


def kernel(x, attn_norm_w, w_in, gdn_conv_w, gdn_a_log, gdn_dt_bias, gdn_norm_w, diff_lambda, diff_norm_w,
           gqa_q_norm_w, gqa_k_norm_w, w_branch_a, w_branch_b, w_branch_c, w_out, ffn_norm_w,
           router_group_w, router_group_b, router_expert_w, router_expert_b,
           expert_w_gate, expert_w_up, expert_w_down, final_norm_w):
    bsz, s, d = x.shape
    t = bsz * s
    depth = w_in.shape[0]
    tm = min(512, t)
    ts = min(512, s)

    rows = s // GRID_W
    row = jnp.broadcast_to(jnp.arange(rows)[:, None], (rows, GRID_W)).reshape(s)
    col = jnp.broadcast_to(jnp.arange(GRID_W)[None, :], (rows, GRID_W)).reshape(s)
    c1, s1 = _rope_tables(jnp.arange(s), DIFF_DQK)
    cr, sr = _rope_tables(row, GQA_DH // 2)
    cc, sc = _rope_tables(col, GQA_DH // 2)
    cos1 = jnp.tile(c1, (1, 2))
    sin1 = jnp.tile(_signed_sin(s1), (1, 2))
    cos2 = jnp.tile(jnp.concatenate([cr, cc], axis=-1), (1, 2))
    sin2 = jnp.tile(jnp.concatenate([_signed_sin(sr), _signed_sin(sc)], axis=-1), (1, 2))

    x2 = x.reshape(t, d)
    for l in range(depth):
        lambda_init = 0.8 - 0.6 * math.exp(-0.3 * l)
        w_main, w_ba = _layout_w_in(w_in[l])
        main2 = _norm_proj(x2, attn_norm_w[l], w_main, BF16, exact=False, tm=tm, tn=N_MAIN // 2)
        ba = _norm_proj(x2, attn_norm_w[l], w_ba, F32, exact=True, tm=tm, tn=LANES)
        main3 = main2.reshape(bsz, s, N_MAIN)

        conv_w = jnp.concatenate([gdn_conv_w[l], jnp.zeros((8 - GDN_CONV, gdn_conv_w.shape[2]), F32)], axis=0)
        qnw = jnp.tile(gqa_q_norm_w[l], 2).reshape(1, LANES)
        knw = jnp.tile(gqa_k_norm_w[l], 2).reshape(1, LANES)
        gq, gk, gv, dq, dk, dv, cq, ck, cv = _prep(main3, conv_w, cos1, sin1, cos2, sin2, qnw, knw, ts=ts)

        b_rows = _rows_layout(ba[:, 0:8], bsz, s)
        a_rows = _rows_layout(ba[:, 8:16], bsz, s)
        alog_row = jnp.repeat(gdn_a_log[l], GDN_CHUNK, axis=1).reshape(2, 1, GDN_ROWS)
        dtb_row = jnp.repeat(gdn_dt_bias[l], GDN_CHUNK, axis=1).reshape(2, 1, GDN_ROWS)
        o_f, o_b = _gdn(a_rows, b_rows, alog_row, dtb_row, gq, gk, gv, nbatch=GDN_NBATCH if bsz % GDN_NBATCH == 0 else 1)

        nw_diff = diff_norm_w[l].reshape(LANES, 1)
        yb = _attention(dq, dk, dv, diff_lambda[l], nw_diff, mode="diff",
                        tq=min(ATTN_TQ, s), tk=min(ATTN_TK, s), lambda_init=lambda_init)
        yc = _attention(cq, ck, cv, diff_lambda[l], nw_diff, mode="gqa",
                        tq=min(ATTN_TQ, s), tk=min(ATTN_TK, s))

        rw = jnp.concatenate([router_group_w[l], router_expert_w[l],
                              jnp.zeros((d, LANES - N_GROUPS - N_EXPERTS), F32)], axis=1)
        rb = jnp.concatenate([router_group_b[l], router_expert_b[l],
                              jnp.zeros((LANES - N_GROUPS - N_EXPERTS,), F32)]).reshape(1, LANES)
        x2, h2, ids, wts = _merge(o_f.reshape(t, 512), o_b.reshape(t, 512), main2, yb.reshape(t, 512), yc.reshape(t, 512), x2,
                                  w_branch_a[l].astype(BF16), w_branch_b[l].astype(BF16),
                                  w_branch_c[l].astype(BF16), w_out[l].astype(BF16),
                                  gdn_norm_w[l].reshape(1, LANES), ffn_norm_w[l].reshape(1, d), rw, rb, tm=min(256, t))

        tok_buf, dest, blk_e, nused = _moe_dispatch(ids[:, :TOPK], t)
        h_pad = jnp.concatenate([h2, jnp.zeros((1, d), BF16)], axis=0)
        yblk = _experts(blk_e, nused, h_pad[tok_buf], expert_w_gate, expert_w_up, expert_w_down, layer=l)
        x2 = _combine(x2, yblk[dest[:, 0]], yblk[dest[:, 1]], wts, final_norm_w, final=(l == depth - 1), tm=tm)

    return x2.reshape(bsz, s, d)
```

```python
import functools
import math

import jax
import jax.numpy as jnp
from jax import lax
from jax.experimental import pallas as pl
from jax.experimental.pallas import tpu as pltpu

GRID_W = 64
ROPE_THETA = 10000.0
NORM_EPS = 1e-6
GDN_HEADS = 4
GDN_DK = 128
GDN_DV = 128
GDN_CONV = 5
GDN_CHUNK = 64
DIFF_HEADS = 4
DIFF_DQK = 64
GQA_HEADS = 8
GQA_KV = 2
GQA_DH = 64
N_GROUPS = 4
EXPERTS_PER_GROUP = 8
N_EXPERTS = N_GROUPS * EXPERTS_PER_GROUP
TOPK = 2
MOE_BLOCK = 256

LANES = 128
VMEM_LIMIT = 56 * 1024 * 1024

COL_GATE = 0
COL_QKV = 3072
COL_Z = 4608
COL_DQ = 5120
COL_DK = 5632
COL_DV = 6144
COL_CQ = 6656
COL_CK = 7168
COL_CV = 7424
N_MAIN = 7680

LOG2E = math.log2(math.e)
ATTN_TQ = 1024
ATTN_TK = 1024
ATTN_STAB_KEYS = 256
VT_PAD = 16

HI = lax.Precision.HIGHEST
F32 = jnp.float32
BF16 = jnp.bfloat16


def _cparams(sem):
    return pltpu.CompilerParams(dimension_semantics=sem, vmem_limit_bytes=VMEM_LIMIT)


def _sigmoid(x):
    return 1.0 / (1.0 + jnp.exp(-x))


def _norm_proj_kernel(x_ref, nw_ref, w_ref, o_ref, *, exact):
    x = x_ref[...]
    h = x * lax.rsqrt(jnp.mean(x * x, axis=-1, keepdims=True) + NORM_EPS) * nw_ref[...]
    if exact:
        o_ref[...] = jnp.dot(h, w_ref[...], precision=HI, preferred_element_type=F32).astype(o_ref.dtype)
    else:
        o_ref[...] = jnp.dot(h.astype(BF16), w_ref[...], preferred_element_type=F32).astype(o_ref.dtype)


def _norm_proj(x2d, norm_w, w, out_dtype, *, exact, tm, tn):
    t, d = x2d.shape
    n = w.shape[1]
    return pl.pallas_call(
        functools.partial(_norm_proj_kernel, exact=exact),
        out_shape=jax.ShapeDtypeStruct((t, n), out_dtype),
        grid=(n // tn, t // tm),
        in_specs=[pl.BlockSpec((tm, d), lambda j, i: (i, 0)),
                  pl.BlockSpec((1, d), lambda j, i: (0, 0)),
                  pl.BlockSpec((d, tn), lambda j, i: (0, j))],
        out_specs=pl.BlockSpec((tm, tn), lambda j, i: (i, j)),
        compiler_params=_cparams(("arbitrary", "arbitrary")),
        name="norm_proj_exact" if exact else "norm_proj",
    )(x2d, norm_w.reshape(1, d), w)


HALO = 16


def _rot_half(x, half):
    lane = lax.broadcasted_iota(jnp.int32, x.shape, 1)
    first = (lane % (2 * half)) < half
    return jnp.where(first, pltpu.roll(x, LANES - half, 1), pltpu.roll(x, half, 1))


def _group_sumsq(x, width):
    x2 = x * x
    if width == LANES:
        return jnp.sum(x2, axis=-1, keepdims=True)
    lane = lax.broadcasted_iota(jnp.int32, x.shape, 1)
    lo = lane < width
    s_lo = jnp.sum(jnp.where(lo, x2, 0.0), axis=-1, keepdims=True)
    s_hi = jnp.sum(jnp.where(lo, 0.0, x2), axis=-1, keepdims=True)
    return jnp.where(lo, s_lo, s_hi)


def _aug_slab(x, m):
    lane = lax.broadcasted_iota(jnp.int32, x.shape, 1)
    half = LANES // 2
    keep = (lane < half) if m == 0 else (lane >= half)
    one = jnp.where(lane == (1 - m) * half, 1.0, 0.0).astype(x.dtype)
    return jnp.where(keep, x, one)


def _prep_kernel(qkv_ref, prev_ref, next_ref, dq_ref, dk_ref, dv_ref, cq_ref, ck_ref, cv_ref,
                 convw_ref, cos1_ref, sin1_ref, cos2_ref, sin2_ref, qnw_ref, knw_ref,
                 gq_ref, gk_ref, gv_ref, dqo_ref, dko_ref, dvo_ref, cqo_ref, cko_ref, cvo_ref, *, ts):
    i = pl.program_id(1)
    n = pl.num_programs(1)
    cur = qkv_ref[0].astype(F32)
    prev = jnp.where(i > 0, prev_ref[0].astype(F32), 0.0)
    nxt = jnp.where(i < n - 1, next_ref[0].astype(F32), 0.0)
    ext = jnp.concatenate([prev, cur, nxt], axis=0)
    pad = GDN_CONV // 2
    acc = jnp.zeros_like(cur)
    for j in range(GDN_CONV):
        off = HALO - pad + j
        acc = acc + ext[off:off + ts, :] * convw_ref[j:j + 1, :]
    act = acc * _sigmoid(acc)
    nqk = GDN_HEADS * GDN_DK
    for h in range(GDN_HEADS):
        sl = slice(h * GDN_DK, (h + 1) * GDN_DK)
        qh = act[:, sl]
        gq_ref[0, :, sl] = (qh * lax.rsqrt(_group_sumsq(qh, LANES) + NORM_EPS) * (GDN_DK ** -0.5)).astype(BF16)
        kh = act[:, nqk + h * GDN_DK: nqk + (h + 1) * GDN_DK]
        gk_ref[0, :, sl] = (kh * lax.rsqrt(_group_sumsq(kh, LANES) + NORM_EPS)).astype(BF16)
    gv_ref[0] = act[:, 2 * nqk:].astype(BF16)
    cos1, sin1 = cos1_ref[...], sin1_ref[...]
    ones_rows = jnp.where(lax.broadcasted_iota(jnp.int32, (VT_PAD, ts), 0) == 0, 1.0, 0.0).astype(BF16)
    for p in range(DIFF_HEADS):
        sl = slice(p * LANES, (p + 1) * LANES)
        xq = dq_ref[0, :, sl].astype(F32)
        dqo_ref[0, :, sl] = ((xq * cos1 + _rot_half(xq, DIFF_DQK // 2) * sin1) * (DIFF_DQK ** -0.5 * LOG2E)).astype(BF16)
        xk = dk_ref[0, :, sl].astype(F32)
        xk = (xk * cos1 + _rot_half(xk, DIFF_DQK // 2) * sin1).astype(BF16)
        for m in range(2):
            dko_ref[0, :, (2 * p + m) * LANES:(2 * p + m + 1) * LANES] = _aug_slab(xk, m)
        r0 = p * (LANES + VT_PAD)
        dvo_ref[0, r0:r0 + LANES, :] = dv_ref[0, :, sl].astype(F32).T.astype(BF16)
        dvo_ref[0, r0 + LANES:r0 + LANES + VT_PAD, :] = ones_rows
    cos2, sin2 = cos2_ref[...], sin2_ref[...]
    for p in range(GQA_HEADS * GQA_DH // LANES):
        sl = slice(p * LANES, (p + 1) * LANES)
        xq = cq_ref[0, :, sl].astype(F32)
        xq = xq * lax.rsqrt(_group_sumsq(xq, GQA_DH) * (1.0 / GQA_DH) + NORM_EPS) * qnw_ref[...]
        cqo_ref[0, :, sl] = ((xq * cos2 + _rot_half(xq, GQA_DH // 4) * sin2) * (GQA_DH ** -0.5 * LOG2E)).astype(BF16)
    for p in range(2):
        sl = slice(p * LANES, (p + 1) * LANES)
        xk = ck_ref[0, :, sl].astype(F32)
        xk = xk * lax.rsqrt(_group_sumsq(xk, GQA_DH) * (1.0 / GQA_DH) + NORM_EPS) * knw_ref[...]
        xk = (xk * cos2 + _rot_half(xk, GQA_DH // 4) * sin2).astype(BF16)
        for m in range(2):
            c = p if m == 0 else 1 - p
            cko_ref[0, :, (2 * c + m) * LANES:(2 * c + m + 1) * LANES] = _aug_slab(xk, m)
    vt = cv_ref[0].astype(F32).T.astype(BF16)
    for c in range(GQA_KV):
        r0 = c * (GQA_DH + VT_PAD)
        cvo_ref[0, r0:r0 + GQA_DH, :] = vt[c * GQA_DH:(c + 1) * GQA_DH]
        cvo_ref[0, r0 + GQA_DH:r0 + GQA_DH + VT_PAD, :] = ones_rows


def _prep(main3, conv_w, cos1, sin1, cos2, sin2, qnw, knw, *, ts):
    b, s, _ = main3.shape
    nt = s // ts
    hb = ts // HALO
    last = s // HALO - 1
    row = lambda w: pl.BlockSpec((1, w), lambda bi, i: (0, 0))
    tab = pl.BlockSpec((ts, LANES), lambda bi, i: (i, 0))
    col = lambda w, off: pl.BlockSpec((1, ts, w), lambda bi, i: (bi, i, off // w))
    out = lambda w: pl.BlockSpec((1, ts, w), lambda bi, i: (bi, i, 0))
    outs = [("tok", 512), ("tok", 512), ("tok", 512), ("tok", 512), ("tok", 1024),
            ("rows", DIFF_HEADS * (LANES + VT_PAD)), ("tok", 512), ("tok", 512), ("rows", GQA_KV * (GQA_DH + VT_PAD))]
    specs = tuple(out(w) if kind == "tok" else pl.BlockSpec((1, w, ts), lambda bi, i: (bi, 0, i)) for kind, w in outs)
    shapes = tuple(jax.ShapeDtypeStruct((b, s, w) if kind == "tok" else (b, w, s), BF16) for kind, w in outs)
    return pl.pallas_call(
        functools.partial(_prep_kernel, ts=ts),
        out_shape=shapes,
        grid=(b, nt),
        in_specs=[
            col(1536, COL_QKV),
            pl.BlockSpec((1, HALO, 1536), lambda bi, i: (bi, jnp.maximum(i * hb - 1, 0), COL_QKV // 1536)),
            pl.BlockSpec((1, HALO, 1536), lambda bi, i: (bi, jnp.minimum((i + 1) * hb, last), COL_QKV // 1536)),
            col(512, COL_DQ), col(512, COL_DK), col(512, COL_DV), col(512, COL_CQ), col(256, COL_CK), col(128, COL_CV),
            pl.BlockSpec((8, 1536), lambda bi, i: (0, 0)),
            tab, tab, tab, tab, row(LANES), row(LANES),
        ],
        out_specs=specs,
        compiler_params=_cparams(("arbitrary", "arbitrary")),
        name="mixer_prep",
    )(*([main3] * 9), conv_w, cos1, sin1, cos2, sin2, qnw, knw)


GDN_G = 8
GDN_NBATCH = 2
GDN_ROWS = GDN_HEADS * GDN_CHUNK


def _stack_heads(x):
    return jnp.concatenate([x[:, h * LANES:(h + 1) * LANES] for h in range(GDN_HEADS)], axis=0)


def _row_to_col(row, eye):
    return jnp.sum(jnp.where(eye, row, 0.0), axis=1, keepdims=True)


def _gdn_kernel(af_ref, ab_ref, bf_ref, bb_ref, alog_ref, dtb_ref, qf_ref, kf_ref, vf_ref, qb_ref, kb_ref, vb_ref,
                of_ref, ob_ref, state_ref, gc_ref, gt_ref, beta_ref, *, nbatch):
    blk = pl.program_id(1)
    n = GDN_ROWS
    c = GDN_CHUNK

    @pl.when(blk == 0)
    def _():
        state_ref[...] = jnp.zeros_like(state_ref)

    ri = lax.broadcasted_iota(jnp.int32, (n, n), 0)
    ci = lax.broadcasted_iota(jnp.int32, (n, n), 1)
    same = (ri // c) == (ci // c)
    eye = ri == ci
    ti = lax.broadcasted_iota(jnp.int32, (n, GDN_HEADS * LANES), 0)
    tj = lax.broadcasted_iota(jnp.int32, (n, GDN_HEADS * LANES), 1)
    tot_m = jnp.where((ti // c) == (tj // LANES), 1.0, 0.0)

    chains = []
    for d, (a_ref, b_ref, q_ref, k_ref, v_ref, o_ref) in enumerate(
            ((af_ref, bf_ref, qf_ref, kf_ref, vf_ref, of_ref), (ab_ref, bb_ref, qb_ref, kb_ref, vb_ref, ob_ref))):
        sgn = 1 - 2 * d
        after = same & ((ri - ci) * sgn > 0)
        incl = same & ((ri - ci) * sgn >= 0)
        cum_m = jnp.where(same & ((ci - ri) * sgn >= 0), 1.0, 0.0)
        for bi in range(nbatch):
            ch = d * nbatch + bi
            x = a_ref[0, bi] + dtb_ref[d]
            softplus = jnp.maximum(x, 0.0) + jnp.log(1.0 + jnp.exp(-jnp.abs(x)))
            g = -jnp.exp(alog_ref[d]) * softplus
            beta_ref[ch] = _sigmoid(b_ref[0, bi])
            gc_ref[ch] = jnp.dot(g, cum_m, precision=HI, preferred_element_type=F32)
            gt_ref[ch] = jnp.dot(g, tot_m, precision=HI, preferred_element_type=F32)
            chains.append((ch, d, bi, after, incl, q_ref, k_ref, v_ref, o_ref))

    def chunk(j, chain):
        ch, d, bi, after, incl, q_ref, k_ref, v_ref, o_ref = chain
        cc = j if d == 0 else GDN_G - 1 - j
        r0 = pl.multiple_of(cc * c, c)
        gc_row = gc_ref[ch, pl.ds(cc, 1), :]
        beta_row = beta_ref[ch, pl.ds(cc, 1), :]
        gt_row = gt_ref[ch, pl.ds(cc, 1), :]
        gc_col = _row_to_col(gc_row, eye)
        beta_col = _row_to_col(beta_row, eye)
        k_st = _stack_heads(k_ref[bi, pl.ds(r0, c), :]).astype(F32)
        q_st = _stack_heads(q_ref[bi, pl.ds(r0, c), :]).astype(F32)
        v_st = _stack_heads(v_ref[bi, pl.ds(r0, c), :]).astype(F32)
        egc = jnp.exp(gc_col)
        decay = jnp.exp(jnp.minimum(gc_col - gc_row, 0.0))
        kb = k_st * beta_col
        k_bf = k_st.astype(BF16)
        kk = lax.dot_general(kb.astype(BF16), k_bf, (((1,), (1,)), ((), ())), preferred_element_type=F32)
        qk = lax.dot_general(q_st.astype(BF16), k_bf, (((1,), (1,)), ((), ())), preferred_element_type=F32)
        yield
        neg_a = jnp.where(after, -(kk * decay), 0.0)
        t_m = jnp.where(eye, 1.0, 0.0) + neg_a
        p_m = neg_a
        for _ in range(int(math.log2(c)) - 1):
            p_bf = p_m.astype(BF16)
            p_m = jnp.dot(p_bf, p_bf, preferred_element_type=F32)
            yield
            t_m = t_m + jnp.dot(t_m.astype(BF16), p_m.astype(BF16), preferred_element_type=F32)
            yield
        rhs = jnp.concatenate([v_st * beta_col, kb * egc], axis=1).astype(BF16)
        sol = jnp.dot(t_m.astype(BF16), rhs, preferred_element_type=F32)
        yield
        u_st, w_st = sol[:, :LANES], sol[:, LANES:]
        intra = jnp.where(incl, qk * decay, 0.0).astype(BF16)
        q_dec = (q_st * egc).astype(BF16)
        vn, oq = [], []
        for h in range(GDN_HEADS):
            rs = slice(h * c, (h + 1) * c)
            s_h = state_ref[ch * GDN_HEADS + h].astype(BF16)
            vn.append(u_st[rs] - jnp.dot(w_st[rs].astype(BF16), s_h, preferred_element_type=F32))
            oq.append(jnp.dot(q_dec[rs], s_h, preferred_element_type=F32))
        yield
        vn_st = jnp.concatenate(vn, axis=0)
        o_st = jnp.concatenate(oq, axis=0) + jnp.dot(intra, vn_st.astype(BF16), preferred_element_type=F32)
        for h in range(GDN_HEADS):
            rs = slice(h * c, (h + 1) * c)
            gt_h = gt_row[:, h * LANES:(h + 1) * LANES]
            k_dec = (k_st[rs] * jnp.exp(gt_h[:, :1] - gc_col[rs])).astype(BF16)
            upd = lax.dot_general(k_dec, vn[h].astype(BF16), (((0,), (0,)), ((), ())), preferred_element_type=F32)
            state_ref[ch * GDN_HEADS + h] = state_ref[ch * GDN_HEADS + h] * jnp.exp(gt_h) + upd
            o_ref[bi, pl.ds(r0, c), h * LANES:(h + 1) * LANES] = o_st[rs]

    def step(j, carry):
        active = [chunk(j, chain) for chain in chains]
        while active:
            active = [g for g in active if next(g, active) is not active]
        return carry

    lax.fori_loop(0, GDN_G, step, 0)


def _gdn(a_rows, b_rows, alog_row, dtb_row, gq, gk, gv, *, nbatch):
    b, s, _ = gq.shape
    nb = s // (GDN_G * GDN_CHUNK)
    ts = GDN_G * GDN_CHUNK
    nchain = 2 * nbatch
    tok_f = pl.BlockSpec((nbatch, ts, 512), lambda bi, i: (bi, i, 0))
    tok_b = pl.BlockSpec((nbatch, ts, 512), lambda bi, i: (bi, nb - 1 - i, 0))
    rows_f = pl.BlockSpec((1, nbatch, GDN_G, GDN_ROWS), lambda bi, i: (0, bi, i, 0))
    rows_b = pl.BlockSpec((1, nbatch, GDN_G, GDN_ROWS), lambda bi, i: (1, bi, nb - 1 - i, 0))
    par = pl.BlockSpec((2, 1, GDN_ROWS), lambda bi, i: (0, 0, 0))
    return pl.pallas_call(
        functools.partial(_gdn_kernel, nbatch=nbatch),
        out_shape=(jax.ShapeDtypeStruct((b, s, 512), F32), jax.ShapeDtypeStruct((b, s, 512), F32)),
        grid=(b // nbatch, nb),
        in_specs=[rows_f, rows_b, rows_f, rows_b, par, par, tok_f, tok_f, tok_f, tok_b, tok_b, tok_b],
        out_specs=(tok_f, tok_b),
        scratch_shapes=[pltpu.VMEM((nchain * GDN_HEADS, GDN_DK, GDN_DV), F32),
                        pltpu.VMEM((nchain, GDN_G, GDN_ROWS), F32),
                        pltpu.VMEM((nchain, GDN_G, GDN_HEADS * LANES), F32),
                        pltpu.VMEM((nchain, GDN_G, GDN_ROWS), F32)],
        compiler_params=_cparams(("arbitrary", "arbitrary")),
        name="gdn_chunked",
    )(a_rows, a_rows, b_rows, b_rows, alog_row, dtb_row, gq, gk, gv, gq, gk, gv)


def _attn_kernel(q_ref, k0_ref, k1_ref, vt_ref, lam_ref, nw_ref, o_ref, acc_ref, *, mode, tk, lambda_init):
    s_len = k0_ref.shape[1]
    tq = q_ref.shape[1]
    rows = vt_ref.shape[1]
    dv = rows - VT_PAD
    half = LANES // 2
    q = q_ref[0]
    lane = lax.broadcasted_iota(jnp.int32, q.shape, 1)
    keep = (lane < half, lane >= half)
    stab = (lane == half, lane == 0)
    zero = jnp.zeros_like(q)
    krefs = (k0_ref, k1_ref)
    nchunks = s_len // tk
    dn = (((1,), (1,)), ((), ()))

    def kchunk(m, ci):
        return krefs[m][0, pl.ds(pl.multiple_of(ci * tk, tk), tk), :]

    def vchunk(ci):
        return vt_ref[0, :, pl.ds(pl.multiple_of(ci * tk, tk), tk)]

    qm, qa = [], []
    for m in range(2):
        qm.append(jnp.where(keep[m], q, zero))
        k_first = krefs[m][0, 0:min(ATTN_STAB_KEYS, s_len), :]
        mx = jnp.max(lax.dot_general(qm[m], k_first, dn, preferred_element_type=F32), axis=-1, keepdims=True)
        qa.append(jnp.where(stab[m], (-mx).astype(BF16), qm[m]))

    def fast(ci, acc):
        st = [lax.dot_general(kchunk(m, ci), qa[m], dn, preferred_element_type=F32) for m in range(2)]
        vt = vchunk(ci)
        return tuple(acc[m] + jnp.dot(vt, jnp.exp2(st[m]).astype(BF16), preferred_element_type=F32) for m in range(2))

    acc = lax.fori_loop(0, nchunks, fast, tuple(jnp.zeros((rows, tq), F32) for _ in range(2)))
    nonfinite = jnp.float32(0.0)
    for m in range(2):
        acc_ref[m] = acc[m]
        nonfinite = nonfinite + jnp.sum(jnp.where(jnp.isfinite(acc[m]), 0.0, 1.0))

    @pl.when(nonfinite > 0.0)
    def _():
        def slow(ci, carry):
            out = []
            vt = vchunk(ci)
            for m in range(2):
                m_i, a_i = carry[m]
                st = lax.dot_general(kchunk(m, ci), qm[m], dn, preferred_element_type=F32)
                m_new = jnp.maximum(m_i, jnp.max(st, axis=0, keepdims=True))
                p = jnp.exp2(st - m_new).astype(BF16)
                out.append((m_new, jnp.exp2(m_i - m_new) * a_i + jnp.dot(vt, p, preferred_element_type=F32)))
            return tuple(out)

        init = tuple((jnp.full((1, tq), -jnp.inf, F32), jnp.zeros((rows, tq), F32)) for _ in range(2))
        res = lax.fori_loop(0, nchunks, slow, init)
        for m in range(2):
            acc_ref[m] = res[m][1]

    o0 = acc_ref[0, 0:dv, :] / acc_ref[0, dv:dv + 1, :]
    o1 = acc_ref[1, 0:dv, :] / acc_ref[1, dv:dv + 1, :]
    if mode == "diff":
        lv = lam_ref[...]
        lam = (jnp.exp(jnp.sum(lv[0:1] * lv[1:2], axis=-1, keepdims=True))
               - jnp.exp(jnp.sum(lv[2:3] * lv[3:4], axis=-1, keepdims=True)) + lambda_init)
        ot = o0 - lam * o1
        ot = ot * lax.rsqrt(jnp.mean(ot * ot, axis=0, keepdims=True) + NORM_EPS) * nw_ref[...] * (1.0 - lambda_init)
    else:
        ot = jnp.concatenate([o0, o1], axis=0)
    o_ref[0] = ot.T.astype(o_ref.dtype)


def _attention(q, k_arr, vt_arr, lam_vecs, norm_w, *, mode, tq, tk, lambda_init=0.0):
    b, s, w = q.shape
    slabs = w // LANES
    if mode == "diff":
        rows = LANES + VT_PAD
        k_col = lambda p, m: 2 * p + m
        v_grp = lambda p: p
    else:
        rows = GQA_DH + VT_PAD
        k_col = lambda p, m: 2 * (p // 2) + m
        v_grp = lambda p: p // 2
    kspec = lambda m: pl.BlockSpec((1, s, LANES), lambda bi, p, i: (bi, 0, k_col(p, m)))
    return pl.pallas_call(
        functools.partial(_attn_kernel, mode=mode, tk=tk, lambda_init=lambda_init),
        out_shape=jax.ShapeDtypeStruct((b, s, w), BF16),
        grid=(b, slabs, s // tq),
        in_specs=[pl.BlockSpec((1, tq, LANES), lambda bi, p, i: (bi, i, p)),
                  kspec(0), kspec(1),
                  pl.BlockSpec((1, rows, s), lambda bi, p, i: (bi, v_grp(p), 0)),
                  pl.BlockSpec((4, DIFF_DQK), lambda bi, p, i: (0, 0)),
                  pl.BlockSpec((LANES, 1), lambda bi, p, i: (0, 0))],
        out_specs=pl.BlockSpec((1, tq, LANES), lambda bi, p, i: (bi, i, p)),
        scratch_shapes=[pltpu.VMEM((2, rows, tq), F32)],
        compiler_params=_cparams(("arbitrary", "arbitrary", "arbitrary")),
        name="attn_" + mode,
    )(q, k_arr, k_arr, vt_arr, lam_vecs, norm_w)


def _merge_kernel(of_ref, ob_ref, z_ref, g0_ref, g1_ref, g2_ref, yb_ref, yc_ref, x_ref,
                  wa_ref, wb_ref, wc_ref, wo_ref, gnw_ref, fnw_ref, rw_ref, rb_ref,
                  xo_ref, h_ref, id_ref, rwgt_ref):
    o = of_ref[...] + ob_ref[...]
    parts = []
    for h in range(GDN_HEADS):
        oh = o[:, h * LANES:(h + 1) * LANES]
        parts.append(oh * lax.rsqrt(jnp.mean(oh * oh, axis=-1, keepdims=True) + NORM_EPS) * gnw_ref[...])
    z = z_ref[...].astype(F32)
    ya = (jnp.concatenate(parts, axis=1) * (z * _sigmoid(z))).astype(BF16)
    merged = _sigmoid(g0_ref[...].astype(F32)) * jnp.dot(ya, wa_ref[...], preferred_element_type=F32)
    merged = merged + _sigmoid(g1_ref[...].astype(F32)) * jnp.dot(yb_ref[...], wb_ref[...], preferred_element_type=F32)
    merged = merged + _sigmoid(g2_ref[...].astype(F32)) * jnp.dot(yc_ref[...], wc_ref[...], preferred_element_type=F32)
    xn = x_ref[...] + jnp.dot(merged.astype(BF16), wo_ref[...], preferred_element_type=F32)
    xo_ref[...] = xn
    hf = xn * lax.rsqrt(jnp.mean(xn * xn, axis=-1, keepdims=True) + NORM_EPS) * fnw_ref[...]
    h_ref[...] = hf.astype(BF16)
    logits = jnp.dot(hf, rw_ref[...], precision=HI, preferred_element_type=F32) + rb_ref[...]
    lane = lax.broadcasted_iota(jnp.int32, logits.shape, 1)
    big = jnp.int32(LANES)
    ninf = -jnp.inf
    glog = jnp.where(lane < N_GROUPS, logits, ninf)
    gmax = jnp.max(glog, axis=-1, keepdims=True)
    gidx = jnp.min(jnp.where(glog == gmax, lane, big), axis=-1, keepdims=True)
    gp = 1.0 / jnp.sum(jnp.exp(glog - gmax), axis=-1, keepdims=True)
    e = lane - N_GROUPS
    sel = (e >= 0) & (e < N_EXPERTS) & ((e // EXPERTS_PER_GROUP) == gidx)
    elog = jnp.where(sel, logits, ninf)
    m1 = jnp.max(elog, axis=-1, keepdims=True)
    i1 = jnp.min(jnp.where(elog == m1, lane, big), axis=-1, keepdims=True)
    elog2 = jnp.where(lane == i1, ninf, elog)
    m2 = jnp.max(elog2, axis=-1, keepdims=True)
    i2 = jnp.min(jnp.where(elog2 == m2, lane, big), axis=-1, keepdims=True)
    e2 = jnp.exp(m2 - m1)
    w1 = 1.0 / (1.0 + e2)
    w2 = e2 * w1
    id_ref[...] = jnp.where(lane == 0, i1 - N_GROUPS, jnp.where(lane == 1, i2 - N_GROUPS, 0))
    rwgt_ref[...] = jnp.where(lane == 0, gp * w1, jnp.where(lane == 1, gp * w2, 0.0))


def _merge(o_f, o_b, main2, yb, yc, x2d, wa, wb, wc, wo, gnw, fnw, rw, rb, *, tm):
    t, d = x2d.shape
    full = lambda shp: pl.BlockSpec(shp, lambda i: tuple(0 for _ in shp))
    return pl.pallas_call(
        _merge_kernel,
        out_shape=(jax.ShapeDtypeStruct((t, d), F32), jax.ShapeDtypeStruct((t, d), BF16),
                   jax.ShapeDtypeStruct((t, LANES), jnp.int32), jax.ShapeDtypeStruct((t, LANES), F32)),
        grid=(t // tm,),
        in_specs=[pl.BlockSpec((tm, 512), lambda i: (i, 0)),
                  pl.BlockSpec((tm, 512), lambda i: (i, 0)),
                  pl.BlockSpec((tm, 512), lambda i: (i, COL_Z // 512)),
                  pl.BlockSpec((tm, d), lambda i: (i, 0)),
                  pl.BlockSpec((tm, d), lambda i: (i, 1)),
                  pl.BlockSpec((tm, d), lambda i: (i, 2)),
                  pl.BlockSpec((tm, 512), lambda i: (i, 0)),
                  pl.BlockSpec((tm, 512), lambda i: (i, 0)),
                  pl.BlockSpec((tm, d), lambda i: (i, 0)),
                  full((512, d)), full((512, d)), full((512, d)), full((d, d)),
                  full((1, LANES)), full((1, d)), full((d, LANES)), full((1, LANES))],
        out_specs=(pl.BlockSpec((tm, d), lambda i: (i, 0)), pl.BlockSpec((tm, d), lambda i: (i, 0)),
                   pl.BlockSpec((tm, LANES), lambda i: (i, 0)), pl.BlockSpec((tm, LANES), lambda i: (i, 0))),
        compiler_params=_cparams(("arbitrary",)),
        name="merge_router",
    )(o_f, o_b, main2, main2, main2, main2, yb, yc, x2d, wa, wb, wc, wo, gnw, fnw, rw, rb)


def _expert_kernel(blk_e_ref, nused_ref, x_ref, w1_ref, w3_ref, w2_ref, o_ref, w1b_ref, w3b_ref, w2b_ref):
    i = pl.program_id(0)

    @pl.when((i == 0) | (blk_e_ref[i] != blk_e_ref[jnp.maximum(i - 1, 0)]))
    def _():
        w1b_ref[...] = w1_ref[0, 0].astype(BF16)
        w3b_ref[...] = w3_ref[0, 0].astype(BF16)
        w2b_ref[...] = w2_ref[0, 0].astype(BF16)

    @pl.when(i < nused_ref[0])
    def _():
        x = x_ref[...]
        a = jnp.dot(x, w1b_ref[...], preferred_element_type=F32)
        u = jnp.dot(x, w3b_ref[...], preferred_element_type=F32)
        hmid = (a * _sigmoid(a) * u).astype(BF16)
        o_ref[...] = jnp.dot(hmid, w2b_ref[...], preferred_element_type=F32).astype(o_ref.dtype)

    @pl.when(i >= nused_ref[0])
    def _():
        o_ref[...] = jnp.zeros_like(o_ref)


def _experts(blk_e, nused, xb, w1, w3, w2, *, layer):
    p_len, d = xb.shape
    ff = w1.shape[3]
    nblk = p_len // MOE_BLOCK
    return pl.pallas_call(
        _expert_kernel,
        out_shape=jax.ShapeDtypeStruct((p_len, d), BF16),
        grid_spec=pltpu.PrefetchScalarGridSpec(
            num_scalar_prefetch=2,
            grid=(nblk,),
            in_specs=[pl.BlockSpec((MOE_BLOCK, d), lambda i, be, nu: (i, 0)),
                      pl.BlockSpec((1, 1, d, ff), lambda i, be, nu: (layer, be[i], 0, 0)),
                      pl.BlockSpec((1, 1, d, ff), lambda i, be, nu: (layer, be[i], 0, 0)),
                      pl.BlockSpec((1, 1, ff, d), lambda i, be, nu: (layer, be[i], 0, 0))],
            out_specs=pl.BlockSpec((MOE_BLOCK, d), lambda i, be, nu: (i, 0)),
            scratch_shapes=[pltpu.VMEM((d, ff), BF16), pltpu.VMEM((d, ff), BF16), pltpu.VMEM((ff, d), BF16)],
        ),
        compiler_params=_cparams(("arbitrary",)),
        name="expert_mlp",
    )(blk_e, nused, xb, w1, w3, w2)


def _combine_kernel(x_ref, y0_ref, y1_ref, w_ref, nw_ref, o_ref, *, final):
    w = w_ref[...]
    x = x_ref[...] + w[:, 0:1] * y0_ref[...].astype(F32) + w[:, 1:2] * y1_ref[...].astype(F32)
    if final:
        x = x * lax.rsqrt(jnp.mean(x * x, axis=-1, keepdims=True) + NORM_EPS) * nw_ref[...]
    o_ref[...] = x


def _combine(x2d, y0, y1, wts, norm_w, *, final, tm):
    t, d = x2d.shape
    tile = pl.BlockSpec((tm, d), lambda i: (i, 0))
    return pl.pallas_call(
        functools.partial(_combine_kernel, final=final),
        out_shape=jax.ShapeDtypeStruct((t, d), F32),
        grid=(t // tm,),
        in_specs=[tile, tile, tile, pl.BlockSpec((tm, LANES), lambda i: (i, 0)), pl.BlockSpec((1, d), lambda i: (0, 0))],
        out_specs=tile,
        compiler_params=_cparams(("arbitrary",)),
        name="moe_combine",
    )(x2d, y0, y1, wts, norm_w.reshape(1, d))


def _rope_tables(pos, dim):
    inv = 1.0 / (ROPE_THETA ** (jnp.arange(0, dim, 2, dtype=F32) / dim))
    ang = pos.astype(F32)[:, None] * inv[None, :]
    ang = jnp.concatenate([ang, ang], axis=-1)
    return jnp.cos(ang), jnp.sin(ang)


def _signed_sin(sin):
    half = sin.shape[-1] // 2
    return jnp.concatenate([-sin[:, :half], sin[:, half:]], axis=-1)


def _layout_w_in(w):
    o = 0
    parts = {}
    for name, size in (("qkv", 1536), ("z", 512), ("b", 8), ("a", 8), ("dq", 512), ("dk", 512), ("dv", 512),
                       ("cq", 512), ("ck", 128), ("cv", 128), ("gate", 3072)):
        parts[name] = w[:, o:o + size]
        o += size
    swap = lambda m: jnp.concatenate([m[:, 64:], m[:, :64]], axis=1)
    main = jnp.concatenate([parts["gate"], parts["qkv"], parts["z"], parts["dq"], parts["dk"], parts["dv"],
                            parts["cq"], parts["ck"], swap(parts["ck"]), parts["cv"], swap(parts["cv"])], axis=1)
    ba = jnp.concatenate([parts["b"], parts["a"], jnp.zeros((w.shape[0], LANES - 16), w.dtype)], axis=1)
    return main.astype(BF16), ba


def _rows_layout(t, bsz, s):
    nc = s // GDN_CHUNK
    t = t.reshape(bsz, nc, GDN_CHUNK, 2, GDN_HEADS)
    return jnp.transpose(t, (3, 0, 1, 4, 2)).reshape(2, bsz, nc, GDN_ROWS)


def _moe_dispatch(ids, t):
    a = t * TOPK
    p_len = ((a + N_EXPERTS * (MOE_BLOCK - 1) + MOE_BLOCK - 1) // MOE_BLOCK) * MOE_BLOCK
    n_blocks = p_len // MOE_BLOCK
    flat_e = ids.reshape(-1)
    iota_a = jnp.arange(a, dtype=jnp.int32)
    skey = jnp.sort(flat_e * a + iota_a)
    order = skey % a
    se = skey // a
    experts = jnp.arange(N_EXPERTS, dtype=jnp.int32)
    counts = jnp.sum((flat_e[:, None] == experts[None, :]).astype(jnp.int32), axis=0)
    start = jnp.cumsum(counts) - counts
    pcounts = ((counts + MOE_BLOCK - 1) // MOE_BLOCK) * MOE_BLOCK
    pend = jnp.cumsum(pcounts)
    pstart = pend - pcounts
    dest_sorted = pstart[se] + (iota_a - start[se])
    blk_first = jnp.arange(n_blocks, dtype=jnp.int32) * MOE_BLOCK
    blk_e = jnp.minimum(jnp.sum((pend[None, :] <= blk_first[:, None]).astype(jnp.int32), axis=1), N_EXPERTS - 1)
    row = jnp.arange(p_len, dtype=jnp.int32)
    row_e = jnp.repeat(blk_e, MOE_BLOCK)
    j = row - pstart[row_e]
    valid = j < counts[row_e]
    tok_buf = jnp.where(valid, order[jnp.minimum(start[row_e] + j, a - 1)] // TOPK, 0)
    _, dest = lax.sort((order, dest_sorted), num_keys=1)
    nused = (pend[-1] // MOE_BLOCK).astype(jnp.int32).reshape(1)
    return tok_buf, dest.reshape(t, TOPK), blk_e, nused


def kernel(x, attn_norm_w, w_in, gdn_conv_w, gdn_a_log, gdn_dt_bias, gdn_norm_w, diff_lambda, diff_norm_w,
           gqa_q_norm_w, gqa_k_norm_w, w_branch_a, w_branch_b, w_branch_c, w_out, ffn_norm_w,
           router_group_w, router_group_b, router_expert_w, router_expert_b,
           expert_w_gate, expert_w_up, expert_w_down, final_norm_w):
    bsz, s, d = x.shape
    t = bsz * s
    depth = w_in.shape[0]
    tm = min(512, t)
    ts = min(512, s)

    rows = s // GRID_W
    row = jnp.broadcast_to(jnp.arange(rows)[:, None], (rows, GRID_W)).reshape(s)
    col = jnp.broadcast_to(jnp.arange(GRID_W)[None, :], (rows, GRID_W)).reshape(s)
    c1, s1 = _rope_tables(jnp.arange(s), DIFF_DQK)
    cr, sr = _rope_tables(row, GQA_DH // 2)
    cc, sc = _rope_tables(col, GQA_DH // 2)
    cos1 = jnp.tile(c1, (1, 2))
    sin1 = jnp.tile(_signed_sin(s1), (1, 2))
    cos2 = jnp.tile(jnp.concatenate([cr, cc], axis=-1), (1, 2))
    sin2 = jnp.tile(jnp.concatenate([_signed_sin(sr), _signed_sin(sc)], axis=-1), (1, 2))

    x2 = x.reshape(t, d)
    for l in range(depth):
        lambda_init = 0.8 - 0.6 * math.exp(-0.3 * l)
        w_main, w_ba = _layout_w_in(w_in[l])
        main2 = _norm_proj(x2, attn_norm_w[l], w_main, BF16, exact=False, tm=tm, tn=N_MAIN // 2)
        ba = _norm_proj(x2, attn_norm_w[l], w_ba, F32, exact=True, tm=tm, tn=LANES)
        main3 = main2.reshape(bsz, s, N_MAIN)

        conv_w = jnp.concatenate([gdn_conv_w[l], jnp.zeros((8 - GDN_CONV, gdn_conv_w.shape[2]), F32)], axis=0)
        qnw = jnp.tile(gqa_q_norm_w[l], 2).reshape(1, LANES)
        knw = jnp.tile(gqa_k_norm_w[l], 2).reshape(1, LANES)
        gq, gk, gv, dq, dk, dv, cq, ck, cv = _prep(main3, conv_w, cos1, sin1, cos2, sin2, qnw, knw, ts=ts)

        b_rows = _rows_layout(ba[:, 0:8], bsz, s)
        a_rows = _rows_layout(ba[:, 8:16], bsz, s)
        alog_row = jnp.repeat(gdn_a_log[l], GDN_CHUNK, axis=1).reshape(2, 1, GDN_ROWS)
        dtb_row = jnp.repeat(gdn_dt_bias[l], GDN_CHUNK, axis=1).reshape(2, 1, GDN_ROWS)
        o_f, o_b = _gdn(a_rows, b_rows, alog_row, dtb_row, gq, gk, gv, nbatch=GDN_NBATCH if bsz % GDN_NBATCH == 0 else 1)

        nw_diff = diff_norm_w[l].reshape(LANES, 1)
        yb = _attention(dq, dk, dv, diff_lambda[l], nw_diff, mode="diff",
                        tq=min(ATTN_TQ, s), tk=min(ATTN_TK, s), lambda_init=lambda_init)
        yc = _attention(cq, ck, cv, diff_lambda[l], nw_diff, mode="gqa",
                        tq=min(ATTN_TQ, s), tk=min(ATTN_TK, s))

        rw = jnp.concatenate([router_group_w[l], router_expert_w[l],
                              jnp.zeros((d, LANES - N_GROUPS - N_EXPERTS), F32)], axis=1)
        rb = jnp.concatenate([router_group_b[l], router_expert_b[l],
                              jnp.zeros((LANES - N_GROUPS - N_EXPERTS,), F32)]).reshape(1, LANES)
        x2, h2, ids, wts = _merge(o_f.reshape(t, 512), o_b.reshape(t, 512), main2, yb.reshape(t, 512), yc.reshape(t, 512), x2,
                                  w_branch_a[l].astype(BF16), w_branch_b[l].astype(BF16),
                                  w_branch_c[l].astype(BF16), w_out[l].astype(BF16),
                                  gdn_norm_w[l].reshape(1, LANES), ffn_norm_w[l].reshape(1, d), rw, rb, tm=tm)

        tok_buf, dest, blk_e, nused = _moe_dispatch(ids[:, :TOPK], t)
        yblk = _experts(blk_e, nused, h2[tok_buf], expert_w_gate, expert_w_up, expert_w_down, layer=l)
        x2 = _combine(x2, yblk[dest[:, 0]], yblk[dest[:, 1]], wts, final_norm_w, final=(l == depth - 1), tm=tm)

    return x2.reshape(bsz, s, d)
```

```python
import functools
import math

import jax
import jax.numpy as jnp
from jax import lax
from jax.experimental import pallas as pl
from jax.experimental.pallas import tpu as pltpu

GRID_W = 64
ROPE_THETA = 10000.0
NORM_EPS = 1e-6
GDN_HEADS = 4
GDN_DK = 128
GDN_DV = 128
GDN_CONV = 5
GDN_CHUNK = 64
DIFF_HEADS = 4
DIFF_DQK = 64
GQA_HEADS = 8
GQA_KV = 2
GQA_DH = 64
N_GROUPS = 4
EXPERTS_PER_GROUP = 8
N_EXPERTS = N_GROUPS * EXPERTS_PER_GROUP
TOPK = 2
MOE_BLOCK = 256

LANES = 128
VMEM_LIMIT = 56 * 1024 * 1024

COL_GATE = 0
COL_QKV = 3072
COL_Z = 4608
COL_DQ = 5120
COL_DK = 5632
COL_DV = 6144
COL_CQ = 6656
COL_CK = 7168
COL_CV = 7424
N_MAIN = 7680

LOG2E = math.log2(math.e)
ATTN_TQ = 1024
ATTN_TK = 1024
ATTN_STAB_KEYS = 256
VT_PAD = 16

HI = lax.Precision.HIGHEST
F32 = jnp.float32
BF16 = jnp.bfloat16


def _cparams(sem):
    return pltpu.CompilerParams(dimension_semantics=sem, vmem_limit_bytes=VMEM_LIMIT)


def _sigmoid(x):
    return 1.0 / (1.0 + jnp.exp(-x))


def _dot_split(a, w):
    a_hi = a.astype(BF16)
    a_lo = (a - a_hi.astype(F32)).astype(BF16)
    w_hi = w.astype(BF16)
    w_lo = (w - w_hi.astype(F32)).astype(BF16)
    return (jnp.dot(a_hi, w_hi, preferred_element_type=F32) + jnp.dot(a_lo, w_hi, preferred_element_type=F32)
            + jnp.dot(a_hi, w_lo, preferred_element_type=F32))


def _norm_proj_kernel(x_ref, nw_ref, w_ref, o_ref, *, exact):
    x = x_ref[...]
    h = x * lax.rsqrt(jnp.mean(x * x, axis=-1, keepdims=True) + NORM_EPS) * nw_ref[...]
    if exact:
        o_ref[...] = _dot_split(h, w_ref[...]).astype(o_ref.dtype)
    else:
        o_ref[...] = jnp.dot(h.astype(BF16), w_ref[...], preferred_element_type=F32).astype(o_ref.dtype)


def _norm_proj(x2d, norm_w, w, out_dtype, *, exact, tm, tn):
    t, d = x2d.shape
    n = w.shape[1]
    return pl.pallas_call(
        functools.partial(_norm_proj_kernel, exact=exact),
        out_shape=jax.ShapeDtypeStruct((t, n), out_dtype),
        grid=(n // tn, t // tm),
        in_specs=[pl.BlockSpec((tm, d), lambda j, i: (i, 0)),
                  pl.BlockSpec((1, d), lambda j, i: (0, 0)),
                  pl.BlockSpec((d, tn), lambda j, i: (0, j))],
        out_specs=pl.BlockSpec((tm, tn), lambda j, i: (i, j)),
        compiler_params=_cparams(("arbitrary", "arbitrary")),
        name="norm_proj_exact" if exact else "norm_proj",
    )(x2d, norm_w.reshape(1, d), w)


HALO = 16


def _rot_half(x, half):
    lane = lax.broadcasted_iota(jnp.int32, x.shape, 1)
    first = (lane % (2 * half)) < half
    return jnp.where(first, pltpu.roll(x, LANES - half, 1), pltpu.roll(x, half, 1))


def _group_sumsq(x, width):
    x2 = x * x
    if width == LANES:
        return jnp.sum(x2, axis=-1, keepdims=True)
    lane = lax.broadcasted_iota(jnp.int32, x.shape, 1)
    lo = lane < width
    s_lo = jnp.sum(jnp.where(lo, x2, 0.0), axis=-1, keepdims=True)
    s_hi = jnp.sum(jnp.where(lo, 0.0, x2), axis=-1, keepdims=True)
    return jnp.where(lo, s_lo, s_hi)


def _aug_slab(x, m):
    lane = lax.broadcasted_iota(jnp.int32, x.shape, 1)
    half = LANES // 2
    keep = (lane < half) if m == 0 else (lane >= half)
    one = jnp.where(lane == (1 - m) * half, 1.0, 0.0).astype(x.dtype)
    return jnp.where(keep, x, one)


def _prep_kernel(qkv_ref, prev_ref, next_ref, dq_ref, dk_ref, dv_ref, cq_ref, ck_ref, cv_ref,
                 convw_ref, cos1_ref, sin1_ref, cos2_ref, sin2_ref, qnw_ref, knw_ref,
                 gq_ref, gk_ref, gv_ref, dqo_ref, dko_ref, dvo_ref, cqo_ref, cko_ref, cvo_ref, *, ts):
    i = pl.program_id(1)
    n = pl.num_programs(1)
    cur = qkv_ref[0].astype(F32)
    prev = jnp.where(i > 0, prev_ref[0].astype(F32), 0.0)
    nxt = jnp.where(i < n - 1, next_ref[0].astype(F32), 0.0)
    ext = jnp.concatenate([prev, cur, nxt], axis=0)
    pad = GDN_CONV // 2
    acc = jnp.zeros_like(cur)
    for j in range(GDN_CONV):
        off = HALO - pad + j
        acc = acc + ext[off:off + ts, :] * convw_ref[j:j + 1, :]
    act = acc * _sigmoid(acc)
    nqk = GDN_HEADS * GDN_DK
    for h in range(GDN_HEADS):
        sl = slice(h * GDN_DK, (h + 1) * GDN_DK)
        qh = act[:, sl]
        gq_ref[0, :, sl] = (qh * lax.rsqrt(_group_sumsq(qh, LANES) + NORM_EPS) * (GDN_DK ** -0.5)).astype(BF16)
        kh = act[:, nqk + h * GDN_DK: nqk + (h + 1) * GDN_DK]
        gk_ref[0, :, sl] = (kh * lax.rsqrt(_group_sumsq(kh, LANES) + NORM_EPS)).astype(BF16)
    gv_ref[0] = act[:, 2 * nqk:].astype(BF16)
    cos1, sin1 = cos1_ref[...], sin1_ref[...]
    ones_rows = jnp.where(lax.broadcasted_iota(jnp.int32, (VT_PAD, ts), 0) == 0, 1.0, 0.0).astype(BF16)
    for p in range(DIFF_HEADS):
        sl = slice(p * LANES, (p + 1) * LANES)
        xq = dq_ref[0, :, sl].astype(F32)
        dqo_ref[0, :, sl] = ((xq * cos1 + _rot_half(xq, DIFF_DQK // 2) * sin1) * (DIFF_DQK ** -0.5 * LOG2E)).astype(BF16)
        xk = dk_ref[0, :, sl].astype(F32)
        xk = (xk * cos1 + _rot_half(xk, DIFF_DQK // 2) * sin1).astype(BF16)
        for m in range(2):
            dko_ref[0, :, (2 * p + m) * LANES:(2 * p + m + 1) * LANES] = _aug_slab(xk, m)
        r0 = p * (LANES + VT_PAD)
        dvo_ref[0, r0:r0 + LANES, :] = dv_ref[0, :, sl].astype(F32).T.astype(BF16)
        dvo_ref[0, r0 + LANES:r0 + LANES + VT_PAD, :] = ones_rows
    cos2, sin2 = cos2_ref[...], sin2_ref[...]
    for p in range(GQA_HEADS * GQA_DH // LANES):
        sl = slice(p * LANES, (p + 1) * LANES)
        xq = cq_ref[0, :, sl].astype(F32)
        xq = xq * lax.rsqrt(_group_sumsq(xq, GQA_DH) * (1.0 / GQA_DH) + NORM_EPS) * qnw_ref[...]
        cqo_ref[0, :, sl] = ((xq * cos2 + _rot_half(xq, GQA_DH // 4) * sin2) * (GQA_DH ** -0.5 * LOG2E)).astype(BF16)
    for p in range(2):
        sl = slice(p * LANES, (p + 1) * LANES)
        xk = ck_ref[0, :, sl].astype(F32)
        xk = xk * lax.rsqrt(_group_sumsq(xk, GQA_DH) * (1.0 / GQA_DH) + NORM_EPS) * knw_ref[...]
        xk = (xk * cos2 + _rot_half(xk, GQA_DH // 4) * sin2).astype(BF16)
        for m in range(2):
            c = p if m == 0 else 1 - p
            cko_ref[0, :, (2 * c + m) * LANES:(2 * c + m + 1) * LANES] = _aug_slab(xk, m)
    vt = cv_ref[0].astype(F32).T.astype(BF16)
    for c in range(GQA_KV):
        r0 = c * (GQA_DH + VT_PAD)
        cvo_ref[0, r0:r0 + GQA_DH, :] = vt[c * GQA_DH:(c + 1) * GQA_DH]
        cvo_ref[0, r0 + GQA_DH:r0 + GQA_DH + VT_PAD, :] = ones_rows


def _prep(main3, conv_w, cos1, sin1, cos2, sin2, qnw, knw, *, ts):
    b, s, _ = main3.shape
    nt = s // ts
    hb = ts // HALO
    last = s // HALO - 1
    row = lambda w: pl.BlockSpec((1, w), lambda bi, i: (0, 0))
    tab = pl.BlockSpec((ts, LANES), lambda bi, i: (i, 0))
    col = lambda w, off: pl.BlockSpec((1, ts, w), lambda bi, i: (bi, i, off // w))
    out = lambda w: pl.BlockSpec((1, ts, w), lambda bi, i: (bi, i, 0))
    outs = [("tok", 512), ("tok", 512), ("tok", 512), ("tok", 512), ("tok", 1024),
            ("rows", DIFF_HEADS * (LANES + VT_PAD)), ("tok", 512), ("tok", 512), ("rows", GQA_KV * (GQA_DH + VT_PAD))]
    specs = tuple(out(w) if kind == "tok" else pl.BlockSpec((1, w, ts), lambda bi, i: (bi, 0, i)) for kind, w in outs)
    shapes = tuple(jax.ShapeDtypeStruct((b, s, w) if kind == "tok" else (b, w, s), BF16) for kind, w in outs)
    return pl.pallas_call(
        functools.partial(_prep_kernel, ts=ts),
        out_shape=shapes,
        grid=(b, nt),
        in_specs=[
            col(1536, COL_QKV),
            pl.BlockSpec((1, HALO, 1536), lambda bi, i: (bi, jnp.maximum(i * hb - 1, 0), COL_QKV // 1536)),
            pl.BlockSpec((1, HALO, 1536), lambda bi, i: (bi, jnp.minimum((i + 1) * hb, last), COL_QKV // 1536)),
            col(512, COL_DQ), col(512, COL_DK), col(512, COL_DV), col(512, COL_CQ), col(256, COL_CK), col(128, COL_CV),
            pl.BlockSpec((8, 1536), lambda bi, i: (0, 0)),
            tab, tab, tab, tab, row(LANES), row(LANES),
        ],
        out_specs=specs,
        compiler_params=_cparams(("arbitrary", "arbitrary")),
        name="mixer_prep",
    )(*([main3] * 9), conv_w, cos1, sin1, cos2, sin2, qnw, knw)


GDN_G = 8
GDN_NBATCH = 2
GDN_ROWS = GDN_HEADS * GDN_CHUNK


def _stack_heads(x):
    return jnp.concatenate([x[:, h * LANES:(h + 1) * LANES] for h in range(GDN_HEADS)], axis=0)


def _row_to_col(row, eye):
    return jnp.sum(jnp.where(eye, row, 0.0), axis=1, keepdims=True)


def _gdn_kernel(af_ref, ab_ref, bf_ref, bb_ref, alog_ref, dtb_ref, qf_ref, kf_ref, vf_ref, qb_ref, kb_ref, vb_ref,
                of_ref, ob_ref, state_ref, gc_ref, gt_ref, beta_ref, *, nbatch):
    blk = pl.program_id(1)
    n = GDN_ROWS
    c = GDN_CHUNK

    @pl.when(blk == 0)
    def _():
        state_ref[...] = jnp.zeros_like(state_ref)

    ri = lax.broadcasted_iota(jnp.int32, (n, n), 0)
    ci = lax.broadcasted_iota(jnp.int32, (n, n), 1)
    same = (ri // c) == (ci // c)
    eye = ri == ci
    ti = lax.broadcasted_iota(jnp.int32, (n, GDN_HEADS * LANES), 0)
    tj = lax.broadcasted_iota(jnp.int32, (n, GDN_HEADS * LANES), 1)
    tot_m = jnp.where((ti // c) == (tj // LANES), 1.0, 0.0)

    chains = []
    for d, (a_ref, b_ref, q_ref, k_ref, v_ref, o_ref) in enumerate(
            ((af_ref, bf_ref, qf_ref, kf_ref, vf_ref, of_ref), (ab_ref, bb_ref, qb_ref, kb_ref, vb_ref, ob_ref))):
        sgn = 1 - 2 * d
        after = same & ((ri - ci) * sgn > 0)
        incl = same & ((ri - ci) * sgn >= 0)
        cum_m = jnp.where(same & ((ci - ri) * sgn >= 0), 1.0, 0.0)
        for bi in range(nbatch):
            ch = d * nbatch + bi
            x = a_ref[0, bi] + dtb_ref[d]
            softplus = jnp.maximum(x, 0.0) + jnp.log(1.0 + jnp.exp(-jnp.abs(x)))
            g = -jnp.exp(alog_ref[d]) * softplus
            beta_ref[ch] = _sigmoid(b_ref[0, bi])
            gc_ref[ch] = jnp.dot(g, cum_m, precision=HI, preferred_element_type=F32)
            gt_ref[ch] = jnp.dot(g, tot_m, precision=HI, preferred_element_type=F32)
            chains.append((ch, d, bi, after, incl, q_ref, k_ref, v_ref, o_ref))

    def chunk(j, chain):
        ch, d, bi, after, incl, q_ref, k_ref, v_ref, o_ref = chain
        cc = j if d == 0 else GDN_G - 1 - j
        r0 = pl.multiple_of(cc * c, c)
        gc_row = gc_ref[ch, pl.ds(cc, 1), :]
        beta_row = beta_ref[ch, pl.ds(cc, 1), :]
        gt_row = gt_ref[ch, pl.ds(cc, 1), :]
        gc_col = _row_to_col(gc_row, eye)
        beta_col = _row_to_col(beta_row, eye)
        k_st = _stack_heads(k_ref[bi, pl.ds(r0, c), :]).astype(F32)
        q_st = _stack_heads(q_ref[bi, pl.ds(r0, c), :]).astype(F32)
        v_st = _stack_heads(v_ref[bi, pl.ds(r0, c), :]).astype(F32)
        egc = jnp.exp(gc_col)
        decay = jnp.exp(jnp.minimum(gc_col - gc_row, 0.0))
        kb = k_st * beta_col
        k_bf = k_st.astype(BF16)
        kk = lax.dot_general(kb.astype(BF16), k_bf, (((1,), (1,)), ((), ())), preferred_element_type=F32)
        qk = lax.dot_general(q_st.astype(BF16), k_bf, (((1,), (1,)), ((), ())), preferred_element_type=F32)
        yield
        neg_a = jnp.where(after, -(kk * decay), 0.0)
        t_m = jnp.where(eye, 1.0, 0.0) + neg_a
        p_m = neg_a
        for _ in range(int(math.log2(c)) - 1):
            p_bf = p_m.astype(BF16)
            p_m = jnp.dot(p_bf, p_bf, preferred_element_type=F32)
            yield
            t_m = t_m + jnp.dot(t_m.astype(BF16), p_m.astype(BF16), preferred_element_type=F32)
            yield
        rhs = jnp.concatenate([v_st * beta_col, kb * egc], axis=1).astype(BF16)
        sol = jnp.dot(t_m.astype(BF16), rhs, preferred_element_type=F32)
        yield
        u_st, w_st = sol[:, :LANES], sol[:, LANES:]
        intra = jnp.where(incl, qk * decay, 0.0).astype(BF16)
        q_dec = (q_st * egc).astype(BF16)
        vn, oq = [], []
        for h in range(GDN_HEADS):
            rs = slice(h * c, (h + 1) * c)
            s_h = state_ref[ch * GDN_HEADS + h].astype(BF16)
            vn.append(u_st[rs] - jnp.dot(w_st[rs].astype(BF16), s_h, preferred_element_type=F32))
            oq.append(jnp.dot(q_dec[rs], s_h, preferred_element_type=F32))
        yield
        vn_st = jnp.concatenate(vn, axis=0)
        o_st = jnp.concatenate(oq, axis=0) + jnp.dot(intra, vn_st.astype(BF16), preferred_element_type=F32)
        for h in range(GDN_HEADS):
            rs = slice(h * c, (h + 1) * c)
            gt_h = gt_row[:, h * LANES:(h + 1) * LANES]
            k_dec = (k_st[rs] * jnp.exp(gt_h[:, :1] - gc_col[rs])).astype(BF16)
            upd = lax.dot_general(k_dec, vn[h].astype(BF16), (((0,), (0,)), ((), ())), preferred_element_type=F32)
            state_ref[ch * GDN_HEADS + h] = state_ref[ch * GDN_HEADS + h] * jnp.exp(gt_h) + upd
            o_ref[bi, pl.ds(r0, c), h * LANES:(h + 1) * LANES] = o_st[rs]

    def step(j, carry):
        active = [chunk(j, chain) for chain in chains]
        while active:
            active = [g for g in active if next(g, active) is not active]
        return carry

    lax.fori_loop(0, GDN_G, step, 0)


def _gdn(a_rows, b_rows, alog_row, dtb_row, gq, gk, gv, *, nbatch):
    b, s, _ = gq.shape
    nb = s // (GDN_G * GDN_CHUNK)
    ts = GDN_G * GDN_CHUNK
    nchain = 2 * nbatch
    tok_f = pl.BlockSpec((nbatch, ts, 512), lambda bi, i: (bi, i, 0))
    tok_b = pl.BlockSpec((nbatch, ts, 512), lambda bi, i: (bi, nb - 1 - i, 0))
    rows_f = pl.BlockSpec((1, nbatch, GDN_G, GDN_ROWS), lambda bi, i: (0, bi, i, 0))
    rows_b = pl.BlockSpec((1, nbatch, GDN_G, GDN_ROWS), lambda bi, i: (1, bi, nb - 1 - i, 0))
    par = pl.BlockSpec((2, 1, GDN_ROWS), lambda bi, i: (0, 0, 0))
    return pl.pallas_call(
        functools.partial(_gdn_kernel, nbatch=nbatch),
        out_shape=(jax.ShapeDtypeStruct((b, s, 512), F32), jax.ShapeDtypeStruct((b, s, 512), F32)),
        grid=(b // nbatch, nb),
        in_specs=[rows_f, rows_b, rows_f, rows_b, par, par, tok_f, tok_f, tok_f, tok_b, tok_b, tok_b],
        out_specs=(tok_f, tok_b),
        scratch_shapes=[pltpu.VMEM((nchain * GDN_HEADS, GDN_DK, GDN_DV), F32),
                        pltpu.VMEM((nchain, GDN_G, GDN_ROWS), F32),
                        pltpu.VMEM((nchain, GDN_G, GDN_HEADS * LANES), F32),
                        pltpu.VMEM((nchain, GDN_G, GDN_ROWS), F32)],
        compiler_params=_cparams(("arbitrary", "arbitrary")),
        name="gdn_chunked",
    )(a_rows, a_rows, b_rows, b_rows, alog_row, dtb_row, gq, gk, gv, gq, gk, gv)


def _attn_kernel(q_ref, k0_ref, k1_ref, vt_ref, lam_ref, nw_ref, o_ref, acc_ref, *, mode, tk, lambda_init):
    s_len = k0_ref.shape[1]
    tq = q_ref.shape[1]
    rows = vt_ref.shape[1]
    dv = rows - VT_PAD
    half = LANES // 2
    q = q_ref[0]
    lane = lax.broadcasted_iota(jnp.int32, q.shape, 1)
    keep = (lane < half, lane >= half)
    stab = (lane == half, lane == 0)
    zero = jnp.zeros_like(q)
    krefs = (k0_ref, k1_ref)
    nchunks = s_len // tk
    dn = (((1,), (1,)), ((), ()))

    def kchunk(m, ci):
        return krefs[m][0, pl.ds(pl.multiple_of(ci * tk, tk), tk), :]

    def vchunk(ci):
        return vt_ref[0, :, pl.ds(pl.multiple_of(ci * tk, tk), tk)]

    qm, qa = [], []
    for m in range(2):
        qm.append(jnp.where(keep[m], q, zero))
        k_first = krefs[m][0, 0:min(ATTN_STAB_KEYS, s_len), :]
        mx = jnp.max(lax.dot_general(qm[m], k_first, dn, preferred_element_type=F32), axis=-1, keepdims=True)
        qa.append(jnp.where(stab[m], (-mx).astype(BF16), qm[m]))

    def fast(ci, acc):
        st = [lax.dot_general(kchunk(m, ci), qa[m], dn, preferred_element_type=F32) for m in range(2)]
        vt = vchunk(ci)
        return tuple(acc[m] + jnp.dot(vt, jnp.exp2(st[m]).astype(BF16), preferred_element_type=F32) for m in range(2))

    acc = lax.fori_loop(0, nchunks, fast, tuple(jnp.zeros((rows, tq), F32) for _ in range(2)))
    nonfinite = jnp.float32(0.0)
    for m in range(2):
        acc_ref[m] = acc[m]
        nonfinite = nonfinite + jnp.sum(jnp.where(jnp.isfinite(acc[m]), 0.0, 1.0))

    @pl.when(nonfinite > 0.0)
    def _():
        def slow(ci, carry):
            out = []
            vt = vchunk(ci)
            for m in range(2):
                m_i, a_i = carry[m]
                st = lax.dot_general(kchunk(m, ci), qm[m], dn, preferred_element_type=F32)
                m_new = jnp.maximum(m_i, jnp.max(st, axis=0, keepdims=True))
                p = jnp.exp2(st - m_new).astype(BF16)
                out.append((m_new, jnp.exp2(m_i - m_new) * a_i + jnp.dot(vt, p, preferred_element_type=F32)))
            return tuple(out)

        init = tuple((jnp.full((1, tq), -jnp.inf, F32), jnp.zeros((rows, tq), F32)) for _ in range(2))
        res = lax.fori_loop(0, nchunks, slow, init)
        for m in range(2):
            acc_ref[m] = res[m][1]

    o0 = acc_ref[0, 0:dv, :] / acc_ref[0, dv:dv + 1, :]
    o1 = acc_ref[1, 0:dv, :] / acc_ref[1, dv:dv + 1, :]
    if mode == "diff":
        lv = lam_ref[...]
        lam = (jnp.exp(jnp.sum(lv[0:1] * lv[1:2], axis=-1, keepdims=True))
               - jnp.exp(jnp.sum(lv[2:3] * lv[3:4], axis=-1, keepdims=True)) + lambda_init)
        ot = o0 - lam * o1
        ot = ot * lax.rsqrt(jnp.mean(ot * ot, axis=0, keepdims=True) + NORM_EPS) * nw_ref[...] * (1.0 - lambda_init)
    else:
        ot = jnp.concatenate([o0, o1], axis=0)
    o_ref[0] = ot.T.astype(o_ref.dtype)


def _attention(q, k_arr, vt_arr, lam_vecs, norm_w, *, mode, tq, tk, lambda_init=0.0):
    b, s, w = q.shape
    slabs = w // LANES
    if mode == "diff":
        rows = LANES + VT_PAD
        k_col = lambda p, m: 2 * p + m
        v_grp = lambda p: p
    else:
        rows = GQA_DH + VT_PAD
        k_col = lambda p, m: 2 * (p // 2) + m
        v_grp = lambda p: p // 2
    kspec = lambda m: pl.BlockSpec((1, s, LANES), lambda bi, p, i: (bi, 0, k_col(p, m)))
    return pl.pallas_call(
        functools.partial(_attn_kernel, mode=mode, tk=tk, lambda_init=lambda_init),
        out_shape=jax.ShapeDtypeStruct((b, s, w), BF16),
        grid=(b, slabs, s // tq),
        in_specs=[pl.BlockSpec((1, tq, LANES), lambda bi, p, i: (bi, i, p)),
                  kspec(0), kspec(1),
                  pl.BlockSpec((1, rows, s), lambda bi, p, i: (bi, v_grp(p), 0)),
                  pl.BlockSpec((4, DIFF_DQK), lambda bi, p, i: (0, 0)),
                  pl.BlockSpec((LANES, 1), lambda bi, p, i: (0, 0))],
        out_specs=pl.BlockSpec((1, tq, LANES), lambda bi, p, i: (bi, i, p)),
        scratch_shapes=[pltpu.VMEM((2, rows, tq), F32)],
        compiler_params=_cparams(("arbitrary", "arbitrary", "arbitrary")),
        name="attn_" + mode,
    )(q, k_arr, k_arr, vt_arr, lam_vecs, norm_w)


def _merge_kernel(of_ref, ob_ref, z_ref, g0_ref, g1_ref, g2_ref, yb_ref, yc_ref, x_ref,
                  wa_ref, wb_ref, wc_ref, wo_ref, gnw_ref, fnw_ref, rw_ref, rb_ref,
                  xo_ref, h_ref, id_ref, rwgt_ref):
    o = of_ref[...] + ob_ref[...]
    parts = []
    for h in range(GDN_HEADS):
        oh = o[:, h * LANES:(h + 1) * LANES]
        parts.append(oh * lax.rsqrt(jnp.mean(oh * oh, axis=-1, keepdims=True) + NORM_EPS) * gnw_ref[...])
    z = z_ref[...].astype(F32)
    ya = (jnp.concatenate(parts, axis=1) * (z * _sigmoid(z))).astype(BF16)
    merged = _sigmoid(g0_ref[...].astype(F32)) * jnp.dot(ya, wa_ref[...], preferred_element_type=F32)
    merged = merged + _sigmoid(g1_ref[...].astype(F32)) * jnp.dot(yb_ref[...], wb_ref[...], preferred_element_type=F32)
    merged = merged + _sigmoid(g2_ref[...].astype(F32)) * jnp.dot(yc_ref[...], wc_ref[...], preferred_element_type=F32)
    xn = x_ref[...] + jnp.dot(merged.astype(BF16), wo_ref[...], preferred_element_type=F32)
    xo_ref[...] = xn
    hf = xn * lax.rsqrt(jnp.mean(xn * xn, axis=-1, keepdims=True) + NORM_EPS) * fnw_ref[...]
    h_ref[...] = hf.astype(BF16)
    logits = _dot_split(hf, rw_ref[...]) + rb_ref[...]
    lane = lax.broadcasted_iota(jnp.int32, logits.shape, 1)
    big = jnp.int32(LANES)
    ninf = -jnp.inf
    glog = jnp.where(lane < N_GROUPS, logits, ninf)
    gmax = jnp.max(glog, axis=-1, keepdims=True)
    gidx = jnp.min(jnp.where(glog == gmax, lane, big), axis=-1, keepdims=True)
    gp = 1.0 / jnp.sum(jnp.exp(glog - gmax), axis=-1, keepdims=True)
    e = lane - N_GROUPS
    sel = (e >= 0) & (e < N_EXPERTS) & ((e // EXPERTS_PER_GROUP) == gidx)
    elog = jnp.where(sel, logits, ninf)
    m1 = jnp.max(elog, axis=-1, keepdims=True)
    i1 = jnp.min(jnp.where(elog == m1, lane, big), axis=-1, keepdims=True)
    elog2 = jnp.where(lane == i1, ninf, elog)
    m2 = jnp.max(elog2, axis=-1, keepdims=True)
    i2 = jnp.min(jnp.where(elog2 == m2, lane, big), axis=-1, keepdims=True)
    e2 = jnp.exp(m2 - m1)
    w1 = 1.0 / (1.0 + e2)
    w2 = e2 * w1
    id_ref[...] = jnp.where(lane == 0, i1 - N_GROUPS, jnp.where(lane == 1, i2 - N_GROUPS, 0))
    rwgt_ref[...] = jnp.where(lane == 0, gp * w1, jnp.where(lane == 1, gp * w2, 0.0))


def _merge(o_f, o_b, main2, yb, yc, x2d, wa, wb, wc, wo, gnw, fnw, rw, rb, *, tm):
    t, d = x2d.shape
    full = lambda shp: pl.BlockSpec(shp, lambda i: tuple(0 for _ in shp))
    return pl.pallas_call(
        _merge_kernel,
        out_shape=(jax.ShapeDtypeStruct((t, d), F32), jax.ShapeDtypeStruct((t, d), BF16),
                   jax.ShapeDtypeStruct((t, LANES), jnp.int32), jax.ShapeDtypeStruct((t, LANES), F32)),
        grid=(t // tm,),
        in_specs=[pl.BlockSpec((tm, 512), lambda i: (i, 0)),
                  pl.BlockSpec((tm, 512), lambda i: (i, 0)),
                  pl.BlockSpec((tm, 512), lambda i: (i, COL_Z // 512)),
                  pl.BlockSpec((tm, d), lambda i: (i, 0)),
                  pl.BlockSpec((tm, d), lambda i: (i, 1)),
                  pl.BlockSpec((tm, d), lambda i: (i, 2)),
                  pl.BlockSpec((tm, 512), lambda i: (i, 0)),
                  pl.BlockSpec((tm, 512), lambda i: (i, 0)),
                  pl.BlockSpec((tm, d), lambda i: (i, 0)),
                  full((512, d)), full((512, d)), full((512, d)), full((d, d)),
                  full((1, LANES)), full((1, d)), full((d, LANES)), full((1, LANES))],
        out_specs=(pl.BlockSpec((tm, d), lambda i: (i, 0)), pl.BlockSpec((tm, d), lambda i: (i, 0)),
                   pl.BlockSpec((tm, LANES), lambda i: (i, 0)), pl.BlockSpec((tm, LANES), lambda i: (i, 0))),
        compiler_params=_cparams(("arbitrary",)),
        name="merge_router",
    )(o_f, o_b, main2, main2, main2, main2, yb, yc, x2d, wa, wb, wc, wo, gnw, fnw, rw, rb)


def _expert_kernel(blk_e_ref, nused_ref, x_ref, w1_ref, w3_ref, w2_ref, o_ref, w1b_ref, w3b_ref, w2b_ref):
    i = pl.program_id(0)

    @pl.when((i == 0) | (blk_e_ref[i] != blk_e_ref[jnp.maximum(i - 1, 0)]))
    def _():
        w1b_ref[...] = w1_ref[0, 0].astype(BF16)
        w3b_ref[...] = w3_ref[0, 0].astype(BF16)
        w2b_ref[...] = w2_ref[0, 0].astype(BF16)

    @pl.when(i < nused_ref[0])
    def _():
        x = x_ref[...]
        a = jnp.dot(x, w1b_ref[...], preferred_element_type=F32)
        u = jnp.dot(x, w3b_ref[...], preferred_element_type=F32)
        hmid = (a * _sigmoid(a) * u).astype(BF16)
        o_ref[...] = jnp.dot(hmid, w2b_ref[...], preferred_element_type=F32).astype(o_ref.dtype)

    @pl.when(i >= nused_ref[0])
    def _():
        o_ref[...] = jnp.zeros_like(o_ref)


def _experts(blk_e, nused, xb, w1, w3, w2, *, layer):
    p_len, d = xb.shape
    ff = w1.shape[3]
    nblk = p_len // MOE_BLOCK
    return pl.pallas_call(
        _expert_kernel,
        out_shape=jax.ShapeDtypeStruct((p_len, d), BF16),
        grid_spec=pltpu.PrefetchScalarGridSpec(
            num_scalar_prefetch=2,
            grid=(nblk,),
            in_specs=[pl.BlockSpec((MOE_BLOCK, d), lambda i, be, nu: (i, 0)),
                      pl.BlockSpec((1, 1, d, ff), lambda i, be, nu: (layer, be[i], 0, 0)),
                      pl.BlockSpec((1, 1, d, ff), lambda i, be, nu: (layer, be[i], 0, 0)),
                      pl.BlockSpec((1, 1, ff, d), lambda i, be, nu: (layer, be[i], 0, 0))],
            out_specs=pl.BlockSpec((MOE_BLOCK, d), lambda i, be, nu: (i, 0)),
            scratch_shapes=[pltpu.VMEM((d, ff), BF16), pltpu.VMEM((d, ff), BF16), pltpu.VMEM((ff, d), BF16)],
        ),
        compiler_params=_cparams(("arbitrary",)),
        name="expert_mlp",
    )(blk_e, nused, xb, w1, w3, w2)


def _combine_kernel(x_ref, y0_ref, y1_ref, w_ref, nw_ref, o_ref, *, final):
    w = w_ref[...]
    x = x_ref[...] + w[:, 0:1] * y0_ref[...].astype(F32) + w[:, 1:2] * y1_ref[...].astype(F32)
    if final:
        x = x * lax.rsqrt(jnp.mean(x * x, axis=-1, keepdims=True) + NORM_EPS) * nw_ref[...]
    o_ref[...] = x


def _combine(x2d, y0, y1, wts, norm_w, *, final, tm):
    t, d = x2d.shape
    tile = pl.BlockSpec((tm, d), lambda i: (i, 0))
    return pl.pallas_call(
        functools.partial(_combine_kernel, final=final),
        out_shape=jax.ShapeDtypeStruct((t, d), F32),
        grid=(t // tm,),
        in_specs=[tile, tile, tile, pl.BlockSpec((tm, LANES), lambda i: (i, 0)), pl.BlockSpec((1, d), lambda i: (0, 0))],
        out_specs=tile,
        compiler_params=_cparams(("arbitrary",)),
        name="moe_combine",
    )(x2d, y0, y1, wts, norm_w.reshape(1, d))


def _rope_tables(pos, dim):
    inv = 1.0 / (ROPE_THETA ** (jnp.arange(0, dim, 2, dtype=F32) / dim))
    ang = pos.astype(F32)[:, None] * inv[None, :]
    ang = jnp.concatenate([ang, ang], axis=-1)
    return jnp.cos(ang), jnp.sin(ang)


def _signed_sin(sin):
    half = sin.shape[-1] // 2
    return jnp.concatenate([-sin[:, :half], sin[:, half:]], axis=-1)


def _layout_w_in(w):
    o = 0
    parts = {}
    for name, size in (("qkv", 1536), ("z", 512), ("b", 8), ("a", 8), ("dq", 512), ("dk", 512), ("dv", 512),
                       ("cq", 512), ("ck", 128), ("cv", 128), ("gate", 3072)):
        parts[name] = w[:, o:o + size]
        o += size
    swap = lambda m: jnp.concatenate([m[:, 64:], m[:, :64]], axis=1)
    main = jnp.concatenate([parts["gate"], parts["qkv"], parts["z"], parts["dq"], parts["dk"], parts["dv"],
                            parts["cq"], parts["ck"], swap(parts["ck"]), parts["cv"], swap(parts["cv"])], axis=1)
    ba = jnp.concatenate([parts["b"], parts["a"], jnp.zeros((w.shape[0], LANES - 16), w.dtype)], axis=1)
    return main.astype(BF16), ba


def _rows_layout(t, bsz, s):
    nc = s // GDN_CHUNK
    t = t.reshape(bsz, nc, GDN_CHUNK, 2, GDN_HEADS)
    return jnp.transpose(t, (3, 0, 1, 4, 2)).reshape(2, bsz, nc, GDN_ROWS)


def _moe_dispatch(ids, t):
    a = t * TOPK
    p_len = ((a + N_EXPERTS * (MOE_BLOCK - 1) + MOE_BLOCK - 1) // MOE_BLOCK) * MOE_BLOCK
    n_blocks = p_len // MOE_BLOCK
    flat_e = ids.reshape(-1)
    iota_a = jnp.arange(a, dtype=jnp.int32)
    skey = jnp.sort(flat_e * a + iota_a)
    order = skey % a
    se = skey // a
    experts = jnp.arange(N_EXPERTS, dtype=jnp.int32)
    counts = jnp.sum((flat_e[:, None] == experts[None, :]).astype(jnp.int32), axis=0)
    start = jnp.cumsum(counts) - counts
    pcounts = ((counts + MOE_BLOCK - 1) // MOE_BLOCK) * MOE_BLOCK
    pend = jnp.cumsum(pcounts)
    pstart = pend - pcounts
    dest_sorted = pstart[se] + (iota_a - start[se])
    blk_first = jnp.arange(n_blocks, dtype=jnp.int32) * MOE_BLOCK
    blk_e = jnp.minimum(jnp.sum((pend[None, :] <= blk_first[:, None]).astype(jnp.int32), axis=1), N_EXPERTS - 1)
    row = jnp.arange(p_len, dtype=jnp.int32)
    row_e = jnp.repeat(blk_e, MOE_BLOCK)
    j = row - pstart[row_e]
    valid = j < counts[row_e]
    tok_buf = jnp.where(valid, order[jnp.minimum(start[row_e] + j, a - 1)] // TOPK, row % t)
    _, dest = lax.sort((order, dest_sorted), num_keys=1)
    nused = (pend[-1] // MOE_BLOCK).astype(jnp.int32).reshape(1)
    return tok_buf, dest.reshape(t, TOPK), blk_e, nused


def kernel(x, attn_norm_w, w_in, gdn_conv_w, gdn_a_log, gdn_dt_bias, gdn_norm_w, diff_lambda, diff_norm_w,
           gqa_q_norm_w, gqa_k_norm_w, w_branch_a, w_branch_b, w_branch_c, w_out, ffn_norm_w,
           router_group_w, router_group_b, router_expert_w, router_expert_b,
           expert_w_gate, expert_w_up, expert_w_down, final_norm_w):
    bsz, s, d = x.shape
    t = bsz * s
    depth = w_in.shape[0]
    tm = min(512, t)
    ts = min(512, s)

    rows = s // GRID_W
    row = jnp.broadcast_to(jnp.arange(rows)[:, None], (rows, GRID_W)).reshape(s)
    col = jnp.broadcast_to(jnp.arange(GRID_W)[None, :], (rows, GRID_W)).reshape(s)
    c1, s1 = _rope_tables(jnp.arange(s), DIFF_DQK)
    cr, sr = _rope_tables(row, GQA_DH // 2)
    cc, sc = _rope_tables(col, GQA_DH // 2)
    cos1 = jnp.tile(c1, (1, 2))
    sin1 = jnp.tile(_signed_sin(s1), (1, 2))
    cos2 = jnp.tile(jnp.concatenate([cr, cc], axis=-1), (1, 2))
    sin2 = jnp.tile(jnp.concatenate([_signed_sin(sr), _signed_sin(sc)], axis=-1), (1, 2))

    x2 = x.reshape(t, d)
    for l in range(depth):
        lambda_init = 0.8 - 0.6 * math.exp(-0.3 * l)
        w_main, w_ba = _layout_w_in(w_in[l])
        main2 = _norm_proj(x2, attn_norm_w[l], w_main, BF16, exact=False, tm=tm, tn=N_MAIN // 2)
        ba = _norm_proj(x2, attn_norm_w[l], w_ba, F32, exact=True, tm=tm, tn=LANES)
        main3 = main2.reshape(bsz, s, N_MAIN)

        conv_w = jnp.concatenate([gdn_conv_w[l], jnp.zeros((8 - GDN_CONV, gdn_conv_w.shape[2]), F32)], axis=0)
        qnw = jnp.tile(gqa_q_norm_w[l], 2).reshape(1, LANES)
        knw = jnp.tile(gqa_k_norm_w[l], 2).reshape(1, LANES)
        gq, gk, gv, dq, dk, dv, cq, ck, cv = _prep(main3, conv_w, cos1, sin1, cos2, sin2, qnw, knw, ts=ts)

        b_rows = _rows_layout(ba[:, 0:8], bsz, s)
        a_rows = _rows_layout(ba[:, 8:16], bsz, s)
        alog_row = jnp.repeat(gdn_a_log[l], GDN_CHUNK, axis=1).reshape(2, 1, GDN_ROWS)
        dtb_row = jnp.repeat(gdn_dt_bias[l], GDN_CHUNK, axis=1).reshape(2, 1, GDN_ROWS)
        o_f, o_b = _gdn(a_rows, b_rows, alog_row, dtb_row, gq, gk, gv, nbatch=GDN_NBATCH if bsz % GDN_NBATCH == 0 else 1)

        nw_diff = diff_norm_w[l].reshape(LANES, 1)
        yb = _attention(dq, dk, dv, diff_lambda[l], nw_diff, mode="diff",
                        tq=min(ATTN_TQ, s), tk=min(ATTN_TK, s), lambda_init=lambda_init)
        yc = _attention(cq, ck, cv, diff_lambda[l], nw_diff, mode="gqa",
                        tq=min(ATTN_TQ, s), tk=min(ATTN_TK, s))

        rw = jnp.concatenate([router_group_w[l], router_expert_w[l],
                              jnp.zeros((d, LANES - N_GROUPS - N_EXPERTS), F32)], axis=1)
        rb = jnp.concatenate([router_group_b[l], router_expert_b[l],
                              jnp.zeros((LANES - N_GROUPS - N_EXPERTS,), F32)]).reshape(1, LANES)
        x2, h2, ids, wts = _merge(o_f.reshape(t, 512), o_b.reshape(t, 512), main2, yb.reshape(t, 512), yc.reshape(t, 512), x2,
                                  w_branch_a[l].astype(BF16), w_branch_b[l].astype(BF16),
                                  w_branch_c[l].astype(BF16), w_out[l].astype(BF16),
                                  gdn_norm_w[l].reshape(1, LANES), ffn_norm_w[l].reshape(1, d), rw, rb, tm=tm)

        tok_buf, dest, blk_e, nused = _moe_dispatch(ids[:, :TOPK], t)
        yblk = _experts(blk_e, nused, h2[tok_buf], expert_w_gate, expert_w_up, expert_w_down, layer=l)
        x2 = _combine(x2, yblk[dest[:, 0]], yblk[dest[:, 1]], wts, final_norm_w, final=(l == depth - 1), tm=tm)

    return x2.reshape(bsz, s, d)
```

```python
import functools
import math

import jax
import jax.numpy as jnp
from jax import lax
from jax.experimental import pallas as pl
from jax.experimental.pallas import tpu as pltpu

GRID_W = 64
ROPE_THETA = 10000.0
NORM_EPS = 1e-6
GDN_HEADS = 4
GDN_DK = 128
GDN_DV = 128
GDN_CONV = 5
GDN_CHUNK = 64
DIFF_HEADS = 4
DIFF_DQK = 64
GQA_HEADS = 8
GQA_KV = 2
GQA_DH = 64
N_GROUPS = 4
EXPERTS_PER_GROUP = 8
N_EXPERTS = N_GROUPS * EXPERTS_PER_GROUP
TOPK = 2
MOE_BLOCK = 256

LANES = 128
VMEM_LIMIT = 56 * 1024 * 1024

COL_GATE = 0
COL_QKV = 3072
COL_Z = 4608
COL_DQ = 5120
COL_DK = 5632
COL_DV = 6144
COL_CQ = 6656
COL_CK = 7168
COL_CV = 7424
N_MAIN = 7680

LOG2E = math.log2(math.e)
ATTN_TQ = 2048
ATTN_TK = 512
ATTN_STAB_KEYS = 256
VT_PAD = 16

HI = lax.Precision.HIGHEST
F32 = jnp.float32
BF16 = jnp.bfloat16


def _cparams(sem):
    return pltpu.CompilerParams(dimension_semantics=sem, vmem_limit_bytes=VMEM_LIMIT)


def _sigmoid(x):
    return 1.0 / (1.0 + jnp.exp(-x))


def _dot_split(a, w):
    a_hi = a.astype(BF16)
    a_lo = (a - a_hi.astype(F32)).astype(BF16)
    w_hi = w.astype(BF16)
    w_lo = (w - w_hi.astype(F32)).astype(BF16)
    return (jnp.dot(a_hi, w_hi, preferred_element_type=F32) + jnp.dot(a_lo, w_hi, preferred_element_type=F32)
            + jnp.dot(a_hi, w_lo, preferred_element_type=F32))


def _norm_proj_kernel(x_ref, nw_ref, w_ref, o_ref, *, exact):
    x = x_ref[...]
    h = x * lax.rsqrt(jnp.mean(x * x, axis=-1, keepdims=True) + NORM_EPS) * nw_ref[...]
    if exact:
        o_ref[...] = _dot_split(h, w_ref[...]).astype(o_ref.dtype)
    else:
        o_ref[...] = jnp.dot(h.astype(BF16), w_ref[...], preferred_element_type=F32).astype(o_ref.dtype)


def _norm_proj(x2d, norm_w, w, out_dtype, *, exact, tm, tn):
    t, d = x2d.shape
    n = w.shape[1]
    return pl.pallas_call(
        functools.partial(_norm_proj_kernel, exact=exact),
        out_shape=jax.ShapeDtypeStruct((t, n), out_dtype),
        grid=(n // tn, t // tm),
        in_specs=[pl.BlockSpec((tm, d), lambda j, i: (i, 0)),
                  pl.BlockSpec((1, d), lambda j, i: (0, 0)),
                  pl.BlockSpec((d, tn), lambda j, i: (0, j))],
        out_specs=pl.BlockSpec((tm, tn), lambda j, i: (i, j)),
        compiler_params=_cparams(("arbitrary", "arbitrary")),
        name="norm_proj_exact" if exact else "norm_proj",
    )(x2d, norm_w.reshape(1, d), w)


HALO = 16


def _rot_half(x, half):
    lane = lax.broadcasted_iota(jnp.int32, x.shape, 1)
    first = (lane % (2 * half)) < half
    return jnp.where(first, pltpu.roll(x, LANES - half, 1), pltpu.roll(x, half, 1))


def _group_sumsq(x, width):
    x2 = x * x
    if width == LANES:
        return jnp.sum(x2, axis=-1, keepdims=True)
    lane = lax.broadcasted_iota(jnp.int32, x.shape, 1)
    lo = lane < width
    s_lo = jnp.sum(jnp.where(lo, x2, 0.0), axis=-1, keepdims=True)
    s_hi = jnp.sum(jnp.where(lo, 0.0, x2), axis=-1, keepdims=True)
    return jnp.where(lo, s_lo, s_hi)


def _aug_slab(x, m):
    lane = lax.broadcasted_iota(jnp.int32, x.shape, 1)
    half = LANES // 2
    keep = (lane < half) if m == 0 else (lane >= half)
    one = jnp.where(lane == (1 - m) * half, 1.0, 0.0).astype(x.dtype)
    return jnp.where(keep, x, one)


def _prep_kernel(qkv_ref, prev_ref, next_ref, dq_ref, dk_ref, dv_ref, cq_ref, ck_ref, cv_ref,
                 convw_ref, cos1_ref, sin1_ref, cos2_ref, sin2_ref, qnw_ref, knw_ref,
                 gq_ref, gk_ref, gv_ref, dqo_ref, dko_ref, dvo_ref, cqo_ref, cko_ref, cvo_ref, *, ts):
    i = pl.program_id(1)
    n = pl.num_programs(1)
    cur = qkv_ref[0].astype(F32)
    prev = jnp.where(i > 0, prev_ref[0].astype(F32), 0.0)
    nxt = jnp.where(i < n - 1, next_ref[0].astype(F32), 0.0)
    ext = jnp.concatenate([prev, cur, nxt], axis=0)
    pad = GDN_CONV // 2
    acc = jnp.zeros_like(cur)
    for j in range(GDN_CONV):
        off = HALO - pad + j
        acc = acc + ext[off:off + ts, :] * convw_ref[j:j + 1, :]
    act = acc * _sigmoid(acc)
    nqk = GDN_HEADS * GDN_DK
    for h in range(GDN_HEADS):
        sl = slice(h * GDN_DK, (h + 1) * GDN_DK)
        qh = act[:, sl]
        gq_ref[0, :, sl] = (qh * lax.rsqrt(_group_sumsq(qh, LANES) + NORM_EPS) * (GDN_DK ** -0.5)).astype(BF16)
        kh = act[:, nqk + h * GDN_DK: nqk + (h + 1) * GDN_DK]
        gk_ref[0, :, sl] = (kh * lax.rsqrt(_group_sumsq(kh, LANES) + NORM_EPS)).astype(BF16)
    gv_ref[0] = act[:, 2 * nqk:].astype(BF16)
    cos1, sin1 = cos1_ref[...], sin1_ref[...]
    ones_rows = jnp.where(lax.broadcasted_iota(jnp.int32, (VT_PAD, ts), 0) == 0, 1.0, 0.0).astype(BF16)
    for p in range(DIFF_HEADS):
        sl = slice(p * LANES, (p + 1) * LANES)
        xq = dq_ref[0, :, sl].astype(F32)
        dqo_ref[0, :, sl] = ((xq * cos1 + _rot_half(xq, DIFF_DQK // 2) * sin1) * (DIFF_DQK ** -0.5 * LOG2E)).astype(BF16)
        xk = dk_ref[0, :, sl].astype(F32)
        xk = (xk * cos1 + _rot_half(xk, DIFF_DQK // 2) * sin1).astype(BF16)
        for m in range(2):
            dko_ref[0, :, (2 * p + m) * LANES:(2 * p + m + 1) * LANES] = _aug_slab(xk, m)
        r0 = p * (LANES + VT_PAD)
        dvo_ref[0, r0:r0 + LANES, :] = dv_ref[0, :, sl].astype(F32).T.astype(BF16)
        dvo_ref[0, r0 + LANES:r0 + LANES + VT_PAD, :] = ones_rows
    cos2, sin2 = cos2_ref[...], sin2_ref[...]
    for p in range(GQA_HEADS * GQA_DH // LANES):
        sl = slice(p * LANES, (p + 1) * LANES)
        xq = cq_ref[0, :, sl].astype(F32)
        xq = xq * lax.rsqrt(_group_sumsq(xq, GQA_DH) * (1.0 / GQA_DH) + NORM_EPS) * qnw_ref[...]
        cqo_ref[0, :, sl] = ((xq * cos2 + _rot_half(xq, GQA_DH // 4) * sin2) * (GQA_DH ** -0.5 * LOG2E)).astype(BF16)
    for p in range(2):
        sl = slice(p * LANES, (p + 1) * LANES)
        xk = ck_ref[0, :, sl].astype(F32)
        xk = xk * lax.rsqrt(_group_sumsq(xk, GQA_DH) * (1.0 / GQA_DH) + NORM_EPS) * knw_ref[...]
        xk = (xk * cos2 + _rot_half(xk, GQA_DH // 4) * sin2).astype(BF16)
        for m in range(2):
            c = p if m == 0 else 1 - p
            cko_ref[0, :, (2 * c + m) * LANES:(2 * c + m + 1) * LANES] = _aug_slab(xk, m)
    vt = cv_ref[0].astype(F32).T.astype(BF16)
    for c in range(GQA_KV):
        r0 = c * (GQA_DH + VT_PAD)
        cvo_ref[0, r0:r0 + GQA_DH, :] = vt[c * GQA_DH:(c + 1) * GQA_DH]
        cvo_ref[0, r0 + GQA_DH:r0 + GQA_DH + VT_PAD, :] = ones_rows


def _prep(main3, conv_w, cos1, sin1, cos2, sin2, qnw, knw, *, ts):
    b, s, _ = main3.shape
    nt = s // ts
    hb = ts // HALO
    last = s // HALO - 1
    row = lambda w: pl.BlockSpec((1, w), lambda bi, i: (0, 0))
    tab = pl.BlockSpec((ts, LANES), lambda bi, i: (i, 0))
    col = lambda w, off: pl.BlockSpec((1, ts, w), lambda bi, i: (bi, i, off // w))
    out = lambda w: pl.BlockSpec((1, ts, w), lambda bi, i: (bi, i, 0))
    outs = [("tok", 512), ("tok", 512), ("tok", 512), ("tok", 512), ("tok", 1024),
            ("rows", DIFF_HEADS * (LANES + VT_PAD)), ("tok", 512), ("tok", 512), ("rows", GQA_KV * (GQA_DH + VT_PAD))]
    specs = tuple(out(w) if kind == "tok" else pl.BlockSpec((1, w, ts), lambda bi, i: (bi, 0, i)) for kind, w in outs)
    shapes = tuple(jax.ShapeDtypeStruct((b, s, w) if kind == "tok" else (b, w, s), BF16) for kind, w in outs)
    return pl.pallas_call(
        functools.partial(_prep_kernel, ts=ts),
        out_shape=shapes,
        grid=(b, nt),
        in_specs=[
            col(1536, COL_QKV),
            pl.BlockSpec((1, HALO, 1536), lambda bi, i: (bi, jnp.maximum(i * hb - 1, 0), COL_QKV // 1536)),
            pl.BlockSpec((1, HALO, 1536), lambda bi, i: (bi, jnp.minimum((i + 1) * hb, last), COL_QKV // 1536)),
            col(512, COL_DQ), col(512, COL_DK), col(512, COL_DV), col(512, COL_CQ), col(256, COL_CK), col(128, COL_CV),
            pl.BlockSpec((8, 1536), lambda bi, i: (0, 0)),
            tab, tab, tab, tab, row(LANES), row(LANES),
        ],
        out_specs=specs,
        compiler_params=_cparams(("arbitrary", "arbitrary")),
        name="mixer_prep",
    )(*([main3] * 9), conv_w, cos1, sin1, cos2, sin2, qnw, knw)


GDN_G = 8
GDN_NBATCH = 2
GDN_ROWS = GDN_HEADS * GDN_CHUNK


def _stack_heads(x):
    return jnp.concatenate([x[:, h * LANES:(h + 1) * LANES] for h in range(GDN_HEADS)], axis=0)


def _row_to_col(row, eye):
    return jnp.sum(jnp.where(eye, row, 0.0), axis=1, keepdims=True)


def _gdn_kernel(af_ref, ab_ref, bf_ref, bb_ref, alog_ref, dtb_ref, qf_ref, kf_ref, vf_ref, qb_ref, kb_ref, vb_ref,
                of_ref, ob_ref, state_ref, gc_ref, gt_ref, beta_ref, *, nbatch):
    blk = pl.program_id(1)
    n = GDN_ROWS
    c = GDN_CHUNK

    @pl.when(blk == 0)
    def _():
        state_ref[...] = jnp.zeros_like(state_ref)

    ri = lax.broadcasted_iota(jnp.int32, (n, n), 0)
    ci = lax.broadcasted_iota(jnp.int32, (n, n), 1)
    same = (ri // c) == (ci // c)
    eye = ri == ci
    ti = lax.broadcasted_iota(jnp.int32, (n, GDN_HEADS * LANES), 0)
    tj = lax.broadcasted_iota(jnp.int32, (n, GDN_HEADS * LANES), 1)
    tot_m = jnp.where((ti // c) == (tj // LANES), 1.0, 0.0)

    chains = []
    for d, (a_ref, b_ref, q_ref, k_ref, v_ref, o_ref) in enumerate(
            ((af_ref, bf_ref, qf_ref, kf_ref, vf_ref, of_ref), (ab_ref, bb_ref, qb_ref, kb_ref, vb_ref, ob_ref))):
        sgn = 1 - 2 * d
        after = same & ((ri - ci) * sgn > 0)
        incl = same & ((ri - ci) * sgn >= 0)
        cum_m = jnp.where(same & ((ci - ri) * sgn >= 0), 1.0, 0.0)
        for bi in range(nbatch):
            ch = d * nbatch + bi
            x = a_ref[0, bi] + dtb_ref[d]
            softplus = jnp.maximum(x, 0.0) + jnp.log(1.0 + jnp.exp(-jnp.abs(x)))
            g = -jnp.exp(alog_ref[d]) * softplus
            beta_ref[ch] = _sigmoid(b_ref[0, bi])
            gc_ref[ch] = jnp.dot(g, cum_m, precision=HI, preferred_element_type=F32)
            gt_ref[ch] = jnp.dot(g, tot_m, precision=HI, preferred_element_type=F32)
            chains.append((ch, d, bi, after, incl, q_ref, k_ref, v_ref, o_ref))

    def chunk(j, chain):
        ch, d, bi, after, incl, q_ref, k_ref, v_ref, o_ref = chain
        cc = j if d == 0 else GDN_G - 1 - j
        r0 = pl.multiple_of(cc * c, c)
        gc_row = gc_ref[ch, pl.ds(cc, 1), :]
        beta_row = beta_ref[ch, pl.ds(cc, 1), :]
        gt_row = gt_ref[ch, pl.ds(cc, 1), :]
        gc_col = _row_to_col(gc_row, eye)
        beta_col = _row_to_col(beta_row, eye)
        k_st = _stack_heads(k_ref[bi, pl.ds(r0, c), :]).astype(F32)
        q_st = _stack_heads(q_ref[bi, pl.ds(r0, c), :]).astype(F32)
        v_st = _stack_heads(v_ref[bi, pl.ds(r0, c), :]).astype(F32)
        egc = jnp.exp(gc_col)
        decay = jnp.exp(jnp.minimum(gc_col - gc_row, 0.0))
        kb = k_st * beta_col
        k_bf = k_st.astype(BF16)
        kk = lax.dot_general(kb.astype(BF16), k_bf, (((1,), (1,)), ((), ())), preferred_element_type=F32)
        qk = lax.dot_general(q_st.astype(BF16), k_bf, (((1,), (1,)), ((), ())), preferred_element_type=F32)
        yield
        neg_a = jnp.where(after, -(kk * decay), 0.0)
        t_m = jnp.where(eye, 1.0, 0.0) + neg_a
        p_m = neg_a
        for _ in range(int(math.log2(c)) - 1):
            p_bf = p_m.astype(BF16)
            p_m = jnp.dot(p_bf, p_bf, preferred_element_type=F32)
            yield
            t_m = t_m + jnp.dot(t_m.astype(BF16), p_m.astype(BF16), preferred_element_type=F32)
            yield
        rhs = jnp.concatenate([v_st * beta_col, kb * egc], axis=1).astype(BF16)
        sol = jnp.dot(t_m.astype(BF16), rhs, preferred_element_type=F32)
        yield
        u_st, w_st = sol[:, :LANES], sol[:, LANES:]
        intra = jnp.where(incl, qk * decay, 0.0).astype(BF16)
        q_dec = (q_st * egc).astype(BF16)
        vn, oq = [], []
        w_bf = w_st.astype(BF16)
        for h in range(GDN_HEADS):
            rs = slice(h * c, (h + 1) * c)
            s_h = state_ref[ch * GDN_HEADS + h].astype(BF16)
            ws_qs = jnp.dot(jnp.concatenate([w_bf[rs], q_dec[rs]], axis=0), s_h, preferred_element_type=F32)
            vn.append(u_st[rs] - ws_qs[:c])
            oq.append(ws_qs[c:])
        yield
        vn_st = jnp.concatenate(vn, axis=0)
        o_st = jnp.concatenate(oq, axis=0) + jnp.dot(intra, vn_st.astype(BF16), preferred_element_type=F32)
        for h in range(GDN_HEADS):
            rs = slice(h * c, (h + 1) * c)
            gt_h = gt_row[:, h * LANES:(h + 1) * LANES]
            k_dec = (k_st[rs] * jnp.exp(gt_h[:, :1] - gc_col[rs])).astype(BF16)
            upd = lax.dot_general(k_dec, vn[h].astype(BF16), (((0,), (0,)), ((), ())), preferred_element_type=F32)
            state_ref[ch * GDN_HEADS + h] = state_ref[ch * GDN_HEADS + h] * jnp.exp(gt_h) + upd
            o_ref[bi, pl.ds(r0, c), h * LANES:(h + 1) * LANES] = o_st[rs]

    def step(j, carry):
        active = [chunk(j, chain) for chain in chains]
        while active:
            active = [g for g in active if next(g, active) is not active]
        return carry

    lax.fori_loop(0, GDN_G, step, 0)


def _gdn(a_rows, b_rows, alog_row, dtb_row, gq, gk, gv, *, nbatch):
    b, s, _ = gq.shape
    nb = s // (GDN_G * GDN_CHUNK)
    ts = GDN_G * GDN_CHUNK
    nchain = 2 * nbatch
    tok_f = pl.BlockSpec((nbatch, ts, 512), lambda bi, i: (bi, i, 0))
    tok_b = pl.BlockSpec((nbatch, ts, 512), lambda bi, i: (bi, nb - 1 - i, 0))
    rows_f = pl.BlockSpec((1, nbatch, GDN_G, GDN_ROWS), lambda bi, i: (0, bi, i, 0))
    rows_b = pl.BlockSpec((1, nbatch, GDN_G, GDN_ROWS), lambda bi, i: (1, bi, nb - 1 - i, 0))
    par = pl.BlockSpec((2, 1, GDN_ROWS), lambda bi, i: (0, 0, 0))
    return pl.pallas_call(
        functools.partial(_gdn_kernel, nbatch=nbatch),
        out_shape=(jax.ShapeDtypeStruct((b, s, 512), F32), jax.ShapeDtypeStruct((b, s, 512), F32)),
        grid=(b // nbatch, nb),
        in_specs=[rows_f, rows_b, rows_f, rows_b, par, par, tok_f, tok_f, tok_f, tok_b, tok_b, tok_b],
        out_specs=(tok_f, tok_b),
        scratch_shapes=[pltpu.VMEM((nchain * GDN_HEADS, GDN_DK, GDN_DV), F32),
                        pltpu.VMEM((nchain, GDN_G, GDN_ROWS), F32),
                        pltpu.VMEM((nchain, GDN_G, GDN_HEADS * LANES), F32),
                        pltpu.VMEM((nchain, GDN_G, GDN_ROWS), F32)],
        compiler_params=_cparams(("arbitrary", "arbitrary")),
        name="gdn_chunked",
    )(a_rows, a_rows, b_rows, b_rows, alog_row, dtb_row, gq, gk, gv, gq, gk, gv)


def _attn_kernel(q_ref, k0_ref, k1_ref, vt_ref, lam_ref, nw_ref, o_ref, acc_ref, *, mode, tk, lambda_init):
    s_len = k0_ref.shape[1]
    tq = q_ref.shape[1]
    rows = vt_ref.shape[1]
    dv = rows - VT_PAD
    half = LANES // 2
    q = q_ref[0]
    lane = lax.broadcasted_iota(jnp.int32, q.shape, 1)
    keep = (lane < half, lane >= half)
    stab = (lane == half, lane == 0)
    zero = jnp.zeros_like(q)
    krefs = (k0_ref, k1_ref)
    nchunks = s_len // tk
    dn = (((1,), (1,)), ((), ()))

    def kchunk(m, ci):
        return krefs[m][0, pl.ds(pl.multiple_of(ci * tk, tk), tk), :]

    def vchunk(ci):
        return vt_ref[0, :, pl.ds(pl.multiple_of(ci * tk, tk), tk)]

    qm, qa = [], []
    for m in range(2):
        qm.append(jnp.where(keep[m], q, zero))
        k_first = krefs[m][0, 0:min(ATTN_STAB_KEYS, s_len), :]
        mx = jnp.max(lax.dot_general(qm[m], k_first, dn, preferred_element_type=F32), axis=-1, keepdims=True)
        qa.append(jnp.where(stab[m], (-mx).astype(BF16), qm[m]))

    def fast(ci, acc):
        st = [lax.dot_general(kchunk(m, ci), qa[m], dn, preferred_element_type=F32) for m in range(2)]
        vt = vchunk(ci)
        return tuple(acc[m] + jnp.dot(vt, jnp.exp2(st[m]).astype(BF16), preferred_element_type=F32) for m in range(2))

    acc = lax.fori_loop(0, nchunks, fast, tuple(jnp.zeros((rows, tq), F32) for _ in range(2)))
    nonfinite = jnp.float32(0.0)
    for m in range(2):
        acc_ref[m] = acc[m]
        nonfinite = nonfinite + jnp.sum(jnp.where(jnp.isfinite(acc[m]), 0.0, 1.0))

    @pl.when(nonfinite > 0.0)
    def _():
        def slow(ci, carry):
            out = []
            vt = vchunk(ci)
            for m in range(2):
                m_i, a_i = carry[m]
                st = lax.dot_general(kchunk(m, ci), qm[m], dn, preferred_element_type=F32)
                m_new = jnp.maximum(m_i, jnp.max(st, axis=0, keepdims=True))
                p = jnp.exp2(st - m_new).astype(BF16)
                out.append((m_new, jnp.exp2(m_i - m_new) * a_i + jnp.dot(vt, p, preferred_element_type=F32)))
            return tuple(out)

        init = tuple((jnp.full((1, tq), -jnp.inf, F32), jnp.zeros((rows, tq), F32)) for _ in range(2))
        res = lax.fori_loop(0, nchunks, slow, init)
        for m in range(2):
            acc_ref[m] = res[m][1]

    o0 = acc_ref[0, 0:dv, :] / acc_ref[0, dv:dv + 1, :]
    o1 = acc_ref[1, 0:dv, :] / acc_ref[1, dv:dv + 1, :]
    if mode == "diff":
        lv = lam_ref[...]
        lam = (jnp.exp(jnp.sum(lv[0:1] * lv[1:2], axis=-1, keepdims=True))
               - jnp.exp(jnp.sum(lv[2:3] * lv[3:4], axis=-1, keepdims=True)) + lambda_init)
        ot = o0 - lam * o1
        ot = ot * lax.rsqrt(jnp.mean(ot * ot, axis=0, keepdims=True) + NORM_EPS) * nw_ref[...] * (1.0 - lambda_init)
    else:
        ot = jnp.concatenate([o0, o1], axis=0)
    o_ref[0] = ot.T.astype(o_ref.dtype)


def _attention(q, k_arr, vt_arr, lam_vecs, norm_w, *, mode, tq, tk, lambda_init=0.0):
    b, s, w = q.shape
    slabs = w // LANES
    if mode == "diff":
        rows = LANES + VT_PAD
        k_col = lambda p, m: 2 * p + m
        v_grp = lambda p: p
    else:
        rows = GQA_DH + VT_PAD
        k_col = lambda p, m: 2 * (p // 2) + m
        v_grp = lambda p: p // 2
    kspec = lambda m: pl.BlockSpec((1, s, LANES), lambda bi, p, i: (bi, 0, k_col(p, m)))
    return pl.pallas_call(
        functools.partial(_attn_kernel, mode=mode, tk=tk, lambda_init=lambda_init),
        out_shape=jax.ShapeDtypeStruct((b, s, w), BF16),
        grid=(b, slabs, s // tq),
        in_specs=[pl.BlockSpec((1, tq, LANES), lambda bi, p, i: (bi, i, p)),
                  kspec(0), kspec(1),
                  pl.BlockSpec((1, rows, s), lambda bi, p, i: (bi, v_grp(p), 0)),
                  pl.BlockSpec((4, DIFF_DQK), lambda bi, p, i: (0, 0)),
                  pl.BlockSpec((LANES, 1), lambda bi, p, i: (0, 0))],
        out_specs=pl.BlockSpec((1, tq, LANES), lambda bi, p, i: (bi, i, p)),
        scratch_shapes=[pltpu.VMEM((2, rows, tq), F32)],
        compiler_params=_cparams(("arbitrary", "arbitrary", "arbitrary")),
        name="attn_" + mode,
    )(q, k_arr, k_arr, vt_arr, lam_vecs, norm_w)


def _merge_kernel(of_ref, ob_ref, z_ref, g0_ref, g1_ref, g2_ref, yb_ref, yc_ref, x_ref,
                  wa_ref, wb_ref, wc_ref, wo_ref, gnw_ref, fnw_ref, rw_ref, rb_ref,
                  xo_ref, h_ref, id_ref, rwgt_ref):
    o = of_ref[...] + ob_ref[...]
    parts = []
    for h in range(GDN_HEADS):
        oh = o[:, h * LANES:(h + 1) * LANES]
        parts.append(oh * lax.rsqrt(jnp.mean(oh * oh, axis=-1, keepdims=True) + NORM_EPS) * gnw_ref[...])
    z = z_ref[...].astype(F32)
    ya = (jnp.concatenate(parts, axis=1) * (z * _sigmoid(z))).astype(BF16)
    merged = _sigmoid(g0_ref[...].astype(F32)) * jnp.dot(ya, wa_ref[...], preferred_element_type=F32)
    merged = merged + _sigmoid(g1_ref[...].astype(F32)) * jnp.dot(yb_ref[...], wb_ref[...], preferred_element_type=F32)
    merged = merged + _sigmoid(g2_ref[...].astype(F32)) * jnp.dot(yc_ref[...], wc_ref[...], preferred_element_type=F32)
    xn = x_ref[...] + jnp.dot(merged.astype(BF16), wo_ref[...], preferred_element_type=F32)
    xo_ref[...] = xn
    hf = xn * lax.rsqrt(jnp.mean(xn * xn, axis=-1, keepdims=True) + NORM_EPS) * fnw_ref[...]
    h_ref[...] = hf.astype(BF16)
    logits = _dot_split(hf, rw_ref[...]) + rb_ref[...]
    lane = lax.broadcasted_iota(jnp.int32, logits.shape, 1)
    big = jnp.int32(LANES)
    ninf = -jnp.inf
    glog = jnp.where(lane < N_GROUPS, logits, ninf)
    gmax = jnp.max(glog, axis=-1, keepdims=True)
    gidx = jnp.min(jnp.where(glog == gmax, lane, big), axis=-1, keepdims=True)
    gp = 1.0 / jnp.sum(jnp.exp(glog - gmax), axis=-1, keepdims=True)
    e = lane - N_GROUPS
    sel = (e >= 0) & (e < N_EXPERTS) & ((e // EXPERTS_PER_GROUP) == gidx)
    elog = jnp.where(sel, logits, ninf)
    m1 = jnp.max(elog, axis=-1, keepdims=True)
    i1 = jnp.min(jnp.where(elog == m1, lane, big), axis=-1, keepdims=True)
    elog2 = jnp.where(lane == i1, ninf, elog)
    m2 = jnp.max(elog2, axis=-1, keepdims=True)
    i2 = jnp.min(jnp.where(elog2 == m2, lane, big), axis=-1, keepdims=True)
    e2 = jnp.exp(m2 - m1)
    w1 = 1.0 / (1.0 + e2)
    w2 = e2 * w1
    id_ref[...] = jnp.where(lane == 0, i1 - N_GROUPS, jnp.where(lane == 1, i2 - N_GROUPS, 0))
    rwgt_ref[...] = jnp.where(lane == 0, gp * w1, jnp.where(lane == 1, gp * w2, 0.0))


def _merge(o_f, o_b, main2, yb, yc, x2d, wa, wb, wc, wo, gnw, fnw, rw, rb, *, tm):
    t, d = x2d.shape
    full = lambda shp: pl.BlockSpec(shp, lambda i: tuple(0 for _ in shp))
    return pl.pallas_call(
        _merge_kernel,
        out_shape=(jax.ShapeDtypeStruct((t, d), F32), jax.ShapeDtypeStruct((t, d), BF16),
                   jax.ShapeDtypeStruct((t, LANES), jnp.int32), jax.ShapeDtypeStruct((t, LANES), F32)),
        grid=(t // tm,),
        in_specs=[pl.BlockSpec((tm, 512), lambda i: (i, 0)),
                  pl.BlockSpec((tm, 512), lambda i: (i, 0)),
                  pl.BlockSpec((tm, 512), lambda i: (i, COL_Z // 512)),
                  pl.BlockSpec((tm, d), lambda i: (i, 0)),
                  pl.BlockSpec((tm, d), lambda i: (i, 1)),
                  pl.BlockSpec((tm, d), lambda i: (i, 2)),
                  pl.BlockSpec((tm, 512), lambda i: (i, 0)),
                  pl.BlockSpec((tm, 512), lambda i: (i, 0)),
                  pl.BlockSpec((tm, d), lambda i: (i, 0)),
                  full((512, d)), full((512, d)), full((512, d)), full((d, d)),
                  full((1, LANES)), full((1, d)), full((d, LANES)), full((1, LANES))],
        out_specs=(pl.BlockSpec((tm, d), lambda i: (i, 0)), pl.BlockSpec((tm, d), lambda i: (i, 0)),
                   pl.BlockSpec((tm, LANES), lambda i: (i, 0)), pl.BlockSpec((tm, LANES), lambda i: (i, 0))),
        compiler_params=_cparams(("arbitrary",)),
        name="merge_router",
    )(o_f, o_b, main2, main2, main2, main2, yb, yc, x2d, wa, wb, wc, wo, gnw, fnw, rw, rb)


def _expert_kernel(blk_e_ref, nused_ref, x_ref, w1_ref, w3_ref, w2_ref, o_ref, w1b_ref, w3b_ref, w2b_ref):
    i = pl.program_id(0)

    @pl.when((i == 0) | (blk_e_ref[i] != blk_e_ref[jnp.maximum(i - 1, 0)]))
    def _():
        w1b_ref[...] = w1_ref[0, 0].astype(BF16)
        w3b_ref[...] = w3_ref[0, 0].astype(BF16)
        w2b_ref[...] = w2_ref[0, 0].astype(BF16)

    @pl.when(i < nused_ref[0])
    def _():
        x = x_ref[...]
        a = jnp.dot(x, w1b_ref[...], preferred_element_type=F32)
        u = jnp.dot(x, w3b_ref[...], preferred_element_type=F32)
        hmid = (a * _sigmoid(a) * u).astype(BF16)
        o_ref[...] = jnp.dot(hmid, w2b_ref[...], preferred_element_type=F32).astype(o_ref.dtype)

    @pl.when(i >= nused_ref[0])
    def _():
        o_ref[...] = jnp.zeros_like(o_ref)


def _experts(blk_e, nused, xb, w1, w3, w2, *, layer):
    p_len, d = xb.shape
    ff = w1.shape[3]
    nblk = p_len // MOE_BLOCK
    return pl.pallas_call(
        _expert_kernel,
        out_shape=jax.ShapeDtypeStruct((p_len, d), BF16),
        grid_spec=pltpu.PrefetchScalarGridSpec(
            num_scalar_prefetch=2,
            grid=(nblk,),
            in_specs=[pl.BlockSpec((MOE_BLOCK, d), lambda i, be, nu: (i, 0)),
                      pl.BlockSpec((1, 1, d, ff), lambda i, be, nu: (layer, be[i], 0, 0)),
                      pl.BlockSpec((1, 1, d, ff), lambda i, be, nu: (layer, be[i], 0, 0)),
                      pl.BlockSpec((1, 1, ff, d), lambda i, be, nu: (layer, be[i], 0, 0))],
            out_specs=pl.BlockSpec((MOE_BLOCK, d), lambda i, be, nu: (i, 0)),
            scratch_shapes=[pltpu.VMEM((d, ff), BF16), pltpu.VMEM((d, ff), BF16), pltpu.VMEM((ff, d), BF16)],
        ),
        compiler_params=_cparams(("arbitrary",)),
        name="expert_mlp",
    )(blk_e, nused, xb, w1, w3, w2)


def _combine_kernel(x_ref, y0_ref, y1_ref, w_ref, nw_ref, o_ref, *, final):
    w = w_ref[...]
    x = x_ref[...] + w[:, 0:1] * y0_ref[...].astype(F32) + w[:, 1:2] * y1_ref[...].astype(F32)
    if final:
        x = x * lax.rsqrt(jnp.mean(x * x, axis=-1, keepdims=True) + NORM_EPS) * nw_ref[...]
    o_ref[...] = x


def _combine(x2d, y0, y1, wts, norm_w, *, final, tm):
    t, d = x2d.shape
    tile = pl.BlockSpec((tm, d), lambda i: (i, 0))
    return pl.pallas_call(
        functools.partial(_combine_kernel, final=final),
        out_shape=jax.ShapeDtypeStruct((t, d), F32),
        grid=(t // tm,),
        in_specs=[tile, tile, tile, pl.BlockSpec((tm, LANES), lambda i: (i, 0)), pl.BlockSpec((1, d), lambda i: (0, 0))],
        out_specs=tile,
        compiler_params=_cparams(("arbitrary",)),
        name="moe_combine",
    )(x2d, y0, y1, wts, norm_w.reshape(1, d))


def _rope_tables(pos, dim):
    inv = 1.0 / (ROPE_THETA ** (jnp.arange(0, dim, 2, dtype=F32) / dim))
    ang = pos.astype(F32)[:, None] * inv[None, :]
    ang = jnp.concatenate([ang, ang], axis=-1)
    return jnp.cos(ang), jnp.sin(ang)


def _signed_sin(sin):
    half = sin.shape[-1] // 2
    return jnp.concatenate([-sin[:, :half], sin[:, half:]], axis=-1)


def _layout_w_in(w):
    o = 0
    parts = {}
    for name, size in (("qkv", 1536), ("z", 512), ("b", 8), ("a", 8), ("dq", 512), ("dk", 512), ("dv", 512),
                       ("cq", 512), ("ck", 128), ("cv", 128), ("gate", 3072)):
        parts[name] = w[:, o:o + size]
        o += size
    swap = lambda m: jnp.concatenate([m[:, 64:], m[:, :64]], axis=1)
    main = jnp.concatenate([parts["gate"], parts["qkv"], parts["z"], parts["dq"], parts["dk"], parts["dv"],
                            parts["cq"], parts["ck"], swap(parts["ck"]), parts["cv"], swap(parts["cv"])], axis=1)
    ba = jnp.concatenate([parts["b"], parts["a"], jnp.zeros((w.shape[0], LANES - 16), w.dtype)], axis=1)
    return main.astype(BF16), ba


def _rows_layout(t, bsz, s):
    nc = s // GDN_CHUNK
    t = t.reshape(bsz, nc, GDN_CHUNK, 2, GDN_HEADS)
    return jnp.transpose(t, (3, 0, 1, 4, 2)).reshape(2, bsz, nc, GDN_ROWS)


def _moe_dispatch(ids, t):
    a = t * TOPK
    p_len = ((a + N_EXPERTS * (MOE_BLOCK - 1) + MOE_BLOCK - 1) // MOE_BLOCK) * MOE_BLOCK
    n_blocks = p_len // MOE_BLOCK
    flat_e = ids.reshape(-1)
    iota_a = jnp.arange(a, dtype=jnp.int32)
    skey = jnp.sort(flat_e * a + iota_a)
    order = skey % a
    se = skey // a
    experts = jnp.arange(N_EXPERTS, dtype=jnp.int32)
    counts = jnp.sum((flat_e[:, None] == experts[None, :]).astype(jnp.int32), axis=0)
    start = jnp.cumsum(counts) - counts
    pcounts = ((counts + MOE_BLOCK - 1) // MOE_BLOCK) * MOE_BLOCK
    pend = jnp.cumsum(pcounts)
    pstart = pend - pcounts
    dest_sorted = pstart[se] + (iota_a - start[se])
    blk_first = jnp.arange(n_blocks, dtype=jnp.int32) * MOE_BLOCK
    blk_e = jnp.minimum(jnp.sum((pend[None, :] <= blk_first[:, None]).astype(jnp.int32), axis=1), N_EXPERTS - 1)
    row = jnp.arange(p_len, dtype=jnp.int32)
    row_e = jnp.repeat(blk_e, MOE_BLOCK)
    j = row - pstart[row_e]
    valid = j < counts[row_e]
    tok_buf = jnp.where(valid, order[jnp.minimum(start[row_e] + j, a - 1)] // TOPK, row % t)
    _, dest = lax.sort((order, dest_sorted), num_keys=1)
    nused = (pend[-1] // MOE_BLOCK).astype(jnp.int32).reshape(1)
    return tok_buf, dest.reshape(t, TOPK), blk_e, nused


def kernel(x, attn_norm_w, w_in, gdn_conv_w, gdn_a_log, gdn_dt_bias, gdn_norm_w, diff_lambda, diff_norm_w,
           gqa_q_norm_w, gqa_k_norm_w, w_branch_a, w_branch_b, w_branch_c, w_out, ffn_norm_w,
           router_group_w, router_group_b, router_expert_w, router_expert_b,
           expert_w_gate, expert_w_up, expert_w_down, final_norm_w):
    bsz, s, d = x.shape
    t = bsz * s
    depth = w_in.shape[0]
    tm = min(512, t)
    ts = min(512, s)

    rows = s // GRID_W
    row = jnp.broadcast_to(jnp.arange(rows)[:, None], (rows, GRID_W)).reshape(s)
    col = jnp.broadcast_to(jnp.arange(GRID_W)[None, :], (rows, GRID_W)).reshape(s)
    c1, s1 = _rope_tables(jnp.arange(s), DIFF_DQK)
    cr, sr = _rope_tables(row, GQA_DH // 2)
    cc, sc = _rope_tables(col, GQA_DH // 2)
    cos1 = jnp.tile(c1, (1, 2))
    sin1 = jnp.tile(_signed_sin(s1), (1, 2))
    cos2 = jnp.tile(jnp.concatenate([cr, cc], axis=-1), (1, 2))
    sin2 = jnp.tile(jnp.concatenate([_signed_sin(sr), _signed_sin(sc)], axis=-1), (1, 2))

    x2 = x.reshape(t, d)
    for l in range(depth):
        lambda_init = 0.8 - 0.6 * math.exp(-0.3 * l)
        w_main, w_ba = _layout_w_in(w_in[l])
        main2 = _norm_proj(x2, attn_norm_w[l], w_main, BF16, exact=False, tm=tm, tn=N_MAIN // 2)
        ba = _norm_proj(x2, attn_norm_w[l], w_ba, F32, exact=True, tm=tm, tn=LANES)
        main3 = main2.reshape(bsz, s, N_MAIN)

        conv_w = jnp.concatenate([gdn_conv_w[l], jnp.zeros((8 - GDN_CONV, gdn_conv_w.shape[2]), F32)], axis=0)
        qnw = jnp.tile(gqa_q_norm_w[l], 2).reshape(1, LANES)
        knw = jnp.tile(gqa_k_norm_w[l], 2).reshape(1, LANES)
        gq, gk, gv, dq, dk, dv, cq, ck, cv = _prep(main3, conv_w, cos1, sin1, cos2, sin2, qnw, knw, ts=ts)

        b_rows = _rows_layout(ba[:, 0:8], bsz, s)
        a_rows = _rows_layout(ba[:, 8:16], bsz, s)
        alog_row = jnp.repeat(gdn_a_log[l], GDN_CHUNK, axis=1).reshape(2, 1, GDN_ROWS)
        dtb_row = jnp.repeat(gdn_dt_bias[l], GDN_CHUNK, axis=1).reshape(2, 1, GDN_ROWS)
        o_f, o_b = _gdn(a_rows, b_rows, alog_row, dtb_row, gq, gk, gv, nbatch=GDN_NBATCH if bsz % GDN_NBATCH == 0 else 1)

        nw_diff = diff_norm_w[l].reshape(LANES, 1)
        yb = _attention(dq, dk, dv, diff_lambda[l], nw_diff, mode="diff",
                        tq=min(ATTN_TQ, s), tk=min(ATTN_TK, s), lambda_init=lambda_init)
        yc = _attention(cq, ck, cv, diff_lambda[l], nw_diff, mode="gqa",
                        tq=min(ATTN_TQ, s), tk=min(ATTN_TK, s))

        rw = jnp.concatenate([router_group_w[l], router_expert_w[l],
                              jnp.zeros((d, LANES - N_GROUPS - N_EXPERTS), F32)], axis=1)
        rb = jnp.concatenate([router_group_b[l], router_expert_b[l],
                              jnp.zeros((LANES - N_GROUPS - N_EXPERTS,), F32)]).reshape(1, LANES)
        x2, h2, ids, wts = _merge(o_f.reshape(t, 512), o_b.reshape(t, 512), main2, yb.reshape(t, 512), yc.reshape(t, 512), x2,
                                  w_branch_a[l].astype(BF16), w_branch_b[l].astype(BF16),
                                  w_branch_c[l].astype(BF16), w_out[l].astype(BF16),
                                  gdn_norm_w[l].reshape(1, LANES), ffn_norm_w[l].reshape(1, d), rw, rb, tm=tm)

        tok_buf, dest, blk_e, nused = _moe_dispatch(ids[:, :TOPK], t)
        yblk = _experts(blk_e, nused, h2[tok_buf], expert_w_gate, expert_w_up, expert_w_down, layer=l)
        x2 = _combine(x2, yblk[dest[:, 0]], yblk[dest[:, 1]], wts, final_norm_w, final=(l == depth - 1), tm=tm)

    return x2.reshape(bsz, s, d)
```

```python
import functools
import math

import jax
import jax.numpy as jnp
from jax import lax
from jax.experimental import pallas as pl
from jax.experimental.pallas import tpu as pltpu

GRID_W = 64
ROPE_THETA = 10000.0
NORM_EPS = 1e-6
GDN_HEADS = 4
GDN_DK = 128
GDN_DV = 128
GDN_CONV = 5
GDN_CHUNK = 64
DIFF_HEADS = 4
DIFF_DQK = 64
GQA_HEADS = 8
GQA_KV = 2
GQA_DH = 64
N_GROUPS = 4
EXPERTS_PER_GROUP = 8
N_EXPERTS = N_GROUPS * EXPERTS_PER_GROUP
TOPK = 2
MOE_BLOCK = 256

LANES = 128
VMEM_LIMIT = 56 * 1024 * 1024

COL_GATE = 0
COL_QKV = 3072
COL_Z = 4608
COL_DQ = 5120
COL_DK = 5632
COL_DV = 6144
COL_CQ = 6656
COL_CK = 7168
COL_CV = 7424
N_MAIN = 7680

LOG2E = math.log2(math.e)
ATTN_TQ = 1024
ATTN_TK = 1024
ATTN_STAB_KEYS = 256
VT_PAD = 16

HI = lax.Precision.HIGHEST
F32 = jnp.float32
BF16 = jnp.bfloat16


def _cparams(sem):
    return pltpu.CompilerParams(dimension_semantics=sem, vmem_limit_bytes=VMEM_LIMIT)


def _sigmoid(x):
    return 1.0 / (1.0 + jnp.exp(-x))


def _dot_split(a, w):
    a_hi = a.astype(BF16)
    a_lo = (a - a_hi.astype(F32)).astype(BF16)
    w_hi = w.astype(BF16)
    w_lo = (w - w_hi.astype(F32)).astype(BF16)
    return (jnp.dot(a_hi, w_hi, preferred_element_type=F32) + jnp.dot(a_lo, w_hi, preferred_element_type=F32)
            + jnp.dot(a_hi, w_lo, preferred_element_type=F32))


def _norm_proj_kernel(x_ref, nw_ref, w_ref, o_ref, *, exact):
    x = x_ref[...]
    h = x * lax.rsqrt(jnp.mean(x * x, axis=-1, keepdims=True) + NORM_EPS) * nw_ref[...]
    if exact:
        o_ref[...] = _dot_split(h, w_ref[...]).astype(o_ref.dtype)
    else:
        o_ref[...] = jnp.dot(h.astype(BF16), w_ref[...], preferred_element_type=F32).astype(o_ref.dtype)


def _norm_proj(x2d, norm_w, w, out_dtype, *, exact, tm, tn):
    t, d = x2d.shape
    n = w.shape[1]
    return pl.pallas_call(
        functools.partial(_norm_proj_kernel, exact=exact),
        out_shape=jax.ShapeDtypeStruct((t, n), out_dtype),
        grid=(n // tn, t // tm),
        in_specs=[pl.BlockSpec((tm, d), lambda j, i: (i, 0)),
                  pl.BlockSpec((1, d), lambda j, i: (0, 0)),
                  pl.BlockSpec((d, tn), lambda j, i: (0, j))],
        out_specs=pl.BlockSpec((tm, tn), lambda j, i: (i, j)),
        compiler_params=_cparams(("arbitrary", "arbitrary")),
        name="norm_proj_exact" if exact else "norm_proj",
    )(x2d, norm_w.reshape(1, d), w)


HALO = 16


def _rot_half(x, half):
    lane = lax.broadcasted_iota(jnp.int32, x.shape, 1)
    first = (lane % (2 * half)) < half
    return jnp.where(first, pltpu.roll(x, LANES - half, 1), pltpu.roll(x, half, 1))


def _group_sumsq(x, width):
    x2 = x * x
    if width == LANES:
        return jnp.sum(x2, axis=-1, keepdims=True)
    lane = lax.broadcasted_iota(jnp.int32, x.shape, 1)
    lo = lane < width
    s_lo = jnp.sum(jnp.where(lo, x2, 0.0), axis=-1, keepdims=True)
    s_hi = jnp.sum(jnp.where(lo, 0.0, x2), axis=-1, keepdims=True)
    return jnp.where(lo, s_lo, s_hi)


def _aug_slab(x, m):
    lane = lax.broadcasted_iota(jnp.int32, x.shape, 1)
    half = LANES // 2
    keep = (lane < half) if m == 0 else (lane >= half)
    one = jnp.where(lane == (1 - m) * half, 1.0, 0.0).astype(x.dtype)
    return jnp.where(keep, x, one)


def _prep_kernel(qkv_ref, prev_ref, next_ref, dq_ref, dk_ref, dv_ref, cq_ref, ck_ref, cv_ref,
                 convw_ref, cos1_ref, sin1_ref, cos2_ref, sin2_ref, qnw_ref, knw_ref,
                 gq_ref, gk_ref, gv_ref, dqo_ref, dko_ref, dvo_ref, cqo_ref, cko_ref, cvo_ref, *, ts):
    i = pl.program_id(1)
    n = pl.num_programs(1)
    cur = qkv_ref[0].astype(F32)
    prev = jnp.where(i > 0, prev_ref[0].astype(F32), 0.0)
    nxt = jnp.where(i < n - 1, next_ref[0].astype(F32), 0.0)
    ext = jnp.concatenate([prev, cur, nxt], axis=0)
    pad = GDN_CONV // 2
    acc = jnp.zeros_like(cur)
    for j in range(GDN_CONV):
        off = HALO - pad + j
        acc = acc + ext[off:off + ts, :] * convw_ref[j:j + 1, :]
    act = acc * _sigmoid(acc)
    nqk = GDN_HEADS * GDN_DK
    for h in range(GDN_HEADS):
        sl = slice(h * GDN_DK, (h + 1) * GDN_DK)
        qh = act[:, sl]
        gq_ref[0, :, sl] = (qh * lax.rsqrt(_group_sumsq(qh, LANES) + NORM_EPS) * (GDN_DK ** -0.5)).astype(BF16)
        kh = act[:, nqk + h * GDN_DK: nqk + (h + 1) * GDN_DK]
        gk_ref[0, :, sl] = (kh * lax.rsqrt(_group_sumsq(kh, LANES) + NORM_EPS)).astype(BF16)
    gv_ref[0] = act[:, 2 * nqk:].astype(BF16)
    cos1, sin1 = cos1_ref[...], sin1_ref[...]
    ones_rows = jnp.where(lax.broadcasted_iota(jnp.int32, (VT_PAD, ts), 0) == 0, 1.0, 0.0).astype(BF16)
    for p in range(DIFF_HEADS):
        sl = slice(p * LANES, (p + 1) * LANES)
        xq = dq_ref[0, :, sl].astype(F32)
        dqo_ref[0, :, sl] = ((xq * cos1 + _rot_half(xq, DIFF_DQK // 2) * sin1) * (DIFF_DQK ** -0.5 * LOG2E)).astype(BF16)
        xk = dk_ref[0, :, sl].astype(F32)
        xk = (xk * cos1 + _rot_half(xk, DIFF_DQK // 2) * sin1).astype(BF16)
        for m in range(2):
            dko_ref[0, :, (2 * p + m) * LANES:(2 * p + m + 1) * LANES] = _aug_slab(xk, m)
        r0 = p * (LANES + VT_PAD)
        dvo_ref[0, r0:r0 + LANES, :] = dv_ref[0, :, sl].astype(F32).T.astype(BF16)
        dvo_ref[0, r0 + LANES:r0 + LANES + VT_PAD, :] = ones_rows
    cos2, sin2 = cos2_ref[...], sin2_ref[...]
    for p in range(GQA_HEADS * GQA_DH // LANES):
        sl = slice(p * LANES, (p + 1) * LANES)
        xq = cq_ref[0, :, sl].astype(F32)
        xq = xq * lax.rsqrt(_group_sumsq(xq, GQA_DH) * (1.0 / GQA_DH) + NORM_EPS) * qnw_ref[...]
        cqo_ref[0, :, sl] = ((xq * cos2 + _rot_half(xq, GQA_DH // 4) * sin2) * (GQA_DH ** -0.5 * LOG2E)).astype(BF16)
    for p in range(2):
        sl = slice(p * LANES, (p + 1) * LANES)
        xk = ck_ref[0, :, sl].astype(F32)
        xk = xk * lax.rsqrt(_group_sumsq(xk, GQA_DH) * (1.0 / GQA_DH) + NORM_EPS) * knw_ref[...]
        xk = (xk * cos2 + _rot_half(xk, GQA_DH // 4) * sin2).astype(BF16)
        for m in range(2):
            c = p if m == 0 else 1 - p
            cko_ref[0, :, (2 * c + m) * LANES:(2 * c + m + 1) * LANES] = _aug_slab(xk, m)
    vt = cv_ref[0].astype(F32).T.astype(BF16)
    for c in range(GQA_KV):
        r0 = c * (GQA_DH + VT_PAD)
        cvo_ref[0, r0:r0 + GQA_DH, :] = vt[c * GQA_DH:(c + 1) * GQA_DH]
        cvo_ref[0, r0 + GQA_DH:r0 + GQA_DH + VT_PAD, :] = ones_rows


def _prep(main3, conv_w, cos1, sin1, cos2, sin2, qnw, knw, *, ts):
    b, s, _ = main3.shape
    nt = s // ts
    hb = ts // HALO
    last = s // HALO - 1
    row = lambda w: pl.BlockSpec((1, w), lambda bi, i: (0, 0))
    tab = pl.BlockSpec((ts, LANES), lambda bi, i: (i, 0))
    col = lambda w, off: pl.BlockSpec((1, ts, w), lambda bi, i: (bi, i, off // w))
    out = lambda w: pl.BlockSpec((1, ts, w), lambda bi, i: (bi, i, 0))
    outs = [("tok", 512), ("tok", 512), ("tok", 512), ("tok", 512), ("tok", 1024),
            ("rows", DIFF_HEADS * (LANES + VT_PAD)), ("tok", 512), ("tok", 512), ("rows", GQA_KV * (GQA_DH + VT_PAD))]
    specs = tuple(out(w) if kind == "tok" else pl.BlockSpec((1, w, ts), lambda bi, i: (bi, 0, i)) for kind, w in outs)
    shapes = tuple(jax.ShapeDtypeStruct((b, s, w) if kind == "tok" else (b, w, s), BF16) for kind, w in outs)
    return pl.pallas_call(
        functools.partial(_prep_kernel, ts=ts),
        out_shape=shapes,
        grid=(b, nt),
        in_specs=[
            col(1536, COL_QKV),
            pl.BlockSpec((1, HALO, 1536), lambda bi, i: (bi, jnp.maximum(i * hb - 1, 0), COL_QKV // 1536)),
            pl.BlockSpec((1, HALO, 1536), lambda bi, i: (bi, jnp.minimum((i + 1) * hb, last), COL_QKV // 1536)),
            col(512, COL_DQ), col(512, COL_DK), col(512, COL_DV), col(512, COL_CQ), col(256, COL_CK), col(128, COL_CV),
            pl.BlockSpec((8, 1536), lambda bi, i: (0, 0)),
            tab, tab, tab, tab, row(LANES), row(LANES),
        ],
        out_specs=specs,
        compiler_params=_cparams(("arbitrary", "arbitrary")),
        name="mixer_prep",
    )(*([main3] * 9), conv_w, cos1, sin1, cos2, sin2, qnw, knw)


GDN_G = 8
GDN_NBATCH = 4
GDN_ROWS = GDN_HEADS * GDN_CHUNK


def _stack_heads(x):
    return jnp.concatenate([x[:, h * LANES:(h + 1) * LANES] for h in range(GDN_HEADS)], axis=0)


def _row_to_col(row, eye):
    return jnp.sum(jnp.where(eye, row, 0.0), axis=1, keepdims=True)


def _gdn_kernel(af_ref, ab_ref, bf_ref, bb_ref, alog_ref, dtb_ref, qf_ref, kf_ref, vf_ref, qb_ref, kb_ref, vb_ref,
                of_ref, ob_ref, state_ref, gc_ref, gt_ref, beta_ref, *, nbatch):
    blk = pl.program_id(1)
    n = GDN_ROWS
    c = GDN_CHUNK

    @pl.when(blk == 0)
    def _():
        state_ref[...] = jnp.zeros_like(state_ref)

    ri = lax.broadcasted_iota(jnp.int32, (n, n), 0)
    ci = lax.broadcasted_iota(jnp.int32, (n, n), 1)
    same = (ri // c) == (ci // c)
    eye = ri == ci
    ti = lax.broadcasted_iota(jnp.int32, (n, GDN_HEADS * LANES), 0)
    tj = lax.broadcasted_iota(jnp.int32, (n, GDN_HEADS * LANES), 1)
    tot_m = jnp.where((ti // c) == (tj // LANES), 1.0, 0.0)

    chains = []
    for d, (a_ref, b_ref, q_ref, k_ref, v_ref, o_ref) in enumerate(
            ((af_ref, bf_ref, qf_ref, kf_ref, vf_ref, of_ref), (ab_ref, bb_ref, qb_ref, kb_ref, vb_ref, ob_ref))):
        sgn = 1 - 2 * d
        after = same & ((ri - ci) * sgn > 0)
        incl = same & ((ri - ci) * sgn >= 0)
        cum_m = jnp.where(same & ((ci - ri) * sgn >= 0), 1.0, 0.0)
        for bi in range(nbatch):
            ch = d * nbatch + bi
            x = a_ref[0, bi] + dtb_ref[d]
            softplus = jnp.maximum(x, 0.0) + jnp.log(1.0 + jnp.exp(-jnp.abs(x)))
            g = -jnp.exp(alog_ref[d]) * softplus
            beta_ref[ch] = _sigmoid(b_ref[0, bi])
            gc_ref[ch] = jnp.dot(g, cum_m, precision=HI, preferred_element_type=F32)
            gt_ref[ch] = jnp.dot(g, tot_m, precision=HI, preferred_element_type=F32)
            chains.append((ch, d, bi, after, incl, q_ref, k_ref, v_ref, o_ref))

    def chunk(j, chain):
        ch, d, bi, after, incl, q_ref, k_ref, v_ref, o_ref = chain
        cc = j if d == 0 else GDN_G - 1 - j
        r0 = pl.multiple_of(cc * c, c)
        gc_row = gc_ref[ch, pl.ds(cc, 1), :]
        beta_row = beta_ref[ch, pl.ds(cc, 1), :]
        gt_row = gt_ref[ch, pl.ds(cc, 1), :]
        gc_col = _row_to_col(gc_row, eye)
        beta_col = _row_to_col(beta_row, eye)
        k_st = _stack_heads(k_ref[bi, pl.ds(r0, c), :]).astype(F32)
        q_st = _stack_heads(q_ref[bi, pl.ds(r0, c), :]).astype(F32)
        v_st = _stack_heads(v_ref[bi, pl.ds(r0, c), :]).astype(F32)
        egc = jnp.exp(gc_col)
        decay = jnp.exp(jnp.minimum(gc_col - gc_row, 0.0))
        kb = k_st * beta_col
        k_bf = k_st.astype(BF16)
        kk = lax.dot_general(kb.astype(BF16), k_bf, (((1,), (1,)), ((), ())), preferred_element_type=F32)
        qk = lax.dot_general(q_st.astype(BF16), k_bf, (((1,), (1,)), ((), ())), preferred_element_type=F32)
        yield
        neg_a = jnp.where(after, -(kk * decay), 0.0)
        t_m = jnp.where(eye, 1.0, 0.0) + neg_a
        p_m = neg_a
        for _ in range(int(math.log2(c)) - 1):
            p_bf = p_m.astype(BF16)
            p_m = jnp.dot(p_bf, p_bf, preferred_element_type=F32)
            yield
            t_m = t_m + jnp.dot(t_m.astype(BF16), p_m.astype(BF16), preferred_element_type=F32)
            yield
        rhs = jnp.concatenate([v_st * beta_col, kb * egc], axis=1).astype(BF16)
        sol = jnp.dot(t_m.astype(BF16), rhs, preferred_element_type=F32)
        yield
        u_st, w_st = sol[:, :LANES], sol[:, LANES:]
        intra = jnp.where(incl, qk * decay, 0.0).astype(BF16)
        q_dec = (q_st * egc).astype(BF16)
        vn, oq = [], []
        w_bf = w_st.astype(BF16)
        for h in range(GDN_HEADS):
            rs = slice(h * c, (h + 1) * c)
            s_h = state_ref[ch * GDN_HEADS + h].astype(BF16)
            ws_qs = jnp.dot(jnp.concatenate([w_bf[rs], q_dec[rs]], axis=0), s_h, preferred_element_type=F32)
            vn.append(u_st[rs] - ws_qs[:c])
            oq.append(ws_qs[c:])
        yield
        vn_st = jnp.concatenate(vn, axis=0)
        o_st = jnp.concatenate(oq, axis=0) + jnp.dot(intra, vn_st.astype(BF16), preferred_element_type=F32)
        for h in range(GDN_HEADS):
            rs = slice(h * c, (h + 1) * c)
            gt_h = gt_row[:, h * LANES:(h + 1) * LANES]
            k_dec = (k_st[rs] * jnp.exp(gt_h[:, :1] - gc_col[rs])).astype(BF16)
            upd = lax.dot_general(k_dec, vn[h].astype(BF16), (((0,), (0,)), ((), ())), preferred_element_type=F32)
            state_ref[ch * GDN_HEADS + h] = state_ref[ch * GDN_HEADS + h] * jnp.exp(gt_h) + upd
            o_ref[bi, pl.ds(r0, c), h * LANES:(h + 1) * LANES] = o_st[rs]

    def step(j, carry):
        active = [chunk(j, chain) for chain in chains]
        while active:
            active = [g for g in active if next(g, active) is not active]
        return carry

    lax.fori_loop(0, GDN_G, step, 0)


def _gdn(a_rows, b_rows, alog_row, dtb_row, gq, gk, gv, *, nbatch):
    b, s, _ = gq.shape
    nb = s // (GDN_G * GDN_CHUNK)
    ts = GDN_G * GDN_CHUNK
    nchain = 2 * nbatch
    tok_f = pl.BlockSpec((nbatch, ts, 512), lambda bi, i: (bi, i, 0))
    tok_b = pl.BlockSpec((nbatch, ts, 512), lambda bi, i: (bi, nb - 1 - i, 0))
    rows_f = pl.BlockSpec((1, nbatch, GDN_G, GDN_ROWS), lambda bi, i: (0, bi, i, 0))
    rows_b = pl.BlockSpec((1, nbatch, GDN_G, GDN_ROWS), lambda bi, i: (1, bi, nb - 1 - i, 0))
    par = pl.BlockSpec((2, 1, GDN_ROWS), lambda bi, i: (0, 0, 0))
    return pl.pallas_call(
        functools.partial(_gdn_kernel, nbatch=nbatch),
        out_shape=(jax.ShapeDtypeStruct((b, s, 512), F32), jax.ShapeDtypeStruct((b, s, 512), F32)),
        grid=(b // nbatch, nb),
        in_specs=[rows_f, rows_b, rows_f, rows_b, par, par, tok_f, tok_f, tok_f, tok_b, tok_b, tok_b],
        out_specs=(tok_f, tok_b),
        scratch_shapes=[pltpu.VMEM((nchain * GDN_HEADS, GDN_DK, GDN_DV), F32),
                        pltpu.VMEM((nchain, GDN_G, GDN_ROWS), F32),
                        pltpu.VMEM((nchain, GDN_G, GDN_HEADS * LANES), F32),
                        pltpu.VMEM((nchain, GDN_G, GDN_ROWS), F32)],
        compiler_params=_cparams(("arbitrary", "arbitrary")),
        name="gdn_chunked",
    )(a_rows, a_rows, b_rows, b_rows, alog_row, dtb_row, gq, gk, gv, gq, gk, gv)


def _attn_kernel(q_ref, k0_ref, k1_ref, vt_ref, lam_ref, nw_ref, o_ref, acc_ref, *, mode, tk, lambda_init):
    s_len = k0_ref.shape[1]
    tq = q_ref.shape[1]
    rows = vt_ref.shape[1]
    dv = rows - VT_PAD
    half = LANES // 2
    q = q_ref[0]
    lane = lax.broadcasted_iota(jnp.int32, q.shape, 1)
    keep = (lane < half, lane >= half)
    stab = (lane == half, lane == 0)
    zero = jnp.zeros_like(q)
    krefs = (k0_ref, k1_ref)
    nchunks = s_len // tk
    dn = (((1,), (1,)), ((), ()))

    def kchunk(m, ci):
        return krefs[m][0, pl.ds(pl.multiple_of(ci * tk, tk), tk), :]

    def vchunk(ci):
        return vt_ref[0, :, pl.ds(pl.multiple_of(ci * tk, tk), tk)]

    qm, qa = [], []
    for m in range(2):
        qm.append(jnp.where(keep[m], q, zero))
        k_first = krefs[m][0, 0:min(ATTN_STAB_KEYS, s_len), :]
        mx = jnp.max(lax.dot_general(qm[m], k_first, dn, preferred_element_type=F32), axis=-1, keepdims=True)
        qa.append(jnp.where(stab[m], (-mx).astype(BF16), qm[m]))

    def fast(ci, acc):
        st = [lax.dot_general(kchunk(m, ci), qa[m], dn, preferred_element_type=F32) for m in range(2)]
        vt = vchunk(ci)
        return tuple(acc[m] + jnp.dot(vt, jnp.exp2(st[m]).astype(BF16), preferred_element_type=F32) for m in range(2))

    acc = lax.fori_loop(0, nchunks, fast, tuple(jnp.zeros((rows, tq), F32) for _ in range(2)))
    nonfinite = jnp.float32(0.0)
    for m in range(2):
        acc_ref[m] = acc[m]
        nonfinite = nonfinite + jnp.sum(jnp.where(jnp.isfinite(acc[m]), 0.0, 1.0))

    @pl.when(nonfinite > 0.0)
    def _():
        def slow(ci, carry):
            out = []
            vt = vchunk(ci)
            for m in range(2):
                m_i, a_i = carry[m]
                st = lax.dot_general(kchunk(m, ci), qm[m], dn, preferred_element_type=F32)
                m_new = jnp.maximum(m_i, jnp.max(st, axis=0, keepdims=True))
                p = jnp.exp2(st - m_new).astype(BF16)
                out.append((m_new, jnp.exp2(m_i - m_new) * a_i + jnp.dot(vt, p, preferred_element_type=F32)))
            return tuple(out)

        init = tuple((jnp.full((1, tq), -jnp.inf, F32), jnp.zeros((rows, tq), F32)) for _ in range(2))
        res = lax.fori_loop(0, nchunks, slow, init)
        for m in range(2):
            acc_ref[m] = res[m][1]

    o0 = acc_ref[0, 0:dv, :] / acc_ref[0, dv:dv + 1, :]
    o1 = acc_ref[1, 0:dv, :] / acc_ref[1, dv:dv + 1, :]
    if mode == "diff":
        lv = lam_ref[...]
        lam = (jnp.exp(jnp.sum(lv[0:1] * lv[1:2], axis=-1, keepdims=True))
               - jnp.exp(jnp.sum(lv[2:3] * lv[3:4], axis=-1, keepdims=True)) + lambda_init)
        ot = o0 - lam * o1
        ot = ot * lax.rsqrt(jnp.mean(ot * ot, axis=0, keepdims=True) + NORM_EPS) * nw_ref[...] * (1.0 - lambda_init)
    else:
        ot = jnp.concatenate([o0, o1], axis=0)
    o_ref[0] = ot.T.astype(o_ref.dtype)


def _attention(q, k_arr, vt_arr, lam_vecs, norm_w, *, mode, tq, tk, lambda_init=0.0):
    b, s, w = q.shape
    slabs = w // LANES
    if mode == "diff":
        rows = LANES + VT_PAD
        k_col = lambda p, m: 2 * p + m
        v_grp = lambda p: p
    else:
        rows = GQA_DH + VT_PAD
        k_col = lambda p, m: 2 * (p // 2) + m
        v_grp = lambda p: p // 2
    kspec = lambda m: pl.BlockSpec((1, s, LANES), lambda bi, p, i: (bi, 0, k_col(p, m)))
    return pl.pallas_call(
        functools.partial(_attn_kernel, mode=mode, tk=tk, lambda_init=lambda_init),
        out_shape=jax.ShapeDtypeStruct((b, s, w), BF16),
        grid=(b, slabs, s // tq),
        in_specs=[pl.BlockSpec((1, tq, LANES), lambda bi, p, i: (bi, i, p)),
                  kspec(0), kspec(1),
                  pl.BlockSpec((1, rows, s), lambda bi, p, i: (bi, v_grp(p), 0)),
                  pl.BlockSpec((4, DIFF_DQK), lambda bi, p, i: (0, 0)),
                  pl.BlockSpec((LANES, 1), lambda bi, p, i: (0, 0))],
        out_specs=pl.BlockSpec((1, tq, LANES), lambda bi, p, i: (bi, i, p)),
        scratch_shapes=[pltpu.VMEM((2, rows, tq), F32)],
        compiler_params=_cparams(("arbitrary", "arbitrary", "arbitrary")),
        name="attn_" + mode,
    )(q, k_arr, k_arr, vt_arr, lam_vecs, norm_w)


def _merge_kernel(of_ref, ob_ref, z_ref, g0_ref, g1_ref, g2_ref, yb_ref, yc_ref, x_ref,
                  wa_ref, wb_ref, wc_ref, wo_ref, gnw_ref, fnw_ref, rw_ref, rb_ref,
                  xo_ref, h_ref, id_ref, rwgt_ref):
    o = of_ref[...] + ob_ref[...]
    parts = []
    for h in range(GDN_HEADS):
        oh = o[:, h * LANES:(h + 1) * LANES]
        parts.append(oh * lax.rsqrt(jnp.mean(oh * oh, axis=-1, keepdims=True) + NORM_EPS) * gnw_ref[...])
    z = z_ref[...].astype(F32)
    ya = (jnp.concatenate(parts, axis=1) * (z * _sigmoid(z))).astype(BF16)
    merged = _sigmoid(g0_ref[...].astype(F32)) * jnp.dot(ya, wa_ref[...], preferred_element_type=F32)
    merged = merged + _sigmoid(g1_ref[...].astype(F32)) * jnp.dot(yb_ref[...], wb_ref[...], preferred_element_type=F32)
    merged = merged + _sigmoid(g2_ref[...].astype(F32)) * jnp.dot(yc_ref[...], wc_ref[...], preferred_element_type=F32)
    xn = x_ref[...] + jnp.dot(merged.astype(BF16), wo_ref[...], preferred_element_type=F32)
    xo_ref[...] = xn
    hf = xn * lax.rsqrt(jnp.mean(xn * xn, axis=-1, keepdims=True) + NORM_EPS) * fnw_ref[...]
    h_ref[...] = hf.astype(BF16)
    logits = _dot_split(hf, rw_ref[...]) + rb_ref[...]
    lane = lax.broadcasted_iota(jnp.int32, logits.shape, 1)
    big = jnp.int32(LANES)
    ninf = -jnp.inf
    glog = jnp.where(lane < N_GROUPS, logits, ninf)
    gmax = jnp.max(glog, axis=-1, keepdims=True)
    gidx = jnp.min(jnp.where(glog == gmax, lane, big), axis=-1, keepdims=True)
    gp = 1.0 / jnp.sum(jnp.exp(glog - gmax), axis=-1, keepdims=True)
    e = lane - N_GROUPS
    sel = (e >= 0) & (e < N_EXPERTS) & ((e // EXPERTS_PER_GROUP) == gidx)
    elog = jnp.where(sel, logits, ninf)
    m1 = jnp.max(elog, axis=-1, keepdims=True)
    i1 = jnp.min(jnp.where(elog == m1, lane, big), axis=-1, keepdims=True)
    elog2 = jnp.where(lane == i1, ninf, elog)
    m2 = jnp.max(elog2, axis=-1, keepdims=True)
    i2 = jnp.min(jnp.where(elog2 == m2, lane, big), axis=-1, keepdims=True)
    e2 = jnp.exp(m2 - m1)
    w1 = 1.0 / (1.0 + e2)
    w2 = e2 * w1
    id_ref[...] = jnp.where(lane == 0, i1 - N_GROUPS, jnp.where(lane == 1, i2 - N_GROUPS, 0))
    rwgt_ref[...] = jnp.where(lane == 0, gp * w1, jnp.where(lane == 1, gp * w2, 0.0))


def _merge(o_f, o_b, main2, yb, yc, x2d, wa, wb, wc, wo, gnw, fnw, rw, rb, *, tm):
    t, d = x2d.shape
    full = lambda shp: pl.BlockSpec(shp, lambda i: tuple(0 for _ in shp))
    return pl.pallas_call(
        _merge_kernel,
        out_shape=(jax.ShapeDtypeStruct((t, d), F32), jax.ShapeDtypeStruct((t, d), BF16),
                   jax.ShapeDtypeStruct((t, LANES), jnp.int32), jax.ShapeDtypeStruct((t, LANES), F32)),
        grid=(t // tm,),
        in_specs=[pl.BlockSpec((tm, 512), lambda i: (i, 0)),
                  pl.BlockSpec((tm, 512), lambda i: (i, 0)),
                  pl.BlockSpec((tm, 512), lambda i: (i, COL_Z // 512)),
                  pl.BlockSpec((tm, d), lambda i: (i, 0)),
                  pl.BlockSpec((tm, d), lambda i: (i, 1)),
                  pl.BlockSpec((tm, d), lambda i: (i, 2)),
                  pl.BlockSpec((tm, 512), lambda i: (i, 0)),
                  pl.BlockSpec((tm, 512), lambda i: (i, 0)),
                  pl.BlockSpec((tm, d), lambda i: (i, 0)),
                  full((512, d)), full((512, d)), full((512, d)), full((d, d)),
                  full((1, LANES)), full((1, d)), full((d, LANES)), full((1, LANES))],
        out_specs=(pl.BlockSpec((tm, d), lambda i: (i, 0)), pl.BlockSpec((tm, d), lambda i: (i, 0)),
                   pl.BlockSpec((tm, LANES), lambda i: (i, 0)), pl.BlockSpec((tm, LANES), lambda i: (i, 0))),
        compiler_params=_cparams(("arbitrary",)),
        name="merge_router",
    )(o_f, o_b, main2, main2, main2, main2, yb, yc, x2d, wa, wb, wc, wo, gnw, fnw, rw, rb)


def _expert_kernel(blk_e_ref, nused_ref, x_ref, w1_ref, w3_ref, w2_ref, o_ref, w1b_ref, w3b_ref, w2b_ref):
    i = pl.program_id(0)

    @pl.when((i == 0) | (blk_e_ref[i] != blk_e_ref[jnp.maximum(i - 1, 0)]))
    def _():
        w1b_ref[...] = w1_ref[0, 0].astype(BF16)
        w3b_ref[...] = w3_ref[0, 0].astype(BF16)
        w2b_ref[...] = w2_ref[0, 0].astype(BF16)

    @pl.when(i < nused_ref[0])
    def _():
        x = x_ref[...]
        a = jnp.dot(x, w1b_ref[...], preferred_element_type=F32)
        u = jnp.dot(x, w3b_ref[...], preferred_element_type=F32)
        hmid = (a * _sigmoid(a) * u).astype(BF16)
        o_ref[...] = jnp.dot(hmid, w2b_ref[...], preferred_element_type=F32).astype(o_ref.dtype)

    @pl.when(i >= nused_ref[0])
    def _():
        o_ref[...] = jnp.zeros_like(o_ref)


def _experts(blk_e, nused, xb, w1, w3, w2, *, layer):
    p_len, d = xb.shape
    ff = w1.shape[3]
    nblk = p_len // MOE_BLOCK
    return pl.pallas_call(
        _expert_kernel,
        out_shape=jax.ShapeDtypeStruct((p_len, d), BF16),
        grid_spec=pltpu.PrefetchScalarGridSpec(
            num_scalar_prefetch=2,
            grid=(nblk,),
            in_specs=[pl.BlockSpec((MOE_BLOCK, d), lambda i, be, nu: (i, 0)),
                      pl.BlockSpec((1, 1, d, ff), lambda i, be, nu: (layer, be[i], 0, 0)),
                      pl.BlockSpec((1, 1, d, ff), lambda i, be, nu: (layer, be[i], 0, 0)),
                      pl.BlockSpec((1, 1, ff, d), lambda i, be, nu: (layer, be[i], 0, 0))],
            out_specs=pl.BlockSpec((MOE_BLOCK, d), lambda i, be, nu: (i, 0)),
            scratch_shapes=[pltpu.VMEM((d, ff), BF16), pltpu.VMEM((d, ff), BF16), pltpu.VMEM((ff, d), BF16)],
        ),
        compiler_params=_cparams(("arbitrary",)),
        name="expert_mlp",
    )(blk_e, nused, xb, w1, w3, w2)


def _combine_kernel(x_ref, y0_ref, y1_ref, w_ref, nw_ref, o_ref, *, final):
    w = w_ref[...]
    x = x_ref[...] + w[:, 0:1] * y0_ref[...].astype(F32) + w[:, 1:2] * y1_ref[...].astype(F32)
    if final:
        x = x * lax.rsqrt(jnp.mean(x * x, axis=-1, keepdims=True) + NORM_EPS) * nw_ref[...]
    o_ref[...] = x


def _combine(x2d, y0, y1, wts, norm_w, *, final, tm):
    t, d = x2d.shape
    tile = pl.BlockSpec((tm, d), lambda i: (i, 0))
    return pl.pallas_call(
        functools.partial(_combine_kernel, final=final),
        out_shape=jax.ShapeDtypeStruct((t, d), F32),
        grid=(t // tm,),
        in_specs=[tile, tile, tile, pl.BlockSpec((tm, LANES), lambda i: (i, 0)), pl.BlockSpec((1, d), lambda i: (0, 0))],
        out_specs=tile,
        compiler_params=_cparams(("arbitrary",)),
        name="moe_combine",
    )(x2d, y0, y1, wts, norm_w.reshape(1, d))


def _rope_tables(pos, dim):
    inv = 1.0 / (ROPE_THETA ** (jnp.arange(0, dim, 2, dtype=F32) / dim))
    ang = pos.astype(F32)[:, None] * inv[None, :]
    ang = jnp.concatenate([ang, ang], axis=-1)
    return jnp.cos(ang), jnp.sin(ang)


def _signed_sin(sin):
    half = sin.shape[-1] // 2
    return jnp.concatenate([-sin[:, :half], sin[:, half:]], axis=-1)


def _layout_w_in(w):
    o = 0
    parts = {}
    for name, size in (("qkv", 1536), ("z", 512), ("b", 8), ("a", 8), ("dq", 512), ("dk", 512), ("dv", 512),
                       ("cq", 512), ("ck", 128), ("cv", 128), ("gate", 3072)):
        parts[name] = w[:, o:o + size]
        o += size
    swap = lambda m: jnp.concatenate([m[:, 64:], m[:, :64]], axis=1)
    main = jnp.concatenate([parts["gate"], parts["qkv"], parts["z"], parts["dq"], parts["dk"], parts["dv"],
                            parts["cq"], parts["ck"], swap(parts["ck"]), parts["cv"], swap(parts["cv"])], axis=1)
    ba = jnp.concatenate([parts["b"], parts["a"], jnp.zeros((w.shape[0], LANES - 16), w.dtype)], axis=1)
    return main.astype(BF16), ba


def _rows_layout(t, bsz, s):
    nc = s // GDN_CHUNK
    t = t.reshape(bsz, nc, GDN_CHUNK, 2, GDN_HEADS)
    return jnp.transpose(t, (3, 0, 1, 4, 2)).reshape(2, bsz, nc, GDN_ROWS)


def _moe_dispatch(ids, t):
    a = t * TOPK
    p_len = ((a + N_EXPERTS * (MOE_BLOCK - 1) + MOE_BLOCK - 1) // MOE_BLOCK) * MOE_BLOCK
    n_blocks = p_len // MOE_BLOCK
    flat_e = ids.reshape(-1)
    iota_a = jnp.arange(a, dtype=jnp.int32)
    skey = jnp.sort(flat_e * a + iota_a)
    order = skey % a
    se = skey // a
    experts = jnp.arange(N_EXPERTS, dtype=jnp.int32)
    counts = jnp.sum((flat_e[:, None] == experts[None, :]).astype(jnp.int32), axis=0)
    start = jnp.cumsum(counts) - counts
    pcounts = ((counts + MOE_BLOCK - 1) // MOE_BLOCK) * MOE_BLOCK
    pend = jnp.cumsum(pcounts)
    pstart = pend - pcounts
    dest_sorted = pstart[se] + (iota_a - start[se])
    blk_first = jnp.arange(n_blocks, dtype=jnp.int32) * MOE_BLOCK
    blk_e = jnp.minimum(jnp.sum((pend[None, :] <= blk_first[:, None]).astype(jnp.int32), axis=1), N_EXPERTS - 1)
    row = jnp.arange(p_len, dtype=jnp.int32)
    row_e = jnp.repeat(blk_e, MOE_BLOCK)
    j = row - pstart[row_e]
    valid = j < counts[row_e]
    tok_buf = jnp.where(valid, order[jnp.minimum(start[row_e] + j, a - 1)] // TOPK, row % t)
    _, dest = lax.sort((order, dest_sorted), num_keys=1)
    nused = (pend[-1] // MOE_BLOCK).astype(jnp.int32).reshape(1)
    return tok_buf, dest.reshape(t, TOPK), blk_e, nused


def kernel(x, attn_norm_w, w_in, gdn_conv_w, gdn_a_log, gdn_dt_bias, gdn_norm_w, diff_lambda, diff_norm_w,
           gqa_q_norm_w, gqa_k_norm_w, w_branch_a, w_branch_b, w_branch_c, w_out, ffn_norm_w,
           router_group_w, router_group_b, router_expert_w, router_expert_b,
           expert_w_gate, expert_w_up, expert_w_down, final_norm_w):
    bsz, s, d = x.shape
    t = bsz * s
    depth = w_in.shape[0]
    tm = min(512, t)
    ts = min(512, s)

    rows = s // GRID_W
    row = jnp.broadcast_to(jnp.arange(rows)[:, None], (rows, GRID_W)).reshape(s)
    col = jnp.broadcast_to(jnp.arange(GRID_W)[None, :], (rows, GRID_W)).reshape(s)
    c1, s1 = _rope_tables(jnp.arange(s), DIFF_DQK)
    cr, sr = _rope_tables(row, GQA_DH // 2)
    cc, sc = _rope_tables(col, GQA_DH // 2)
    cos1 = jnp.tile(c1, (1, 2))
    sin1 = jnp.tile(_signed_sin(s1), (1, 2))
    cos2 = jnp.tile(jnp.concatenate([cr, cc], axis=-1), (1, 2))
    sin2 = jnp.tile(jnp.concatenate([_signed_sin(sr), _signed_sin(sc)], axis=-1), (1, 2))

    x2 = x.reshape(t, d)
    for l in range(depth):
        lambda_init = 0.8 - 0.6 * math.exp(-0.3 * l)
        w_main, w_ba = _layout_w_in(w_in[l])
        main2 = _norm_proj(x2, attn_norm_w[l], w_main, BF16, exact=False, tm=tm, tn=N_MAIN // 2)
        ba = _norm_proj(x2, attn_norm_w[l], w_ba, F32, exact=True, tm=tm, tn=LANES)
        main3 = main2.reshape(bsz, s, N_MAIN)

        conv_w = jnp.concatenate([gdn_conv_w[l], jnp.zeros((8 - GDN_CONV, gdn_conv_w.shape[2]), F32)], axis=0)
        qnw = jnp.tile(gqa_q_norm_w[l], 2).reshape(1, LANES)
        knw = jnp.tile(gqa_k_norm_w[l], 2).reshape(1, LANES)
        gq, gk, gv, dq, dk, dv, cq, ck, cv = _prep(main3, conv_w, cos1, sin1, cos2, sin2, qnw, knw, ts=ts)

        b_rows = _rows_layout(ba[:, 0:8], bsz, s)
        a_rows = _rows_layout(ba[:, 8:16], bsz, s)
        alog_row = jnp.repeat(gdn_a_log[l], GDN_CHUNK, axis=1).reshape(2, 1, GDN_ROWS)
        dtb_row = jnp.repeat(gdn_dt_bias[l], GDN_CHUNK, axis=1).reshape(2, 1, GDN_ROWS)
        o_f, o_b = _gdn(a_rows, b_rows, alog_row, dtb_row, gq, gk, gv, nbatch=GDN_NBATCH if bsz % GDN_NBATCH == 0 else 1)

        nw_diff = diff_norm_w[l].reshape(LANES, 1)
        yb = _attention(dq, dk, dv, diff_lambda[l], nw_diff, mode="diff",
                        tq=min(ATTN_TQ, s), tk=min(ATTN_TK, s), lambda_init=lambda_init)
        yc = _attention(cq, ck, cv, diff_lambda[l], nw_diff, mode="gqa",
                        tq=min(ATTN_TQ, s), tk=min(ATTN_TK, s))

        rw = jnp.concatenate([router_group_w[l], router_expert_w[l],
                              jnp.zeros((d, LANES - N_GROUPS - N_EXPERTS), F32)], axis=1)
        rb = jnp.concatenate([router_group_b[l], router_expert_b[l],
                              jnp.zeros((LANES - N_GROUPS - N_EXPERTS,), F32)]).reshape(1, LANES)
        x2, h2, ids, wts = _merge(o_f.reshape(t, 512), o_b.reshape(t, 512), main2, yb.reshape(t, 512), yc.reshape(t, 512), x2,
                                  w_branch_a[l].astype(BF16), w_branch_b[l].astype(BF16),
                                  w_branch_c[l].astype(BF16), w_out[l].astype(BF16),
                                  gdn_norm_w[l].reshape(1, LANES), ffn_norm_w[l].reshape(1, d), rw, rb, tm=tm)

        tok_buf, dest, blk_e, nused = _moe_dispatch(ids[:, :TOPK], t)
        yblk = _experts(blk_e, nused, h2[tok_buf], expert_w_gate, expert_w_up, expert_w_down, layer=l)
        x2 = _combine(x2, yblk[dest[:, 0]], yblk[dest[:, 1]], wts, final_norm_w, final=(l == depth - 1), tm=tm)

    return x2.reshape(bsz, s, d)
```

```python
import functools
import math

import jax
import jax.numpy as jnp
from jax import lax
from jax.experimental import pallas as pl
from jax.experimental.pallas import tpu as pltpu

GRID_W = 64
ROPE_THETA = 10000.0
NORM_EPS = 1e-6
GDN_HEADS = 4
GDN_DK = 128
GDN_DV = 128
GDN_CONV = 5
GDN_CHUNK = 64
DIFF_HEADS = 4
DIFF_DQK = 64
GQA_HEADS = 8
GQA_KV = 2
GQA_DH = 64
N_GROUPS = 4
EXPERTS_PER_GROUP = 8
N_EXPERTS = N_GROUPS * EXPERTS_PER_GROUP
TOPK = 2
MOE_BLOCK = 256

LANES = 128
VMEM_LIMIT = 56 * 1024 * 1024

COL_GATE = 0
COL_QKV = 3072
COL_Z = 4608
COL_DQ = 5120
COL_DK = 5632
COL_DV = 6144
COL_CQ = 6656
COL_CK = 7168
COL_CV = 7424
COL_BA = 7552
N_MAIN = 7680

LOG2E = math.log2(math.e)
ATTN_TQ = 1024
ATTN_TK = 1024
ATTN_STAB_KEYS = 256
VT_PAD = 16

HI = lax.Precision.HIGHEST
F32 = jnp.float32
BF16 = jnp.bfloat16


def _cparams(sem):
    return pltpu.CompilerParams(dimension_semantics=sem, vmem_limit_bytes=VMEM_LIMIT)


def _sigmoid(x):
    return 1.0 / (1.0 + jnp.exp(-x))


def _dot_split(a, w):
    a_hi = a.astype(BF16)
    a_lo = (a - a_hi.astype(F32)).astype(BF16)
    w_hi = w.astype(BF16)
    w_lo = (w - w_hi.astype(F32)).astype(BF16)
    return (jnp.dot(a_hi, w_hi, preferred_element_type=F32) + jnp.dot(a_lo, w_hi, preferred_element_type=F32)
            + jnp.dot(a_hi, w_lo, preferred_element_type=F32))


def _norm_proj_kernel(x_ref, nw_ref, w_ref, o_ref, tail_ref):
    x = x_ref[...]
    h = x * lax.rsqrt(jnp.mean(x * x, axis=-1, keepdims=True) + NORM_EPS) * nw_ref[...]
    acc = jnp.dot(h.astype(BF16), w_ref[...], preferred_element_type=F32)
    o_ref[...] = acc.astype(o_ref.dtype)
    tail_ref[0] = acc[:, acc.shape[1] - LANES:]


def _norm_proj(x2d, norm_w, w, *, tm, tn):
    t, d = x2d.shape
    n = w.shape[1]
    main, tails = pl.pallas_call(
        _norm_proj_kernel,
        out_shape=(jax.ShapeDtypeStruct((t, n), BF16), jax.ShapeDtypeStruct((n // tn, t, LANES), F32)),
        grid=(n // tn, t // tm),
        in_specs=[pl.BlockSpec((tm, d), lambda j, i: (i, 0)),
                  pl.BlockSpec((1, d), lambda j, i: (0, 0)),
                  pl.BlockSpec((d, tn), lambda j, i: (0, j))],
        out_specs=(pl.BlockSpec((tm, tn), lambda j, i: (i, j)), pl.BlockSpec((1, tm, LANES), lambda j, i: (j, i, 0))),
        compiler_params=_cparams(("arbitrary", "arbitrary")),
        name="norm_proj",
    )(x2d, norm_w.reshape(1, d), w)
    return main, tails[n // tn - 1]


HALO = 16


def _rot_half(x, half):
    lane = lax.broadcasted_iota(jnp.int32, x.shape, 1)
    first = (lane % (2 * half)) < half
    return jnp.where(first, pltpu.roll(x, LANES - half, 1), pltpu.roll(x, half, 1))


def _group_sumsq(x, width):
    x2 = x * x
    if width == LANES:
        return jnp.sum(x2, axis=-1, keepdims=True)
    lane = lax.broadcasted_iota(jnp.int32, x.shape, 1)
    lo = lane < width
    s_lo = jnp.sum(jnp.where(lo, x2, 0.0), axis=-1, keepdims=True)
    s_hi = jnp.sum(jnp.where(lo, 0.0, x2), axis=-1, keepdims=True)
    return jnp.where(lo, s_lo, s_hi)


def _aug_slab(x, m):
    lane = lax.broadcasted_iota(jnp.int32, x.shape, 1)
    half = LANES // 2
    keep = (lane < half) if m == 0 else (lane >= half)
    one = jnp.where(lane == (1 - m) * half, 1.0, 0.0).astype(x.dtype)
    return jnp.where(keep, x, one)


def _prep_kernel(qkv_ref, prev_ref, next_ref, dq_ref, dk_ref, dv_ref, cq_ref, ck_ref, cv_ref,
                 convw_ref, cos1_ref, sin1_ref, cos2_ref, sin2_ref, qnw_ref, knw_ref,
                 gq_ref, gk_ref, gv_ref, dqo_ref, dko_ref, dvo_ref, cqo_ref, cko_ref, cvo_ref, *, ts):
    i = pl.program_id(1)
    n = pl.num_programs(1)
    cur = qkv_ref[0].astype(F32)
    prev = jnp.where(i > 0, prev_ref[0].astype(F32), 0.0)
    nxt = jnp.where(i < n - 1, next_ref[0].astype(F32), 0.0)
    ext = jnp.concatenate([prev, cur, nxt], axis=0)
    pad = GDN_CONV // 2
    acc = jnp.zeros_like(cur)
    for j in range(GDN_CONV):
        off = HALO - pad + j
        acc = acc + ext[off:off + ts, :] * convw_ref[j:j + 1, :]
    act = acc * _sigmoid(acc)
    nqk = GDN_HEADS * GDN_DK
    for h in range(GDN_HEADS):
        sl = slice(h * GDN_DK, (h + 1) * GDN_DK)
        qh = act[:, sl]
        gq_ref[0, :, sl] = (qh * lax.rsqrt(_group_sumsq(qh, LANES) + NORM_EPS) * (GDN_DK ** -0.5)).astype(BF16)
        kh = act[:, nqk + h * GDN_DK: nqk + (h + 1) * GDN_DK]
        gk_ref[0, :, sl] = (kh * lax.rsqrt(_group_sumsq(kh, LANES) + NORM_EPS)).astype(BF16)
    gv_ref[0] = act[:, 2 * nqk:].astype(BF16)
    cos1, sin1 = cos1_ref[...], sin1_ref[...]
    ones_rows = jnp.where(lax.broadcasted_iota(jnp.int32, (VT_PAD, ts), 0) == 0, 1.0, 0.0).astype(BF16)
    for p in range(DIFF_HEADS):
        sl = slice(p * LANES, (p + 1) * LANES)
        xq = dq_ref[0, :, sl].astype(F32)
        dqo_ref[0, :, sl] = ((xq * cos1 + _rot_half(xq, DIFF_DQK // 2) * sin1) * (DIFF_DQK ** -0.5 * LOG2E)).astype(BF16)
        xk = dk_ref[0, :, sl].astype(F32)
        xk = (xk * cos1 + _rot_half(xk, DIFF_DQK // 2) * sin1).astype(BF16)
        for m in range(2):
            dko_ref[0, :, (2 * p + m) * LANES:(2 * p + m + 1) * LANES] = _aug_slab(xk, m)
        r0 = p * (LANES + VT_PAD)
        dvo_ref[0, r0:r0 + LANES, :] = dv_ref[0, :, sl].astype(F32).T.astype(BF16)
        dvo_ref[0, r0 + LANES:r0 + LANES + VT_PAD, :] = ones_rows
    cos2, sin2 = cos2_ref[...], sin2_ref[...]
    for p in range(GQA_HEADS * GQA_DH // LANES):
        sl = slice(p * LANES, (p + 1) * LANES)
        xq = cq_ref[0, :, sl].astype(F32)
        xq = xq * lax.rsqrt(_group_sumsq(xq, GQA_DH) * (1.0 / GQA_DH) + NORM_EPS) * qnw_ref[...]
        cqo_ref[0, :, sl] = ((xq * cos2 + _rot_half(xq, GQA_DH // 4) * sin2) * (GQA_DH ** -0.5 * LOG2E)).astype(BF16)
    for p in range(2):
        sl = slice(p * LANES, (p + 1) * LANES)
        xk = ck_ref[0, :, sl].astype(F32)
        xk = xk * lax.rsqrt(_group_sumsq(xk, GQA_DH) * (1.0 / GQA_DH) + NORM_EPS) * knw_ref[...]
        xk = (xk * cos2 + _rot_half(xk, GQA_DH // 4) * sin2).astype(BF16)
        for m in range(2):
            c = p if m == 0 else 1 - p
            cko_ref[0, :, (2 * c + m) * LANES:(2 * c + m + 1) * LANES] = _aug_slab(xk, m)
    vt = cv_ref[0].astype(F32).T.astype(BF16)
    for c in range(GQA_KV):
        r0 = c * (GQA_DH + VT_PAD)
        cvo_ref[0, r0:r0 + GQA_DH, :] = vt[c * GQA_DH:(c + 1) * GQA_DH]
        cvo_ref[0, r0 + GQA_DH:r0 + GQA_DH + VT_PAD, :] = ones_rows


def _prep(main3, conv_w, cos1, sin1, cos2, sin2, qnw, knw, *, ts):
    b, s, _ = main3.shape
    nt = s // ts
    hb = ts // HALO
    last = s // HALO - 1
    row = lambda w: pl.BlockSpec((1, w), lambda bi, i: (0, 0))
    tab = pl.BlockSpec((ts, LANES), lambda bi, i: (i, 0))
    col = lambda w, off: pl.BlockSpec((1, ts, w), lambda bi, i: (bi, i, off // w))
    out = lambda w: pl.BlockSpec((1, ts, w), lambda bi, i: (bi, i, 0))
    outs = [("tok", 512), ("tok", 512), ("tok", 512), ("tok", 512), ("tok", 1024),
            ("rows", DIFF_HEADS * (LANES + VT_PAD)), ("tok", 512), ("tok", 512), ("rows", GQA_KV * (GQA_DH + VT_PAD))]
    specs = tuple(out(w) if kind == "tok" else pl.BlockSpec((1, w, ts), lambda bi, i: (bi, 0, i)) for kind, w in outs)
    shapes = tuple(jax.ShapeDtypeStruct((b, s, w) if kind == "tok" else (b, w, s), BF16) for kind, w in outs)
    return pl.pallas_call(
        functools.partial(_prep_kernel, ts=ts),
        out_shape=shapes,
        grid=(b, nt),
        in_specs=[
            col(1536, COL_QKV),
            pl.BlockSpec((1, HALO, 1536), lambda bi, i: (bi, jnp.maximum(i * hb - 1, 0), COL_QKV // 1536)),
            pl.BlockSpec((1, HALO, 1536), lambda bi, i: (bi, jnp.minimum((i + 1) * hb, last), COL_QKV // 1536)),
            col(512, COL_DQ), col(512, COL_DK), col(512, COL_DV), col(512, COL_CQ), col(256, COL_CK), col(128, COL_CV),
            pl.BlockSpec((8, 1536), lambda bi, i: (0, 0)),
            tab, tab, tab, tab, row(LANES), row(LANES),
        ],
        out_specs=specs,
        compiler_params=_cparams(("arbitrary", "arbitrary")),
        name="mixer_prep",
    )(*([main3] * 9), conv_w, cos1, sin1, cos2, sin2, qnw, knw)


GDN_G = 8
GDN_NBATCH = 4
GDN_ROWS = GDN_HEADS * GDN_CHUNK


def _stack_heads(x):
    return jnp.concatenate([x[:, h * LANES:(h + 1) * LANES] for h in range(GDN_HEADS)], axis=0)


def _row_to_col(row, eye):
    return jnp.sum(jnp.where(eye, row, 0.0), axis=1, keepdims=True)


def _gdn_kernel(af_ref, ab_ref, bf_ref, bb_ref, alog_ref, dtb_ref, qf_ref, kf_ref, vf_ref, qb_ref, kb_ref, vb_ref,
                of_ref, ob_ref, state_ref, gc_ref, gt_ref, beta_ref, *, nbatch):
    blk = pl.program_id(1)
    n = GDN_ROWS
    c = GDN_CHUNK

    @pl.when(blk == 0)
    def _():
        state_ref[...] = jnp.zeros_like(state_ref)

    ri = lax.broadcasted_iota(jnp.int32, (n, n), 0)
    ci = lax.broadcasted_iota(jnp.int32, (n, n), 1)
    same = (ri // c) == (ci // c)
    eye = ri == ci
    ti = lax.broadcasted_iota(jnp.int32, (n, GDN_HEADS * LANES), 0)
    tj = lax.broadcasted_iota(jnp.int32, (n, GDN_HEADS * LANES), 1)
    tot_m = jnp.where((ti // c) == (tj // LANES), 1.0, 0.0)

    chains = []
    for d, (a_ref, b_ref, q_ref, k_ref, v_ref, o_ref) in enumerate(
            ((af_ref, bf_ref, qf_ref, kf_ref, vf_ref, of_ref), (ab_ref, bb_ref, qb_ref, kb_ref, vb_ref, ob_ref))):
        sgn = 1 - 2 * d
        after = same & ((ri - ci) * sgn > 0)
        incl = same & ((ri - ci) * sgn >= 0)
        cum_m = jnp.where(same & ((ci - ri) * sgn >= 0), 1.0, 0.0)
        for bi in range(nbatch):
            ch = d * nbatch + bi
            x = a_ref[0, bi] + dtb_ref[d]
            softplus = jnp.maximum(x, 0.0) + jnp.log(1.0 + jnp.exp(-jnp.abs(x)))
            g = -jnp.exp(alog_ref[d]) * softplus
            beta_ref[ch] = _sigmoid(b_ref[0, bi])
            gc_ref[ch] = jnp.dot(g, cum_m, precision=HI, preferred_element_type=F32)
            gt_ref[ch] = jnp.dot(g, tot_m, precision=HI, preferred_element_type=F32)
            chains.append((ch, d, bi, after, incl, q_ref, k_ref, v_ref, o_ref))

    def chunk(j, chain):
        ch, d, bi, after, incl, q_ref, k_ref, v_ref, o_ref = chain
        cc = j if d == 0 else GDN_G - 1 - j
        r0 = pl.multiple_of(cc * c, c)
        gc_row = gc_ref[ch, pl.ds(cc, 1), :]
        beta_row = beta_ref[ch, pl.ds(cc, 1), :]
        gt_row = gt_ref[ch, pl.ds(cc, 1), :]
        gc_col = _row_to_col(gc_row, eye)
        beta_col = _row_to_col(beta_row, eye)
        k_st = _stack_heads(k_ref[bi, pl.ds(r0, c), :]).astype(F32)
        q_st = _stack_heads(q_ref[bi, pl.ds(r0, c), :]).astype(F32)
        v_st = _stack_heads(v_ref[bi, pl.ds(r0, c), :]).astype(F32)
        egc = jnp.exp(gc_col)
        decay = jnp.exp(jnp.minimum(gc_col - gc_row, 0.0))
        kb = k_st * beta_col
        k_bf = k_st.astype(BF16)
        kk = lax.dot_general(kb.astype(BF16), k_bf, (((1,), (1,)), ((), ())), preferred_element_type=F32)
        qk = lax.dot_general(q_st.astype(BF16), k_bf, (((1,), (1,)), ((), ())), preferred_element_type=F32)
        yield
        neg_a = jnp.where(after, -(kk * decay), 0.0)
        t_m = jnp.where(eye, 1.0, 0.0) + neg_a
        p_m = neg_a
        for _ in range(int(math.log2(c)) - 1):
            p_bf = p_m.astype(BF16)
            p_m = jnp.dot(p_bf, p_bf, preferred_element_type=F32)
            yield
            t_m = t_m + jnp.dot(t_m.astype(BF16), p_m.astype(BF16), preferred_element_type=F32)
            yield
        rhs = jnp.concatenate([v_st * beta_col, kb * egc], axis=1).astype(BF16)
        sol = jnp.dot(t_m.astype(BF16), rhs, preferred_element_type=F32)
        yield
        u_st, w_st = sol[:, :LANES], sol[:, LANES:]
        intra = jnp.where(incl, qk * decay, 0.0).astype(BF16)
        q_dec = (q_st * egc).astype(BF16)
        vn, oq = [], []
        w_bf = w_st.astype(BF16)
        for h in range(GDN_HEADS):
            rs = slice(h * c, (h + 1) * c)
            s_h = state_ref[ch * GDN_HEADS + h].astype(BF16)
            ws_qs = jnp.dot(jnp.concatenate([w_bf[rs], q_dec[rs]], axis=0), s_h, preferred_element_type=F32)
            vn.append(u_st[rs] - ws_qs[:c])
            oq.append(ws_qs[c:])
        yield
        vn_st = jnp.concatenate(vn, axis=0)
        o_st = jnp.concatenate(oq, axis=0) + jnp.dot(intra, vn_st.astype(BF16), preferred_element_type=F32)
        for h in range(GDN_HEADS):
            rs = slice(h * c, (h + 1) * c)
            gt_h = gt_row[:, h * LANES:(h + 1) * LANES]
            k_dec = (k_st[rs] * jnp.exp(gt_h[:, :1] - gc_col[rs])).astype(BF16)
            upd = lax.dot_general(k_dec, vn[h].astype(BF16), (((0,), (0,)), ((), ())), preferred_element_type=F32)
            state_ref[ch * GDN_HEADS + h] = state_ref[ch * GDN_HEADS + h] * jnp.exp(gt_h) + upd
            o_ref[bi, pl.ds(r0, c), h * LANES:(h + 1) * LANES] = o_st[rs]

    def step(j, carry):
        active = [chunk(j, chain) for chain in chains]
        while active:
            active = [g for g in active if next(g, active) is not active]
        return carry

    lax.fori_loop(0, GDN_G, step, 0)


def _gdn(a_rows, b_rows, alog_row, dtb_row, gq, gk, gv, *, nbatch):
    b, s, _ = gq.shape
    nb = s // (GDN_G * GDN_CHUNK)
    ts = GDN_G * GDN_CHUNK
    nchain = 2 * nbatch
    tok_f = pl.BlockSpec((nbatch, ts, 512), lambda bi, i: (bi, i, 0))
    tok_b = pl.BlockSpec((nbatch, ts, 512), lambda bi, i: (bi, nb - 1 - i, 0))
    rows_f = pl.BlockSpec((1, nbatch, GDN_G, GDN_ROWS), lambda bi, i: (0, bi, i, 0))
    rows_b = pl.BlockSpec((1, nbatch, GDN_G, GDN_ROWS), lambda bi, i: (1, bi, nb - 1 - i, 0))
    par = pl.BlockSpec((2, 1, GDN_ROWS), lambda bi, i: (0, 0, 0))
    return pl.pallas_call(
        functools.partial(_gdn_kernel, nbatch=nbatch),
        out_shape=(jax.ShapeDtypeStruct((b, s, 512), F32), jax.ShapeDtypeStruct((b, s, 512), F32)),
        grid=(b // nbatch, nb),
        in_specs=[rows_f, rows_b, rows_f, rows_b, par, par, tok_f, tok_f, tok_f, tok_b, tok_b, tok_b],
        out_specs=(tok_f, tok_b),
        scratch_shapes=[pltpu.VMEM((nchain * GDN_HEADS, GDN_DK, GDN_DV), F32),
                        pltpu.VMEM((nchain, GDN_G, GDN_ROWS), F32),
                        pltpu.VMEM((nchain, GDN_G, GDN_HEADS * LANES), F32),
                        pltpu.VMEM((nchain, GDN_G, GDN_ROWS), F32)],
        compiler_params=_cparams(("arbitrary", "arbitrary")),
        name="gdn_chunked",
    )(a_rows, a_rows, b_rows, b_rows, alog_row, dtb_row, gq, gk, gv, gq, gk, gv)


def _attn_kernel(q_ref, k0_ref, k1_ref, vt_ref, lam_ref, nw_ref, o_ref, acc_ref, *, mode, tk, lambda_init):
    s_len = k0_ref.shape[1]
    tq = q_ref.shape[1]
    rows = vt_ref.shape[1]
    dv = rows - VT_PAD
    half = LANES // 2
    q = q_ref[0]
    lane = lax.broadcasted_iota(jnp.int32, q.shape, 1)
    keep = (lane < half, lane >= half)
    stab = (lane == half, lane == 0)
    zero = jnp.zeros_like(q)
    krefs = (k0_ref, k1_ref)
    nchunks = s_len // tk
    dn = (((1,), (1,)), ((), ()))

    def kchunk(m, ci):
        return krefs[m][0, pl.ds(pl.multiple_of(ci * tk, tk), tk), :]

    def vchunk(ci):
        return vt_ref[0, :, pl.ds(pl.multiple_of(ci * tk, tk), tk)]

    qm, qa = [], []
    for m in range(2):
        qm.append(jnp.where(keep[m], q, zero))
        k_first = krefs[m][0, 0:min(ATTN_STAB_KEYS, s_len), :]
        mx = jnp.max(lax.dot_general(qm[m], k_first, dn, preferred_element_type=F32), axis=-1, keepdims=True)
        qa.append(jnp.where(stab[m], (-mx).astype(BF16), qm[m]))

    def fast(ci, acc):
        st = [lax.dot_general(kchunk(m, ci), qa[m], dn, preferred_element_type=F32) for m in range(2)]
        vt = vchunk(ci)
        return tuple(acc[m] + jnp.dot(vt, jnp.exp2(st[m]).astype(BF16), preferred_element_type=F32) for m in range(2))

    acc = lax.fori_loop(0, nchunks, fast, tuple(jnp.zeros((rows, tq), F32) for _ in range(2)))
    nonfinite = jnp.float32(0.0)
    for m in range(2):
        acc_ref[m] = acc[m]
        nonfinite = nonfinite + jnp.sum(jnp.where(jnp.isfinite(acc[m]), 0.0, 1.0))

    @pl.when(nonfinite > 0.0)
    def _():
        def slow(ci, carry):
            out = []
            vt = vchunk(ci)
            for m in range(2):
                m_i, a_i = carry[m]
                st = lax.dot_general(kchunk(m, ci), qm[m], dn, preferred_element_type=F32)
                m_new = jnp.maximum(m_i, jnp.max(st, axis=0, keepdims=True))
                p = jnp.exp2(st - m_new).astype(BF16)
                out.append((m_new, jnp.exp2(m_i - m_new) * a_i + jnp.dot(vt, p, preferred_element_type=F32)))
            return tuple(out)

        init = tuple((jnp.full((1, tq), -jnp.inf, F32), jnp.zeros((rows, tq), F32)) for _ in range(2))
        res = lax.fori_loop(0, nchunks, slow, init)
        for m in range(2):
            acc_ref[m] = res[m][1]

    o0 = acc_ref[0, 0:dv, :] / acc_ref[0, dv:dv + 1, :]
    o1 = acc_ref[1, 0:dv, :] / acc_ref[1, dv:dv + 1, :]
    if mode == "diff":
        lv = lam_ref[...]
        lam = (jnp.exp(jnp.sum(lv[0:1] * lv[1:2], axis=-1, keepdims=True))
               - jnp.exp(jnp.sum(lv[2:3] * lv[3:4], axis=-1, keepdims=True)) + lambda_init)
        ot = o0 - lam * o1
        ot = ot * lax.rsqrt(jnp.mean(ot * ot, axis=0, keepdims=True) + NORM_EPS) * nw_ref[...] * (1.0 - lambda_init)
    else:
        ot = jnp.concatenate([o0, o1], axis=0)
    o_ref[0] = ot.T.astype(o_ref.dtype)


def _attention(q, k_arr, vt_arr, lam_vecs, norm_w, *, mode, tq, tk, lambda_init=0.0):
    b, s, w = q.shape
    slabs = w // LANES
    if mode == "diff":
        rows = LANES + VT_PAD
        k_col = lambda p, m: 2 * p + m
        v_grp = lambda p: p
    else:
        rows = GQA_DH + VT_PAD
        k_col = lambda p, m: 2 * (p // 2) + m
        v_grp = lambda p: p // 2
    kspec = lambda m: pl.BlockSpec((1, s, LANES), lambda bi, p, i: (bi, 0, k_col(p, m)))
    return pl.pallas_call(
        functools.partial(_attn_kernel, mode=mode, tk=tk, lambda_init=lambda_init),
        out_shape=jax.ShapeDtypeStruct((b, s, w), BF16),
        grid=(b, slabs, s // tq),
        in_specs=[pl.BlockSpec((1, tq, LANES), lambda bi, p, i: (bi, i, p)),
                  kspec(0), kspec(1),
                  pl.BlockSpec((1, rows, s), lambda bi, p, i: (bi, v_grp(p), 0)),
                  pl.BlockSpec((4, DIFF_DQK), lambda bi, p, i: (0, 0)),
                  pl.BlockSpec((LANES, 1), lambda bi, p, i: (0, 0))],
        out_specs=pl.BlockSpec((1, tq, LANES), lambda bi, p, i: (bi, i, p)),
        scratch_shapes=[pltpu.VMEM((2, rows, tq), F32)],
        compiler_params=_cparams(("arbitrary", "arbitrary", "arbitrary")),
        name="attn_" + mode,
    )(q, k_arr, k_arr, vt_arr, lam_vecs, norm_w)


def _merge_kernel(of_ref, ob_ref, z_ref, g0_ref, g1_ref, g2_ref, yb_ref, yc_ref, x_ref,
                  wa_ref, wb_ref, wc_ref, wo_ref, gnw_ref, fnw_ref, rw_ref, rb_ref,
                  xo_ref, h_ref, id_ref, rwgt_ref):
    o = of_ref[...] + ob_ref[...]
    parts = []
    for h in range(GDN_HEADS):
        oh = o[:, h * LANES:(h + 1) * LANES]
        parts.append(oh * lax.rsqrt(jnp.mean(oh * oh, axis=-1, keepdims=True) + NORM_EPS) * gnw_ref[...])
    z = z_ref[...].astype(F32)
    ya = (jnp.concatenate(parts, axis=1) * (z * _sigmoid(z))).astype(BF16)
    merged = _sigmoid(g0_ref[...].astype(F32)) * jnp.dot(ya, wa_ref[...], preferred_element_type=F32)
    merged = merged + _sigmoid(g1_ref[...].astype(F32)) * jnp.dot(yb_ref[...], wb_ref[...], preferred_element_type=F32)
    merged = merged + _sigmoid(g2_ref[...].astype(F32)) * jnp.dot(yc_ref[...], wc_ref[...], preferred_element_type=F32)
    xn = x_ref[...] + jnp.dot(merged.astype(BF16), wo_ref[...], preferred_element_type=F32)
    xo_ref[...] = xn
    hf = xn * lax.rsqrt(jnp.mean(xn * xn, axis=-1, keepdims=True) + NORM_EPS) * fnw_ref[...]
    h_ref[...] = hf.astype(BF16)
    logits = _dot_split(hf, rw_ref[...]) + rb_ref[...]
    lane = lax.broadcasted_iota(jnp.int32, logits.shape, 1)
    big = jnp.int32(LANES)
    ninf = -jnp.inf
    glog = jnp.where(lane < N_GROUPS, logits, ninf)
    gmax = jnp.max(glog, axis=-1, keepdims=True)
    gidx = jnp.min(jnp.where(glog == gmax, lane, big), axis=-1, keepdims=True)
    gp = 1.0 / jnp.sum(jnp.exp(glog - gmax), axis=-1, keepdims=True)
    e = lane - N_GROUPS
    sel = (e >= 0) & (e < N_EXPERTS) & ((e // EXPERTS_PER_GROUP) == gidx)
    elog = jnp.where(sel, logits, ninf)
    m1 = jnp.max(elog, axis=-1, keepdims=True)
    i1 = jnp.min(jnp.where(elog == m1, lane, big), axis=-1, keepdims=True)
    elog2 = jnp.where(lane == i1, ninf, elog)
    m2 = jnp.max(elog2, axis=-1, keepdims=True)
    i2 = jnp.min(jnp.where(elog2 == m2, lane, big), axis=-1, keepdims=True)
    e2 = jnp.exp(m2 - m1)
    w1 = 1.0 / (1.0 + e2)
    w2 = e2 * w1
    id_ref[...] = jnp.where(lane == 0, i1 - N_GROUPS, jnp.where(lane == 1, i2 - N_GROUPS, 0))
    rwgt_ref[...] = jnp.where(lane == 0, gp * w1, jnp.where(lane == 1, gp * w2, 0.0))


def _merge(o_f, o_b, main2, yb, yc, x2d, wa, wb, wc, wo, gnw, fnw, rw, rb, *, tm):
    t, d = x2d.shape
    full = lambda shp: pl.BlockSpec(shp, lambda i: tuple(0 for _ in shp))
    return pl.pallas_call(
        _merge_kernel,
        out_shape=(jax.ShapeDtypeStruct((t, d), F32), jax.ShapeDtypeStruct((t, d), BF16),
                   jax.ShapeDtypeStruct((t, LANES), jnp.int32), jax.ShapeDtypeStruct((t, LANES), F32)),
        grid=(t // tm,),
        in_specs=[pl.BlockSpec((tm, 512), lambda i: (i, 0)),
                  pl.BlockSpec((tm, 512), lambda i: (i, 0)),
                  pl.BlockSpec((tm, 512), lambda i: (i, COL_Z // 512)),
                  pl.BlockSpec((tm, d), lambda i: (i, 0)),
                  pl.BlockSpec((tm, d), lambda i: (i, 1)),
                  pl.BlockSpec((tm, d), lambda i: (i, 2)),
                  pl.BlockSpec((tm, 512), lambda i: (i, 0)),
                  pl.BlockSpec((tm, 512), lambda i: (i, 0)),
                  pl.BlockSpec((tm, d), lambda i: (i, 0)),
                  full((512, d)), full((512, d)), full((512, d)), full((d, d)),
                  full((1, LANES)), full((1, d)), full((d, LANES)), full((1, LANES))],
        out_specs=(pl.BlockSpec((tm, d), lambda i: (i, 0)), pl.BlockSpec((tm, d), lambda i: (i, 0)),
                   pl.BlockSpec((tm, LANES), lambda i: (i, 0)), pl.BlockSpec((tm, LANES), lambda i: (i, 0))),
        compiler_params=_cparams(("arbitrary",)),
        name="merge_router",
    )(o_f, o_b, main2, main2, main2, main2, yb, yc, x2d, wa, wb, wc, wo, gnw, fnw, rw, rb)


def _expert_kernel(blk_e_ref, nused_ref, x_ref, w1_ref, w3_ref, w2_ref, o_ref, w1b_ref, w3b_ref, w2b_ref):
    i = pl.program_id(0)

    @pl.when((i == 0) | (blk_e_ref[i] != blk_e_ref[jnp.maximum(i - 1, 0)]))
    def _():
        w1b_ref[...] = w1_ref[0, 0].astype(BF16)
        w3b_ref[...] = w3_ref[0, 0].astype(BF16)
        w2b_ref[...] = w2_ref[0, 0].astype(BF16)

    @pl.when(i < nused_ref[0])
    def _():
        x = x_ref[...]
        a = jnp.dot(x, w1b_ref[...], preferred_element_type=F32)
        u = jnp.dot(x, w3b_ref[...], preferred_element_type=F32)
        hmid = (a * _sigmoid(a) * u).astype(BF16)
        o_ref[...] = jnp.dot(hmid, w2b_ref[...], preferred_element_type=F32).astype(o_ref.dtype)

    @pl.when(i >= nused_ref[0])
    def _():
        o_ref[...] = jnp.zeros_like(o_ref)


def _experts(blk_e, nused, xb, w1, w3, w2, *, layer):
    p_len, d = xb.shape
    ff = w1.shape[3]
    nblk = p_len // MOE_BLOCK
    return pl.pallas_call(
        _expert_kernel,
        out_shape=jax.ShapeDtypeStruct((p_len, d), BF16),
        grid_spec=pltpu.PrefetchScalarGridSpec(
            num_scalar_prefetch=2,
            grid=(nblk,),
            in_specs=[pl.BlockSpec((MOE_BLOCK, d), lambda i, be, nu: (i, 0)),
                      pl.BlockSpec((1, 1, d, ff), lambda i, be, nu: (layer, be[i], 0, 0)),
                      pl.BlockSpec((1, 1, d, ff), lambda i, be, nu: (layer, be[i], 0, 0)),
                      pl.BlockSpec((1, 1, ff, d), lambda i, be, nu: (layer, be[i], 0, 0))],
            out_specs=pl.BlockSpec((MOE_BLOCK, d), lambda i, be, nu: (i, 0)),
            scratch_shapes=[pltpu.VMEM((d, ff), BF16), pltpu.VMEM((d, ff), BF16), pltpu.VMEM((ff, d), BF16)],
        ),
        compiler_params=_cparams(("arbitrary",)),
        name="expert_mlp",
    )(blk_e, nused, xb, w1, w3, w2)


def _combine_kernel(x_ref, y0_ref, y1_ref, w_ref, nw_ref, o_ref, *, final):
    w = w_ref[...]
    x = x_ref[...] + w[:, 0:1] * y0_ref[...].astype(F32) + w[:, 1:2] * y1_ref[...].astype(F32)
    if final:
        x = x * lax.rsqrt(jnp.mean(x * x, axis=-1, keepdims=True) + NORM_EPS) * nw_ref[...]
    o_ref[...] = x


def _combine(x2d, y0, y1, wts, norm_w, *, final, tm):
    t, d = x2d.shape
    tile = pl.BlockSpec((tm, d), lambda i: (i, 0))
    return pl.pallas_call(
        functools.partial(_combine_kernel, final=final),
        out_shape=jax.ShapeDtypeStruct((t, d), F32),
        grid=(t // tm,),
        in_specs=[tile, tile, tile, pl.BlockSpec((tm, LANES), lambda i: (i, 0)), pl.BlockSpec((1, d), lambda i: (0, 0))],
        out_specs=tile,
        compiler_params=_cparams(("arbitrary",)),
        name="moe_combine",
    )(x2d, y0, y1, wts, norm_w.reshape(1, d))


def _rope_tables(pos, dim):
    inv = 1.0 / (ROPE_THETA ** (jnp.arange(0, dim, 2, dtype=F32) / dim))
    ang = pos.astype(F32)[:, None] * inv[None, :]
    ang = jnp.concatenate([ang, ang], axis=-1)
    return jnp.cos(ang), jnp.sin(ang)


def _signed_sin(sin):
    half = sin.shape[-1] // 2
    return jnp.concatenate([-sin[:, :half], sin[:, half:]], axis=-1)


def _layout_w_in(w):
    o = 0
    parts = {}
    for name, size in (("qkv", 1536), ("z", 512), ("b", 8), ("a", 8), ("dq", 512), ("dk", 512), ("dv", 512),
                       ("cq", 512), ("ck", 128), ("cv", 128), ("gate", 3072)):
        parts[name] = w[:, o:o + size]
        o += size
    swap = lambda m: jnp.concatenate([m[:, 64:], m[:, :64]], axis=1)
    ba = jnp.concatenate([parts["b"], parts["a"], jnp.zeros((w.shape[0], LANES - 16), w.dtype)], axis=1)
    main = jnp.concatenate([parts["gate"], parts["qkv"], parts["z"], parts["dq"], parts["dk"], parts["dv"],
                            parts["cq"], parts["ck"], swap(parts["ck"]), parts["cv"], ba], axis=1)
    return main.astype(BF16)


def _rows_layout(t, bsz, s):
    nc = s // GDN_CHUNK
    t = t.reshape(bsz, nc, GDN_CHUNK, 2, GDN_HEADS)
    return jnp.transpose(t, (3, 0, 1, 4, 2)).reshape(2, bsz, nc, GDN_ROWS)


def _moe_dispatch(ids, t):
    a = t * TOPK
    p_len = ((a + N_EXPERTS * (MOE_BLOCK - 1) + MOE_BLOCK - 1) // MOE_BLOCK) * MOE_BLOCK
    n_blocks = p_len // MOE_BLOCK
    flat_e = ids.reshape(-1)
    iota_a = jnp.arange(a, dtype=jnp.int32)
    skey = jnp.sort(flat_e * a + iota_a)
    order = skey % a
    se = skey // a
    experts = jnp.arange(N_EXPERTS, dtype=jnp.int32)
    counts = jnp.sum((flat_e[:, None] == experts[None, :]).astype(jnp.int32), axis=0)
    start = jnp.cumsum(counts) - counts
    pcounts = ((counts + MOE_BLOCK - 1) // MOE_BLOCK) * MOE_BLOCK
    pend = jnp.cumsum(pcounts)
    pstart = pend - pcounts
    dest_sorted = pstart[se] + (iota_a - start[se])
    blk_first = jnp.arange(n_blocks, dtype=jnp.int32) * MOE_BLOCK
    blk_e = jnp.minimum(jnp.sum((pend[None, :] <= blk_first[:, None]).astype(jnp.int32), axis=1), N_EXPERTS - 1)
    row = jnp.arange(p_len, dtype=jnp.int32)
    row_e = jnp.repeat(blk_e, MOE_BLOCK)
    j = row - pstart[row_e]
    valid = j < counts[row_e]
    tok_buf = jnp.where(valid, order[jnp.minimum(start[row_e] + j, a - 1)] // TOPK, row % t)
    _, dest = lax.sort((order, dest_sorted), num_keys=1)
    nused = (pend[-1] // MOE_BLOCK).astype(jnp.int32).reshape(1)
    return tok_buf, dest.reshape(t, TOPK), blk_e, nused


def kernel(x, attn_norm_w, w_in, gdn_conv_w, gdn_a_log, gdn_dt_bias, gdn_norm_w, diff_lambda, diff_norm_w,
           gqa_q_norm_w, gqa_k_norm_w, w_branch_a, w_branch_b, w_branch_c, w_out, ffn_norm_w,
           router_group_w, router_group_b, router_expert_w, router_expert_b,
           expert_w_gate, expert_w_up, expert_w_down, final_norm_w):
    bsz, s, d = x.shape
    t = bsz * s
    depth = w_in.shape[0]
    tm = min(512, t)
    ts = min(512, s)

    rows = s // GRID_W
    row = jnp.broadcast_to(jnp.arange(rows)[:, None], (rows, GRID_W)).reshape(s)
    col = jnp.broadcast_to(jnp.arange(GRID_W)[None, :], (rows, GRID_W)).reshape(s)
    c1, s1 = _rope_tables(jnp.arange(s), DIFF_DQK)
    cr, sr = _rope_tables(row, GQA_DH // 2)
    cc, sc = _rope_tables(col, GQA_DH // 2)
    cos1 = jnp.tile(c1, (1, 2))
    sin1 = jnp.tile(_signed_sin(s1), (1, 2))
    cos2 = jnp.tile(jnp.concatenate([cr, cc], axis=-1), (1, 2))
    sin2 = jnp.tile(jnp.concatenate([_signed_sin(sr), _signed_sin(sc)], axis=-1), (1, 2))

    x2 = x.reshape(t, d)
    for l in range(depth):
        lambda_init = 0.8 - 0.6 * math.exp(-0.3 * l)
        main2, ba = _norm_proj(x2, attn_norm_w[l], _layout_w_in(w_in[l]), tm=tm, tn=N_MAIN // 2)
        main3 = main2.reshape(bsz, s, N_MAIN)

        conv_w = jnp.concatenate([gdn_conv_w[l], jnp.zeros((8 - GDN_CONV, gdn_conv_w.shape[2]), F32)], axis=0)
        qnw = jnp.tile(gqa_q_norm_w[l], 2).reshape(1, LANES)
        knw = jnp.tile(gqa_k_norm_w[l], 2).reshape(1, LANES)
        gq, gk, gv, dq, dk, dv, cq, ck, cv = _prep(main3, conv_w, cos1, sin1, cos2, sin2, qnw, knw, ts=ts)

        b_rows = _rows_layout(ba[:, 0:8], bsz, s)
        a_rows = _rows_layout(ba[:, 8:16], bsz, s)
        alog_row = jnp.repeat(gdn_a_log[l], GDN_CHUNK, axis=1).reshape(2, 1, GDN_ROWS)
        dtb_row = jnp.repeat(gdn_dt_bias[l], GDN_CHUNK, axis=1).reshape(2, 1, GDN_ROWS)
        o_f, o_b = _gdn(a_rows, b_rows, alog_row, dtb_row, gq, gk, gv, nbatch=GDN_NBATCH if bsz % GDN_NBATCH == 0 else 1)

        nw_diff = diff_norm_w[l].reshape(LANES, 1)
        yb = _attention(dq, dk, dv, diff_lambda[l], nw_diff, mode="diff",
                        tq=min(ATTN_TQ, s), tk=min(ATTN_TK, s), lambda_init=lambda_init)
        yc = _attention(cq, ck, cv, diff_lambda[l], nw_diff, mode="gqa",
                        tq=min(ATTN_TQ, s), tk=min(ATTN_TK, s))

        rw = jnp.concatenate([router_group_w[l], router_expert_w[l],
                              jnp.zeros((d, LANES - N_GROUPS - N_EXPERTS), F32)], axis=1)
        rb = jnp.concatenate([router_group_b[l], router_expert_b[l],
                              jnp.zeros((LANES - N_GROUPS - N_EXPERTS,), F32)]).reshape(1, LANES)
        x2, h2, ids, wts = _merge(o_f.reshape(t, 512), o_b.reshape(t, 512), main2, yb.reshape(t, 512), yc.reshape(t, 512), x2,
                                  w_branch_a[l].astype(BF16), w_branch_b[l].astype(BF16),
                                  w_branch_c[l].astype(BF16), w_out[l].astype(BF16),
                                  gdn_norm_w[l].reshape(1, LANES), ffn_norm_w[l].reshape(1, d), rw, rb, tm=tm)

        tok_buf, dest, blk_e, nused = _moe_dispatch(ids[:, :TOPK], t)
        yblk = _experts(blk_e, nused, h2[tok_buf], expert_w_gate, expert_w_up, expert_w_down, layer=l)
        x2 = _combine(x2, yblk[dest[:, 0]], yblk[dest[:, 1]], wts, final_norm_w, final=(l == depth - 1), tm=tm)

    return x2.reshape(bsz, s, d)
```

```python
import functools
import math

import jax
import jax.numpy as jnp
from jax import lax
from jax.experimental import pallas as pl
from jax.experimental.pallas import tpu as pltpu

GRID_W = 64
ROPE_THETA = 10000.0
NORM_EPS = 1e-6
GDN_HEADS = 4
GDN_DK = 128
GDN_DV = 128
GDN_CONV = 5
GDN_CHUNK = 64
DIFF_HEADS = 4
DIFF_DQK = 64
GQA_HEADS = 8
GQA_KV = 2
GQA_DH = 64
N_GROUPS = 4
EXPERTS_PER_GROUP = 8
N_EXPERTS = N_GROUPS * EXPERTS_PER_GROUP
TOPK = 2
MOE_BLOCK = 256

LANES = 128
VMEM_LIMIT = 56 * 1024 * 1024

COL_GATE = 0
COL_QKV = 3072
COL_Z = 4608
COL_DQ = 5120
COL_DK = 5632
COL_DV = 6144
COL_CQ = 6656
COL_CK = 7168
COL_CV = 7424
COL_BA = 7552
N_MAIN = 7680

LOG2E = math.log2(math.e)
ATTN_TQ = 1024
ATTN_TK = 1024
ATTN_STAB_KEYS = 256
VT_PAD = 16

HI = lax.Precision.HIGHEST
F32 = jnp.float32
BF16 = jnp.bfloat16


def _cparams(sem):
    return pltpu.CompilerParams(dimension_semantics=sem, vmem_limit_bytes=VMEM_LIMIT)


def _sigmoid(x):
    return 1.0 / (1.0 + jnp.exp(-x))


def _dot_split(a, w):
    a_hi = a.astype(BF16)
    a_lo = (a - a_hi.astype(F32)).astype(BF16)
    w_hi = w.astype(BF16)
    w_lo = (w - w_hi.astype(F32)).astype(BF16)
    return (jnp.dot(a_hi, w_hi, preferred_element_type=F32) + jnp.dot(a_lo, w_hi, preferred_element_type=F32)
            + jnp.dot(a_hi, w_lo, preferred_element_type=F32))


def _norm_proj_kernel(x_ref, nw_ref, w_ref, o_ref, tail_ref):
    x = x_ref[...]
    h = x * lax.rsqrt(jnp.mean(x * x, axis=-1, keepdims=True) + NORM_EPS) * nw_ref[...]
    acc = jnp.dot(h.astype(BF16), w_ref[...], preferred_element_type=F32)
    o_ref[...] = acc.astype(o_ref.dtype)
    tail_ref[0] = acc[:, acc.shape[1] - LANES:]


def _norm_proj(x2d, norm_w, w, *, tm, tn):
    t, d = x2d.shape
    n = w.shape[1]
    main, tails = pl.pallas_call(
        _norm_proj_kernel,
        out_shape=(jax.ShapeDtypeStruct((t, n), BF16), jax.ShapeDtypeStruct((n // tn, t, LANES), F32)),
        grid=(n // tn, t // tm),
        in_specs=[pl.BlockSpec((tm, d), lambda j, i: (i, 0)),
                  pl.BlockSpec((1, d), lambda j, i: (0, 0)),
                  pl.BlockSpec((d, tn), lambda j, i: (0, j))],
        out_specs=(pl.BlockSpec((tm, tn), lambda j, i: (i, j)), pl.BlockSpec((1, tm, LANES), lambda j, i: (j, i, 0))),
        compiler_params=_cparams(("arbitrary", "arbitrary")),
        name="norm_proj",
    )(x2d, norm_w.reshape(1, d), w)
    return main, tails[n // tn - 1]


HALO = 16


def _rot_half(x, half):
    lane = lax.broadcasted_iota(jnp.int32, x.shape, 1)
    first = (lane % (2 * half)) < half
    return jnp.where(first, pltpu.roll(x, LANES - half, 1), pltpu.roll(x, half, 1))


def _group_sumsq(x, width):
    x2 = x * x
    if width == LANES:
        return jnp.sum(x2, axis=-1, keepdims=True)
    lane = lax.broadcasted_iota(jnp.int32, x.shape, 1)
    lo = lane < width
    s_lo = jnp.sum(jnp.where(lo, x2, 0.0), axis=-1, keepdims=True)
    s_hi = jnp.sum(jnp.where(lo, 0.0, x2), axis=-1, keepdims=True)
    return jnp.where(lo, s_lo, s_hi)


def _aug_slab(x, m):
    lane = lax.broadcasted_iota(jnp.int32, x.shape, 1)
    half = LANES // 2
    keep = (lane < half) if m == 0 else (lane >= half)
    one = jnp.where(lane == (1 - m) * half, 1.0, 0.0).astype(x.dtype)
    return jnp.where(keep, x, one)


def _prep_kernel(qkv_ref, prev_ref, next_ref, dq_ref, dk_ref, dv_ref, cq_ref, ck_ref, cv_ref,
                 convw_ref, cos1_ref, sin1_ref, cos2_ref, sin2_ref, qnw_ref, knw_ref,
                 gq_ref, gk_ref, gv_ref, dqo_ref, dko_ref, dvo_ref, cqo_ref, cko_ref, cvo_ref, ext_ref, *, ts):
    i = pl.program_id(1)
    n = pl.num_programs(1)
    ext_ref[0:HALO, :] = jnp.where(i > 0, prev_ref[0].astype(F32), 0.0)
    ext_ref[HALO:HALO + ts, :] = qkv_ref[0].astype(F32)
    ext_ref[HALO + ts:, :] = jnp.where(i < n - 1, next_ref[0].astype(F32), 0.0)
    pad = GDN_CONV // 2
    acc = jnp.zeros((ts, qkv_ref.shape[2]), F32)
    for j in range(GDN_CONV):
        off = HALO - pad + j
        acc = acc + ext_ref[off:off + ts, :] * convw_ref[j:j + 1, :]
    act = acc * _sigmoid(acc)
    nqk = GDN_HEADS * GDN_DK
    for h in range(GDN_HEADS):
        sl = slice(h * GDN_DK, (h + 1) * GDN_DK)
        qh = act[:, sl]
        gq_ref[0, :, sl] = (qh * lax.rsqrt(_group_sumsq(qh, LANES) + NORM_EPS) * (GDN_DK ** -0.5)).astype(BF16)
        kh = act[:, nqk + h * GDN_DK: nqk + (h + 1) * GDN_DK]
        gk_ref[0, :, sl] = (kh * lax.rsqrt(_group_sumsq(kh, LANES) + NORM_EPS)).astype(BF16)
    gv_ref[0] = act[:, 2 * nqk:].astype(BF16)
    cos1, sin1 = cos1_ref[...], sin1_ref[...]
    ones_rows = jnp.where(lax.broadcasted_iota(jnp.int32, (VT_PAD, ts), 0) == 0, 1.0, 0.0).astype(BF16)
    for p in range(DIFF_HEADS):
        sl = slice(p * LANES, (p + 1) * LANES)
        xq = dq_ref[0, :, sl].astype(F32)
        dqo_ref[0, :, sl] = ((xq * cos1 + _rot_half(xq, DIFF_DQK // 2) * sin1) * (DIFF_DQK ** -0.5 * LOG2E)).astype(BF16)
        xk = dk_ref[0, :, sl].astype(F32)
        xk = (xk * cos1 + _rot_half(xk, DIFF_DQK // 2) * sin1).astype(BF16)
        for m in range(2):
            dko_ref[0, :, (2 * p + m) * LANES:(2 * p + m + 1) * LANES] = _aug_slab(xk, m)
        r0 = p * (LANES + VT_PAD)
        dvo_ref[0, r0:r0 + LANES, :] = dv_ref[0, :, sl].astype(F32).T.astype(BF16)
        dvo_ref[0, r0 + LANES:r0 + LANES + VT_PAD, :] = ones_rows
    cos2, sin2 = cos2_ref[...], sin2_ref[...]
    for p in range(GQA_HEADS * GQA_DH // LANES):
        sl = slice(p * LANES, (p + 1) * LANES)
        xq = cq_ref[0, :, sl].astype(F32)
        xq = xq * lax.rsqrt(_group_sumsq(xq, GQA_DH) * (1.0 / GQA_DH) + NORM_EPS) * qnw_ref[...]
        cqo_ref[0, :, sl] = ((xq * cos2 + _rot_half(xq, GQA_DH // 4) * sin2) * (GQA_DH ** -0.5 * LOG2E)).astype(BF16)
    for p in range(2):
        sl = slice(p * LANES, (p + 1) * LANES)
        xk = ck_ref[0, :, sl].astype(F32)
        xk = xk * lax.rsqrt(_group_sumsq(xk, GQA_DH) * (1.0 / GQA_DH) + NORM_EPS) * knw_ref[...]
        xk = (xk * cos2 + _rot_half(xk, GQA_DH // 4) * sin2).astype(BF16)
        for m in range(2):
            c = p if m == 0 else 1 - p
            cko_ref[0, :, (2 * c + m) * LANES:(2 * c + m + 1) * LANES] = _aug_slab(xk, m)
    vt = cv_ref[0].astype(F32).T.astype(BF16)
    for c in range(GQA_KV):
        r0 = c * (GQA_DH + VT_PAD)
        cvo_ref[0, r0:r0 + GQA_DH, :] = vt[c * GQA_DH:(c + 1) * GQA_DH]
        cvo_ref[0, r0 + GQA_DH:r0 + GQA_DH + VT_PAD, :] = ones_rows


def _prep(main3, conv_w, cos1, sin1, cos2, sin2, qnw, knw, *, ts):
    b, s, _ = main3.shape
    nt = s // ts
    hb = ts // HALO
    last = s // HALO - 1
    row = lambda w: pl.BlockSpec((1, w), lambda bi, i: (0, 0))
    tab = pl.BlockSpec((ts, LANES), lambda bi, i: (i, 0))
    col = lambda w, off: pl.BlockSpec((1, ts, w), lambda bi, i: (bi, i, off // w))
    out = lambda w: pl.BlockSpec((1, ts, w), lambda bi, i: (bi, i, 0))
    outs = [("tok", 512), ("tok", 512), ("tok", 512), ("tok", 512), ("tok", 1024),
            ("rows", DIFF_HEADS * (LANES + VT_PAD)), ("tok", 512), ("tok", 512), ("rows", GQA_KV * (GQA_DH + VT_PAD))]
    specs = tuple(out(w) if kind == "tok" else pl.BlockSpec((1, w, ts), lambda bi, i: (bi, 0, i)) for kind, w in outs)
    shapes = tuple(jax.ShapeDtypeStruct((b, s, w) if kind == "tok" else (b, w, s), BF16) for kind, w in outs)
    return pl.pallas_call(
        functools.partial(_prep_kernel, ts=ts),
        out_shape=shapes,
        grid=(b, nt),
        in_specs=[
            col(1536, COL_QKV),
            pl.BlockSpec((1, HALO, 1536), lambda bi, i: (bi, jnp.maximum(i * hb - 1, 0), COL_QKV // 1536)),
            pl.BlockSpec((1, HALO, 1536), lambda bi, i: (bi, jnp.minimum((i + 1) * hb, last), COL_QKV // 1536)),
            col(512, COL_DQ), col(512, COL_DK), col(512, COL_DV), col(512, COL_CQ), col(256, COL_CK), col(128, COL_CV),
            pl.BlockSpec((8, 1536), lambda bi, i: (0, 0)),
            tab, tab, tab, tab, row(LANES), row(LANES),
        ],
        out_specs=specs,
        scratch_shapes=[pltpu.VMEM((ts + 2 * HALO, 1536), F32)],
        compiler_params=_cparams(("arbitrary", "arbitrary")),
        name="mixer_prep",
    )(*([main3] * 9), conv_w, cos1, sin1, cos2, sin2, qnw, knw)


GDN_G = 8
GDN_NBATCH = 4
GDN_ROWS = GDN_HEADS * GDN_CHUNK


def _stack_heads(x):
    return jnp.concatenate([x[:, h * LANES:(h + 1) * LANES] for h in range(GDN_HEADS)], axis=0)


def _row_to_col(row, eye):
    return jnp.sum(jnp.where(eye, row, 0.0), axis=1, keepdims=True)


def _gdn_kernel(af_ref, ab_ref, bf_ref, bb_ref, alog_ref, dtb_ref, qf_ref, kf_ref, vf_ref, qb_ref, kb_ref, vb_ref,
                of_ref, ob_ref, state_ref, gc_ref, gt_ref, beta_ref, *, nbatch):
    blk = pl.program_id(1)
    n = GDN_ROWS
    c = GDN_CHUNK

    @pl.when(blk == 0)
    def _():
        state_ref[...] = jnp.zeros_like(state_ref)

    ri = lax.broadcasted_iota(jnp.int32, (n, n), 0)
    ci = lax.broadcasted_iota(jnp.int32, (n, n), 1)
    same = (ri // c) == (ci // c)
    eye = ri == ci
    ti = lax.broadcasted_iota(jnp.int32, (n, GDN_HEADS * LANES), 0)
    tj = lax.broadcasted_iota(jnp.int32, (n, GDN_HEADS * LANES), 1)
    tot_m = jnp.where((ti // c) == (tj // LANES), 1.0, 0.0)

    chains = []
    for d, (a_ref, b_ref, q_ref, k_ref, v_ref, o_ref) in enumerate(
            ((af_ref, bf_ref, qf_ref, kf_ref, vf_ref, of_ref), (ab_ref, bb_ref, qb_ref, kb_ref, vb_ref, ob_ref))):
        sgn = 1 - 2 * d
        after = same & ((ri - ci) * sgn > 0)
        incl = same & ((ri - ci) * sgn >= 0)
        cum_m = jnp.where(same & ((ci - ri) * sgn >= 0), 1.0, 0.0)
        for bi in range(nbatch):
            ch = d * nbatch + bi
            x = a_ref[0, bi] + dtb_ref[d]
            softplus = jnp.maximum(x, 0.0) + jnp.log(1.0 + jnp.exp(-jnp.abs(x)))
            g = -jnp.exp(alog_ref[d]) * softplus
            beta_ref[ch] = _sigmoid(b_ref[0, bi])
            gc_ref[ch] = jnp.dot(g, cum_m, precision=HI, preferred_element_type=F32)
            gt_ref[ch] = jnp.dot(g, tot_m, precision=HI, preferred_element_type=F32)
            chains.append((ch, d, bi, after, incl, q_ref, k_ref, v_ref, o_ref))

    def chunk(j, chain):
        ch, d, bi, after, incl, q_ref, k_ref, v_ref, o_ref = chain
        cc = j if d == 0 else GDN_G - 1 - j
        r0 = pl.multiple_of(cc * c, c)
        gc_row = gc_ref[ch, pl.ds(cc, 1), :]
        beta_row = beta_ref[ch, pl.ds(cc, 1), :]
        gt_row = gt_ref[ch, pl.ds(cc, 1), :]
        gc_col = _row_to_col(gc_row, eye)
        beta_col = _row_to_col(beta_row, eye)
        k_st = _stack_heads(k_ref[bi, pl.ds(r0, c), :]).astype(F32)
        q_st = _stack_heads(q_ref[bi, pl.ds(r0, c), :]).astype(F32)
        v_st = _stack_heads(v_ref[bi, pl.ds(r0, c), :]).astype(F32)
        egc = jnp.exp(gc_col)
        decay = jnp.exp(jnp.minimum(gc_col - gc_row, 0.0))
        kb = k_st * beta_col
        k_bf = k_st.astype(BF16)
        kk = lax.dot_general(kb.astype(BF16), k_bf, (((1,), (1,)), ((), ())), preferred_element_type=F32)
        qk = lax.dot_general(q_st.astype(BF16), k_bf, (((1,), (1,)), ((), ())), preferred_element_type=F32)
        yield
        neg_a = jnp.where(after, -(kk * decay), 0.0)
        t_m = jnp.where(eye, 1.0, 0.0) + neg_a
        p_m = neg_a
        for _ in range(int(math.log2(c)) - 1):
            p_bf = p_m.astype(BF16)
            p_m = jnp.dot(p_bf, p_bf, preferred_element_type=F32)
            yield
            t_m = t_m + jnp.dot(t_m.astype(BF16), p_m.astype(BF16), preferred_element_type=F32)
            yield
        rhs = jnp.concatenate([v_st * beta_col, kb * egc], axis=1).astype(BF16)
        sol = jnp.dot(t_m.astype(BF16), rhs, preferred_element_type=F32)
        yield
        u_st, w_st = sol[:, :LANES], sol[:, LANES:]
        intra = jnp.where(incl, qk * decay, 0.0).astype(BF16)
        q_dec = (q_st * egc).astype(BF16)
        vn, oq = [], []
        w_bf = w_st.astype(BF16)
        for h in range(GDN_HEADS):
            rs = slice(h * c, (h + 1) * c)
            s_h = state_ref[ch * GDN_HEADS + h].astype(BF16)
            ws_qs = jnp.dot(jnp.concatenate([w_bf[rs], q_dec[rs]], axis=0), s_h, preferred_element_type=F32)
            vn.append(u_st[rs] - ws_qs[:c])
            oq.append(ws_qs[c:])
        yield
        vn_st = jnp.concatenate(vn, axis=0)
        o_st = jnp.concatenate(oq, axis=0) + jnp.dot(intra, vn_st.astype(BF16), preferred_element_type=F32)
        for h in range(GDN_HEADS):
            rs = slice(h * c, (h + 1) * c)
            gt_h = gt_row[:, h * LANES:(h + 1) * LANES]
            k_dec = (k_st[rs] * jnp.exp(gt_h[:, :1] - gc_col[rs])).astype(BF16)
            upd = lax.dot_general(k_dec, vn[h].astype(BF16), (((0,), (0,)), ((), ())), preferred_element_type=F32)
            state_ref[ch * GDN_HEADS + h] = state_ref[ch * GDN_HEADS + h] * jnp.exp(gt_h) + upd
            o_ref[bi, pl.ds(r0, c), h * LANES:(h + 1) * LANES] = o_st[rs]

    def step(j, carry):
        active = [chunk(j, chain) for chain in chains]
        while active:
            active = [g for g in active if next(g, active) is not active]
        return carry

    lax.fori_loop(0, GDN_G, step, 0)


def _gdn(a_rows, b_rows, alog_row, dtb_row, gq, gk, gv, *, nbatch):
    b, s, _ = gq.shape
    nb = s // (GDN_G * GDN_CHUNK)
    ts = GDN_G * GDN_CHUNK
    nchain = 2 * nbatch
    tok_f = pl.BlockSpec((nbatch, ts, 512), lambda bi, i: (bi, i, 0))
    tok_b = pl.BlockSpec((nbatch, ts, 512), lambda bi, i: (bi, nb - 1 - i, 0))
    rows_f = pl.BlockSpec((1, nbatch, GDN_G, GDN_ROWS), lambda bi, i: (0, bi, i, 0))
    rows_b = pl.BlockSpec((1, nbatch, GDN_G, GDN_ROWS), lambda bi, i: (1, bi, nb - 1 - i, 0))
    par = pl.BlockSpec((2, 1, GDN_ROWS), lambda bi, i: (0, 0, 0))
    return pl.pallas_call(
        functools.partial(_gdn_kernel, nbatch=nbatch),
        out_shape=(jax.ShapeDtypeStruct((b, s, 512), F32), jax.ShapeDtypeStruct((b, s, 512), F32)),
        grid=(b // nbatch, nb),
        in_specs=[rows_f, rows_b, rows_f, rows_b, par, par, tok_f, tok_f, tok_f, tok_b, tok_b, tok_b],
        out_specs=(tok_f, tok_b),
        scratch_shapes=[pltpu.VMEM((nchain * GDN_HEADS, GDN_DK, GDN_DV), F32),
                        pltpu.VMEM((nchain, GDN_G, GDN_ROWS), F32),
                        pltpu.VMEM((nchain, GDN_G, GDN_HEADS * LANES), F32),
                        pltpu.VMEM((nchain, GDN_G, GDN_ROWS), F32)],
        compiler_params=_cparams(("arbitrary", "arbitrary")),
        name="gdn_chunked",
    )(a_rows, a_rows, b_rows, b_rows, alog_row, dtb_row, gq, gk, gv, gq, gk, gv)


def _attn_kernel(q_ref, k0_ref, k1_ref, vt_ref, lam_ref, nw_ref, o_ref, acc_ref, *, mode, tk, lambda_init):
    s_len = k0_ref.shape[1]
    tq = q_ref.shape[1]
    rows = vt_ref.shape[1]
    dv = rows - VT_PAD
    half = LANES // 2
    q = q_ref[0]
    lane = lax.broadcasted_iota(jnp.int32, q.shape, 1)
    keep = (lane < half, lane >= half)
    stab = (lane == half, lane == 0)
    zero = jnp.zeros_like(q)
    krefs = (k0_ref, k1_ref)
    nchunks = s_len // tk
    dn = (((1,), (1,)), ((), ()))

    def kchunk(m, ci):
        return krefs[m][0, pl.ds(pl.multiple_of(ci * tk, tk), tk), :]

    def vchunk(ci):
        return vt_ref[0, :, pl.ds(pl.multiple_of(ci * tk, tk), tk)]

    qm, qa = [], []
    for m in range(2):
        qm.append(jnp.where(keep[m], q, zero))
        k_first = krefs[m][0, 0:min(ATTN_STAB_KEYS, s_len), :]
        mx = jnp.max(lax.dot_general(qm[m], k_first, dn, preferred_element_type=F32), axis=-1, keepdims=True)
        qa.append(jnp.where(stab[m], (-mx).astype(BF16), qm[m]))

    def fast(ci, acc):
        st = [lax.dot_general(kchunk(m, ci), qa[m], dn, preferred_element_type=F32) for m in range(2)]
        vt = vchunk(ci)
        return tuple(acc[m] + jnp.dot(vt, jnp.exp2(st[m]).astype(BF16), preferred_element_type=F32) for m in range(2))

    acc = lax.fori_loop(0, nchunks, fast, tuple(jnp.zeros((rows, tq), F32) for _ in range(2)))
    nonfinite = jnp.float32(0.0)
    for m in range(2):
        acc_ref[m] = acc[m]
        nonfinite = nonfinite + jnp.sum(jnp.where(jnp.isfinite(acc[m]), 0.0, 1.0))
        nonfinite = nonfinite + jnp.sum(jnp.where(acc[m][dv:dv + 1, :] > 0.0, 0.0, 1.0))

    @pl.when(nonfinite > 0.0)
    def _():
        def slow(ci, carry):
            out = []
            vt = vchunk(ci)
            for m in range(2):
                m_i, a_i = carry[m]
                st = lax.dot_general(kchunk(m, ci), qm[m], dn, preferred_element_type=F32)
                m_new = jnp.maximum(m_i, jnp.max(st, axis=0, keepdims=True))
                p = jnp.exp2(st - m_new).astype(BF16)
                out.append((m_new, jnp.exp2(m_i - m_new) * a_i + jnp.dot(vt, p, preferred_element_type=F32)))
            return tuple(out)

        init = tuple((jnp.full((1, tq), -jnp.inf, F32), jnp.zeros((rows, tq), F32)) for _ in range(2))
        res = lax.fori_loop(0, nchunks, slow, init)
        for m in range(2):
            acc_ref[m] = res[m][1]

    o0 = acc_ref[0, 0:dv, :] / acc_ref[0, dv:dv + 1, :]
    o1 = acc_ref[1, 0:dv, :] / acc_ref[1, dv:dv + 1, :]
    if mode == "diff":
        lv = lam_ref[...]
        lam = (jnp.exp(jnp.sum(lv[0:1] * lv[1:2], axis=-1, keepdims=True))
               - jnp.exp(jnp.sum(lv[2:3] * lv[3:4], axis=-1, keepdims=True)) + lambda_init)
        ot = o0 - lam * o1
        ot = ot * lax.rsqrt(jnp.mean(ot * ot, axis=0, keepdims=True) + NORM_EPS) * nw_ref[...] * (1.0 - lambda_init)
    else:
        ot = jnp.concatenate([o0, o1], axis=0)
    o_ref[0] = ot.T.astype(o_ref.dtype)


def _attention(q, k_arr, vt_arr, lam_vecs, norm_w, *, mode, tq, tk, lambda_init=0.0):
    b, s, w = q.shape
    slabs = w // LANES
    if mode == "diff":
        rows = LANES + VT_PAD
        k_col = lambda p, m: 2 * p + m
        v_grp = lambda p: p
    else:
        rows = GQA_DH + VT_PAD
        k_col = lambda p, m: 2 * (p // 2) + m
        v_grp = lambda p: p // 2
    kspec = lambda m: pl.BlockSpec((1, s, LANES), lambda bi, p, i: (bi, 0, k_col(p, m)))
    return pl.pallas_call(
        functools.partial(_attn_kernel, mode=mode, tk=tk, lambda_init=lambda_init),
        out_shape=jax.ShapeDtypeStruct((b, s, w), BF16),
        grid=(b, slabs, s // tq),
        in_specs=[pl.BlockSpec((1, tq, LANES), lambda bi, p, i: (bi, i, p)),
                  kspec(0), kspec(1),
                  pl.BlockSpec((1, rows, s), lambda bi, p, i: (bi, v_grp(p), 0)),
                  pl.BlockSpec((4, DIFF_DQK), lambda bi, p, i: (0, 0)),
                  pl.BlockSpec((LANES, 1), lambda bi, p, i: (0, 0))],
        out_specs=pl.BlockSpec((1, tq, LANES), lambda bi, p, i: (bi, i, p)),
        scratch_shapes=[pltpu.VMEM((2, rows, tq), F32)],
        compiler_params=_cparams(("arbitrary", "arbitrary", "arbitrary")),
        name="attn_" + mode,
    )(q, k_arr, k_arr, vt_arr, lam_vecs, norm_w)


def _merge_kernel(of_ref, ob_ref, z_ref, g0_ref, g1_ref, g2_ref, yb_ref, yc_ref, x_ref,
                  wa_ref, wb_ref, wc_ref, wo_ref, gnw_ref, fnw_ref, rw_ref, rb_ref,
                  xo_ref, h_ref, id_ref, rwgt_ref):
    o = of_ref[...] + ob_ref[...]
    parts = []
    for h in range(GDN_HEADS):
        oh = o[:, h * LANES:(h + 1) * LANES]
        parts.append(oh * lax.rsqrt(jnp.mean(oh * oh, axis=-1, keepdims=True) + NORM_EPS) * gnw_ref[...])
    z = z_ref[...].astype(F32)
    ya = (jnp.concatenate(parts, axis=1) * (z * _sigmoid(z))).astype(BF16)
    merged = _sigmoid(g0_ref[...].astype(F32)) * jnp.dot(ya, wa_ref[...], preferred_element_type=F32)
    merged = merged + _sigmoid(g1_ref[...].astype(F32)) * jnp.dot(yb_ref[...], wb_ref[...], preferred_element_type=F32)
    merged = merged + _sigmoid(g2_ref[...].astype(F32)) * jnp.dot(yc_ref[...], wc_ref[...], preferred_element_type=F32)
    xn = x_ref[...] + jnp.dot(merged.astype(BF16), wo_ref[...], preferred_element_type=F32)
    xo_ref[...] = xn
    hf = xn * lax.rsqrt(jnp.mean(xn * xn, axis=-1, keepdims=True) + NORM_EPS) * fnw_ref[...]
    h_ref[...] = hf.astype(BF16)
    logits = _dot_split(hf, rw_ref[...]) + rb_ref[...]
    lane = lax.broadcasted_iota(jnp.int32, logits.shape, 1)
    big = jnp.int32(LANES)
    ninf = -jnp.inf
    glog = jnp.where(lane < N_GROUPS, logits, ninf)
    gmax = jnp.max(glog, axis=-1, keepdims=True)
    gidx = jnp.min(jnp.where(glog == gmax, lane, big), axis=-1, keepdims=True)
    gp = 1.0 / jnp.sum(jnp.exp(glog - gmax), axis=-1, keepdims=True)
    e = lane - N_GROUPS
    sel = (e >= 0) & (e < N_EXPERTS) & ((e // EXPERTS_PER_GROUP) == gidx)
    elog = jnp.where(sel, logits, ninf)
    m1 = jnp.max(elog, axis=-1, keepdims=True)
    i1 = jnp.min(jnp.where(elog == m1, lane, big), axis=-1, keepdims=True)
    elog2 = jnp.where(lane == i1, ninf, elog)
    m2 = jnp.max(elog2, axis=-1, keepdims=True)
    i2 = jnp.min(jnp.where(elog2 == m2, lane, big), axis=-1, keepdims=True)
    e2 = jnp.exp(m2 - m1)
    w1 = 1.0 / (1.0 + e2)
    w2 = e2 * w1
    id_ref[...] = jnp.where(lane == 0, i1 - N_GROUPS, jnp.where(lane == 1, i2 - N_GROUPS, 0))
    rwgt_ref[...] = jnp.where(lane == 0, gp * w1, jnp.where(lane == 1, gp * w2, 0.0))


def _merge(o_f, o_b, main2, yb, yc, x2d, wa, wb, wc, wo, gnw, fnw, rw, rb, *, tm):
    t, d = x2d.shape
    full = lambda shp: pl.BlockSpec(shp, lambda i: tuple(0 for _ in shp))
    return pl.pallas_call(
        _merge_kernel,
        out_shape=(jax.ShapeDtypeStruct((t, d), F32), jax.ShapeDtypeStruct((t, d), BF16),
                   jax.ShapeDtypeStruct((t, LANES), jnp.int32), jax.ShapeDtypeStruct((t, LANES), F32)),
        grid=(t // tm,),
        in_specs=[pl.BlockSpec((tm, 512), lambda i: (i, 0)),
                  pl.BlockSpec((tm, 512), lambda i: (i, 0)),
                  pl.BlockSpec((tm, 512), lambda i: (i, COL_Z // 512)),
                  pl.BlockSpec((tm, d), lambda i: (i, 0)),
                  pl.BlockSpec((tm, d), lambda i: (i, 1)),
                  pl.BlockSpec((tm, d), lambda i: (i, 2)),
                  pl.BlockSpec((tm, 512), lambda i: (i, 0)),
                  pl.BlockSpec((tm, 512), lambda i: (i, 0)),
                  pl.BlockSpec((tm, d), lambda i: (i, 0)),
                  full((512, d)), full((512, d)), full((512, d)), full((d, d)),
                  full((1, LANES)), full((1, d)), full((d, LANES)), full((1, LANES))],
        out_specs=(pl.BlockSpec((tm, d), lambda i: (i, 0)), pl.BlockSpec((tm, d), lambda i: (i, 0)),
                   pl.BlockSpec((tm, LANES), lambda i: (i, 0)), pl.BlockSpec((tm, LANES), lambda i: (i, 0))),
        compiler_params=_cparams(("arbitrary",)),
        name="merge_router",
    )(o_f, o_b, main2, main2, main2, main2, yb, yc, x2d, wa, wb, wc, wo, gnw, fnw, rw, rb)


def _expert_kernel(blk_e_ref, nused_ref, x_ref, w1_ref, w3_ref, w2_ref, o_ref, w1b_ref, w3b_ref, w2b_ref):
    i = pl.program_id(0)

    @pl.when((i == 0) | (blk_e_ref[i] != blk_e_ref[jnp.maximum(i - 1, 0)]))
    def _():
        w1b_ref[...] = w1_ref[0, 0].astype(BF16)
        w3b_ref[...] = w3_ref[0, 0].astype(BF16)
        w2b_ref[...] = w2_ref[0, 0].astype(BF16)

    @pl.when(i < nused_ref[0])
    def _():
        x = x_ref[...]
        a = jnp.dot(x, w1b_ref[...], preferred_element_type=F32)
        u = jnp.dot(x, w3b_ref[...], preferred_element_type=F32)
        hmid = (a * _sigmoid(a) * u).astype(BF16)
        o_ref[...] = jnp.dot(hmid, w2b_ref[...], preferred_element_type=F32).astype(o_ref.dtype)

    @pl.when(i >= nused_ref[0])
    def _():
        o_ref[...] = jnp.zeros_like(o_ref)


def _experts(blk_e, nused, xb, w1, w3, w2, *, layer):
    p_len, d = xb.shape
    ff = w1.shape[3]
    nblk = p_len // MOE_BLOCK
    return pl.pallas_call(
        _expert_kernel,
        out_shape=jax.ShapeDtypeStruct((p_len, d), BF16),
        grid_spec=pltpu.PrefetchScalarGridSpec(
            num_scalar_prefetch=2,
            grid=(nblk,),
            in_specs=[pl.BlockSpec((MOE_BLOCK, d), lambda i, be, nu: (i, 0)),
                      pl.BlockSpec((1, 1, d, ff), lambda i, be, nu: (layer, be[i], 0, 0)),
                      pl.BlockSpec((1, 1, d, ff), lambda i, be, nu: (layer, be[i], 0, 0)),
                      pl.BlockSpec((1, 1, ff, d), lambda i, be, nu: (layer, be[i], 0, 0))],
            out_specs=pl.BlockSpec((MOE_BLOCK, d), lambda i, be, nu: (i, 0)),
            scratch_shapes=[pltpu.VMEM((d, ff), BF16), pltpu.VMEM((d, ff), BF16), pltpu.VMEM((ff, d), BF16)],
        ),
        compiler_params=_cparams(("arbitrary",)),
        name="expert_mlp",
    )(blk_e, nused, xb, w1, w3, w2)


def _combine_kernel(x_ref, y0_ref, y1_ref, w_ref, nw_ref, o_ref, *, final):
    w = w_ref[...]
    x = x_ref[...] + w[:, 0:1] * y0_ref[...].astype(F32) + w[:, 1:2] * y1_ref[...].astype(F32)
    if final:
        x = x * lax.rsqrt(jnp.mean(x * x, axis=-1, keepdims=True) + NORM_EPS) * nw_ref[...]
    o_ref[...] = x


def _combine(x2d, y0, y1, wts, norm_w, *, final, tm):
    t, d = x2d.shape
    tile = pl.BlockSpec((tm, d), lambda i: (i, 0))
    return pl.pallas_call(
        functools.partial(_combine_kernel, final=final),
        out_shape=jax.ShapeDtypeStruct((t, d), F32),
        grid=(t // tm,),
        in_specs=[tile, tile, tile, pl.BlockSpec((tm, LANES), lambda i: (i, 0)), pl.BlockSpec((1, d), lambda i: (0, 0))],
        out_specs=tile,
        compiler_params=_cparams(("arbitrary",)),
        name="moe_combine",
    )(x2d, y0, y1, wts, norm_w.reshape(1, d))


def _rope_tables(pos, dim):
    inv = 1.0 / (ROPE_THETA ** (jnp.arange(0, dim, 2, dtype=F32) / dim))
    ang = pos.astype(F32)[:, None] * inv[None, :]
    ang = jnp.concatenate([ang, ang], axis=-1)
    return jnp.cos(ang), jnp.sin(ang)


def _signed_sin(sin):
    half = sin.shape[-1] // 2
    return jnp.concatenate([-sin[:, :half], sin[:, half:]], axis=-1)


def _layout_w_in(w):
    o = 0
    parts = {}
    for name, size in (("qkv", 1536), ("z", 512), ("b", 8), ("a", 8), ("dq", 512), ("dk", 512), ("dv", 512),
                       ("cq", 512), ("ck", 128), ("cv", 128), ("gate", 3072)):
        parts[name] = w[:, o:o + size]
        o += size
    swap = lambda m: jnp.concatenate([m[:, 64:], m[:, :64]], axis=1)
    ba = jnp.concatenate([parts["b"], parts["a"], jnp.zeros((w.shape[0], LANES - 16), w.dtype)], axis=1)
    main = jnp.concatenate([parts["gate"], parts["qkv"], parts["z"], parts["dq"], parts["dk"], parts["dv"],
                            parts["cq"], parts["ck"], swap(parts["ck"]), parts["cv"], ba], axis=1)
    return main.astype(BF16)


def _rows_layout(t, bsz, s):
    nc = s // GDN_CHUNK
    t = t.reshape(bsz, nc, GDN_CHUNK, 2, GDN_HEADS)
    return jnp.transpose(t, (3, 0, 1, 4, 2)).reshape(2, bsz, nc, GDN_ROWS)


def _moe_dispatch(ids, t):
    a = t * TOPK
    p_len = ((a + N_EXPERTS * (MOE_BLOCK - 1) + MOE_BLOCK - 1) // MOE_BLOCK) * MOE_BLOCK
    n_blocks = p_len // MOE_BLOCK
    flat_e = ids.reshape(-1)
    iota_a = jnp.arange(a, dtype=jnp.int32)
    skey = jnp.sort(flat_e * a + iota_a)
    order = skey % a
    se = skey // a
    experts = jnp.arange(N_EXPERTS, dtype=jnp.int32)
    counts = jnp.sum((flat_e[:, None] == experts[None, :]).astype(jnp.int32), axis=0)
    start = jnp.cumsum(counts) - counts
    pcounts = ((counts + MOE_BLOCK - 1) // MOE_BLOCK) * MOE_BLOCK
    pend = jnp.cumsum(pcounts)
    pstart = pend - pcounts
    dest_sorted = pstart[se] + (iota_a - start[se])
    blk_first = jnp.arange(n_blocks, dtype=jnp.int32) * MOE_BLOCK
    blk_e = jnp.minimum(jnp.sum((pend[None, :] <= blk_first[:, None]).astype(jnp.int32), axis=1), N_EXPERTS - 1)
    row = jnp.arange(p_len, dtype=jnp.int32)
    row_e = jnp.repeat(blk_e, MOE_BLOCK)
    j = row - pstart[row_e]
    valid = j < counts[row_e]
    tok_buf = jnp.where(valid, order[jnp.minimum(start[row_e] + j, a - 1)] // TOPK, row % t)
    _, dest = lax.sort((order, dest_sorted), num_keys=1)
    nused = (pend[-1] // MOE_BLOCK).astype(jnp.int32).reshape(1)
    return tok_buf, dest.reshape(t, TOPK), blk_e, nused


def kernel(x, attn_norm_w, w_in, gdn_conv_w, gdn_a_log, gdn_dt_bias, gdn_norm_w, diff_lambda, diff_norm_w,
           gqa_q_norm_w, gqa_k_norm_w, w_branch_a, w_branch_b, w_branch_c, w_out, ffn_norm_w,
           router_group_w, router_group_b, router_expert_w, router_expert_b,
           expert_w_gate, expert_w_up, expert_w_down, final_norm_w):
    bsz, s, d = x.shape
    t = bsz * s
    depth = w_in.shape[0]
    tm = min(512, t)
    ts = min(512, s)

    rows = s // GRID_W
    row = jnp.broadcast_to(jnp.arange(rows)[:, None], (rows, GRID_W)).reshape(s)
    col = jnp.broadcast_to(jnp.arange(GRID_W)[None, :], (rows, GRID_W)).reshape(s)
    c1, s1 = _rope_tables(jnp.arange(s), DIFF_DQK)
    cr, sr = _rope_tables(row, GQA_DH // 2)
    cc, sc = _rope_tables(col, GQA_DH // 2)
    cos1 = jnp.tile(c1, (1, 2))
    sin1 = jnp.tile(_signed_sin(s1), (1, 2))
    cos2 = jnp.tile(jnp.concatenate([cr, cc], axis=-1), (1, 2))
    sin2 = jnp.tile(jnp.concatenate([_signed_sin(sr), _signed_sin(sc)], axis=-1), (1, 2))

    x2 = x.reshape(t, d)
    for l in range(depth):
        lambda_init = 0.8 - 0.6 * math.exp(-0.3 * l)
        main2, ba = _norm_proj(x2, attn_norm_w[l], _layout_w_in(w_in[l]), tm=tm, tn=N_MAIN // 2)
        main3 = main2.reshape(bsz, s, N_MAIN)

        conv_w = jnp.concatenate([gdn_conv_w[l], jnp.zeros((8 - GDN_CONV, gdn_conv_w.shape[2]), F32)], axis=0)
        qnw = jnp.tile(gqa_q_norm_w[l], 2).reshape(1, LANES)
        knw = jnp.tile(gqa_k_norm_w[l], 2).reshape(1, LANES)
        gq, gk, gv, dq, dk, dv, cq, ck, cv = _prep(main3, conv_w, cos1, sin1, cos2, sin2, qnw, knw, ts=ts)

        b_rows = _rows_layout(ba[:, 0:8], bsz, s)
        a_rows = _rows_layout(ba[:, 8:16], bsz, s)
        alog_row = jnp.repeat(gdn_a_log[l], GDN_CHUNK, axis=1).reshape(2, 1, GDN_ROWS)
        dtb_row = jnp.repeat(gdn_dt_bias[l], GDN_CHUNK, axis=1).reshape(2, 1, GDN_ROWS)
        o_f, o_b = _gdn(a_rows, b_rows, alog_row, dtb_row, gq, gk, gv, nbatch=GDN_NBATCH if bsz % GDN_NBATCH == 0 else 1)

        nw_diff = diff_norm_w[l].reshape(LANES, 1)
        yb = _attention(dq, dk, dv, diff_lambda[l], nw_diff, mode="diff",
                        tq=min(ATTN_TQ, s), tk=min(ATTN_TK, s), lambda_init=lambda_init)
        yc = _attention(cq, ck, cv, diff_lambda[l], nw_diff, mode="gqa",
                        tq=min(ATTN_TQ, s), tk=min(ATTN_TK, s))

        rw = jnp.concatenate([router_group_w[l], router_expert_w[l],
                              jnp.zeros((d, LANES - N_GROUPS - N_EXPERTS), F32)], axis=1)
        rb = jnp.concatenate([router_group_b[l], router_expert_b[l],
                              jnp.zeros((LANES - N_GROUPS - N_EXPERTS,), F32)]).reshape(1, LANES)
        x2, h2, ids, wts = _merge(o_f.reshape(t, 512), o_b.reshape(t, 512), main2, yb.reshape(t, 512), yc.reshape(t, 512), x2,
                                  w_branch_a[l].astype(BF16), w_branch_b[l].astype(BF16),
                                  w_branch_c[l].astype(BF16), w_out[l].astype(BF16),
                                  gdn_norm_w[l].reshape(1, LANES), ffn_norm_w[l].reshape(1, d), rw, rb, tm=tm)

        tok_buf, dest, blk_e, nused = _moe_dispatch(ids[:, :TOPK], t)
        yblk = _experts(blk_e, nused, h2[tok_buf], expert_w_gate, expert_w_up, expert_w_down, layer=l)
        x2 = _combine(x2, yblk[dest[:, 0]], yblk[dest[:, 1]], wts, final_norm_w, final=(l == depth - 1), tm=tm)

    return x2.reshape(bsz, s, d)
```
